```python
import jax, jax.numpy as jnp
from jax import lax
import numpy as np

D_MODEL = 1024
BATCH = 32
SEQ = 2048
DEPTH = 2

HEAD_DIM = 64
BLOCK = 128
EPS = 1e-6
FOX_HEADS = 4
MLA_HEADS = 4
MLA_Q_RANK = 256
MLA_KV_RANK = 128
MLA_NOPE_DIM = 64
MLA_ROPE_DIM = 32
MLA_V_DIM = 64
ROPE_THETA = 10000.0
SWA_HEADS = 4
SWA_KV_HEADS = 2
WINDOW = 128
DSA_HEADS = 4
DSA_IDX_HEADS = 8
DSA_IDX_DIM = 32
DSA_TOPK_MAX = 256
N_ALIBI = SWA_HEADS + DSA_HEADS
N_EXPERTS = 16
N_GROUPS = 4
EXPERTS_PER_GROUP = N_EXPERTS // N_GROUPS
TOP_K_EXPERTS = 2
D_EXPERT = 256

IN_SIZES = (
    FOX_HEADS * HEAD_DIM, FOX_HEADS * HEAD_DIM, FOX_HEADS * HEAD_DIM, FOX_HEADS,
    MLA_Q_RANK, MLA_KV_RANK, MLA_ROPE_DIM,
    SWA_HEADS * HEAD_DIM, SWA_KV_HEADS * HEAD_DIM, SWA_KV_HEADS * HEAD_DIM,
    DSA_HEADS * HEAD_DIM, HEAD_DIM, HEAD_DIM,
    DSA_IDX_HEADS * DSA_IDX_DIM, DSA_IDX_DIM, DSA_IDX_HEADS,
)
N_IN = sum(IN_SIZES)
D_MIX = FOX_HEADS * HEAD_DIM + MLA_HEADS * MLA_V_DIM + SWA_HEADS * HEAD_DIM + DSA_HEADS * HEAD_DIM

kernel_name = 'hybrid_fox_mla_swa_dsa_moe_block'


def rmsnorm(x, g):
    xf = x.astype(jnp.float32)
    y = xf * lax.rsqrt(jnp.mean(xf * xf, axis=-1, keepdims=True) + EPS)
    return (y * g.astype(jnp.float32)).astype(x.dtype)


def rope(x, pos):
    half = x.shape[-1] // 2
    inv = ROPE_THETA ** (-jnp.arange(half, dtype=jnp.float32) / half)
    ang = pos.astype(jnp.float32)[:, None] * inv[None, :]
    cos = jnp.cos(ang)[None, :, None, :]
    sin = jnp.sin(ang)[None, :, None, :]
    x1 = x[..., :half].astype(jnp.float32)
    x2 = x[..., half:].astype(jnp.float32)
    return jnp.concatenate([x1 * cos - x2 * sin, x2 * cos + x1 * sin], axis=-1).astype(x.dtype)


def causal_block_attention(q, k, v, log_forget=None):
    B, S, H, Dq = q.shape
    nb = S // BLOCK
    scale = Dq ** -0.5
    key_pos = jnp.arange(S)
    cum = None if log_forget is None else jnp.cumsum(log_forget.astype(jnp.float32), axis=1).swapaxes(1, 2)

    def one_block(i):
        start = i * BLOCK
        q_pos = start + jnp.arange(BLOCK)
        qi = lax.dynamic_slice_in_dim(q, start, BLOCK, axis=1)
        s = jnp.einsum('bqhd,bkhd->bhqk', qi, k, preferred_element_type=jnp.float32) * scale
        if cum is not None:
            ci = lax.dynamic_slice_in_dim(cum, start, BLOCK, axis=2)
            s = s + ci[..., :, None] - cum[..., None, :]
        causal = key_pos[None, :] <= q_pos[:, None]
        s = jnp.where(causal[None, None], s, -jnp.inf)
        p = jax.nn.softmax(s, axis=-1)
        return jnp.einsum('bhqk,bkhd->bqhd', p.astype(v.dtype), v)

    out = lax.map(one_block, jnp.arange(nb))
    return out.swapaxes(0, 1).reshape(B, S, H * v.shape[-1])


def sliding_window_sink_attention(q, k, v, sinks, slopes):
    B, S, Hq, Dh = q.shape
    Hkv = k.shape[2]
    G = Hq // Hkv
    nb = S // BLOCK
    scale = Dh ** -0.5
    qb = q.reshape(B, nb, BLOCK, Hkv, G, Dh)
    kb = k.reshape(B, nb, BLOCK, Hkv, Dh)
    vb = v.reshape(B, nb, BLOCK, Hkv, Dh)
    pad = ((0, 0), (1, 0), (0, 0), (0, 0), (0, 0))
    k_band = jnp.concatenate([jnp.pad(kb, pad)[:, :-1], kb], axis=2)
    v_band = jnp.concatenate([jnp.pad(vb, pad)[:, :-1], vb], axis=2)
    s = jnp.einsum('bnqhgd,bnkhd->bnhgqk', qb, k_band, preferred_element_type=jnp.float32) * scale
    qi = jnp.arange(BLOCK)[:, None]
    kj = jnp.arange(2 * BLOCK)[None, :]
    dist = qi - kj + BLOCK
    key_pos = jnp.arange(nb)[:, None, None] * BLOCK - BLOCK + kj[None]
    valid = (dist >= 0)[None] & (dist < WINDOW)[None] & (key_pos >= 0)
    alibi = slopes.astype(jnp.float32).reshape(Hkv, G, 1, 1) * dist.astype(jnp.float32)
    s = s - alibi
    s = jnp.where(valid[None, :, None, None], s, -jnp.inf)
    sink_col = jnp.broadcast_to(sinks.astype(jnp.float32).reshape(1, 1, Hkv, G, 1, 1), s.shape[:-1] + (1,))
    p = jax.nn.softmax(jnp.concatenate([s, sink_col], axis=-1), axis=-1)[..., :-1]
    out = jnp.einsum('bnhgqk,bnkhd->bnqhgd', p.astype(v.dtype), v_band)
    return out.reshape(B, S, Hq * Dh)


def dsa_attention(q, k, v, q_idx, k_idx, w_idx, slopes):
    B, S, H, Dh = q.shape
    top_k = min(DSA_TOPK_MAX, S // 4)
    nb = S // BLOCK
    scale = Dh ** -0.5
    key_pos = jnp.arange(S)
    w_idx = w_idx.astype(jnp.float32) * (DSA_IDX_HEADS ** -0.5) * (DSA_IDX_DIM ** -0.5)
    gather = jax.vmap(lambda table, idx: table[idx])
    slopes = slopes.astype(jnp.float32)

    def one_block(i):
        start = i * BLOCK
        q_pos = start + jnp.arange(BLOCK)
        qi = lax.dynamic_slice_in_dim(q, start, BLOCK, axis=1)
        qii = lax.dynamic_slice_in_dim(q_idx, start, BLOCK, axis=1)
        wi = lax.dynamic_slice_in_dim(w_idx, start, BLOCK, axis=1)
        idx_logits = jnp.einsum('bqhe,bke->bqhk', qii, k_idx, preferred_element_type=jnp.float32)
        score = jnp.einsum('bqh,bqhk->bqk', wi, jax.nn.relu(idx_logits))
        causal = key_pos[None, :] <= q_pos[:, None]
        score = jnp.where(causal[None], score, -jnp.inf)
        _, sel = lax.top_k(score, top_k)
        k_sel = gather(k, sel)
        v_sel = gather(v, sel)
        logits = jnp.einsum('bqhd,bqkd->bhqk', qi, k_sel, preferred_element_type=jnp.float32) * scale
        dist = (q_pos[None, :, None] - sel).astype(jnp.float32)
        logits = logits - slopes[None, :, None, None] * dist[:, None]
        ok = (sel <= q_pos[None, :, None])[:, None]
        logits = jnp.where(ok, logits, -jnp.inf)
        p = jax.nn.softmax(logits, axis=-1)
        return jnp.einsum('bhqk,bqkd->bqhd', p.astype(v.dtype), v_sel)

    out = lax.map(one_block, jnp.arange(nb))
    return out.swapaxes(0, 1).reshape(B, S, H * Dh)


def hybrid_mixer(h, w_in, b_forget, g_q_mla, w_q_up, g_kv_mla, w_kv_up, sinks, w_out):
    B, S, _ = h.shape
    pos = jnp.arange(S)
    alibi = 2.0 ** (-jnp.arange(1, N_ALIBI + 1, dtype=jnp.float32) * (8.0 / N_ALIBI))
    proj = jnp.einsum('bsd,dn->bsn', h, w_in)
    split_points = np.cumsum(IN_SIZES)[:-1].tolist()
    (a_q, a_k, a_v, a_f, b_cq, b_ckv, b_kr, c_q, c_k, c_v,
     d_q, d_k, d_v, d_qi, d_ki, d_wi) = jnp.split(proj, split_points, axis=-1)

    log_f = jax.nn.log_sigmoid(a_f.astype(jnp.float32) + b_forget.astype(jnp.float32))
    out_a = causal_block_attention(a_q.reshape(B, S, FOX_HEADS, HEAD_DIM),
                                   a_k.reshape(B, S, FOX_HEADS, HEAD_DIM),
                                   a_v.reshape(B, S, FOX_HEADS, HEAD_DIM), log_f)

    q_full = jnp.einsum('bsr,rn->bsn', rmsnorm(b_cq, g_q_mla), w_q_up).reshape(B, S, MLA_HEADS, MLA_NOPE_DIM + MLA_ROPE_DIM)
    kv_full = jnp.einsum('bsr,rn->bsn', rmsnorm(b_ckv, g_kv_mla), w_kv_up).reshape(B, S, MLA_HEADS, MLA_NOPE_DIM + MLA_V_DIM)
    q_nope, q_rot = q_full[..., :MLA_NOPE_DIM], q_full[..., MLA_NOPE_DIM:]
    k_nope, v_mla = kv_full[..., :MLA_NOPE_DIM], kv_full[..., MLA_NOPE_DIM:]
    k_rot = jnp.broadcast_to(rope(b_kr[:, :, None, :], pos), (B, S, MLA_HEADS, MLA_ROPE_DIM))
    q_mla = jnp.concatenate([q_nope, rope(q_rot, pos)], axis=-1)
    k_mla = jnp.concatenate([k_nope, k_rot], axis=-1)
    out_b = causal_block_attention(q_mla, k_mla, v_mla)

    out_c = sliding_window_sink_attention(c_q.reshape(B, S, SWA_HEADS, HEAD_DIM),
                                          c_k.reshape(B, S, SWA_KV_HEADS, HEAD_DIM),
                                          c_v.reshape(B, S, SWA_KV_HEADS, HEAD_DIM),
                                          sinks, alibi[:SWA_HEADS])

    out_d = dsa_attention(d_q.reshape(B, S, DSA_HEADS, HEAD_DIM), d_k, d_v,
                          d_qi.reshape(B, S, DSA_IDX_HEADS, DSA_IDX_DIM), d_ki, d_wi,
                          alibi[SWA_HEADS:])

    merged = jnp.concatenate([out_a, out_b, out_c, out_d], axis=-1)
    return jnp.einsum('bsm,md->bsd', merged, w_out)


def grouped_moe(h, w_router, b_router, w_gate, w_up, w_down):
    B, S, D = h.shape
    t = h.reshape(B * S, D)
    scores = jax.nn.sigmoid(jnp.einsum('td,de->te', t, w_router, preferred_element_type=jnp.float32))
    biased = scores + b_router.astype(jnp.float32)
    grouped = biased.reshape(-1, N_GROUPS, EXPERTS_PER_GROUP)
    group_score = lax.top_k(grouped, 2)[0].sum(-1)
    best_group = jnp.argmax(group_score, axis=-1)
    in_group = jax.nn.one_hot(best_group, N_GROUPS, dtype=jnp.bool_)[:, :, None]
    masked = jnp.where(in_group, grouped, -jnp.inf).reshape(-1, N_EXPERTS)
    _, expert_idx = lax.top_k(masked, TOP_K_EXPERTS)
    w = jnp.take_along_axis(scores, expert_idx, axis=1)
    w = w / jnp.sum(w, axis=-1, keepdims=True)
    gates = jnp.sum(jax.nn.one_hot(expert_idx, N_EXPERTS, dtype=jnp.float32) * w[..., None], axis=1)
    out = jnp.zeros((t.shape[0], D), jnp.float32)
    for e in range(N_EXPERTS):
        hid = jax.nn.silu(t @ w_gate[e]) * (t @ w_up[e])
        out = out + gates[:, e:e + 1] * (hid @ w_down[e]).astype(jnp.float32)
    return out.astype(h.dtype).reshape(B, S, D)


def setup_inputs(seed: int = 0) -> dict:
    key = jax.random.key(seed)
    ks = jax.random.split(key, 24)
    f32 = jnp.float32
    nrm = lambda k, shape, s: jax.random.normal(k, shape, f32) * s
    D = D_MODEL
    return {
        'x': nrm(ks[0], (BATCH, SEQ, D), 1.0),
        'c': nrm(ks[1], (BATCH, D), 1.0),
        'w_ada': nrm(ks[2], (DEPTH, D, 6 * D), 0.5 * D ** -0.5),
        'b_ada': nrm(ks[3], (DEPTH, 6 * D), 0.02),
        'g_norm1': 1.0 + nrm(ks[4], (DEPTH, D), 0.02),
        'w_in': nrm(ks[5], (DEPTH, D, N_IN), D ** -0.5),
        'b_forget': 1.0 + nrm(ks[6], (DEPTH, FOX_HEADS), 0.1),
        'g_q_mla': 1.0 + nrm(ks[7], (DEPTH, MLA_Q_RANK), 0.02),
        'w_q_up': nrm(ks[8], (DEPTH, MLA_Q_RANK, MLA_HEADS * (MLA_NOPE_DIM + MLA_ROPE_DIM)), MLA_Q_RANK ** -0.5),
        'g_kv_mla': 1.0 + nrm(ks[9], (DEPTH, MLA_KV_RANK), 0.02),
        'w_kv_up': nrm(ks[10], (DEPTH, MLA_KV_RANK, MLA_HEADS * (MLA_NOPE_DIM + MLA_V_DIM)), MLA_KV_RANK ** -0.5),
        'sinks': nrm(ks[11], (DEPTH, SWA_HEADS), 0.5),
        'w_out': nrm(ks[12], (DEPTH, D_MIX, D), D_MIX ** -0.5),
        'g_norm2': 1.0 + nrm(ks[13], (DEPTH, D), 0.02),
        'w_router': nrm(ks[14], (D, N_EXPERTS), D ** -0.5),
        'b_router': nrm(ks[15], (N_EXPERTS,), 0.01),
        'w_gate': nrm(ks[16], (DEPTH, N_EXPERTS, D, D_EXPERT), D ** -0.5),
        'w_up': nrm(ks[17], (DEPTH, N_EXPERTS, D, D_EXPERT), D ** -0.5),
        'w_down': nrm(ks[18], (DEPTH, N_EXPERTS, D_EXPERT, D), D_EXPERT ** -0.5),
        'g_final': 1.0 + nrm(ks[19], (D,), 0.02),
    }


def reference(x, c, w_ada, b_ada, g_norm1, w_in, b_forget, g_q_mla, w_q_up, g_kv_mla, w_kv_up,
              sinks, w_out, g_norm2, w_router, b_router, w_gate, w_up, w_down, g_final):
    c_act = jax.nn.silu(c)
    for l in range(DEPTH):
        mod = jnp.einsum('bd,dn->bn', c_act, w_ada[l]) + b_ada[l]
        sh1, sc1, gt1, sh2, sc2, gt2 = jnp.split(mod[:, None, :], 6, axis=-1)
        h = rmsnorm(x, g_norm1[l]) * (1.0 + sc1) + sh1
        x = x + gt1 * hybrid_mixer(h, w_in[l], b_forget[l], g_q_mla[l], w_q_up[l], g_kv_mla[l],
                                   w_kv_up[l], sinks[l], w_out[l])
        h = rmsnorm(x, g_norm2[l]) * (1.0 + sc2) + sh2
        x = x + gt2 * grouped_moe(h, w_router, b_router, w_gate[l], w_up[l], w_down[l])
    return rmsnorm(x, g_final)
```

```python
import functools

import jax
import jax.numpy as jnp
import numpy as np
from jax import lax
from jax.experimental import pallas as pl
from jax.experimental.pallas import tpu as pltpu

F32 = jnp.float32
BF16 = jnp.bfloat16
I32 = jnp.int32

EPS = 1e-6
HEAD_DIM = 64
LANES = 128
N_HEADS = 4
MLA_Q_RANK = 256
MLA_KV_RANK = 128
MLA_NOPE = 64
MLA_ROPE = 32
ROPE_THETA = 10000.0
WINDOW = 128
IDX_HEADS = 8
IDX_DIM = 32
TOPK_MAX = 256
N_EXPERTS = 16
N_GROUPS = 4
D_EXPERT = 256
IN_SIZES = (256, 256, 256, 4, 256, 128, 32, 256, 128, 128, 256, 64, 64, 256, 32, 8)

NEG = -1e30
INT_MIN = -(2 ** 31)
KEY_NEG_INF = INT_MIN + 0x7FFFFF
VMEM_LIMIT = 56 * 1024 * 1024

_CA, _CC, _CD, _CS, _CQ, _CKV, _CEND = 0, 768, 1536, 2432, 2560, 2816, 2944
_F_LANE, _WI_LANE, _KR_LANE = 0, 4, 32


def _nt_dot(a, b):
    return lax.dot_general(a, b, (((1,), (1,)), ((), ())), preferred_element_type=F32)


def _rms(x, g):
    return x * lax.rsqrt(jnp.mean(x * x, axis=-1, keepdims=True) + EPS) * g


def _ada_kernel(c_ref, w_ref, b_ref, o_ref):
    c = c_ref[...]
    act = (c / (1.0 + jnp.exp(-c))).astype(BF16)
    o_ref[0] = jnp.dot(act, w_ref[0].astype(BF16), preferred_element_type=F32) + b_ref[0]


def _ada_call(c, w_ada, b_ada):
    depth, d, n = w_ada.shape
    bsz = c.shape[0]
    tn = 1024
    return pl.pallas_call(
        _ada_kernel,
        grid=(depth, n // tn),
        in_specs=[pl.BlockSpec((bsz, d), lambda l, j: (0, 0)),
                  pl.BlockSpec((1, d, tn), lambda l, j: (l, 0, j)),
                  pl.BlockSpec((1, 1, tn), lambda l, j: (l, 0, j))],
        out_specs=pl.BlockSpec((1, bsz, tn), lambda l, j: (l, 0, j)),
        out_shape=jax.ShapeDtypeStruct((depth, bsz, n), F32),
        compiler_params=pltpu.CompilerParams(dimension_semantics=("arbitrary", "arbitrary"),
                                             vmem_limit_bytes=VMEM_LIMIT),
        name="adaln",
    )(c, w_ada, b_ada.reshape(depth, 1, n))


def _inproj_kernel(x_ref, mod_ref, g1_ref, w_ref, wq_ref, wkv_ref, gq_ref, gkv_ref, tab_ref,
                   oa_ref, oc_ref, od_ref, os_ref, oqb_ref, okb_ref, ovb_ref):
    h = _rms(x_ref[...], g1_ref[...]) * (1.0 + mod_ref[0, 1:2, :]) + mod_ref[0, 0:1, :]
    h = h.astype(BF16)

    def proj(lo, hi):
        return jnp.dot(h, w_ref[:, lo:hi], preferred_element_type=F32)

    oa_ref[...] = proj(_CA, _CC).astype(BF16)
    oc_ref[...] = proj(_CC, _CD).astype(BF16)
    od_ref[...] = proj(_CD, _CS).astype(BF16)
    small = proj(_CS, _CQ)
    os_ref[...] = small

    cq = _rms(proj(_CQ, _CKV), gq_ref[...]).astype(BF16)
    qf = jnp.dot(cq, wq_ref[...], preferred_element_type=F32)
    tab_q, tab_qs, tab_k, tab_ks = tab_ref[0], tab_ref[1], tab_ref[2], tab_ref[3]
    for hd in range(N_HEADS):
        qg = qf[:, hd * LANES:(hd + 1) * LANES]
        qr = qg * tab_q + pltpu.roll(qg, LANES - MLA_ROPE, axis=1) * tab_qs
        oqb_ref[:, hd * LANES:(hd + 1) * LANES] = qr.astype(BF16)

    ckv = _rms(proj(_CKV, _CEND), gkv_ref[...]).astype(BF16)
    kr = (small * tab_k + pltpu.roll(small, LANES - MLA_ROPE, axis=1) * tab_ks).astype(BF16)
    kvf = jnp.dot(jnp.concatenate([ckv, kr], axis=1), wkv_ref[...], preferred_element_type=F32)
    okb_ref[...] = kvf[:, :4 * LANES].astype(BF16)
    ovb_ref[...] = kvf[:, 4 * LANES:].astype(BF16)


def _inproj_call(x2, mod_l, g1, w_all, wq, wkv, gq, gkv, tabs, seq, tm):
    t, d = x2.shape
    tpb = seq // tm
    row = lambda n: pl.BlockSpec((tm, n), lambda i: (i, 0))
    const = lambda shape: pl.BlockSpec(shape, lambda i: (0,) * len(shape))
    widths = (768, 768, 896, 128, 512, 512, 256)
    dtypes = (BF16, BF16, BF16, F32, BF16, BF16, BF16)
    return pl.pallas_call(
        _inproj_kernel,
        grid=(t // tm,),
        in_specs=[row(d),
                  pl.BlockSpec((1, 6, d), lambda i: (i // tpb, 0, 0)),
                  const((1, d)), const(w_all.shape), const(wq.shape), const(wkv.shape),
                  const((1, MLA_Q_RANK)), const((1, MLA_KV_RANK)),
                  pl.BlockSpec((4, tm, LANES), lambda i: (0, i % tpb, 0))],
        out_specs=[row(n) for n in widths],
        out_shape=[jax.ShapeDtypeStruct((t, n), dt) for n, dt in zip(widths, dtypes)],
        compiler_params=pltpu.CompilerParams(dimension_semantics=("arbitrary",), vmem_limit_bytes=VMEM_LIMIT),
        name="inproj",
    )(x2, mod_l, g1, w_all, wq, wkv, gq, gkv, tabs)


def _fox_cumsum_kernel(s_ref, b_ref, o_ref):
    z = s_ref[0] + b_ref[...]
    lf = jnp.minimum(z, 0.0) - jnp.log(1.0 + jnp.exp(-jnp.abs(z)))
    n = lf.shape[0]
    row = lax.broadcasted_iota(I32, lf.shape, 0)
    d = 1
    while d < n:
        lf = lf + jnp.where(row >= d, pltpu.roll(lf, d, axis=0), 0.0)
        d *= 2
    o_ref[0] = lf


def _fox_cumsum_call(small3, bf):
    b, s, _ = small3.shape
    return pl.pallas_call(
        _fox_cumsum_kernel,
        grid=(b,),
        in_specs=[pl.BlockSpec((1, s, LANES), lambda i: (i, 0, 0)),
                  pl.BlockSpec((1, LANES), lambda i: (0, 0))],
        out_specs=pl.BlockSpec((1, s, LANES), lambda i: (i, 0, 0)),
        out_shape=jax.ShapeDtypeStruct((b, s, LANES), F32),
        compiler_params=pltpu.CompilerParams(dimension_semantics=("arbitrary",), vmem_limit_bytes=VMEM_LIMIT),
        name="fox_cumsum",
    )(small3, bf)


def _softmax_update(s, v, m_ref, l_ref, acc_ref, idx):
    m_prev = m_ref[idx]
    m_new = jnp.maximum(m_prev, jnp.max(s, axis=1, keepdims=True))
    alpha = jnp.exp(m_prev - m_new)
    p = jnp.exp(s - m_new[:, 0:1])
    l_ref[idx] = alpha * l_ref[idx] + jnp.sum(p, axis=1, keepdims=True)
    acc_ref[idx] = alpha * acc_ref[idx] + jnp.dot(p.astype(BF16), v, preferred_element_type=F32)
    m_ref[idx] = m_new


def _pair_output(l_ref, acc_ref, idx0):
    lane = lax.broadcasted_iota(I32, (1, LANES), 1)
    o0 = acc_ref[idx0] / l_ref[idx0]
    o1 = acc_ref[idx0 + 1] / l_ref[idx0 + 1]
    return jnp.where(lane < HEAD_DIM, o0, o1)


def _split_pair(q):
    lane = lax.broadcasted_iota(I32, (1, LANES), 1)
    zero = jnp.zeros_like(q)
    return jnp.where(lane < HEAD_DIM, q, zero), jnp.where(lane >= HEAD_DIM, q, zero)


def _causal_attn_kernel(*refs, fox, tq, tk):
    if fox:
        q_ref, k_ref, v_ref, fc_ref, fr_ref, o_ref, m_ref, l_ref, acc_ref = refs
    else:
        q_ref, k_ref, v_ref, o_ref, m_ref, l_ref, acc_ref = refs
    hp = pl.program_id(1)
    i = pl.program_id(2)
    if fox:
        qs = _split_pair(q_ref[0])
        lane = lax.broadcasted_iota(I32, (1, LANES), 1)
        fcol = [jnp.sum(jnp.where(lane == 2 * hp + j, fc_ref[0], 0.0), axis=1, keepdims=True) for j in range(2)]
    else:
        qs = (q_ref[0, :, :LANES], q_ref[0, :, LANES:])
    m_ref[...] = jnp.full(m_ref.shape, NEG, F32)
    l_ref[...] = jnp.zeros(l_ref.shape, F32)
    acc_ref[...] = jnp.zeros(acc_ref.shape, F32)
    causal = (lax.broadcasted_iota(I32, (tq, tk), 1) <= lax.broadcasted_iota(I32, (tq, tk), 0))

    def step(c, diagonal):
        start = pl.multiple_of(c * tk, tk)
        kc = k_ref[0, pl.ds(start, tk), :]
        vc = v_ref[0, pl.ds(start, tk), :]
        for j in range(2):
            s = _nt_dot(qs[j], kc if fox else kc[:, j * LANES:(j + 1) * LANES])
            if fox:
                s = (s + fcol[j]) - fr_ref[0, pl.ds(2 * hp + j, 1), pl.ds(start, tk)]
            if diagonal:
                s = jnp.where(causal, s, NEG)
            _softmax_update(s, vc, m_ref, l_ref, acc_ref, j)

    def body(c, carry):
        step(c, False)
        return carry

    lax.fori_loop(0, i, body, 0)
    step(i, True)
    o_ref[0] = _pair_output(l_ref, acc_ref, 0).astype(o_ref.dtype)


def _causal_attn_call(arrs, fox, b, s, tq):
    tk = tq
    scratch = [pltpu.VMEM((2, tq, LANES), F32) for _ in range(3)]
    if fox:
        qkv, fcol, frow = arrs
        operands = (qkv, qkv, qkv, fcol, frow)
        in_specs = [pl.BlockSpec((1, tq, LANES), lambda bi, hp, i: (bi, i, hp)),
                    pl.BlockSpec((1, s, LANES), lambda bi, hp, i: (bi, 0, 2 + hp)),
                    pl.BlockSpec((1, s, LANES), lambda bi, hp, i: (bi, 0, 4 + hp)),
                    pl.BlockSpec((1, tq, LANES), lambda bi, hp, i: (bi, i, 0)),
                    pl.BlockSpec((1, 8, s), lambda bi, hp, i: (bi, 0, 0))]
    else:
        operands = arrs
        in_specs = [pl.BlockSpec((1, tq, 2 * LANES), lambda bi, hp, i: (bi, i, hp)),
                    pl.BlockSpec((1, s, 2 * LANES), lambda bi, hp, i: (bi, 0, hp)),
                    pl.BlockSpec((1, s, LANES), lambda bi, hp, i: (bi, 0, hp))]
    return pl.pallas_call(
        functools.partial(_causal_attn_kernel, fox=fox, tq=tq, tk=tk),
        grid=(b, 2, s // tq),
        in_specs=in_specs,
        out_specs=pl.BlockSpec((1, tq, LANES), lambda bi, hp, i: (bi, i, hp)),
        out_shape=jax.ShapeDtypeStruct((b, s, 2 * LANES), BF16),
        scratch_shapes=scratch,
        compiler_params=pltpu.CompilerParams(dimension_semantics=("arbitrary",) * 3, vmem_limit_bytes=VMEM_LIMIT),
        name="attn_fox" if fox else "attn_mla",
    )(*operands)


def _alibi_slope(head):
    return lax.shift_left(jnp.int32(1), 7 - head).astype(F32) * (2.0 ** -8)


def _swa_kernel(sink_ref, q_ref, k_ref, v_ref, o_ref, m_ref, l_ref, acc_ref):
    hp = pl.program_id(1)
    i = pl.program_id(2)
    qs = _split_pair(q_ref[0])
    for j in range(2):
        m_ref[j] = jnp.full(m_ref.shape[1:], sink_ref[2 * hp + j], F32)
    l_ref[...] = jnp.ones(l_ref.shape, F32)
    acc_ref[...] = jnp.zeros(acc_ref.shape, F32)
    rows = lax.broadcasted_iota(I32, (WINDOW, WINDOW), 0)
    cols = lax.broadcasted_iota(I32, (WINDOW, WINDOW), 1)

    def step(c, dist, valid):
        start = pl.multiple_of(c * WINDOW, WINDOW)
        kc = k_ref[0, pl.ds(start, WINDOW), :]
        vc = v_ref[0, pl.ds(start, WINDOW), :]
        distf = dist.astype(F32)
        for j in range(2):
            s = _nt_dot(qs[j], kc) - _alibi_slope(2 * hp + j) * distf
            _softmax_update(jnp.where(valid, s, NEG), vc, m_ref, l_ref, acc_ref, j)

    @pl.when(i > 0)
    def _():
        dist = rows - cols + WINDOW
        step(i - 1, dist, dist < WINDOW)

    dist = rows - cols
    step(i, dist, dist >= 0)
    o_ref[0] = _pair_output(l_ref, acc_ref, 0).astype(o_ref.dtype)


def _swa_call(qkv, sinks, b, s):
    scratch = [pltpu.VMEM((2, WINDOW, LANES), F32) for _ in range(3)]
    return pl.pallas_call(
        _swa_kernel,
        grid=(b, 2, s // WINDOW),
        in_specs=[pl.BlockSpec(memory_space=pltpu.SMEM),
                  pl.BlockSpec((1, WINDOW, LANES), lambda bi, hp, i: (bi, i, hp)),
                  pl.BlockSpec((1, s, LANES), lambda bi, hp, i: (bi, 0, 2 + hp)),
                  pl.BlockSpec((1, s, LANES), lambda bi, hp, i: (bi, 0, 4 + hp))],
        out_specs=pl.BlockSpec((1, WINDOW, LANES), lambda bi, hp, i: (bi, i, hp)),
        out_shape=jax.ShapeDtypeStruct((b, s, 2 * LANES), BF16),
        scratch_shapes=scratch,
        compiler_params=pltpu.CompilerParams(dimension_semantics=("arbitrary",) * 3, vmem_limit_bytes=VMEM_LIMIT),
        name="attn_swa",
    )(sinks, qkv, qkv, qkv)


def _dsa_kernel(q_ref, k_ref, v_ref, qi_ref, ki_ref, wi_ref, o_ref, keys_ref, thr_ref, m_ref, l_ref, acc_ref,
                *, seq, topk):
    blk = LANES
    i = pl.program_id(1)
    nchunk = i + 1
    lane = lax.broadcasted_iota(I32, (1, LANES), 1)
    rows = lax.broadcasted_iota(I32, (blk, blk), 0)
    cols = lax.broadcasted_iota(I32, (blk, blk), 1)

    qi = qi_ref[0]
    wi = wi_ref[0]
    qms, wbs = [], []
    for hd in range(IDX_HEADS):
        g, r = divmod(hd, LANES // IDX_DIM)
        sel = (lane >= r * IDX_DIM) & (lane < (r + 1) * IDX_DIM)
        qg = qi[:, g * LANES:(g + 1) * LANES]
        qms.append(jnp.where(sel, qg, jnp.zeros_like(qg)))
        wbs.append(jnp.broadcast_to(wi[:, _WI_LANE + hd:_WI_LANE + hd + 1], (blk, blk)))

    def score_chunk(c, carry):
        kic = ki_ref[0, pl.ds(pl.multiple_of(c * blk, blk), blk), :]
        sc = jnp.zeros((blk, blk), F32)
        for hd in range(IDX_HEADS):
            sc = sc + wbs[hd] * jnp.maximum(_nt_dot(qms[hd], kic), 0.0)
        sc = jnp.where((cols <= rows) | (c < i), sc, -jnp.inf)
        sc = jnp.where(sc == 0.0, 0.0, sc)
        bits = pltpu.bitcast(sc, I32)
        keys_ref[c] = bits ^ ((bits >> 31) & 0x7FFFFFFF)
        return carry

    lax.fori_loop(0, nchunk, score_chunk, 0)

    thr_ref[...] = jnp.full(thr_ref.shape, KEY_NEG_INF + 1, I32)

    def count(pred):
        def body(c, acc):
            return acc + jnp.where(pred(keys_ref[c], c), 1.0, 0.0)
        acc = lax.fori_loop(0, nchunk, body, jnp.zeros((blk, blk), F32))
        return jnp.sum(acc, axis=1, keepdims=True)

    @pl.when(nchunk * blk > topk)
    def _():
        kf = float(topk)

        def bit_step(it, carry):
            thr, cnt_thr = carry
            cand = thr + lax.shift_left(jnp.int32(1), 31 - it)
            cand_b = jnp.broadcast_to(cand, (blk, blk))
            cnt = count(lambda kk, c: kk >= cand_b)
            ok = cnt >= kf
            return jnp.where(ok, cand, thr), jnp.where(ok, cnt, cnt_thr)

        thr0 = jnp.full((blk, 1), INT_MIN, I32)
        thr, cnt_thr = lax.fori_loop(0, 32, bit_step, (thr0, jnp.full((blk, 1), float(seq), F32)))
        thr_b = jnp.broadcast_to(thr, (blk, blk))
        thr_ref[...] = thr_b

        @pl.when(jnp.max(cnt_thr) > kf)
        def _():
            need = kf - count(lambda kk, c: kk > thr_b)

            def idx_step(it, pos):
                cand = pos + lax.shift_left(jnp.int32(1), (seq.bit_length() - 2) - it)
                cand_b = jnp.broadcast_to(cand, (blk, blk))
                cnt = count(lambda kk, c: (kk == thr_b) & (cols + c * blk < cand_b))
                return jnp.where(cnt < need, cand, pos)

            pos = lax.fori_loop(0, seq.bit_length() - 1, idx_step, jnp.zeros((blk, 1), I32))
            pos_b = jnp.broadcast_to(pos, (blk, blk))

            def demote(c, carry):
                kk = keys_ref[c]
                keys_ref[c] = jnp.where((kk == thr_b) & (cols + c * blk > pos_b), kk - 1, kk)
                return carry

            lax.fori_loop(0, nchunk, demote, 0)

    @pl.when(nchunk % 2 == 1)
    def _():
        keys_ref[nchunk] = jnp.full((blk, blk), INT_MIN, I32)

    q = q_ref[0]
    qs = _split_pair(q[:, :LANES]) + _split_pair(q[:, LANES:])
    m_ref[...] = jnp.full(m_ref.shape, NEG, F32)
    l_ref[...] = jnp.zeros(l_ref.shape, F32)
    acc_ref[...] = jnp.zeros(acc_ref.shape, F32)
    thr_b2 = jnp.concatenate([thr_ref[...], thr_ref[...]], axis=1)
    rel = (lax.broadcasted_iota(I32, (blk, 2 * blk), 0) - lax.broadcasted_iota(I32, (blk, 2 * blk), 1))

    def attend(c2, carry):
        start = pl.multiple_of(c2 * 2 * blk, 2 * blk)
        kc = k_ref[0, pl.ds(start, 2 * blk), :]
        vc = v_ref[0, pl.ds(start, 2 * blk), :]
        keep = jnp.concatenate([keys_ref[2 * c2], keys_ref[2 * c2 + 1]], axis=1) >= thr_b2
        distf = (rel + (i * blk - start)).astype(F32)
        for hd in range(N_HEADS):
            s = _nt_dot(qs[hd], kc) - (2.0 ** -(N_HEADS + hd + 1)) * distf
            _softmax_update(jnp.where(keep, s, NEG), vc, m_ref, l_ref, acc_ref, hd)
        return carry

    lax.fori_loop(0, (nchunk + 1) // 2, attend, 0)
    o_ref[0] = jnp.concatenate([_pair_output(l_ref, acc_ref, 0), _pair_output(l_ref, acc_ref, 2)],
                               axis=1).astype(o_ref.dtype)


def _dsa_call(od3, small3, b, s, topk):
    blk = LANES
    scratch = [pltpu.VMEM((s // blk, blk, blk), I32), pltpu.VMEM((blk, blk), I32)]
    scratch += [pltpu.VMEM((N_HEADS, blk, LANES), F32) for _ in range(3)]
    return pl.pallas_call(
        functools.partial(_dsa_kernel, seq=s, topk=topk),
        grid=(b, s // blk),
        in_specs=[pl.BlockSpec((1, blk, 2 * LANES), lambda bi, i: (bi, i, 0)),
                  pl.BlockSpec((1, s, LANES), lambda bi, i: (bi, 0, 2)),
                  pl.BlockSpec((1, s, LANES), lambda bi, i: (bi, 0, 3)),
                  pl.BlockSpec((1, blk, 2 * LANES), lambda bi, i: (bi, i, 2)),
                  pl.BlockSpec((1, s, LANES), lambda bi, i: (bi, 0, 6)),
                  pl.BlockSpec((1, blk, LANES), lambda bi, i: (bi, i, 0))],
        out_specs=pl.BlockSpec((1, blk, 2 * LANES), lambda bi, i: (bi, i, 0)),
        out_shape=jax.ShapeDtypeStruct((b, s, 2 * LANES), BF16),
        scratch_shapes=scratch,
        compiler_params=pltpu.CompilerParams(dimension_semantics=("arbitrary",) * 2, vmem_limit_bytes=VMEM_LIMIT),
        name="attn_dsa",
    )(od3, od3, od3, od3, od3, small3)


def _outproj_kernel(x_ref, oa_ref, ob_ref, oc_ref, od_ref, w_ref, mod_ref, g2_ref, wr_ref, br_ref,
                    xo_ref, h_ref, gt_ref):
    mix = None
    for n, o_ref in enumerate((oa_ref, ob_ref, oc_ref, od_ref)):
        part = jnp.dot(o_ref[...], w_ref[n * 2 * LANES:(n + 1) * 2 * LANES, :], preferred_element_type=F32)
        mix = part if mix is None else mix + part
    xn = x_ref[...] + mod_ref[0, 2:3, :] * mix
    xo_ref[...] = xn
    h = _rms(xn, g2_ref[...]) * (1.0 + mod_ref[0, 4:5, :]) + mod_ref[0, 3:4, :]
    h_ref[...] = h.astype(BF16)

    logits = lax.dot_general(wr_ref[...], h, (((1,), (1,)), ((), ())), precision=lax.Precision.HIGHEST,
                             preferred_element_type=F32)
    score = 1.0 / (1.0 + jnp.exp(-logits))
    biased = score + br_ref[...]
    srow = [score[e:e + 1, :] for e in range(N_EXPERTS)]
    brow = [biased[e:e + 1, :] for e in range(N_EXPERTS)]
    per = N_EXPERTS // N_GROUPS
    best_v = best_g = None
    for g in range(N_GROUPS):
        r = brow[g * per:(g + 1) * per]
        top2 = None
        for a in range(per):
            for c in range(a + 1, per):
                top2 = r[a] + r[c] if top2 is None else jnp.maximum(top2, r[a] + r[c])
        if g == 0:
            best_v, best_g = top2, jnp.zeros_like(top2, dtype=I32)
        else:
            up = top2 > best_v
            best_v = jnp.where(up, top2, best_v)
            best_g = jnp.where(up, g, best_g)
    cand = [jnp.where(best_g == e // per, brow[e], -jnp.inf) for e in range(N_EXPERTS)]

    def first_max(vals):
        v, idx = vals[0], jnp.zeros_like(best_g)
        for e in range(1, N_EXPERTS):
            up = vals[e] > v
            v = jnp.where(up, vals[e], v)
            idx = jnp.where(up, e, idx)
        return idx

    i1 = first_max(cand)
    i2 = first_max([jnp.where(i1 == e, -jnp.inf, cand[e]) for e in range(N_EXPERTS)])
    s1 = sum(jnp.where(i1 == e, srow[e], 0.0) for e in range(N_EXPERTS))
    s2 = sum(jnp.where(i2 == e, srow[e], 0.0) for e in range(N_EXPERTS))
    den = s1 + s2
    for e in range(N_EXPERTS):
        gt_ref[e:e + 1, :] = jnp.where(i1 == e, s1 / den, jnp.where(i2 == e, s2 / den, 0.0))


def _outproj_call(x2, outs, w_out, mod_l, g2, wr_t, br, seq, tm):
    t, d = x2.shape
    tpb = seq // tm
    row = lambda n: pl.BlockSpec((tm, n), lambda i: (i, 0))
    const = lambda shape: pl.BlockSpec(shape, lambda i: (0,) * len(shape))
    return pl.pallas_call(
        _outproj_kernel,
        grid=(t // tm,),
        in_specs=[row(d)] + [row(2 * LANES)] * 4 + [
            const(w_out.shape), pl.BlockSpec((1, 6, d), lambda i: (i // tpb, 0, 0)), const((1, d)),
            const(wr_t.shape), const(br.shape)],
        out_specs=[row(d), row(d), pl.BlockSpec((N_EXPERTS, tm), lambda i: (0, i))],
        out_shape=[jax.ShapeDtypeStruct((t, d), F32), jax.ShapeDtypeStruct((t, d), BF16),
                   jax.ShapeDtypeStruct((N_EXPERTS, t), F32)],
        compiler_params=pltpu.CompilerParams(dimension_semantics=("arbitrary",), vmem_limit_bytes=VMEM_LIMIT),
        name="outproj_router",
    )(x2, *outs, w_out, mod_l, g2, wr_t, br)


def _moe_kernel(h_ref, g_ref, x_ref, mod_ref, gf_ref, wgu_ref, wd_ref, o_ref, acc_ref, *, final):
    e = pl.program_id(1)

    @pl.when(e == 0)
    def _():
        acc_ref[...] = jnp.zeros(acc_ref.shape, F32)

    gu = jnp.dot(h_ref[...], wgu_ref[0], preferred_element_type=F32)
    gate, up = gu[:, :D_EXPERT], gu[:, D_EXPERT:]
    hid = (gate / (1.0 + jnp.exp(-gate)) * up).astype(BF16)
    y = jnp.dot(hid, wd_ref[0], preferred_element_type=F32)
    lane = lax.broadcasted_iota(I32, (1, N_EXPERTS), 1)
    ge = jnp.sum(jnp.where(lane == e, g_ref[...], 0.0), axis=1, keepdims=True)
    acc_ref[...] += ge * y

    @pl.when(e == N_EXPERTS - 1)
    def _():
        xn = x_ref[...] + mod_ref[0, 5:6, :] * acc_ref[...]
        o_ref[...] = _rms(xn, gf_ref[...]) if final else xn


def _moe_call(h2, gates, x2, mod_l, gf, wgu, wd, seq, tm, final):
    t, d = x2.shape
    tpb = seq // tm
    return pl.pallas_call(
        functools.partial(_moe_kernel, final=final),
        grid=(t // tm, N_EXPERTS),
        in_specs=[pl.BlockSpec((tm, d), lambda i, e: (i, 0)),
                  pl.BlockSpec((tm, N_EXPERTS), lambda i, e: (i, 0)),
                  pl.BlockSpec((tm, d), lambda i, e: (i, 0)),
                  pl.BlockSpec((1, 6, d), lambda i, e: (i // tpb, 0, 0)),
                  pl.BlockSpec((1, d), lambda i, e: (0, 0)),
                  pl.BlockSpec((1, d, 2 * D_EXPERT), lambda i, e: (e, 0, 0)),
                  pl.BlockSpec((1, D_EXPERT, d), lambda i, e: (e, 0, 0))],
        out_specs=pl.BlockSpec((tm, d), lambda i, e: (i, 0)),
        out_shape=jax.ShapeDtypeStruct((t, d), F32),
        scratch_shapes=[pltpu.VMEM((tm, d), F32)],
        compiler_params=pltpu.CompilerParams(dimension_semantics=("arbitrary",) * 2, vmem_limit_bytes=VMEM_LIMIT),
        name="moe",
    )(h2, gates, x2, mod_l, gf, wgu, wd)


def _swap_half(w):
    half = w.shape[1] // 2
    return jnp.concatenate([-w[:, half:], w[:, :half]], axis=1)


def _layer_weights(w_in, w_q_up, w_kv_up):
    d = w_in.shape[0]
    pts = np.cumsum(IN_SIZES)[:-1].tolist()
    (a_q, a_k, a_v, a_f, b_cq, b_ckv, b_kr, c_q, c_k, c_v,
     d_q, d_k, d_v, d_qi, d_ki, d_wi) = jnp.split(w_in, pts, axis=1)
    qs = HEAD_DIM ** -0.5
    dup = lambda w: jnp.concatenate([w[:, :HEAD_DIM], w[:, :HEAD_DIM], w[:, HEAD_DIM:], w[:, HEAD_DIM:]], axis=1)
    small = jnp.concatenate([
        a_f, d_wi * ((IDX_HEADS * IDX_DIM) ** -0.5),
        jnp.zeros((d, _KR_LANE - _WI_LANE - IDX_HEADS), F32),
        b_kr, _swap_half(b_kr), jnp.zeros((d, LANES - _KR_LANE - 2 * MLA_ROPE), F32)], axis=1)
    w_all = jnp.concatenate([
        a_q * qs, a_k, a_v,
        c_q * qs, dup(c_k), dup(c_v),
        d_q * qs, d_k, d_k, d_v, d_v, d_qi, d_ki, d_ki, d_ki, d_ki,
        small, b_cq, b_ckv], axis=1).astype(BF16)

    per_q = MLA_NOPE + MLA_ROPE
    wq = []
    for hd in range(N_HEADS):
        blk = w_q_up[:, hd * per_q:(hd + 1) * per_q]
        rot = blk[:, MLA_NOPE:]
        wq += [blk[:, :MLA_NOPE], rot, _swap_half(rot)]
    wq = jnp.concatenate(wq, axis=1).astype(BF16)

    place = np.zeros((LANES, LANES), np.float32)
    place[_KR_LANE + np.arange(MLA_ROPE), MLA_NOPE + np.arange(MLA_ROPE)] = 1.0
    place = jnp.asarray(place)
    kcols, vcols = [], []
    for hd in range(N_HEADS):
        blk = w_kv_up[:, hd * 2 * HEAD_DIM:(hd + 1) * 2 * HEAD_DIM]
        knope = jnp.concatenate([blk[:, :MLA_NOPE], jnp.zeros((MLA_KV_RANK, LANES - MLA_NOPE), F32)], axis=1)
        kcols.append(jnp.concatenate([knope, place], axis=0))
        vcols.append(jnp.concatenate([blk[:, MLA_NOPE:], jnp.zeros((LANES, HEAD_DIM), F32)], axis=0))
    wkv = jnp.concatenate(kcols + vcols, axis=1).astype(BF16)
    return w_all, wq, wkv


def _rope_tables(seq):
    half = MLA_ROPE // 2
    inv = ROPE_THETA ** (-jnp.arange(half, dtype=F32) / half)
    ang = jnp.arange(seq, dtype=F32)[:, None] * inv[None, :]
    cos = jnp.tile(jnp.cos(ang), (1, 2))
    sin = jnp.tile(jnp.sin(ang), (1, 2))
    scale = (MLA_NOPE + MLA_ROPE) ** -0.5
    z = lambda n: jnp.zeros((seq, n), F32)
    tab_q = jnp.concatenate([jnp.full((seq, MLA_NOPE), scale, F32), cos * scale, z(MLA_ROPE)], axis=1)
    tab_qs = jnp.concatenate([z(MLA_NOPE), sin * scale, z(MLA_ROPE)], axis=1)
    tab_k = jnp.concatenate([z(_KR_LANE), cos, z(LANES - _KR_LANE - MLA_ROPE)], axis=1)
    tab_ks = jnp.concatenate([z(_KR_LANE), sin, z(LANES - _KR_LANE - MLA_ROPE)], axis=1)
    return jnp.stack([tab_q, tab_qs, tab_k, tab_ks])


def kernel(x, c, w_ada, b_ada, g_norm1, w_in, b_forget, g_q_mla, w_q_up, g_kv_mla, w_kv_up, sinks, w_out,
           g_norm2, w_router, b_router, w_gate, w_up, w_down, g_final):
    b, s, d = x.shape
    depth = w_in.shape[0]
    t = b * s
    topk = min(TOPK_MAX, s // 4)
    tm = min(512, s)
    tq = min(256, s)
    assert s % tm == 0 and s % tq == 0 and topk % LANES == 0 and (s // LANES) % 2 == 0

    mod = _ada_call(c, w_ada, b_ada).reshape(depth, b, 6, d)
    tabs = _rope_tables(s)
    wr_t = w_router.T
    br = b_router.reshape(N_EXPERTS, 1)
    x2 = x.reshape(t, d)
    for l in range(depth):
        w_all, wq, wkv = _layer_weights(w_in[l], w_q_up[l], w_kv_up[l])
        oa, oc, od, osm, oqb, okb, ovb = _inproj_call(
            x2, mod[l], g_norm1[l].reshape(1, d), w_all, wq, wkv,
            g_q_mla[l].reshape(1, -1), g_kv_mla[l].reshape(1, -1), tabs, s, tm)
        small3 = osm.reshape(b, s, LANES)
        bf = jnp.zeros((1, LANES), F32).at[0, :N_HEADS].set(b_forget[l])
        fcum = _fox_cumsum_call(small3, bf)
        frow = jnp.swapaxes(fcum[:, :, :8], 1, 2)
        out_a = _causal_attn_call((oa.reshape(b, s, -1), fcum, frow), True, b, s, tq)
        out_b = _causal_attn_call((oqb.reshape(b, s, -1), okb.reshape(b, s, -1), ovb.reshape(b, s, -1)),
                                  False, b, s, tq)
        out_c = _swa_call(oc.reshape(b, s, -1), sinks[l], b, s)
        out_d = _dsa_call(od.reshape(b, s, -1), small3, b, s, topk)
        outs = [o.reshape(t, 2 * LANES) for o in (out_a, out_b, out_c, out_d)]
        x2, h2, gates_t = _outproj_call(x2, outs, w_out[l].astype(BF16), mod[l], g_norm2[l].reshape(1, d),
                                        wr_t, br, s, tm)
        wgu = jnp.concatenate([w_gate[l], w_up[l]], axis=2).astype(BF16)
        x2 = _moe_call(h2, gates_t.T, x2, mod[l], g_final.reshape(1, d), wgu, w_down[l].astype(BF16),
                       s, tm, l == depth - 1)
    return x2.reshape(b, s, d)
```

```python
import functools

import jax
import jax.numpy as jnp
import numpy as np
from jax import lax
from jax.experimental import pallas as pl
from jax.experimental.pallas import tpu as pltpu

F32 = jnp.float32
BF16 = jnp.bfloat16
I32 = jnp.int32

EPS = 1e-6
HEAD_DIM = 64
LANES = 128
SUBLANES = 8
N_HEADS = 4
MLA_Q_RANK = 256
MLA_KV_RANK = 128
MLA_NOPE = 64
MLA_ROPE = 32
ROPE_THETA = 10000.0
WINDOW = 128
IDX_HEADS = 8
IDX_DIM = 32
TOPK_MAX = 256
N_EXPERTS = 16
N_GROUPS = 4
D_EXPERT = 256
IN_SIZES = (256, 256, 256, 4, 256, 128, 32, 256, 128, 128, 256, 64, 64, 256, 32, 8)

NEG = -1e30
INT_MIN = -(2 ** 31)
KEY_NEG_INF = INT_MIN + 0x7FFFFF
VMEM_LIMIT = 56 * 1024 * 1024

_CA, _CC, _CD, _CS, _CQ, _CKV, _CEND = 0, 768, 1536, 2432, 2560, 2816, 2944
_F_LANE, _WI_LANE, _KR_LANE = 0, 4, 32


def _nt_dot(a, b):
    return lax.dot_general(a, b, (((1,), (1,)), ((), ())), preferred_element_type=F32)


def _rms(x, g):
    return x * lax.rsqrt(jnp.mean(x * x, axis=-1, keepdims=True) + EPS) * g


def _ada_kernel(c_ref, w_ref, b_ref, o_ref):
    c = c_ref[...]
    act = (c / (1.0 + jnp.exp(-c))).astype(BF16)
    o_ref[0] = jnp.dot(act, w_ref[0].astype(BF16), preferred_element_type=F32) + b_ref[0]


def _ada_call(c, w_ada, b_ada):
    depth, d, n = w_ada.shape
    bsz = c.shape[0]
    tn = 1024
    return pl.pallas_call(
        _ada_kernel,
        grid=(depth, n // tn),
        in_specs=[pl.BlockSpec((bsz, d), lambda l, j: (0, 0)),
                  pl.BlockSpec((1, d, tn), lambda l, j: (l, 0, j)),
                  pl.BlockSpec((1, 1, tn), lambda l, j: (l, 0, j))],
        out_specs=pl.BlockSpec((1, bsz, tn), lambda l, j: (l, 0, j)),
        out_shape=jax.ShapeDtypeStruct((depth, bsz, n), F32),
        compiler_params=pltpu.CompilerParams(dimension_semantics=("arbitrary", "arbitrary"),
                                             vmem_limit_bytes=VMEM_LIMIT),
        name="adaln",
    )(c, w_ada, b_ada.reshape(depth, 1, n))


def _inproj_kernel(x_ref, mod_ref, g1_ref, w_ref, wq_ref, wkv_ref, gq_ref, gkv_ref, tab_ref,
                   oa_ref, oc_ref, od_ref, os_ref, oqb_ref, okb_ref, ovb_ref):
    h = _rms(x_ref[...], g1_ref[...]) * (1.0 + mod_ref[0, 1:2, :]) + mod_ref[0, 0:1, :]
    h = h.astype(BF16)

    def proj(lo, hi):
        return jnp.dot(h, w_ref[:, lo:hi], preferred_element_type=F32)

    oa_ref[...] = proj(_CA, _CC).astype(BF16)
    oc_ref[...] = proj(_CC, _CD).astype(BF16)
    od_ref[...] = proj(_CD, _CS).astype(BF16)
    small = proj(_CS, _CQ)
    os_ref[...] = small

    cq = _rms(proj(_CQ, _CKV), gq_ref[...]).astype(BF16)
    qf = jnp.dot(cq, wq_ref[...], preferred_element_type=F32)
    tab_q, tab_qs, tab_k, tab_ks = tab_ref[0], tab_ref[1], tab_ref[2], tab_ref[3]
    for hd in range(N_HEADS):
        qg = qf[:, hd * LANES:(hd + 1) * LANES]
        qr = qg * tab_q + pltpu.roll(qg, LANES - MLA_ROPE, axis=1) * tab_qs
        oqb_ref[:, hd * LANES:(hd + 1) * LANES] = qr.astype(BF16)

    ckv = _rms(proj(_CKV, _CEND), gkv_ref[...]).astype(BF16)
    kr = (small * tab_k + pltpu.roll(small, LANES - MLA_ROPE, axis=1) * tab_ks).astype(BF16)
    kvf = jnp.dot(jnp.concatenate([ckv, kr], axis=1), wkv_ref[...], preferred_element_type=F32)
    okb_ref[...] = kvf[:, :4 * LANES].astype(BF16)
    ovb_ref[...] = kvf[:, 4 * LANES:].astype(BF16)


def _inproj_call(x2, mod_l, g1, w_all, wq, wkv, gq, gkv, tabs, seq, tm):
    t, d = x2.shape
    tpb = seq // tm
    row = lambda n: pl.BlockSpec((tm, n), lambda i: (i, 0))
    const = lambda shape: pl.BlockSpec(shape, lambda i: (0,) * len(shape))
    widths = (768, 768, 896, 128, 512, 512, 256)
    dtypes = (BF16, BF16, BF16, F32, BF16, BF16, BF16)
    return pl.pallas_call(
        _inproj_kernel,
        grid=(t // tm,),
        in_specs=[row(d),
                  pl.BlockSpec((1, 6, d), lambda i: (i // tpb, 0, 0)),
                  const((1, d)), const(w_all.shape), const(wq.shape), const(wkv.shape),
                  const((1, MLA_Q_RANK)), const((1, MLA_KV_RANK)),
                  pl.BlockSpec((4, tm, LANES), lambda i: (0, i % tpb, 0))],
        out_specs=[row(n) for n in widths],
        out_shape=[jax.ShapeDtypeStruct((t, n), dt) for n, dt in zip(widths, dtypes)],
        compiler_params=pltpu.CompilerParams(dimension_semantics=("arbitrary",), vmem_limit_bytes=VMEM_LIMIT),
        name="inproj",
    )(x2, mod_l, g1, w_all, wq, wkv, gq, gkv, tabs)


def _fox_cumsum_kernel(s_ref, b_ref, o_ref):
    z = s_ref[0] + b_ref[...]
    lf = jnp.minimum(z, 0.0) - jnp.log(1.0 + jnp.exp(-jnp.abs(z)))
    n = lf.shape[0]
    row = lax.broadcasted_iota(I32, lf.shape, 0)
    d = 1
    while d < n:
        lf = lf + jnp.where(row >= d, pltpu.roll(lf, d, axis=0), 0.0)
        d *= 2
    o_ref[0] = lf


def _fox_cumsum_call(small3, bf):
    b, s, _ = small3.shape
    return pl.pallas_call(
        _fox_cumsum_kernel,
        grid=(b,),
        in_specs=[pl.BlockSpec((1, s, LANES), lambda i: (i, 0, 0)),
                  pl.BlockSpec((1, LANES), lambda i: (0, 0))],
        out_specs=pl.BlockSpec((1, s, LANES), lambda i: (i, 0, 0)),
        out_shape=jax.ShapeDtypeStruct((b, s, LANES), F32),
        compiler_params=pltpu.CompilerParams(dimension_semantics=("arbitrary",), vmem_limit_bytes=VMEM_LIMIT),
        name="fox_cumsum",
    )(small3, bf)


def _lane_is_low():
    return lax.broadcasted_iota(I32, (1, LANES), 1) < HEAD_DIM


def _split_pair(q):
    low = _lane_is_low()
    zero = jnp.zeros_like(q)
    return jnp.where(low, q, zero), jnp.where(low, zero, q)


def _values_with_ones(v, j):
    low = _lane_is_low()
    return jnp.where(low if j == 0 else ~low, v, jnp.ones_like(v))


def _pair_output(a0, a1, extra0=None, extra1=None):
    l0 = pltpu.roll(a0, HEAD_DIM, axis=1)
    l1 = pltpu.roll(a1, HEAD_DIM, axis=1)
    if extra0 is not None:
        l0, l1 = l0 + extra0, l1 + extra1
    return jnp.where(_lane_is_low(), a0 / l0, a1 / l1)


def _two_pass_attention(n_before, logits, values, s_ref, mx_ref, acc_ref, n_heads, shared=None):
    tq, tk = s_ref.shape[-2:]
    mx_ref[...] = jnp.full(mx_ref.shape, NEG, F32)

    def store(c, diagonal):
        ctx = shared(c) if shared is not None else None
        for j in range(n_heads):
            s = logits(c, j, diagonal, ctx)
            s_ref[j, c] = s
            part = s[:, :LANES]
            for blk in range(1, tk // LANES):
                part = jnp.maximum(part, s[:, blk * LANES:(blk + 1) * LANES])
            mx_ref[j] = jnp.maximum(mx_ref[j], part)

    def store_body(c, carry):
        store(c, False)
        return carry

    lax.fori_loop(0, n_before, store_body, 0)
    store(n_before, True)

    row_max = [jnp.max(mx_ref[j], axis=1, keepdims=True) for j in range(n_heads)]
    shift = [jnp.broadcast_to(m, (tq, tk)) for m in row_max]
    acc_ref[...] = jnp.zeros(acc_ref.shape, F32)

    def accum(c, carry):
        for j in range(n_heads):
            p = jnp.exp(s_ref[j, c] - shift[j]).astype(BF16)
            acc_ref[j] += jnp.dot(p, values(c, j), preferred_element_type=F32)
        return carry

    lax.fori_loop(0, n_before + 1, accum, 0)
    return row_max


def _causal_attn_kernel(*refs, fox, tq):
    if fox:
        q_ref, k_ref, v_ref, fc_ref, fr_ref, o_ref, s_ref, mx_ref, acc_ref = refs
    else:
        q_ref, k_ref, v_ref, o_ref, s_ref, mx_ref, acc_ref = refs
    tk = tq
    hp = pl.program_id(1)
    i = pl.program_id(2)
    if fox:
        qs = _split_pair(q_ref[0])
        lane = lax.broadcasted_iota(I32, (1, LANES), 1)
        fcol = [jnp.sum(jnp.where(lane == 2 * hp + j, fc_ref[0], 0.0), axis=1, keepdims=True) for j in range(2)]
    else:
        qs = (q_ref[0, :, :LANES], q_ref[0, :, LANES:])
    causal = (lax.broadcasted_iota(I32, (tq, tk), 1) <= lax.broadcasted_iota(I32, (tq, tk), 0))

    def logits(c, j, diagonal, _):
        start = pl.multiple_of(c * tk, tk)
        if fox:
            s = _nt_dot(qs[j], k_ref[0, pl.ds(start, tk), :])
            s = (s + fcol[j]) - fr_ref[0, pl.ds(2 * hp + j, 1), pl.ds(start, tk)]
        else:
            s = _nt_dot(qs[j], k_ref[0, pl.ds(start, tk), j * LANES:(j + 1) * LANES])
        return jnp.where(causal, s, NEG) if diagonal else s

    def values(c, j):
        return _values_with_ones(v_ref[0, pl.ds(pl.multiple_of(c * tk, tk), tk), :], j)

    _two_pass_attention(i, logits, values, s_ref, mx_ref, acc_ref, 2)
    o_ref[0] = _pair_output(acc_ref[0], acc_ref[1]).astype(o_ref.dtype)


def _causal_attn_call(arrs, fox, b, s, tq):
    scratch = [pltpu.VMEM((2, s // tq, tq, tq), F32), pltpu.VMEM((2, tq, LANES), F32),
               pltpu.VMEM((2, tq, LANES), F32)]
    if fox:
        qkv, fcol, frow = arrs
        operands = (qkv, qkv, qkv, fcol, frow)
        in_specs = [pl.BlockSpec((1, tq, LANES), lambda bi, hp, i: (bi, i, hp)),
                    pl.BlockSpec((1, s, LANES), lambda bi, hp, i: (bi, 0, 2 + hp)),
                    pl.BlockSpec((1, s, LANES), lambda bi, hp, i: (bi, 0, 4 + hp)),
                    pl.BlockSpec((1, tq, LANES), lambda bi, hp, i: (bi, i, 0)),
                    pl.BlockSpec((1, 8, s), lambda bi, hp, i: (bi, 0, 0))]
    else:
        operands = arrs
        in_specs = [pl.BlockSpec((1, tq, 2 * LANES), lambda bi, hp, i: (bi, i, hp)),
                    pl.BlockSpec((1, s, 2 * LANES), lambda bi, hp, i: (bi, 0, hp)),
                    pl.BlockSpec((1, s, LANES), lambda bi, hp, i: (bi, 0, hp))]
    return pl.pallas_call(
        functools.partial(_causal_attn_kernel, fox=fox, tq=tq),
        grid=(b, 2, s // tq),
        in_specs=in_specs,
        out_specs=pl.BlockSpec((1, tq, LANES), lambda bi, hp, i: (bi, i, hp)),
        out_shape=jax.ShapeDtypeStruct((b, s, 2 * LANES), BF16),
        scratch_shapes=scratch,
        compiler_params=pltpu.CompilerParams(dimension_semantics=("arbitrary",) * 3, vmem_limit_bytes=VMEM_LIMIT),
        name="attn_fox" if fox else "attn_mla",
    )(*operands)


def _alibi_slope(head):
    return lax.shift_left(jnp.int32(1), 7 - head).astype(F32) * (2.0 ** -8)


def _swa_kernel(sink_ref, q_ref, k_ref, v_ref, o_ref, *, tq):
    hp = pl.program_id(1)
    i = pl.program_id(2)
    band = 2 * WINDOW
    rel = lax.broadcasted_iota(I32, (WINDOW, band), 0) - lax.broadcasted_iota(I32, (WINDOW, band), 1)
    for r in range(tq // WINDOW):
        q_start = i * tq + r * WINDOW
        k_start = pl.multiple_of(jnp.maximum(q_start - WINDOW, 0), WINDOW)
        kb = k_ref[0, pl.ds(k_start, band), :]
        vb = v_ref[0, pl.ds(k_start, band), :]
        dist = rel + (q_start - k_start)
        valid = (dist >= 0) & (dist < WINDOW)
        distf = dist.astype(F32)
        qs = _split_pair(q_ref[0, r * WINDOW:(r + 1) * WINDOW, :])
        acc, sink_term = [], []
        for j in range(2):
            sink = sink_ref[2 * hp + j]
            s = jnp.where(valid, _nt_dot(qs[j], kb) - _alibi_slope(2 * hp + j) * distf, NEG)
            m = jnp.maximum(jnp.max(s, axis=1, keepdims=True), sink)
            p = jnp.exp(s - m).astype(BF16)
            acc.append(jnp.dot(p, _values_with_ones(vb, j), preferred_element_type=F32))
            sink_term.append(jnp.exp(sink - m))
        o_ref[0, r * WINDOW:(r + 1) * WINDOW, :] = _pair_output(acc[0], acc[1], *sink_term).astype(o_ref.dtype)


def _swa_call(qkv, sinks, b, s, tq):
    return pl.pallas_call(
        functools.partial(_swa_kernel, tq=tq),
        grid=(b, 2, s // tq),
        in_specs=[pl.BlockSpec(memory_space=pltpu.SMEM),
                  pl.BlockSpec((1, tq, LANES), lambda bi, hp, i: (bi, i, hp)),
                  pl.BlockSpec((1, s, LANES), lambda bi, hp, i: (bi, 0, 2 + hp)),
                  pl.BlockSpec((1, s, LANES), lambda bi, hp, i: (bi, 0, 4 + hp))],
        out_specs=pl.BlockSpec((1, tq, LANES), lambda bi, hp, i: (bi, i, hp)),
        out_shape=jax.ShapeDtypeStruct((b, s, 2 * LANES), BF16),
        compiler_params=pltpu.CompilerParams(dimension_semantics=("arbitrary",) * 3, vmem_limit_bytes=VMEM_LIMIT),
        name="attn_swa",
    )(sinks, qkv, qkv, qkv)


def _dsa_kernel(q_ref, k_ref, v_ref, qi_ref, ki_ref, wi_ref, o_ref, keys_ref, thr_ref, s_ref, mx_ref, acc_ref,
                *, seq, topk):
    blk = LANES
    i = pl.program_id(1)
    nchunk = i + 1
    lane = lax.broadcasted_iota(I32, (1, LANES), 1)
    key_row = lax.broadcasted_iota(I32, (blk, blk), 0)
    query_col = lax.broadcasted_iota(I32, (blk, blk), 1)

    qi = qi_ref[0]
    wi_t = wi_ref[0].T
    qms, wrows = [], []
    for hd in range(IDX_HEADS):
        g, r = divmod(hd, LANES // IDX_DIM)
        sel = (lane >= r * IDX_DIM) & (lane < (r + 1) * IDX_DIM)
        qg = qi[:, g * LANES:(g + 1) * LANES]
        qms.append(jnp.where(sel, qg, jnp.zeros_like(qg)))
        wrows.append(wi_t[_WI_LANE + hd:_WI_LANE + hd + 1, :])
    q_all = jnp.concatenate(qms, axis=0)
    key_minus_query = (lax.broadcasted_iota(I32, (2 * blk, blk), 0) - lax.broadcasted_iota(I32, (2 * blk, blk), 1))

    def score_chunk(c2, carry):
        start = pl.multiple_of(c2 * 2 * blk, 2 * blk)
        logit = _nt_dot(ki_ref[0, pl.ds(start, 2 * blk), :], q_all)
        sc = wrows[0] * jnp.maximum(logit[:, :blk], 0.0)
        for hd in range(1, IDX_HEADS):
            sc = sc + wrows[hd] * jnp.maximum(logit[:, hd * blk:(hd + 1) * blk], 0.0)
        sc = jnp.where(key_minus_query <= i * blk - start, sc, -jnp.inf)
        sc = jnp.where(sc == 0.0, 0.0, sc)
        bits = pltpu.bitcast(sc, I32)
        keys = bits ^ ((bits >> 31) & 0x7FFFFFFF)
        keys_ref[2 * c2] = keys[:blk]
        keys_ref[2 * c2 + 1] = keys[blk:]
        return carry

    lax.fori_loop(0, (nchunk + 1) // 2, score_chunk, 0)

    thr_ref[...] = jnp.full(thr_ref.shape, KEY_NEG_INF + 1, I32)

    def count(pred):
        def body(c2, acc):
            parts = []
            for c in (2 * c2, 2 * c2 + 1):
                hit = jnp.where(pred(keys_ref[c], c), 1.0, 0.0)
                parts += [hit[r * SUBLANES:(r + 1) * SUBLANES] for r in range(blk // SUBLANES)]
            while len(parts) > 1:
                parts = [a + b for a, b in zip(parts[::2], parts[1::2])]
            return acc + parts[0]
        acc = lax.fori_loop(0, (nchunk + 1) // 2, body, jnp.zeros((SUBLANES, blk), F32))
        return jnp.sum(acc, axis=0, keepdims=True)

    @pl.when(nchunk * blk > topk)
    def _():
        kf = float(topk)

        def bit_step(it, carry):
            thr, cnt_thr = carry
            cand = thr + lax.shift_left(jnp.int32(1), 31 - it)
            cnt = count(lambda kk, c: kk >= cand)
            ok = cnt >= kf
            return jnp.where(ok, cand, thr), jnp.where(ok, cnt, cnt_thr)

        thr0 = jnp.full((1, blk), INT_MIN, I32)
        thr, cnt_thr = lax.fori_loop(0, 32, bit_step, (thr0, jnp.full((1, blk), float(seq), F32)))
        thr_ref[...] = jnp.broadcast_to(thr, thr_ref.shape)

        @pl.when(jnp.max(cnt_thr) > kf)
        def _():
            need = kf - count(lambda kk, c: kk > thr)

            def idx_step(it, pos):
                cand = pos + lax.shift_left(jnp.int32(1), (seq.bit_length() - 2) - it)
                cnt = count(lambda kk, c: (kk == thr) & (key_row + c * blk < cand))
                return jnp.where(cnt < need, cand, pos)

            pos = lax.fori_loop(0, seq.bit_length() - 1, idx_step, jnp.zeros((1, blk), I32))

            def demote(c, carry):
                kk = keys_ref[c]
                keys_ref[c] = jnp.where((kk == thr) & (key_row + c * blk > pos), kk - 1, kk)
                return carry

            lax.fori_loop(0, nchunk, demote, 0)

    q = q_ref[0]
    qs = _split_pair(q[:, :LANES]) + _split_pair(q[:, LANES:])
    thr_row = thr_ref[0:1, :]
    rel = (lax.broadcasted_iota(I32, (blk, 2 * blk), 0) - lax.broadcasted_iota(I32, (blk, 2 * blk), 1))
    last = (nchunk + 1) // 2 - 1

    def shared(c2):
        halves = [jnp.where(keys_ref[2 * c2 + h] >= thr_row, 0.0, NEG).T for h in range(2)]
        distf = (rel + (i * blk - c2 * 2 * blk)).astype(F32)
        return jnp.concatenate(halves, axis=1), distf

    def logits(c2, hd, diagonal, ctx):
        bias, distf = ctx
        start = pl.multiple_of(c2 * 2 * blk, 2 * blk)
        s = _nt_dot(qs[hd], k_ref[0, pl.ds(start, 2 * blk), :]) - (2.0 ** -(N_HEADS + hd + 1)) * distf
        return s + bias

    def values(c2, hd):
        return _values_with_ones(v_ref[0, pl.ds(pl.multiple_of(c2 * 2 * blk, 2 * blk), 2 * blk), :], hd % 2)

    _two_pass_attention(last, logits, values, s_ref, mx_ref, acc_ref, N_HEADS, shared)
    o_ref[0] = jnp.concatenate([_pair_output(acc_ref[0], acc_ref[1]), _pair_output(acc_ref[2], acc_ref[3])],
                               axis=1).astype(o_ref.dtype)


def _dsa_call(od3, small3, b, s, topk):
    blk = LANES
    scratch = [pltpu.VMEM((s // blk, blk, blk), I32), pltpu.VMEM((SUBLANES, blk), I32),
               pltpu.VMEM((N_HEADS, s // (2 * blk), blk, 2 * blk), F32),
               pltpu.VMEM((N_HEADS, blk, LANES), F32), pltpu.VMEM((N_HEADS, blk, LANES), F32)]
    return pl.pallas_call(
        functools.partial(_dsa_kernel, seq=s, topk=topk),
        grid=(b, s // blk),
        in_specs=[pl.BlockSpec((1, blk, 2 * LANES), lambda bi, i: (bi, i, 0)),
                  pl.BlockSpec((1, s, LANES), lambda bi, i: (bi, 0, 2)),
                  pl.BlockSpec((1, s, LANES), lambda bi, i: (bi, 0, 3)),
                  pl.BlockSpec((1, blk, 2 * LANES), lambda bi, i: (bi, i, 2)),
                  pl.BlockSpec((1, s, LANES), lambda bi, i: (bi, 0, 6)),
                  pl.BlockSpec((1, blk, LANES), lambda bi, i: (bi, i, 0))],
        out_specs=pl.BlockSpec((1, blk, 2 * LANES), lambda bi, i: (bi, i, 0)),
        out_shape=jax.ShapeDtypeStruct((b, s, 2 * LANES), BF16),
        scratch_shapes=scratch,
        compiler_params=pltpu.CompilerParams(dimension_semantics=("arbitrary",) * 2, vmem_limit_bytes=VMEM_LIMIT),
        name="attn_dsa",
    )(od3, od3, od3, od3, od3, small3)


def _outproj_kernel(x_ref, oa_ref, ob_ref, oc_ref, od_ref, w_ref, mod_ref, g2_ref, wr_ref, br_ref,
                    xo_ref, h_ref, gt_ref):
    mix = None
    for n, o_ref in enumerate((oa_ref, ob_ref, oc_ref, od_ref)):
        part = jnp.dot(o_ref[...], w_ref[n * 2 * LANES:(n + 1) * 2 * LANES, :], preferred_element_type=F32)
        mix = part if mix is None else mix + part
    xn = x_ref[...] + mod_ref[0, 2:3, :] * mix
    xo_ref[...] = xn
    h = _rms(xn, g2_ref[...]) * (1.0 + mod_ref[0, 4:5, :]) + mod_ref[0, 3:4, :]
    h_ref[...] = h.astype(BF16)

    logits = lax.dot_general(wr_ref[...], h, (((1,), (1,)), ((), ())), precision=lax.Precision.HIGHEST,
                             preferred_element_type=F32)
    score = 1.0 / (1.0 + jnp.exp(-logits))
    biased = score + br_ref[...]
    srow = [score[e:e + 1, :] for e in range(N_EXPERTS)]
    brow = [biased[e:e + 1, :] for e in range(N_EXPERTS)]
    per = N_EXPERTS // N_GROUPS
    best_v = best_g = None
    for g in range(N_GROUPS):
        r = brow[g * per:(g + 1) * per]
        top2 = None
        for a in range(per):
            for c in range(a + 1, per):
                top2 = r[a] + r[c] if top2 is None else jnp.maximum(top2, r[a] + r[c])
        if g == 0:
            best_v, best_g = top2, jnp.zeros_like(top2, dtype=I32)
        else:
            up = top2 > best_v
            best_v = jnp.where(up, top2, best_v)
            best_g = jnp.where(up, g, best_g)
    cand = [jnp.where(best_g == e // per, brow[e], -jnp.inf) for e in range(N_EXPERTS)]

    def first_max(vals):
        v, idx = vals[0], jnp.zeros_like(best_g)
        for e in range(1, N_EXPERTS):
            up = vals[e] > v
            v = jnp.where(up, vals[e], v)
            idx = jnp.where(up, e, idx)
        return idx

    i1 = first_max(cand)
    i2 = first_max([jnp.where(i1 == e, -jnp.inf, cand[e]) for e in range(N_EXPERTS)])
    s1 = sum(jnp.where(i1 == e, srow[e], 0.0) for e in range(N_EXPERTS))
    s2 = sum(jnp.where(i2 == e, srow[e], 0.0) for e in range(N_EXPERTS))
    den = s1 + s2
    for e in range(N_EXPERTS):
        gt_ref[e:e + 1, :] = jnp.where(i1 == e, s1 / den, jnp.where(i2 == e, s2 / den, 0.0))


def _outproj_call(x2, outs, w_out, mod_l, g2, wr_t, br, seq, tm):
    t, d = x2.shape
    tpb = seq // tm
    row = lambda n: pl.BlockSpec((tm, n), lambda i: (i, 0))
    const = lambda shape: pl.BlockSpec(shape, lambda i: (0,) * len(shape))
    return pl.pallas_call(
        _outproj_kernel,
        grid=(t // tm,),
        in_specs=[row(d)] + [row(2 * LANES)] * 4 + [
            const(w_out.shape), pl.BlockSpec((1, 6, d), lambda i: (i // tpb, 0, 0)), const((1, d)),
            const(wr_t.shape), const(br.shape)],
        out_specs=[row(d), row(d), pl.BlockSpec((N_EXPERTS, tm), lambda i: (0, i))],
        out_shape=[jax.ShapeDtypeStruct((t, d), F32), jax.ShapeDtypeStruct((t, d), BF16),
                   jax.ShapeDtypeStruct((N_EXPERTS, t), F32)],
        compiler_params=pltpu.CompilerParams(dimension_semantics=("arbitrary",), vmem_limit_bytes=VMEM_LIMIT),
        name="outproj_router",
    )(x2, *outs, w_out, mod_l, g2, wr_t, br)


def _moe_kernel(h_ref, g_ref, x_ref, mod_ref, gf_ref, wgu_ref, wd_ref, o_ref, acc_ref, *, final):
    e = pl.program_id(1)

    @pl.when(e == 0)
    def _():
        acc_ref[...] = jnp.zeros(acc_ref.shape, F32)

    gu = jnp.dot(h_ref[...], wgu_ref[0], preferred_element_type=F32)
    gate, up = gu[:, :D_EXPERT], gu[:, D_EXPERT:]
    hid = (gate / (1.0 + jnp.exp(-gate)) * up).astype(BF16)
    y = jnp.dot(hid, wd_ref[0], preferred_element_type=F32)
    lane = lax.broadcasted_iota(I32, (1, N_EXPERTS), 1)
    ge = jnp.sum(jnp.where(lane == e, g_ref[...], 0.0), axis=1, keepdims=True)
    acc_ref[...] += ge * y

    @pl.when(e == N_EXPERTS - 1)
    def _():
        xn = x_ref[...] + mod_ref[0, 5:6, :] * acc_ref[...]
        o_ref[...] = _rms(xn, gf_ref[...]) if final else xn


def _moe_call(h2, gates, x2, mod_l, gf, wgu, wd, seq, tm, final):
    t, d = x2.shape
    tpb = seq // tm
    return pl.pallas_call(
        functools.partial(_moe_kernel, final=final),
        grid=(t // tm, N_EXPERTS),
        in_specs=[pl.BlockSpec((tm, d), lambda i, e: (i, 0)),
                  pl.BlockSpec((tm, N_EXPERTS), lambda i, e: (i, 0)),
                  pl.BlockSpec((tm, d), lambda i, e: (i, 0)),
                  pl.BlockSpec((1, 6, d), lambda i, e: (i // tpb, 0, 0)),
                  pl.BlockSpec((1, d), lambda i, e: (0, 0)),
                  pl.BlockSpec((1, d, 2 * D_EXPERT), lambda i, e: (e, 0, 0)),
                  pl.BlockSpec((1, D_EXPERT, d), lambda i, e: (e, 0, 0))],
        out_specs=pl.BlockSpec((tm, d), lambda i, e: (i, 0)),
        out_shape=jax.ShapeDtypeStruct((t, d), F32),
        scratch_shapes=[pltpu.VMEM((tm, d), F32)],
        compiler_params=pltpu.CompilerParams(dimension_semantics=("arbitrary",) * 2, vmem_limit_bytes=VMEM_LIMIT),
        name="moe",
    )(h2, gates, x2, mod_l, gf, wgu, wd)


def _swap_half(w):
    half = w.shape[1] // 2
    return jnp.concatenate([-w[:, half:], w[:, :half]], axis=1)


def _layer_weights(w_in, w_q_up, w_kv_up):
    d = w_in.shape[0]
    pts = np.cumsum(IN_SIZES)[:-1].tolist()
    (a_q, a_k, a_v, a_f, b_cq, b_ckv, b_kr, c_q, c_k, c_v,
     d_q, d_k, d_v, d_qi, d_ki, d_wi) = jnp.split(w_in, pts, axis=1)
    qs = HEAD_DIM ** -0.5
    dup = lambda w: jnp.concatenate([w[:, :HEAD_DIM], w[:, :HEAD_DIM], w[:, HEAD_DIM:], w[:, HEAD_DIM:]], axis=1)
    small = jnp.concatenate([
        a_f, d_wi * ((IDX_HEADS * IDX_DIM) ** -0.5),
        jnp.zeros((d, _KR_LANE - _WI_LANE - IDX_HEADS), F32),
        b_kr, _swap_half(b_kr), jnp.zeros((d, LANES - _KR_LANE - 2 * MLA_ROPE), F32)], axis=1)
    w_all = jnp.concatenate([
        a_q * qs, a_k, a_v,
        c_q * qs, dup(c_k), dup(c_v),
        d_q * qs, d_k, d_k, d_v, d_v, d_qi, d_ki, d_ki, d_ki, d_ki,
        small, b_cq, b_ckv], axis=1).astype(BF16)

    per_q = MLA_NOPE + MLA_ROPE
    wq = []
    for hd in range(N_HEADS):
        blk = w_q_up[:, hd * per_q:(hd + 1) * per_q]
        rot = blk[:, MLA_NOPE:]
        wq += [blk[:, :MLA_NOPE], rot, _swap_half(rot)]
    wq = jnp.concatenate(wq, axis=1).astype(BF16)

    place = np.zeros((LANES, LANES), np.float32)
    place[_KR_LANE + np.arange(MLA_ROPE), MLA_NOPE + np.arange(MLA_ROPE)] = 1.0
    place = jnp.asarray(place)
    kcols, vcols = [], []
    for hd in range(N_HEADS):
        blk = w_kv_up[:, hd * 2 * HEAD_DIM:(hd + 1) * 2 * HEAD_DIM]
        knope = jnp.concatenate([blk[:, :MLA_NOPE], jnp.zeros((MLA_KV_RANK, LANES - MLA_NOPE), F32)], axis=1)
        kcols.append(jnp.concatenate([knope, place], axis=0))
        vcols.append(jnp.concatenate([blk[:, MLA_NOPE:], jnp.zeros((LANES, HEAD_DIM), F32)], axis=0))
    wkv = jnp.concatenate(kcols + vcols, axis=1).astype(BF16)
    return w_all, wq, wkv


def _rope_tables(seq):
    half = MLA_ROPE // 2
    inv = ROPE_THETA ** (-jnp.arange(half, dtype=F32) / half)
    ang = jnp.arange(seq, dtype=F32)[:, None] * inv[None, :]
    cos = jnp.tile(jnp.cos(ang), (1, 2))
    sin = jnp.tile(jnp.sin(ang), (1, 2))
    scale = (MLA_NOPE + MLA_ROPE) ** -0.5
    z = lambda n: jnp.zeros((seq, n), F32)
    tab_q = jnp.concatenate([jnp.full((seq, MLA_NOPE), scale, F32), cos * scale, z(MLA_ROPE)], axis=1)
    tab_qs = jnp.concatenate([z(MLA_NOPE), sin * scale, z(MLA_ROPE)], axis=1)
    tab_k = jnp.concatenate([z(_KR_LANE), cos, z(LANES - _KR_LANE - MLA_ROPE)], axis=1)
    tab_ks = jnp.concatenate([z(_KR_LANE), sin, z(LANES - _KR_LANE - MLA_ROPE)], axis=1)
    return jnp.stack([tab_q, tab_qs, tab_k, tab_ks])


def kernel(x, c, w_ada, b_ada, g_norm1, w_in, b_forget, g_q_mla, w_q_up, g_kv_mla, w_kv_up, sinks, w_out,
           g_norm2, w_router, b_router, w_gate, w_up, w_down, g_final):
    b, s, d = x.shape
    depth = w_in.shape[0]
    t = b * s
    topk = min(TOPK_MAX, s // 4)
    tm = min(512, s)
    tq = min(256, s)
    tq_swa = min(512, s)
    assert s % tm == 0 and s % tq == 0 and topk % LANES == 0 and (s // LANES) % 2 == 0

    mod = _ada_call(c, w_ada, b_ada).reshape(depth, b, 6, d)
    tabs = _rope_tables(s)
    wr_t = w_router.T
    br = b_router.reshape(N_EXPERTS, 1)
    x2 = x.reshape(t, d)
    for l in range(depth):
        w_all, wq, wkv = _layer_weights(w_in[l], w_q_up[l], w_kv_up[l])
        oa, oc, od, osm, oqb, okb, ovb = _inproj_call(
            x2, mod[l], g_norm1[l].reshape(1, d), w_all, wq, wkv,
            g_q_mla[l].reshape(1, -1), g_kv_mla[l].reshape(1, -1), tabs, s, tm)
        small3 = osm.reshape(b, s, LANES)
        bf = jnp.zeros((1, LANES), F32).at[0, :N_HEADS].set(b_forget[l])
        fcum = _fox_cumsum_call(small3, bf)
        frow = jnp.swapaxes(fcum[:, :, :8], 1, 2)
        out_a = _causal_attn_call((oa.reshape(b, s, -1), fcum, frow), True, b, s, tq)
        out_b = _causal_attn_call((oqb.reshape(b, s, -1), okb.reshape(b, s, -1), ovb.reshape(b, s, -1)),
                                  False, b, s, tq)
        out_c = _swa_call(oc.reshape(b, s, -1), sinks[l], b, s, tq_swa)
        out_d = _dsa_call(od.reshape(b, s, -1), small3, b, s, topk)
        outs = [o.reshape(t, 2 * LANES) for o in (out_a, out_b, out_c, out_d)]
        x2, h2, gates_t = _outproj_call(x2, outs, w_out[l].astype(BF16), mod[l], g_norm2[l].reshape(1, d),
                                        wr_t, br, s, tm)
        wgu = jnp.concatenate([w_gate[l], w_up[l]], axis=2).astype(BF16)
        x2 = _moe_call(h2, gates_t.T, x2, mod[l], g_final.reshape(1, d), wgu, w_down[l].astype(BF16),
                       s, tm, l == depth - 1)
    return x2.reshape(b, s, d)
```

```python
import functools

import jax
import jax.numpy as jnp
import numpy as np
from jax import lax
from jax.experimental import pallas as pl
from jax.experimental.pallas import tpu as pltpu

F32 = jnp.float32
BF16 = jnp.bfloat16
I32 = jnp.int32

EPS = 1e-6
HEAD_DIM = 64
LANES = 128
SUBLANES = 8
N_HEADS = 4
MLA_Q_RANK = 256
MLA_KV_RANK = 128
MLA_NOPE = 64
MLA_ROPE = 32
ROPE_THETA = 10000.0
WINDOW = 128
IDX_HEADS = 8
IDX_DIM = 32
TOPK_MAX = 256
N_EXPERTS = 16
N_GROUPS = 4
D_EXPERT = 256
IN_SIZES = (256, 256, 256, 4, 256, 128, 32, 256, 128, 128, 256, 64, 64, 256, 32, 8)

NEG = -1e30
INT_MIN = -(2 ** 31)
KEY_NEG_INF = INT_MIN + 0x7FFFFF
VMEM_LIMIT = 56 * 1024 * 1024

_CA, _CC, _CD, _CS, _CQ, _CKV, _CEND = 0, 768, 1536, 2432, 2560, 2816, 2944
_F_LANE, _WI_LANE, _KR_LANE = 0, 4, 32


def _nt_dot(a, b):
    return lax.dot_general(a, b, (((1,), (1,)), ((), ())), preferred_element_type=F32)


def _rms(x, g):
    return x * lax.rsqrt(jnp.mean(x * x, axis=-1, keepdims=True) + EPS) * g


def _ada_kernel(c_ref, w_ref, b_ref, o_ref):
    c = c_ref[...]
    act = (c / (1.0 + jnp.exp(-c))).astype(BF16)
    o_ref[0] = jnp.dot(act, w_ref[0].astype(BF16), preferred_element_type=F32) + b_ref[0]


def _ada_call(c, w_ada, b_ada):
    depth, d, n = w_ada.shape
    bsz = c.shape[0]
    tn = 1024
    return pl.pallas_call(
        _ada_kernel,
        grid=(depth, n // tn),
        in_specs=[pl.BlockSpec((bsz, d), lambda l, j: (0, 0)),
                  pl.BlockSpec((1, d, tn), lambda l, j: (l, 0, j)),
                  pl.BlockSpec((1, 1, tn), lambda l, j: (l, 0, j))],
        out_specs=pl.BlockSpec((1, bsz, tn), lambda l, j: (l, 0, j)),
        out_shape=jax.ShapeDtypeStruct((depth, bsz, n), F32),
        compiler_params=pltpu.CompilerParams(dimension_semantics=("arbitrary", "arbitrary"),
                                             vmem_limit_bytes=VMEM_LIMIT),
        name="adaln",
    )(c, w_ada, b_ada.reshape(depth, 1, n))


def _inproj_kernel(x_ref, mod_ref, g1_ref, w_ref, wq_ref, wkv_ref, gq_ref, gkv_ref, tab_ref,
                   oa_ref, oc_ref, od_ref, os_ref, oqb_ref, okb_ref, ovb_ref):
    h = _rms(x_ref[...], g1_ref[...]) * (1.0 + mod_ref[0, 1:2, :]) + mod_ref[0, 0:1, :]
    h = h.astype(BF16)

    def proj(lo, hi):
        return jnp.dot(h, w_ref[:, lo:hi], preferred_element_type=F32)

    oa_ref[...] = proj(_CA, _CC).astype(BF16)
    oc_ref[...] = proj(_CC, _CD).astype(BF16)
    od_ref[...] = proj(_CD, _CS).astype(BF16)
    small = proj(_CS, _CQ)
    os_ref[...] = small

    cq = _rms(proj(_CQ, _CKV), gq_ref[...]).astype(BF16)
    qf = jnp.dot(cq, wq_ref[...], preferred_element_type=F32)
    tab_q, tab_qs, tab_k, tab_ks = tab_ref[0], tab_ref[1], tab_ref[2], tab_ref[3]
    for hd in range(N_HEADS):
        qg = qf[:, hd * LANES:(hd + 1) * LANES]
        qr = qg * tab_q + pltpu.roll(qg, LANES - MLA_ROPE, axis=1) * tab_qs
        oqb_ref[:, hd * LANES:(hd + 1) * LANES] = qr.astype(BF16)

    ckv = _rms(proj(_CKV, _CEND), gkv_ref[...]).astype(BF16)
    kr = (small * tab_k + pltpu.roll(small, LANES - MLA_ROPE, axis=1) * tab_ks).astype(BF16)
    kvf = jnp.dot(jnp.concatenate([ckv, kr], axis=1), wkv_ref[...], preferred_element_type=F32)
    okb_ref[...] = kvf[:, :4 * LANES].astype(BF16)
    ovb_ref[...] = kvf[:, 4 * LANES:].astype(BF16)


def _inproj_call(x2, mod_l, g1, w_all, wq, wkv, gq, gkv, tabs, seq, tm):
    t, d = x2.shape
    tpb = seq // tm
    row = lambda n: pl.BlockSpec((tm, n), lambda i: (i, 0))
    const = lambda shape: pl.BlockSpec(shape, lambda i: (0,) * len(shape))
    widths = (768, 768, 896, 128, 512, 512, 256)
    dtypes = (BF16, BF16, BF16, F32, BF16, BF16, BF16)
    return pl.pallas_call(
        _inproj_kernel,
        grid=(t // tm,),
        in_specs=[row(d),
                  pl.BlockSpec((1, 6, d), lambda i: (i // tpb, 0, 0)),
                  const((1, d)), const(w_all.shape), const(wq.shape), const(wkv.shape),
                  const((1, MLA_Q_RANK)), const((1, MLA_KV_RANK)),
                  pl.BlockSpec((4, tm, LANES), lambda i: (0, i % tpb, 0))],
        out_specs=[row(n) for n in widths],
        out_shape=[jax.ShapeDtypeStruct((t, n), dt) for n, dt in zip(widths, dtypes)],
        compiler_params=pltpu.CompilerParams(dimension_semantics=("arbitrary",), vmem_limit_bytes=VMEM_LIMIT),
        name="inproj",
    )(x2, mod_l, g1, w_all, wq, wkv, gq, gkv, tabs)


def _fox_cumsum_kernel(s_ref, b_ref, o_ref):
    z = s_ref[0] + b_ref[...]
    lf = jnp.minimum(z, 0.0) - jnp.log(1.0 + jnp.exp(-jnp.abs(z)))
    n = lf.shape[0]
    row = lax.broadcasted_iota(I32, lf.shape, 0)
    d = 1
    while d < n:
        lf = lf + jnp.where(row >= d, pltpu.roll(lf, d, axis=0), 0.0)
        d *= 2
    o_ref[0] = lf


def _fox_cumsum_call(small3, bf):
    b, s, _ = small3.shape
    return pl.pallas_call(
        _fox_cumsum_kernel,
        grid=(b,),
        in_specs=[pl.BlockSpec((1, s, LANES), lambda i: (i, 0, 0)),
                  pl.BlockSpec((1, LANES), lambda i: (0, 0))],
        out_specs=pl.BlockSpec((1, s, LANES), lambda i: (i, 0, 0)),
        out_shape=jax.ShapeDtypeStruct((b, s, LANES), F32),
        compiler_params=pltpu.CompilerParams(dimension_semantics=("arbitrary",), vmem_limit_bytes=VMEM_LIMIT),
        name="fox_cumsum",
    )(small3, bf)


def _lane_is_low():
    return lax.broadcasted_iota(I32, (1, LANES), 1) < HEAD_DIM


def _split_pair(q):
    low = _lane_is_low()
    zero = jnp.zeros_like(q)
    return jnp.where(low, q, zero), jnp.where(low, zero, q)


def _values_with_ones(v, j):
    low = _lane_is_low()
    return jnp.where(low if j == 0 else ~low, v, jnp.ones_like(v))


def _pair_output(a0, a1, extra0=None, extra1=None):
    l0 = pltpu.roll(a0, HEAD_DIM, axis=1)
    l1 = pltpu.roll(a1, HEAD_DIM, axis=1)
    if extra0 is not None:
        l0, l1 = l0 + extra0, l1 + extra1
    return jnp.where(_lane_is_low(), a0 / l0, a1 / l1)


def _two_pass_attention(n_before, logits, values, s_ref, mx_ref, acc_ref, n_heads, shared=None):
    tq, tk = s_ref.shape[-2:]
    mx_ref[...] = jnp.full(mx_ref.shape, NEG, F32)

    def store(c, n, diagonal):
        ctx = shared(c, n) if shared is not None else None
        for j in range(n_heads):
            s = logits(c, n, j, diagonal, ctx)
            for t in range(n):
                s_ref[j, c + t] = s[:, t * tk:(t + 1) * tk]
            parts = [s[:, blk * LANES:(blk + 1) * LANES] for blk in range(n * tk // LANES)]
            while len(parts) > 1:
                parts = [jnp.maximum(a, b) for a, b in zip(parts[::2], parts[1::2])]
            mx_ref[j] = jnp.maximum(mx_ref[j], parts[0])

    def store_two(c2, carry):
        store(2 * c2, 2, False)
        return carry

    lax.fori_loop(0, n_before // 2, store_two, 0)

    @pl.when(n_before % 2 == 1)
    def _():
        store(n_before - 1, 1, False)

    store(n_before, 1, True)

    row_max = [jnp.max(mx_ref[j], axis=1, keepdims=True) for j in range(n_heads)]
    shift = [jnp.broadcast_to(m, (tq, tk)) for m in row_max]
    acc_ref[...] = jnp.zeros(acc_ref.shape, F32)

    def accum(c, n):
        for j in range(n_heads):
            p = [jnp.exp(s_ref[j, c + t] - shift[j]).astype(BF16) for t in range(n)]
            p = p[0] if n == 1 else jnp.concatenate(p, axis=1)
            acc_ref[j] += jnp.dot(p, values(c, n, j), preferred_element_type=F32)

    def accum_two(c2, carry):
        accum(2 * c2, 2)
        return carry

    lax.fori_loop(0, (n_before + 1) // 2, accum_two, 0)

    @pl.when(n_before % 2 == 0)
    def _():
        accum(n_before, 1)

    return row_max


def _causal_attn_kernel(*refs, fox, tq):
    if fox:
        q_ref, k_ref, v_ref, fc_ref, fr_ref, o_ref, s_ref, mx_ref, acc_ref = refs
    else:
        q_ref, k_ref, v_ref, o_ref, s_ref, mx_ref, acc_ref = refs
    tk = tq
    hp = pl.program_id(1)
    i = pl.program_id(2)
    if fox:
        qs = _split_pair(q_ref[0])
        lane = lax.broadcasted_iota(I32, (1, LANES), 1)
        fcol = [jnp.sum(jnp.where(lane == 2 * hp + j, fc_ref[0], 0.0), axis=1, keepdims=True) for j in range(2)]
    else:
        qs = (q_ref[0, :, :LANES], q_ref[0, :, LANES:])
    causal = (lax.broadcasted_iota(I32, (tq, tk), 1) <= lax.broadcasted_iota(I32, (tq, tk), 0))

    def logits(c, n, j, diagonal, _):
        start = pl.multiple_of(c * tk, tk)
        if fox:
            s = _nt_dot(qs[j], k_ref[0, pl.ds(start, n * tk), :])
            s = (s + fcol[j]) - fr_ref[0, pl.ds(2 * hp + j, 1), pl.ds(start, n * tk)]
        else:
            s = _nt_dot(qs[j], k_ref[0, pl.ds(start, n * tk), j * LANES:(j + 1) * LANES])
        return jnp.where(causal, s, NEG) if diagonal else s

    def values(c, n, j):
        return _values_with_ones(v_ref[0, pl.ds(pl.multiple_of(c * tk, tk), n * tk), :], j)

    _two_pass_attention(i, logits, values, s_ref, mx_ref, acc_ref, 2)
    o_ref[0] = _pair_output(acc_ref[0], acc_ref[1]).astype(o_ref.dtype)


def _causal_attn_call(arrs, fox, b, s, tq):
    scratch = [pltpu.VMEM((2, s // tq, tq, tq), F32), pltpu.VMEM((2, tq, LANES), F32),
               pltpu.VMEM((2, tq, LANES), F32)]
    if fox:
        qkv, fcol, frow = arrs
        operands = (qkv, qkv, qkv, fcol, frow)
        in_specs = [pl.BlockSpec((1, tq, LANES), lambda bi, hp, i: (bi, i, hp)),
                    pl.BlockSpec((1, s, LANES), lambda bi, hp, i: (bi, 0, 2 + hp)),
                    pl.BlockSpec((1, s, LANES), lambda bi, hp, i: (bi, 0, 4 + hp)),
                    pl.BlockSpec((1, tq, LANES), lambda bi, hp, i: (bi, i, 0)),
                    pl.BlockSpec((1, 8, s), lambda bi, hp, i: (bi, 0, 0))]
    else:
        operands = arrs
        in_specs = [pl.BlockSpec((1, tq, 2 * LANES), lambda bi, hp, i: (bi, i, hp)),
                    pl.BlockSpec((1, s, 2 * LANES), lambda bi, hp, i: (bi, 0, hp)),
                    pl.BlockSpec((1, s, LANES), lambda bi, hp, i: (bi, 0, hp))]
    return pl.pallas_call(
        functools.partial(_causal_attn_kernel, fox=fox, tq=tq),
        grid=(b, 2, s // tq),
        in_specs=in_specs,
        out_specs=pl.BlockSpec((1, tq, LANES), lambda bi, hp, i: (bi, i, hp)),
        out_shape=jax.ShapeDtypeStruct((b, s, 2 * LANES), BF16),
        scratch_shapes=scratch,
        compiler_params=pltpu.CompilerParams(dimension_semantics=("arbitrary",) * 3, vmem_limit_bytes=VMEM_LIMIT),
        name="attn_fox" if fox else "attn_mla",
    )(*operands)


def _alibi_slope(head):
    return lax.shift_left(jnp.int32(1), 7 - head).astype(F32) * (2.0 ** -8)


def _swa_kernel(sink_ref, q_ref, k_ref, v_ref, o_ref, *, tq):
    hp = pl.program_id(1)
    i = pl.program_id(2)
    band = 2 * WINDOW
    rel = lax.broadcasted_iota(I32, (WINDOW, band), 0) - lax.broadcasted_iota(I32, (WINDOW, band), 1)
    for r in range(tq // WINDOW):
        q_start = i * tq + r * WINDOW
        k_start = pl.multiple_of(jnp.maximum(q_start - WINDOW, 0), WINDOW)
        kb = k_ref[0, pl.ds(k_start, band), :]
        vb = v_ref[0, pl.ds(k_start, band), :]
        dist = rel + (q_start - k_start)
        valid = (dist >= 0) & (dist < WINDOW)
        distf = dist.astype(F32)
        qs = _split_pair(q_ref[0, r * WINDOW:(r + 1) * WINDOW, :])
        acc, sink_term = [], []
        for j in range(2):
            sink = sink_ref[2 * hp + j]
            s = jnp.where(valid, _nt_dot(qs[j], kb) - _alibi_slope(2 * hp + j) * distf, NEG)
            m = jnp.maximum(jnp.max(s, axis=1, keepdims=True), sink)
            p = jnp.exp(s - m).astype(BF16)
            acc.append(jnp.dot(p, _values_with_ones(vb, j), preferred_element_type=F32))
            sink_term.append(jnp.exp(sink - m))
        o_ref[0, r * WINDOW:(r + 1) * WINDOW, :] = _pair_output(acc[0], acc[1], *sink_term).astype(o_ref.dtype)


def _swa_call(qkv, sinks, b, s, tq):
    return pl.pallas_call(
        functools.partial(_swa_kernel, tq=tq),
        grid=(b, 2, s // tq),
        in_specs=[pl.BlockSpec(memory_space=pltpu.SMEM),
                  pl.BlockSpec((1, tq, LANES), lambda bi, hp, i: (bi, i, hp)),
                  pl.BlockSpec((1, s, LANES), lambda bi, hp, i: (bi, 0, 2 + hp)),
                  pl.BlockSpec((1, s, LANES), lambda bi, hp, i: (bi, 0, 4 + hp))],
        out_specs=pl.BlockSpec((1, tq, LANES), lambda bi, hp, i: (bi, i, hp)),
        out_shape=jax.ShapeDtypeStruct((b, s, 2 * LANES), BF16),
        compiler_params=pltpu.CompilerParams(dimension_semantics=("arbitrary",) * 3, vmem_limit_bytes=VMEM_LIMIT),
        name="attn_swa",
    )(sinks, qkv, qkv, qkv)


def _dsa_kernel(q_ref, k_ref, v_ref, qi_ref, ki_ref, wi_ref, o_ref, keys_ref, thr_ref, s_ref, mx_ref, acc_ref,
                *, seq, topk):
    blk = LANES
    i = pl.program_id(1)
    nchunk = i + 1
    lane = lax.broadcasted_iota(I32, (1, LANES), 1)
    key_row = lax.broadcasted_iota(I32, (blk, blk), 0)
    query_col = lax.broadcasted_iota(I32, (blk, blk), 1)

    qi = qi_ref[0]
    wi_t = wi_ref[0].T
    qms, wrows = [], []
    for hd in range(IDX_HEADS):
        g, r = divmod(hd, LANES // IDX_DIM)
        sel = (lane >= r * IDX_DIM) & (lane < (r + 1) * IDX_DIM)
        qg = qi[:, g * LANES:(g + 1) * LANES]
        qms.append(jnp.where(sel, qg, jnp.zeros_like(qg)))
        wrows.append(wi_t[_WI_LANE + hd:_WI_LANE + hd + 1, :])
    q_all = jnp.concatenate(qms, axis=0)
    key_minus_query = (lax.broadcasted_iota(I32, (2 * blk, blk), 0) - lax.broadcasted_iota(I32, (2 * blk, blk), 1))

    def score_chunk(c2, carry):
        start = pl.multiple_of(c2 * 2 * blk, 2 * blk)
        logit = _nt_dot(ki_ref[0, pl.ds(start, 2 * blk), :], q_all)
        sc = wrows[0] * jnp.maximum(logit[:, :blk], 0.0)
        for hd in range(1, IDX_HEADS):
            sc = sc + wrows[hd] * jnp.maximum(logit[:, hd * blk:(hd + 1) * blk], 0.0)
        sc = jnp.where(key_minus_query <= i * blk - start, sc, -jnp.inf)
        sc = jnp.where(sc == 0.0, 0.0, sc)
        bits = pltpu.bitcast(sc, I32)
        keys = bits ^ ((bits >> 31) & 0x7FFFFFFF)
        keys_ref[2 * c2] = keys[:blk]
        keys_ref[2 * c2 + 1] = keys[blk:]
        return carry

    lax.fori_loop(0, (nchunk + 1) // 2, score_chunk, 0)

    thr_ref[...] = jnp.full(thr_ref.shape, KEY_NEG_INF + 1, I32)

    def count(pred):
        def body(c2, acc):
            parts = []
            for c in (2 * c2, 2 * c2 + 1):
                hit = jnp.where(pred(keys_ref[c], c), 1.0, 0.0)
                parts += [hit[r * SUBLANES:(r + 1) * SUBLANES] for r in range(blk // SUBLANES)]
            while len(parts) > 1:
                parts = [a + b for a, b in zip(parts[::2], parts[1::2])]
            return acc + parts[0]
        acc = lax.fori_loop(0, (nchunk + 1) // 2, body, jnp.zeros((SUBLANES, blk), F32))
        return jnp.sum(acc, axis=0, keepdims=True)

    @pl.when(nchunk * blk > topk)
    def _():
        kf = float(topk)

        def bit_step(it, carry):
            thr, cnt_thr = carry
            cand = thr + lax.shift_left(jnp.int32(1), 31 - it)
            cnt = count(lambda kk, c: kk >= cand)
            ok = cnt >= kf
            return jnp.where(ok, cand, thr), jnp.where(ok, cnt, cnt_thr)

        thr0 = jnp.full((1, blk), INT_MIN, I32)
        thr, cnt_thr = lax.fori_loop(0, 32, bit_step, (thr0, jnp.full((1, blk), float(seq), F32)))
        thr_ref[...] = jnp.broadcast_to(thr, thr_ref.shape)

        @pl.when(jnp.max(cnt_thr) > kf)
        def _():
            need = kf - count(lambda kk, c: kk > thr)

            def idx_step(it, pos):
                cand = pos + lax.shift_left(jnp.int32(1), (seq.bit_length() - 2) - it)
                cnt = count(lambda kk, c: (kk == thr) & (key_row + c * blk < cand))
                return jnp.where(cnt < need, cand, pos)

            pos = lax.fori_loop(0, seq.bit_length() - 1, idx_step, jnp.zeros((1, blk), I32))

            def demote(c, carry):
                kk = keys_ref[c]
                keys_ref[c] = jnp.where((kk == thr) & (key_row + c * blk > pos), kk - 1, kk)
                return carry

            lax.fori_loop(0, nchunk, demote, 0)

    q = q_ref[0]
    qs = _split_pair(q[:, :LANES]) + _split_pair(q[:, LANES:])
    thr_row = thr_ref[0:1, :]
    rel = (lax.broadcasted_iota(I32, (blk, 4 * blk), 0) - lax.broadcasted_iota(I32, (blk, 4 * blk), 1))
    last = (nchunk + 1) // 2 - 1

    def shared(c2, n):
        halves = [jnp.where(keys_ref[2 * c2 + h] >= thr_row, 0.0, NEG).T for h in range(2 * n)]
        distf = (rel[:, :n * 2 * blk] + (i * blk - c2 * 2 * blk)).astype(F32)
        return jnp.concatenate(halves, axis=1), distf

    def logits(c2, n, hd, diagonal, ctx):
        bias, distf = ctx
        start = pl.multiple_of(c2 * 2 * blk, 2 * blk)
        s = _nt_dot(qs[hd], k_ref[0, pl.ds(start, n * 2 * blk), :]) - (2.0 ** -(N_HEADS + hd + 1)) * distf
        return s + bias

    def values(c2, n, hd):
        start = pl.multiple_of(c2 * 2 * blk, 2 * blk)
        return _values_with_ones(v_ref[0, pl.ds(start, n * 2 * blk), :], hd % 2)

    _two_pass_attention(last, logits, values, s_ref, mx_ref, acc_ref, N_HEADS, shared)
    o_ref[0] = jnp.concatenate([_pair_output(acc_ref[0], acc_ref[1]), _pair_output(acc_ref[2], acc_ref[3])],
                               axis=1).astype(o_ref.dtype)


def _dsa_call(od3, small3, b, s, topk):
    blk = LANES
    scratch = [pltpu.VMEM((s // blk, blk, blk), I32), pltpu.VMEM((SUBLANES, blk), I32),
               pltpu.VMEM((N_HEADS, s // (2 * blk), blk, 2 * blk), F32),
               pltpu.VMEM((N_HEADS, blk, LANES), F32), pltpu.VMEM((N_HEADS, blk, LANES), F32)]
    return pl.pallas_call(
        functools.partial(_dsa_kernel, seq=s, topk=topk),
        grid=(b, s // blk),
        in_specs=[pl.BlockSpec((1, blk, 2 * LANES), lambda bi, i: (bi, i, 0)),
                  pl.BlockSpec((1, s, LANES), lambda bi, i: (bi, 0, 2)),
                  pl.BlockSpec((1, s, LANES), lambda bi, i: (bi, 0, 3)),
                  pl.BlockSpec((1, blk, 2 * LANES), lambda bi, i: (bi, i, 2)),
                  pl.BlockSpec((1, s, LANES), lambda bi, i: (bi, 0, 6)),
                  pl.BlockSpec((1, blk, LANES), lambda bi, i: (bi, i, 0))],
        out_specs=pl.BlockSpec((1, blk, 2 * LANES), lambda bi, i: (bi, i, 0)),
        out_shape=jax.ShapeDtypeStruct((b, s, 2 * LANES), BF16),
        scratch_shapes=scratch,
        compiler_params=pltpu.CompilerParams(dimension_semantics=("arbitrary",) * 2, vmem_limit_bytes=VMEM_LIMIT),
        name="attn_dsa",
    )(od3, od3, od3, od3, od3, small3)


def _outproj_kernel(x_ref, oa_ref, ob_ref, oc_ref, od_ref, w_ref, mod_ref, g2_ref, wr_ref, br_ref,
                    xo_ref, h_ref, gt_ref):
    mix = None
    for n, o_ref in enumerate((oa_ref, ob_ref, oc_ref, od_ref)):
        part = jnp.dot(o_ref[...], w_ref[n * 2 * LANES:(n + 1) * 2 * LANES, :], preferred_element_type=F32)
        mix = part if mix is None else mix + part
    xn = x_ref[...] + mod_ref[0, 2:3, :] * mix
    xo_ref[...] = xn
    h = _rms(xn, g2_ref[...]) * (1.0 + mod_ref[0, 4:5, :]) + mod_ref[0, 3:4, :]
    h_ref[...] = h.astype(BF16)

    logits = lax.dot_general(wr_ref[...], h, (((1,), (1,)), ((), ())), precision=lax.Precision.HIGHEST,
                             preferred_element_type=F32)
    score = 1.0 / (1.0 + jnp.exp(-logits))
    biased = score + br_ref[...]
    srow = [score[e:e + 1, :] for e in range(N_EXPERTS)]
    brow = [biased[e:e + 1, :] for e in range(N_EXPERTS)]
    per = N_EXPERTS // N_GROUPS
    best_v = best_g = None
    for g in range(N_GROUPS):
        r = brow[g * per:(g + 1) * per]
        top2 = None
        for a in range(per):
            for c in range(a + 1, per):
                top2 = r[a] + r[c] if top2 is None else jnp.maximum(top2, r[a] + r[c])
        if g == 0:
            best_v, best_g = top2, jnp.zeros_like(top2, dtype=I32)
        else:
            up = top2 > best_v
            best_v = jnp.where(up, top2, best_v)
            best_g = jnp.where(up, g, best_g)
    cand = [jnp.where(best_g == e // per, brow[e], -jnp.inf) for e in range(N_EXPERTS)]

    def first_max(vals):
        v, idx = vals[0], jnp.zeros_like(best_g)
        for e in range(1, N_EXPERTS):
            up = vals[e] > v
            v = jnp.where(up, vals[e], v)
            idx = jnp.where(up, e, idx)
        return idx

    i1 = first_max(cand)
    i2 = first_max([jnp.where(i1 == e, -jnp.inf, cand[e]) for e in range(N_EXPERTS)])
    s1 = sum(jnp.where(i1 == e, srow[e], 0.0) for e in range(N_EXPERTS))
    s2 = sum(jnp.where(i2 == e, srow[e], 0.0) for e in range(N_EXPERTS))
    den = s1 + s2
    for e in range(N_EXPERTS):
        gt_ref[e:e + 1, :] = jnp.where(i1 == e, s1 / den, jnp.where(i2 == e, s2 / den, 0.0))


def _outproj_call(x2, outs, w_out, mod_l, g2, wr_t, br, seq, tm):
    t, d = x2.shape
    tpb = seq // tm
    row = lambda n: pl.BlockSpec((tm, n), lambda i: (i, 0))
    const = lambda shape: pl.BlockSpec(shape, lambda i: (0,) * len(shape))
    return pl.pallas_call(
        _outproj_kernel,
        grid=(t // tm,),
        in_specs=[row(d)] + [row(2 * LANES)] * 4 + [
            const(w_out.shape), pl.BlockSpec((1, 6, d), lambda i: (i // tpb, 0, 0)), const((1, d)),
            const(wr_t.shape), const(br.shape)],
        out_specs=[row(d), row(d), pl.BlockSpec((N_EXPERTS, tm), lambda i: (0, i))],
        out_shape=[jax.ShapeDtypeStruct((t, d), F32), jax.ShapeDtypeStruct((t, d), BF16),
                   jax.ShapeDtypeStruct((N_EXPERTS, t), F32)],
        compiler_params=pltpu.CompilerParams(dimension_semantics=("arbitrary",), vmem_limit_bytes=VMEM_LIMIT),
        name="outproj_router",
    )(x2, *outs, w_out, mod_l, g2, wr_t, br)


def _moe_kernel(h_ref, g_ref, x_ref, mod_ref, gf_ref, wgu_ref, wd_ref, o_ref, acc_ref, *, final):
    e = pl.program_id(1)

    @pl.when(e == 0)
    def _():
        acc_ref[...] = jnp.zeros(acc_ref.shape, F32)

    gu = jnp.dot(h_ref[...], wgu_ref[0], preferred_element_type=F32)
    gate, up = gu[:, :D_EXPERT], gu[:, D_EXPERT:]
    hid = (gate / (1.0 + jnp.exp(-gate)) * up).astype(BF16)
    y = jnp.dot(hid, wd_ref[0], preferred_element_type=F32)
    lane = lax.broadcasted_iota(I32, (1, N_EXPERTS), 1)
    ge = jnp.sum(jnp.where(lane == e, g_ref[...], 0.0), axis=1, keepdims=True)
    acc_ref[...] += ge * y

    @pl.when(e == N_EXPERTS - 1)
    def _():
        xn = x_ref[...] + mod_ref[0, 5:6, :] * acc_ref[...]
        o_ref[...] = _rms(xn, gf_ref[...]) if final else xn


def _moe_call(h2, gates, x2, mod_l, gf, wgu, wd, seq, tm, final):
    t, d = x2.shape
    tpb = seq // tm
    return pl.pallas_call(
        functools.partial(_moe_kernel, final=final),
        grid=(t // tm, N_EXPERTS),
        in_specs=[pl.BlockSpec((tm, d), lambda i, e: (i, 0)),
                  pl.BlockSpec((tm, N_EXPERTS), lambda i, e: (i, 0)),
                  pl.BlockSpec((tm, d), lambda i, e: (i, 0)),
                  pl.BlockSpec((1, 6, d), lambda i, e: (i // tpb, 0, 0)),
                  pl.BlockSpec((1, d), lambda i, e: (0, 0)),
                  pl.BlockSpec((1, d, 2 * D_EXPERT), lambda i, e: (e, 0, 0)),
                  pl.BlockSpec((1, D_EXPERT, d), lambda i, e: (e, 0, 0))],
        out_specs=pl.BlockSpec((tm, d), lambda i, e: (i, 0)),
        out_shape=jax.ShapeDtypeStruct((t, d), F32),
        scratch_shapes=[pltpu.VMEM((tm, d), F32)],
        compiler_params=pltpu.CompilerParams(dimension_semantics=("arbitrary",) * 2, vmem_limit_bytes=VMEM_LIMIT),
        name="moe",
    )(h2, gates, x2, mod_l, gf, wgu, wd)


def _swap_half(w):
    half = w.shape[1] // 2
    return jnp.concatenate([-w[:, half:], w[:, :half]], axis=1)


def _layer_weights(w_in, w_q_up, w_kv_up):
    d = w_in.shape[0]
    pts = np.cumsum(IN_SIZES)[:-1].tolist()
    (a_q, a_k, a_v, a_f, b_cq, b_ckv, b_kr, c_q, c_k, c_v,
     d_q, d_k, d_v, d_qi, d_ki, d_wi) = jnp.split(w_in, pts, axis=1)
    qs = HEAD_DIM ** -0.5
    dup = lambda w: jnp.concatenate([w[:, :HEAD_DIM], w[:, :HEAD_DIM], w[:, HEAD_DIM:], w[:, HEAD_DIM:]], axis=1)
    small = jnp.concatenate([
        a_f, d_wi * ((IDX_HEADS * IDX_DIM) ** -0.5),
        jnp.zeros((d, _KR_LANE - _WI_LANE - IDX_HEADS), F32),
        b_kr, _swap_half(b_kr), jnp.zeros((d, LANES - _KR_LANE - 2 * MLA_ROPE), F32)], axis=1)
    w_all = jnp.concatenate([
        a_q * qs, a_k, a_v,
        c_q * qs, dup(c_k), dup(c_v),
        d_q * qs, d_k, d_k, d_v, d_v, d_qi, d_ki, d_ki, d_ki, d_ki,
        small, b_cq, b_ckv], axis=1).astype(BF16)

    per_q = MLA_NOPE + MLA_ROPE
    wq = []
    for hd in range(N_HEADS):
        blk = w_q_up[:, hd * per_q:(hd + 1) * per_q]
        rot = blk[:, MLA_NOPE:]
        wq += [blk[:, :MLA_NOPE], rot, _swap_half(rot)]
    wq = jnp.concatenate(wq, axis=1).astype(BF16)

    place = np.zeros((LANES, LANES), np.float32)
    place[_KR_LANE + np.arange(MLA_ROPE), MLA_NOPE + np.arange(MLA_ROPE)] = 1.0
    place = jnp.asarray(place)
    kcols, vcols = [], []
    for hd in range(N_HEADS):
        blk = w_kv_up[:, hd * 2 * HEAD_DIM:(hd + 1) * 2 * HEAD_DIM]
        knope = jnp.concatenate([blk[:, :MLA_NOPE], jnp.zeros((MLA_KV_RANK, LANES - MLA_NOPE), F32)], axis=1)
        kcols.append(jnp.concatenate([knope, place], axis=0))
        vcols.append(jnp.concatenate([blk[:, MLA_NOPE:], jnp.zeros((LANES, HEAD_DIM), F32)], axis=0))
    wkv = jnp.concatenate(kcols + vcols, axis=1).astype(BF16)
    return w_all, wq, wkv


def _rope_tables(seq):
    half = MLA_ROPE // 2
    inv = ROPE_THETA ** (-jnp.arange(half, dtype=F32) / half)
    ang = jnp.arange(seq, dtype=F32)[:, None] * inv[None, :]
    cos = jnp.tile(jnp.cos(ang), (1, 2))
    sin = jnp.tile(jnp.sin(ang), (1, 2))
    scale = (MLA_NOPE + MLA_ROPE) ** -0.5
    z = lambda n: jnp.zeros((seq, n), F32)
    tab_q = jnp.concatenate([jnp.full((seq, MLA_NOPE), scale, F32), cos * scale, z(MLA_ROPE)], axis=1)
    tab_qs = jnp.concatenate([z(MLA_NOPE), sin * scale, z(MLA_ROPE)], axis=1)
    tab_k = jnp.concatenate([z(_KR_LANE), cos, z(LANES - _KR_LANE - MLA_ROPE)], axis=1)
    tab_ks = jnp.concatenate([z(_KR_LANE), sin, z(LANES - _KR_LANE - MLA_ROPE)], axis=1)
    return jnp.stack([tab_q, tab_qs, tab_k, tab_ks])


def kernel(x, c, w_ada, b_ada, g_norm1, w_in, b_forget, g_q_mla, w_q_up, g_kv_mla, w_kv_up, sinks, w_out,
           g_norm2, w_router, b_router, w_gate, w_up, w_down, g_final):
    b, s, d = x.shape
    depth = w_in.shape[0]
    t = b * s
    topk = min(TOPK_MAX, s // 4)
    tm = min(512, s)
    tq = min(256, s)
    tq_swa = min(512, s)
    tm_moe = min(1024, s)
    assert s % tm_moe == 0 and s % tm == 0 and s % tq == 0 and topk % LANES == 0 and (s // LANES) % 2 == 0

    mod = _ada_call(c, w_ada, b_ada).reshape(depth, b, 6, d)
    tabs = _rope_tables(s)
    wr_t = w_router.T
    br = b_router.reshape(N_EXPERTS, 1)
    x2 = x.reshape(t, d)
    for l in range(depth):
        w_all, wq, wkv = _layer_weights(w_in[l], w_q_up[l], w_kv_up[l])
        oa, oc, od, osm, oqb, okb, ovb = _inproj_call(
            x2, mod[l], g_norm1[l].reshape(1, d), w_all, wq, wkv,
            g_q_mla[l].reshape(1, -1), g_kv_mla[l].reshape(1, -1), tabs, s, tm)
        small3 = osm.reshape(b, s, LANES)
        bf = jnp.zeros((1, LANES), F32).at[0, :N_HEADS].set(b_forget[l])
        fcum = _fox_cumsum_call(small3, bf)
        frow = jnp.swapaxes(fcum[:, :, :8], 1, 2)
        out_a = _causal_attn_call((oa.reshape(b, s, -1), fcum, frow), True, b, s, tq)
        out_b = _causal_attn_call((oqb.reshape(b, s, -1), okb.reshape(b, s, -1), ovb.reshape(b, s, -1)),
                                  False, b, s, tq)
        out_c = _swa_call(oc.reshape(b, s, -1), sinks[l], b, s, tq_swa)
        out_d = _dsa_call(od.reshape(b, s, -1), small3, b, s, topk)
        outs = [o.reshape(t, 2 * LANES) for o in (out_a, out_b, out_c, out_d)]
        x2, h2, gates_t = _outproj_call(x2, outs, w_out[l].astype(BF16), mod[l], g_norm2[l].reshape(1, d),
                                        wr_t, br, s, tm)
        wgu = jnp.concatenate([w_gate[l], w_up[l]], axis=2).astype(BF16)
        x2 = _moe_call(h2, gates_t.T, x2, mod[l], g_final.reshape(1, d), wgu, w_down[l].astype(BF16),
                       s, tm_moe, l == depth - 1)
    return x2.reshape(b, s, d)
```

```python
import functools

import jax
import jax.numpy as jnp
import numpy as np
from jax import lax
from jax.experimental import pallas as pl
from jax.experimental.pallas import tpu as pltpu

F32 = jnp.float32
BF16 = jnp.bfloat16
I32 = jnp.int32

EPS = 1e-6
HEAD_DIM = 64
LANES = 128
SUBLANES = 8
N_HEADS = 4
MLA_Q_RANK = 256
MLA_KV_RANK = 128
MLA_NOPE = 64
MLA_ROPE = 32
ROPE_THETA = 10000.0
WINDOW = 128
IDX_HEADS = 8
IDX_DIM = 32
TOPK_MAX = 256
N_EXPERTS = 16
N_GROUPS = 4
D_EXPERT = 256
IN_SIZES = (256, 256, 256, 4, 256, 128, 32, 256, 128, 128, 256, 64, 64, 256, 32, 8)

LOG2E = 1.4426950408889634
NEG = -1e30
INT_MIN = -(2 ** 31)
KEY_NEG_INF = INT_MIN + 0x7FFFFF
VMEM_LIMIT = 56 * 1024 * 1024

_CA, _CC, _CD, _CS, _CQ, _CKV, _CEND = 0, 768, 1536, 2432, 2560, 2816, 2944
_F_LANE, _WI_LANE, _KR_LANE = 0, 4, 32


def _nt_dot(a, b):
    return lax.dot_general(a, b, (((1,), (1,)), ((), ())), preferred_element_type=F32)


def _rms(x, g):
    return x * lax.rsqrt(jnp.mean(x * x, axis=-1, keepdims=True) + EPS) * g


def _ada_kernel(c_ref, w_ref, b_ref, o_ref):
    c = c_ref[...]
    act = (c / (1.0 + jnp.exp(-c))).astype(BF16)
    o_ref[0] = jnp.dot(act, w_ref[0].astype(BF16), preferred_element_type=F32) + b_ref[0]


def _ada_call(c, w_ada, b_ada):
    depth, d, n = w_ada.shape
    bsz = c.shape[0]
    tn = 1024
    return pl.pallas_call(
        _ada_kernel,
        grid=(depth, n // tn),
        in_specs=[pl.BlockSpec((bsz, d), lambda l, j: (0, 0)),
                  pl.BlockSpec((1, d, tn), lambda l, j: (l, 0, j)),
                  pl.BlockSpec((1, 1, tn), lambda l, j: (l, 0, j))],
        out_specs=pl.BlockSpec((1, bsz, tn), lambda l, j: (l, 0, j)),
        out_shape=jax.ShapeDtypeStruct((depth, bsz, n), F32),
        compiler_params=pltpu.CompilerParams(dimension_semantics=("arbitrary", "arbitrary"),
                                             vmem_limit_bytes=VMEM_LIMIT),
        name="adaln",
    )(c, w_ada, b_ada.reshape(depth, 1, n))


def _inproj_kernel(x_ref, mod_ref, g1_ref, w_ref, wq_ref, wkv_ref, gq_ref, gkv_ref, tab_ref,
                   oa_ref, oc_ref, od_ref, os_ref, oqb_ref, okb_ref, ovb_ref):
    h = _rms(x_ref[...], g1_ref[...]) * (1.0 + mod_ref[0, 1:2, :]) + mod_ref[0, 0:1, :]
    h = h.astype(BF16)

    def proj(lo, hi):
        return jnp.dot(h, w_ref[:, lo:hi], preferred_element_type=F32)

    q_width = N_HEADS * HEAD_DIM
    for o_ref, lo, hi in ((oa_ref, _CA, _CC), (oc_ref, _CC, _CD), (od_ref, _CD, _CS)):
        o_ref[:, :q_width] = (proj(lo, lo + q_width) * LOG2E).astype(BF16)
        o_ref[:, q_width:] = proj(lo + q_width, hi).astype(BF16)
    small = proj(_CS, _CQ)
    os_ref[...] = small

    cq = _rms(proj(_CQ, _CKV), gq_ref[...]).astype(BF16)
    qf = jnp.dot(cq, wq_ref[...], preferred_element_type=F32)
    tab_q, tab_qs, tab_k, tab_ks = tab_ref[0], tab_ref[1], tab_ref[2], tab_ref[3]
    for hd in range(N_HEADS):
        qg = qf[:, hd * LANES:(hd + 1) * LANES]
        qr = qg * tab_q + pltpu.roll(qg, LANES - MLA_ROPE, axis=1) * tab_qs
        oqb_ref[:, hd * LANES:(hd + 1) * LANES] = qr.astype(BF16)

    ckv = _rms(proj(_CKV, _CEND), gkv_ref[...]).astype(BF16)
    kr = (small * tab_k + pltpu.roll(small, LANES - MLA_ROPE, axis=1) * tab_ks).astype(BF16)
    kvf = jnp.dot(jnp.concatenate([ckv, kr], axis=1), wkv_ref[...], preferred_element_type=F32)
    okb_ref[...] = kvf[:, :4 * LANES].astype(BF16)
    ovb_ref[...] = kvf[:, 4 * LANES:].astype(BF16)


def _inproj_call(x2, mod_l, g1, w_all, wq, wkv, gq, gkv, tabs, seq, tm):
    t, d = x2.shape
    tpb = seq // tm
    row = lambda n: pl.BlockSpec((tm, n), lambda i: (i, 0))
    const = lambda shape: pl.BlockSpec(shape, lambda i: (0,) * len(shape))
    widths = (768, 768, 896, 128, 512, 512, 256)
    dtypes = (BF16, BF16, BF16, F32, BF16, BF16, BF16)
    return pl.pallas_call(
        _inproj_kernel,
        grid=(t // tm,),
        in_specs=[row(d),
                  pl.BlockSpec((1, 6, d), lambda i: (i // tpb, 0, 0)),
                  const((1, d)), const(w_all.shape), const(wq.shape), const(wkv.shape),
                  const((1, MLA_Q_RANK)), const((1, MLA_KV_RANK)),
                  pl.BlockSpec((4, tm, LANES), lambda i: (0, i % tpb, 0))],
        out_specs=[row(n) for n in widths],
        out_shape=[jax.ShapeDtypeStruct((t, n), dt) for n, dt in zip(widths, dtypes)],
        compiler_params=pltpu.CompilerParams(dimension_semantics=("arbitrary",), vmem_limit_bytes=VMEM_LIMIT),
        name="inproj",
    )(x2, mod_l, g1, w_all, wq, wkv, gq, gkv, tabs)


def _fox_cumsum_kernel(s_ref, b_ref, o_ref):
    z = s_ref[0] + b_ref[...]
    lf = jnp.minimum(z, 0.0) - jnp.log(1.0 + jnp.exp(-jnp.abs(z)))
    n = lf.shape[0]
    row = lax.broadcasted_iota(I32, lf.shape, 0)
    d = 1
    while d < n:
        lf = lf + jnp.where(row >= d, pltpu.roll(lf, d, axis=0), 0.0)
        d *= 2
    o_ref[0] = lf * LOG2E


def _fox_cumsum_call(small3, bf):
    b, s, _ = small3.shape
    return pl.pallas_call(
        _fox_cumsum_kernel,
        grid=(b,),
        in_specs=[pl.BlockSpec((1, s, LANES), lambda i: (i, 0, 0)),
                  pl.BlockSpec((1, LANES), lambda i: (0, 0))],
        out_specs=pl.BlockSpec((1, s, LANES), lambda i: (i, 0, 0)),
        out_shape=jax.ShapeDtypeStruct((b, s, LANES), F32),
        compiler_params=pltpu.CompilerParams(dimension_semantics=("arbitrary",), vmem_limit_bytes=VMEM_LIMIT),
        name="fox_cumsum",
    )(small3, bf)


def _lane_is_low():
    return lax.broadcasted_iota(I32, (1, LANES), 1) < HEAD_DIM


def _split_pair(q):
    low = _lane_is_low()
    zero = jnp.zeros_like(q)
    return jnp.where(low, q, zero), jnp.where(low, zero, q)


def _values_with_ones(v, j):
    low = _lane_is_low()
    return jnp.where(low if j == 0 else ~low, v, jnp.ones_like(v))


def _pair_output(a0, a1, extra0=None, extra1=None):
    l0 = pltpu.roll(a0, HEAD_DIM, axis=1)
    l1 = pltpu.roll(a1, HEAD_DIM, axis=1)
    if extra0 is not None:
        l0, l1 = l0 + extra0, l1 + extra1
    return jnp.where(_lane_is_low(), a0 / l0, a1 / l1)


def _two_pass_attention(n_before, logits, values, s_ref, mx_ref, acc_ref, n_heads, shared=None):
    tq, tk = s_ref.shape[-2:]
    mx_ref[...] = jnp.full(mx_ref.shape, NEG, F32)

    def store(c, n, diagonal):
        ctx = shared(c, n) if shared is not None else None
        for j in range(n_heads):
            s = logits(c, n, j, diagonal, ctx)
            for t in range(n):
                s_ref[j, c + t] = s[:, t * tk:(t + 1) * tk]
            parts = [s[:, blk * LANES:(blk + 1) * LANES] for blk in range(n * tk // LANES)]
            while len(parts) > 1:
                parts = [jnp.maximum(a, b) for a, b in zip(parts[::2], parts[1::2])]
            mx_ref[j] = jnp.maximum(mx_ref[j], parts[0])

    def store_two(c2, carry):
        store(2 * c2, 2, False)
        return carry

    lax.fori_loop(0, n_before // 2, store_two, 0)

    @pl.when(n_before % 2 == 1)
    def _():
        store(n_before - 1, 1, False)

    store(n_before, 1, True)

    row_max = [jnp.max(mx_ref[j], axis=1, keepdims=True) for j in range(n_heads)]
    shift = [jnp.broadcast_to(m, (tq, tk)) for m in row_max]
    acc_ref[...] = jnp.zeros(acc_ref.shape, F32)

    def accum(c, n):
        for j in range(n_heads):
            p = [jnp.exp2(s_ref[j, c + t] - shift[j]).astype(BF16) for t in range(n)]
            p = p[0] if n == 1 else jnp.concatenate(p, axis=1)
            acc_ref[j] += jnp.dot(p, values(c, n, j), preferred_element_type=F32)

    def accum_two(c2, carry):
        accum(2 * c2, 2)
        return carry

    lax.fori_loop(0, (n_before + 1) // 2, accum_two, 0)

    @pl.when(n_before % 2 == 0)
    def _():
        accum(n_before, 1)

    return row_max


def _causal_attn_kernel(*refs, fox, tq):
    if fox:
        q_ref, k_ref, v_ref, fc_ref, fr_ref, o_ref, s_ref, mx_ref, acc_ref = refs
    else:
        q_ref, k_ref, v_ref, o_ref, s_ref, mx_ref, acc_ref = refs
    tk = tq
    hp = pl.program_id(1)
    i = pl.program_id(2)
    if fox:
        qs = _split_pair(q_ref[0])
        lane = lax.broadcasted_iota(I32, (1, LANES), 1)
        fcol = [jnp.sum(jnp.where(lane == 2 * hp + j, fc_ref[0], 0.0), axis=1, keepdims=True) for j in range(2)]
    else:
        qs = (q_ref[0, :, :LANES], q_ref[0, :, LANES:])
    causal = (lax.broadcasted_iota(I32, (tq, tk), 1) <= lax.broadcasted_iota(I32, (tq, tk), 0))

    def logits(c, n, j, diagonal, _):
        start = pl.multiple_of(c * tk, tk)
        if fox:
            s = _nt_dot(qs[j], k_ref[0, pl.ds(start, n * tk), :])
            s = (s + fcol[j]) - fr_ref[0, pl.ds(2 * hp + j, 1), pl.ds(start, n * tk)]
        else:
            s = _nt_dot(qs[j], k_ref[0, pl.ds(start, n * tk), j * LANES:(j + 1) * LANES])
        return jnp.where(causal, s, NEG) if diagonal else s

    def values(c, n, j):
        return _values_with_ones(v_ref[0, pl.ds(pl.multiple_of(c * tk, tk), n * tk), :], j)

    _two_pass_attention(i, logits, values, s_ref, mx_ref, acc_ref, 2)
    o_ref[0] = _pair_output(acc_ref[0], acc_ref[1]).astype(o_ref.dtype)


def _causal_attn_call(arrs, fox, b, s, tq):
    scratch = [pltpu.VMEM((2, s // tq, tq, tq), F32), pltpu.VMEM((2, tq, LANES), F32),
               pltpu.VMEM((2, tq, LANES), F32)]
    if fox:
        qkv, fcol, frow = arrs
        operands = (qkv, qkv, qkv, fcol, frow)
        in_specs = [pl.BlockSpec((1, tq, LANES), lambda bi, hp, i: (bi, i, hp)),
                    pl.BlockSpec((1, s, LANES), lambda bi, hp, i: (bi, 0, 2 + hp)),
                    pl.BlockSpec((1, s, LANES), lambda bi, hp, i: (bi, 0, 4 + hp)),
                    pl.BlockSpec((1, tq, LANES), lambda bi, hp, i: (bi, i, 0)),
                    pl.BlockSpec((1, 8, s), lambda bi, hp, i: (bi, 0, 0))]
    else:
        operands = arrs
        in_specs = [pl.BlockSpec((1, tq, 2 * LANES), lambda bi, hp, i: (bi, i, hp)),
                    pl.BlockSpec((1, s, 2 * LANES), lambda bi, hp, i: (bi, 0, hp)),
                    pl.BlockSpec((1, s, LANES), lambda bi, hp, i: (bi, 0, hp))]
    return pl.pallas_call(
        functools.partial(_causal_attn_kernel, fox=fox, tq=tq),
        grid=(b, 2, s // tq),
        in_specs=in_specs,
        out_specs=pl.BlockSpec((1, tq, LANES), lambda bi, hp, i: (bi, i, hp)),
        out_shape=jax.ShapeDtypeStruct((b, s, 2 * LANES), BF16),
        scratch_shapes=scratch,
        compiler_params=pltpu.CompilerParams(dimension_semantics=("arbitrary",) * 3, vmem_limit_bytes=VMEM_LIMIT),
        name="attn_fox" if fox else "attn_mla",
    )(*operands)


def _alibi_slope(head):
    return lax.shift_left(jnp.int32(1), 7 - head).astype(F32) * (LOG2E * 2.0 ** -8)


def _swa_kernel(sink_ref, q_ref, k_ref, v_ref, o_ref, *, tq):
    hp = pl.program_id(1)
    i = pl.program_id(2)
    band = 2 * WINDOW
    rel = lax.broadcasted_iota(I32, (WINDOW, band), 0) - lax.broadcasted_iota(I32, (WINDOW, band), 1)
    for r in range(tq // WINDOW):
        q_start = i * tq + r * WINDOW
        k_start = pl.multiple_of(jnp.maximum(q_start - WINDOW, 0), WINDOW)
        kb = k_ref[0, pl.ds(k_start, band), :]
        vb = v_ref[0, pl.ds(k_start, band), :]
        dist = rel + (q_start - k_start)
        valid = (dist >= 0) & (dist < WINDOW)
        distf = dist.astype(F32)
        qs = _split_pair(q_ref[0, r * WINDOW:(r + 1) * WINDOW, :])
        acc, sink_term = [], []
        for j in range(2):
            sink = sink_ref[2 * hp + j] * LOG2E
            s = jnp.where(valid, _nt_dot(qs[j], kb) - _alibi_slope(2 * hp + j) * distf, NEG)
            m = jnp.maximum(jnp.max(s, axis=1, keepdims=True), sink)
            p = jnp.exp2(s - m).astype(BF16)
            acc.append(jnp.dot(p, _values_with_ones(vb, j), preferred_element_type=F32))
            sink_term.append(jnp.exp2(sink - m))
        o_ref[0, r * WINDOW:(r + 1) * WINDOW, :] = _pair_output(acc[0], acc[1], *sink_term).astype(o_ref.dtype)


def _swa_call(qkv, sinks, b, s, tq):
    return pl.pallas_call(
        functools.partial(_swa_kernel, tq=tq),
        grid=(b, 2, s // tq),
        in_specs=[pl.BlockSpec(memory_space=pltpu.SMEM),
                  pl.BlockSpec((1, tq, LANES), lambda bi, hp, i: (bi, i, hp)),
                  pl.BlockSpec((1, s, LANES), lambda bi, hp, i: (bi, 0, 2 + hp)),
                  pl.BlockSpec((1, s, LANES), lambda bi, hp, i: (bi, 0, 4 + hp))],
        out_specs=pl.BlockSpec((1, tq, LANES), lambda bi, hp, i: (bi, i, hp)),
        out_shape=jax.ShapeDtypeStruct((b, s, 2 * LANES), BF16),
        compiler_params=pltpu.CompilerParams(dimension_semantics=("arbitrary",) * 3, vmem_limit_bytes=VMEM_LIMIT),
        name="attn_swa",
    )(sinks, qkv, qkv, qkv)


def _dsa_kernel(q_ref, k_ref, v_ref, qi_ref, ki_ref, wi_ref, o_ref, keys_ref, thr_ref, s_ref, mx_ref, acc_ref,
                *, seq, topk):
    blk = LANES
    i = pl.program_id(1)
    nchunk = i + 1
    lane = lax.broadcasted_iota(I32, (1, LANES), 1)
    key_row = lax.broadcasted_iota(I32, (blk, blk), 0)
    query_col = lax.broadcasted_iota(I32, (blk, blk), 1)

    qi = qi_ref[0]
    wi_t = wi_ref[0].T
    qms, wrows = [], []
    for hd in range(IDX_HEADS):
        g, r = divmod(hd, LANES // IDX_DIM)
        sel = (lane >= r * IDX_DIM) & (lane < (r + 1) * IDX_DIM)
        qg = qi[:, g * LANES:(g + 1) * LANES]
        qms.append(jnp.where(sel, qg, jnp.zeros_like(qg)))
        wrows.append(wi_t[_WI_LANE + hd:_WI_LANE + hd + 1, :])
    q_all = jnp.concatenate(qms, axis=0)
    key_minus_query = (lax.broadcasted_iota(I32, (2 * blk, blk), 0) - lax.broadcasted_iota(I32, (2 * blk, blk), 1))

    def score_chunk(c2, carry):
        start = pl.multiple_of(c2 * 2 * blk, 2 * blk)
        logit = _nt_dot(ki_ref[0, pl.ds(start, 2 * blk), :], q_all)
        sc = wrows[0] * jnp.maximum(logit[:, :blk], 0.0)
        for hd in range(1, IDX_HEADS):
            sc = sc + wrows[hd] * jnp.maximum(logit[:, hd * blk:(hd + 1) * blk], 0.0)
        sc = jnp.where(key_minus_query <= i * blk - start, sc, -jnp.inf)
        sc = jnp.where(sc == 0.0, 0.0, sc)
        bits = pltpu.bitcast(sc, I32)
        keys = bits ^ ((bits >> 31) & 0x7FFFFFFF)
        keys_ref[2 * c2] = keys[:blk]
        keys_ref[2 * c2 + 1] = keys[blk:]
        return carry

    lax.fori_loop(0, (nchunk + 1) // 2, score_chunk, 0)

    thr_ref[...] = jnp.full(thr_ref.shape, KEY_NEG_INF + 1, I32)

    def count(pred):
        def body(c2, acc):
            parts = []
            for c in (2 * c2, 2 * c2 + 1):
                hit = jnp.where(pred(keys_ref[c], c), 1.0, 0.0)
                parts += [hit[r * SUBLANES:(r + 1) * SUBLANES] for r in range(blk // SUBLANES)]
            while len(parts) > 1:
                parts = [a + b for a, b in zip(parts[::2], parts[1::2])]
            return acc + parts[0]
        acc = lax.fori_loop(0, (nchunk + 1) // 2, body, jnp.zeros((SUBLANES, blk), F32))
        return jnp.sum(acc, axis=0, keepdims=True)

    @pl.when(nchunk * blk > topk)
    def _():
        kf = float(topk)

        def bit_step(it, carry):
            thr, cnt_thr = carry
            cand = thr + lax.shift_left(jnp.int32(1), 31 - it)
            cnt = count(lambda kk, c: kk >= cand)
            ok = cnt >= kf
            return jnp.where(ok, cand, thr), jnp.where(ok, cnt, cnt_thr)

        thr0 = jnp.full((1, blk), INT_MIN, I32)
        thr, cnt_thr = lax.fori_loop(0, 32, bit_step, (thr0, jnp.full((1, blk), float(seq), F32)))
        thr_ref[...] = jnp.broadcast_to(thr, thr_ref.shape)

        @pl.when(jnp.max(cnt_thr) > kf)
        def _():
            need = kf - count(lambda kk, c: kk > thr)

            def idx_step(it, pos):
                cand = pos + lax.shift_left(jnp.int32(1), (seq.bit_length() - 2) - it)
                cnt = count(lambda kk, c: (kk == thr) & (key_row + c * blk < cand))
                return jnp.where(cnt < need, cand, pos)

            pos = lax.fori_loop(0, seq.bit_length() - 1, idx_step, jnp.zeros((1, blk), I32))

            def demote(c, carry):
                kk = keys_ref[c]
                keys_ref[c] = jnp.where((kk == thr) & (key_row + c * blk > pos), kk - 1, kk)
                return carry

            lax.fori_loop(0, nchunk, demote, 0)

    q = q_ref[0]
    qs = _split_pair(q[:, :LANES]) + _split_pair(q[:, LANES:])
    thr_row = thr_ref[0:1, :]
    rel = (lax.broadcasted_iota(I32, (blk, 4 * blk), 0) - lax.broadcasted_iota(I32, (blk, 4 * blk), 1))
    last = (nchunk + 1) // 2 - 1

    def shared(c2, n):
        halves = [jnp.where(keys_ref[2 * c2 + h] >= thr_row, 0.0, NEG).T for h in range(2 * n)]
        distf = (rel[:, :n * 2 * blk] + (i * blk - c2 * 2 * blk)).astype(F32)
        return jnp.concatenate(halves, axis=1), distf

    def logits(c2, n, hd, diagonal, ctx):
        bias, distf = ctx
        start = pl.multiple_of(c2 * 2 * blk, 2 * blk)
        s = _nt_dot(qs[hd], k_ref[0, pl.ds(start, n * 2 * blk), :]) - (LOG2E * 2.0 ** -(N_HEADS + hd + 1)) * distf
        return s + bias

    def values(c2, n, hd):
        start = pl.multiple_of(c2 * 2 * blk, 2 * blk)
        return _values_with_ones(v_ref[0, pl.ds(start, n * 2 * blk), :], hd % 2)

    _two_pass_attention(last, logits, values, s_ref, mx_ref, acc_ref, N_HEADS, shared)
    o_ref[0] = jnp.concatenate([_pair_output(acc_ref[0], acc_ref[1]), _pair_output(acc_ref[2], acc_ref[3])],
                               axis=1).astype(o_ref.dtype)


def _dsa_call(od3, small3, b, s, topk):
    blk = LANES
    scratch = [pltpu.VMEM((s // blk, blk, blk), I32), pltpu.VMEM((SUBLANES, blk), I32),
               pltpu.VMEM((N_HEADS, s // (2 * blk), blk, 2 * blk), F32),
               pltpu.VMEM((N_HEADS, blk, LANES), F32), pltpu.VMEM((N_HEADS, blk, LANES), F32)]
    return pl.pallas_call(
        functools.partial(_dsa_kernel, seq=s, topk=topk),
        grid=(b, s // blk),
        in_specs=[pl.BlockSpec((1, blk, 2 * LANES), lambda bi, i: (bi, i, 0)),
                  pl.BlockSpec((1, s, LANES), lambda bi, i: (bi, 0, 2)),
                  pl.BlockSpec((1, s, LANES), lambda bi, i: (bi, 0, 3)),
                  pl.BlockSpec((1, blk, 2 * LANES), lambda bi, i: (bi, i, 2)),
                  pl.BlockSpec((1, s, LANES), lambda bi, i: (bi, 0, 6)),
                  pl.BlockSpec((1, blk, LANES), lambda bi, i: (bi, i, 0))],
        out_specs=pl.BlockSpec((1, blk, 2 * LANES), lambda bi, i: (bi, i, 0)),
        out_shape=jax.ShapeDtypeStruct((b, s, 2 * LANES), BF16),
        scratch_shapes=scratch,
        compiler_params=pltpu.CompilerParams(dimension_semantics=("arbitrary",) * 2, vmem_limit_bytes=VMEM_LIMIT),
        name="attn_dsa",
    )(od3, od3, od3, od3, od3, small3)


def _outproj_kernel(x_ref, oa_ref, ob_ref, oc_ref, od_ref, w_ref, mod_ref, g2_ref, wr_ref, br_ref,
                    xo_ref, h_ref, gt_ref):
    mix = None
    for n, o_ref in enumerate((oa_ref, ob_ref, oc_ref, od_ref)):
        part = jnp.dot(o_ref[...], w_ref[n * 2 * LANES:(n + 1) * 2 * LANES, :], preferred_element_type=F32)
        mix = part if mix is None else mix + part
    xn = x_ref[...] + mod_ref[0, 2:3, :] * mix
    xo_ref[...] = xn
    h = _rms(xn, g2_ref[...]) * (1.0 + mod_ref[0, 4:5, :]) + mod_ref[0, 3:4, :]
    h_ref[...] = h.astype(BF16)

    logits = lax.dot_general(wr_ref[...], h, (((1,), (1,)), ((), ())), precision=lax.Precision.HIGHEST,
                             preferred_element_type=F32)
    score = 1.0 / (1.0 + jnp.exp(-logits))
    biased = score + br_ref[...]
    srow = [score[e:e + 1, :] for e in range(N_EXPERTS)]
    brow = [biased[e:e + 1, :] for e in range(N_EXPERTS)]
    per = N_EXPERTS // N_GROUPS
    best_v = best_g = None
    for g in range(N_GROUPS):
        r = brow[g * per:(g + 1) * per]
        top2 = None
        for a in range(per):
            for c in range(a + 1, per):
                top2 = r[a] + r[c] if top2 is None else jnp.maximum(top2, r[a] + r[c])
        if g == 0:
            best_v, best_g = top2, jnp.zeros_like(top2, dtype=I32)
        else:
            up = top2 > best_v
            best_v = jnp.where(up, top2, best_v)
            best_g = jnp.where(up, g, best_g)
    cand = [jnp.where(best_g == e // per, brow[e], -jnp.inf) for e in range(N_EXPERTS)]

    def first_max(vals):
        v, idx = vals[0], jnp.zeros_like(best_g)
        for e in range(1, N_EXPERTS):
            up = vals[e] > v
            v = jnp.where(up, vals[e], v)
            idx = jnp.where(up, e, idx)
        return idx

    i1 = first_max(cand)
    i2 = first_max([jnp.where(i1 == e, -jnp.inf, cand[e]) for e in range(N_EXPERTS)])
    s1 = sum(jnp.where(i1 == e, srow[e], 0.0) for e in range(N_EXPERTS))
    s2 = sum(jnp.where(i2 == e, srow[e], 0.0) for e in range(N_EXPERTS))
    den = s1 + s2
    for e in range(N_EXPERTS):
        gt_ref[e:e + 1, :] = jnp.where(i1 == e, s1 / den, jnp.where(i2 == e, s2 / den, 0.0))


def _outproj_call(x2, outs, w_out, mod_l, g2, wr_t, br, seq, tm):
    t, d = x2.shape
    tpb = seq // tm
    row = lambda n: pl.BlockSpec((tm, n), lambda i: (i, 0))
    const = lambda shape: pl.BlockSpec(shape, lambda i: (0,) * len(shape))
    return pl.pallas_call(
        _outproj_kernel,
        grid=(t // tm,),
        in_specs=[row(d)] + [row(2 * LANES)] * 4 + [
            const(w_out.shape), pl.BlockSpec((1, 6, d), lambda i: (i // tpb, 0, 0)), const((1, d)),
            const(wr_t.shape), const(br.shape)],
        out_specs=[row(d), row(d), pl.BlockSpec((N_EXPERTS, tm), lambda i: (0, i))],
        out_shape=[jax.ShapeDtypeStruct((t, d), F32), jax.ShapeDtypeStruct((t, d), BF16),
                   jax.ShapeDtypeStruct((N_EXPERTS, t), F32)],
        compiler_params=pltpu.CompilerParams(dimension_semantics=("arbitrary",), vmem_limit_bytes=VMEM_LIMIT),
        name="outproj_router",
    )(x2, *outs, w_out, mod_l, g2, wr_t, br)


def _moe_kernel(h_ref, g_ref, x_ref, mod_ref, gf_ref, wgu_ref, wd_ref, o_ref, acc_ref, *, final):
    e = pl.program_id(1)

    @pl.when(e == 0)
    def _():
        acc_ref[...] = jnp.zeros(acc_ref.shape, F32)

    gu = jnp.dot(h_ref[...], wgu_ref[0], preferred_element_type=F32)
    gate, up = gu[:, :D_EXPERT], gu[:, D_EXPERT:]
    hid = (gate / (1.0 + jnp.exp(-gate)) * up).astype(BF16)
    y = jnp.dot(hid, wd_ref[0], preferred_element_type=F32)
    lane = lax.broadcasted_iota(I32, (1, N_EXPERTS), 1)
    ge = jnp.sum(jnp.where(lane == e, g_ref[...], 0.0), axis=1, keepdims=True)
    acc_ref[...] += ge * y

    @pl.when(e == N_EXPERTS - 1)
    def _():
        xn = x_ref[...] + mod_ref[0, 5:6, :] * acc_ref[...]
        o_ref[...] = _rms(xn, gf_ref[...]) if final else xn


def _moe_call(h2, gates, x2, mod_l, gf, wgu, wd, seq, tm, final):
    t, d = x2.shape
    tpb = seq // tm
    return pl.pallas_call(
        functools.partial(_moe_kernel, final=final),
        grid=(t // tm, N_EXPERTS),
        in_specs=[pl.BlockSpec((tm, d), lambda i, e: (i, 0)),
                  pl.BlockSpec((tm, N_EXPERTS), lambda i, e: (i, 0)),
                  pl.BlockSpec((tm, d), lambda i, e: (i, 0)),
                  pl.BlockSpec((1, 6, d), lambda i, e: (i // tpb, 0, 0)),
                  pl.BlockSpec((1, d), lambda i, e: (0, 0)),
                  pl.BlockSpec((1, d, 2 * D_EXPERT), lambda i, e: (e, 0, 0)),
                  pl.BlockSpec((1, D_EXPERT, d), lambda i, e: (e, 0, 0))],
        out_specs=pl.BlockSpec((tm, d), lambda i, e: (i, 0)),
        out_shape=jax.ShapeDtypeStruct((t, d), F32),
        scratch_shapes=[pltpu.VMEM((tm, d), F32)],
        compiler_params=pltpu.CompilerParams(dimension_semantics=("arbitrary",) * 2, vmem_limit_bytes=VMEM_LIMIT),
        name="moe",
    )(h2, gates, x2, mod_l, gf, wgu, wd)


def _swap_half(w):
    half = w.shape[1] // 2
    return jnp.concatenate([-w[:, half:], w[:, :half]], axis=1)


def _layer_weights(w_in, w_q_up, w_kv_up):
    d = w_in.shape[0]
    pts = np.cumsum(IN_SIZES)[:-1].tolist()
    (a_q, a_k, a_v, a_f, b_cq, b_ckv, b_kr, c_q, c_k, c_v,
     d_q, d_k, d_v, d_qi, d_ki, d_wi) = jnp.split(w_in, pts, axis=1)
    qs = HEAD_DIM ** -0.5
    dup = lambda w: jnp.concatenate([w[:, :HEAD_DIM], w[:, :HEAD_DIM], w[:, HEAD_DIM:], w[:, HEAD_DIM:]], axis=1)
    small = jnp.concatenate([
        a_f, d_wi * ((IDX_HEADS * IDX_DIM) ** -0.5),
        jnp.zeros((d, _KR_LANE - _WI_LANE - IDX_HEADS), F32),
        b_kr, _swap_half(b_kr), jnp.zeros((d, LANES - _KR_LANE - 2 * MLA_ROPE), F32)], axis=1)
    w_all = jnp.concatenate([
        a_q * qs, a_k, a_v,
        c_q * qs, dup(c_k), dup(c_v),
        d_q * qs, d_k, d_k, d_v, d_v, d_qi, d_ki, d_ki, d_ki, d_ki,
        small, b_cq, b_ckv], axis=1).astype(BF16)

    per_q = MLA_NOPE + MLA_ROPE
    wq = []
    for hd in range(N_HEADS):
        blk = w_q_up[:, hd * per_q:(hd + 1) * per_q]
        rot = blk[:, MLA_NOPE:]
        wq += [blk[:, :MLA_NOPE], rot, _swap_half(rot)]
    wq = jnp.concatenate(wq, axis=1).astype(BF16)

    place = np.zeros((LANES, LANES), np.float32)
    place[_KR_LANE + np.arange(MLA_ROPE), MLA_NOPE + np.arange(MLA_ROPE)] = 1.0
    place = jnp.asarray(place)
    kcols, vcols = [], []
    for hd in range(N_HEADS):
        blk = w_kv_up[:, hd * 2 * HEAD_DIM:(hd + 1) * 2 * HEAD_DIM]
        knope = jnp.concatenate([blk[:, :MLA_NOPE], jnp.zeros((MLA_KV_RANK, LANES - MLA_NOPE), F32)], axis=1)
        kcols.append(jnp.concatenate([knope, place], axis=0))
        vcols.append(jnp.concatenate([blk[:, MLA_NOPE:], jnp.zeros((LANES, HEAD_DIM), F32)], axis=0))
    wkv = jnp.concatenate(kcols + vcols, axis=1).astype(BF16)
    return w_all, wq, wkv


def _rope_tables(seq):
    half = MLA_ROPE // 2
    inv = ROPE_THETA ** (-jnp.arange(half, dtype=F32) / half)
    ang = jnp.arange(seq, dtype=F32)[:, None] * inv[None, :]
    cos = jnp.tile(jnp.cos(ang), (1, 2))
    sin = jnp.tile(jnp.sin(ang), (1, 2))
    scale = LOG2E * (MLA_NOPE + MLA_ROPE) ** -0.5
    z = lambda n: jnp.zeros((seq, n), F32)
    tab_q = jnp.concatenate([jnp.full((seq, MLA_NOPE), scale, F32), cos * scale, z(MLA_ROPE)], axis=1)
    tab_qs = jnp.concatenate([z(MLA_NOPE), sin * scale, z(MLA_ROPE)], axis=1)
    tab_k = jnp.concatenate([z(_KR_LANE), cos, z(LANES - _KR_LANE - MLA_ROPE)], axis=1)
    tab_ks = jnp.concatenate([z(_KR_LANE), sin, z(LANES - _KR_LANE - MLA_ROPE)], axis=1)
    return jnp.stack([tab_q, tab_qs, tab_k, tab_ks])


def kernel(x, c, w_ada, b_ada, g_norm1, w_in, b_forget, g_q_mla, w_q_up, g_kv_mla, w_kv_up, sinks, w_out,
           g_norm2, w_router, b_router, w_gate, w_up, w_down, g_final):
    b, s, d = x.shape
    depth = w_in.shape[0]
    t = b * s
    topk = min(TOPK_MAX, s // 4)
    tm = min(1024, s)
    tq = min(512, s)
    tq_swa = min(512, s)
    tm_moe = min(1024, s)
    assert s % tm_moe == 0 and s % tm == 0 and s % tq == 0 and topk % LANES == 0 and (s // LANES) % 2 == 0

    mod = _ada_call(c, w_ada, b_ada).reshape(depth, b, 6, d)
    tabs = _rope_tables(s)
    wr_t = w_router.T
    br = b_router.reshape(N_EXPERTS, 1)
    x2 = x.reshape(t, d)
    for l in range(depth):
        w_all, wq, wkv = _layer_weights(w_in[l], w_q_up[l], w_kv_up[l])
        oa, oc, od, osm, oqb, okb, ovb = _inproj_call(
            x2, mod[l], g_norm1[l].reshape(1, d), w_all, wq, wkv,
            g_q_mla[l].reshape(1, -1), g_kv_mla[l].reshape(1, -1), tabs, s, tm)
        small3 = osm.reshape(b, s, LANES)
        bf = jnp.zeros((1, LANES), F32).at[0, :N_HEADS].set(b_forget[l])
        fcum = _fox_cumsum_call(small3, bf)
        frow = jnp.swapaxes(fcum[:, :, :8], 1, 2)
        out_a = _causal_attn_call((oa.reshape(b, s, -1), fcum, frow), True, b, s, tq)
        out_b = _causal_attn_call((oqb.reshape(b, s, -1), okb.reshape(b, s, -1), ovb.reshape(b, s, -1)),
                                  False, b, s, tq)
        out_c = _swa_call(oc.reshape(b, s, -1), sinks[l], b, s, tq_swa)
        out_d = _dsa_call(od.reshape(b, s, -1), small3, b, s, topk)
        outs = [o.reshape(t, 2 * LANES) for o in (out_a, out_b, out_c, out_d)]
        x2, h2, gates_t = _outproj_call(x2, outs, w_out[l].astype(BF16), mod[l], g_norm2[l].reshape(1, d),
                                        wr_t, br, s, tm)
        wgu = jnp.concatenate([w_gate[l], w_up[l]], axis=2).astype(BF16)
        x2 = _moe_call(h2, gates_t.T, x2, mod[l], g_final.reshape(1, d), wgu, w_down[l].astype(BF16),
                       s, tm_moe, l == depth - 1)
    return x2.reshape(b, s, d)
```

```python
import functools

import jax
import jax.numpy as jnp
import numpy as np
from jax import lax
from jax.experimental import pallas as pl
from jax.experimental.pallas import tpu as pltpu

F32 = jnp.float32
BF16 = jnp.bfloat16
I32 = jnp.int32

EPS = 1e-6
HEAD_DIM = 64
LANES = 128
SUBLANES = 8
N_HEADS = 4
MLA_Q_RANK = 256
MLA_KV_RANK = 128
MLA_NOPE = 64
MLA_ROPE = 32
ROPE_THETA = 10000.0
WINDOW = 128
IDX_HEADS = 8
IDX_DIM = 32
TOPK_MAX = 256
N_EXPERTS = 16
N_GROUPS = 4
D_EXPERT = 256
IN_SIZES = (256, 256, 256, 4, 256, 128, 32, 256, 128, 128, 256, 64, 64, 256, 32, 8)

LOG2E = 1.4426950408889634
NEG = -1e30
INT_MIN = -(2 ** 31)
KEY_NEG_INF = INT_MIN + 0x7FFFFF
VMEM_LIMIT = 56 * 1024 * 1024

_CA, _CC, _CD, _CS, _CQ, _CKV, _CEND = 0, 768, 1536, 2432, 2560, 2816, 2944
_F_LANE, _WI_LANE, _KR_LANE = 0, 4, 32


def _nt_dot(a, b):
    return lax.dot_general(a, b, (((1,), (1,)), ((), ())), preferred_element_type=F32)


def _rms(x, g):
    return x * lax.rsqrt(jnp.mean(x * x, axis=-1, keepdims=True) + EPS) * g


def _ada_kernel(c_ref, w_ref, b_ref, o_ref):
    c = c_ref[...]
    act = (c / (1.0 + jnp.exp(-c))).astype(BF16)
    o_ref[0] = jnp.dot(act, w_ref[0].astype(BF16), preferred_element_type=F32) + b_ref[0]


def _ada_call(c, w_ada, b_ada):
    depth, d, n = w_ada.shape
    bsz = c.shape[0]
    tn = 1024
    return pl.pallas_call(
        _ada_kernel,
        grid=(depth, n // tn),
        in_specs=[pl.BlockSpec((bsz, d), lambda l, j: (0, 0)),
                  pl.BlockSpec((1, d, tn), lambda l, j: (l, 0, j)),
                  pl.BlockSpec((1, 1, tn), lambda l, j: (l, 0, j))],
        out_specs=pl.BlockSpec((1, bsz, tn), lambda l, j: (l, 0, j)),
        out_shape=jax.ShapeDtypeStruct((depth, bsz, n), F32),
        compiler_params=pltpu.CompilerParams(dimension_semantics=("arbitrary", "arbitrary"),
                                             vmem_limit_bytes=VMEM_LIMIT),
        name="adaln",
    )(c, w_ada, b_ada.reshape(depth, 1, n))


def _inproj_kernel(x_ref, mod_ref, g1_ref, w_ref, wq_ref, wkv_ref, gq_ref, gkv_ref, tab_ref,
                   oa_ref, oc_ref, od_ref, os_ref, oqb_ref, okb_ref, ovb_ref):
    h = _rms(x_ref[...], g1_ref[...]) * (1.0 + mod_ref[0, 1:2, :]) + mod_ref[0, 0:1, :]
    h = h.astype(BF16)

    def proj(lo, hi):
        return jnp.dot(h, w_ref[:, lo:hi], preferred_element_type=F32)

    q_width = N_HEADS * HEAD_DIM
    for o_ref, lo, hi in ((oa_ref, _CA, _CC), (oc_ref, _CC, _CD), (od_ref, _CD, _CS)):
        o_ref[:, :q_width] = (proj(lo, lo + q_width) * LOG2E).astype(BF16)
        o_ref[:, q_width:] = proj(lo + q_width, hi).astype(BF16)
    small = proj(_CS, _CQ)
    os_ref[...] = small

    cq = _rms(proj(_CQ, _CKV), gq_ref[...]).astype(BF16)
    qf = jnp.dot(cq, wq_ref[...], preferred_element_type=F32)
    tab_q, tab_qs, tab_k, tab_ks = tab_ref[0], tab_ref[1], tab_ref[2], tab_ref[3]
    for hd in range(N_HEADS):
        qg = qf[:, hd * LANES:(hd + 1) * LANES]
        qr = qg * tab_q + pltpu.roll(qg, LANES - MLA_ROPE, axis=1) * tab_qs
        oqb_ref[:, hd * LANES:(hd + 1) * LANES] = qr.astype(BF16)

    ckv = _rms(proj(_CKV, _CEND), gkv_ref[...]).astype(BF16)
    kr = (small * tab_k + pltpu.roll(small, LANES - MLA_ROPE, axis=1) * tab_ks).astype(BF16)
    kvf = jnp.dot(jnp.concatenate([ckv, kr], axis=1), wkv_ref[...], preferred_element_type=F32)
    okb_ref[...] = kvf[:, :4 * LANES].astype(BF16)
    ovb_ref[...] = kvf[:, 4 * LANES:].astype(BF16)


def _inproj_call(x2, mod_l, g1, w_all, wq, wkv, gq, gkv, tabs, seq, tm):
    t, d = x2.shape
    tpb = seq // tm
    row = lambda n: pl.BlockSpec((tm, n), lambda i: (i, 0))
    const = lambda shape: pl.BlockSpec(shape, lambda i: (0,) * len(shape))
    widths = (768, 768, 896, 128, 512, 512, 256)
    dtypes = (BF16, BF16, BF16, F32, BF16, BF16, BF16)
    return pl.pallas_call(
        _inproj_kernel,
        grid=(t // tm,),
        in_specs=[row(d),
                  pl.BlockSpec((1, 6, d), lambda i: (i // tpb, 0, 0)),
                  const((1, d)), const(w_all.shape), const(wq.shape), const(wkv.shape),
                  const((1, MLA_Q_RANK)), const((1, MLA_KV_RANK)),
                  pl.BlockSpec((4, tm, LANES), lambda i: (0, i % tpb, 0))],
        out_specs=[row(n) for n in widths],
        out_shape=[jax.ShapeDtypeStruct((t, n), dt) for n, dt in zip(widths, dtypes)],
        compiler_params=pltpu.CompilerParams(dimension_semantics=("arbitrary",), vmem_limit_bytes=VMEM_LIMIT),
        name="inproj",
    )(x2, mod_l, g1, w_all, wq, wkv, gq, gkv, tabs)


def _fox_cumsum_kernel(s_ref, b_ref, o_ref):
    z = s_ref[0] + b_ref[...]
    lf = jnp.minimum(z, 0.0) - jnp.log(1.0 + jnp.exp(-jnp.abs(z)))
    n = lf.shape[0]
    row = lax.broadcasted_iota(I32, lf.shape, 0)
    d = 1
    while d < n:
        lf = lf + jnp.where(row >= d, pltpu.roll(lf, d, axis=0), 0.0)
        d *= 2
    o_ref[0] = lf * LOG2E


def _fox_cumsum_call(small3, bf):
    b, s, _ = small3.shape
    return pl.pallas_call(
        _fox_cumsum_kernel,
        grid=(b,),
        in_specs=[pl.BlockSpec((1, s, LANES), lambda i: (i, 0, 0)),
                  pl.BlockSpec((1, LANES), lambda i: (0, 0))],
        out_specs=pl.BlockSpec((1, s, LANES), lambda i: (i, 0, 0)),
        out_shape=jax.ShapeDtypeStruct((b, s, LANES), F32),
        compiler_params=pltpu.CompilerParams(dimension_semantics=("arbitrary",), vmem_limit_bytes=VMEM_LIMIT),
        name="fox_cumsum",
    )(small3, bf)


def _lane_is_low():
    return lax.broadcasted_iota(I32, (1, LANES), 1) < HEAD_DIM


def _split_pair(q):
    low = _lane_is_low()
    zero = jnp.zeros_like(q)
    return jnp.where(low, q, zero), jnp.where(low, zero, q)


def _values_with_ones(v, j):
    low = _lane_is_low()
    return jnp.where(low if j == 0 else ~low, v, jnp.ones_like(v))


def _pair_output(a0, a1, extra0=None, extra1=None):
    l0 = pltpu.roll(a0, HEAD_DIM, axis=1)
    l1 = pltpu.roll(a1, HEAD_DIM, axis=1)
    if extra0 is not None:
        l0, l1 = l0 + extra0, l1 + extra1
    return jnp.where(_lane_is_low(), a0 / l0, a1 / l1)


def _two_pass_attention(n_before, logits, values, s_ref, mx_ref, acc_ref, n_heads, shared=None):
    tq, tk = s_ref.shape[-2:]
    mx_ref[...] = jnp.full(mx_ref.shape, NEG, F32)

    def store(c, n, diagonal):
        ctx = shared(c, n) if shared is not None else None
        for j in range(n_heads):
            s = logits(c, n, j, diagonal, ctx)
            for t in range(n):
                s_ref[j, c + t] = s[:, t * tk:(t + 1) * tk]
            parts = [s[:, blk * LANES:(blk + 1) * LANES] for blk in range(n * tk // LANES)]
            while len(parts) > 1:
                parts = [jnp.maximum(a, b) for a, b in zip(parts[::2], parts[1::2])]
            mx_ref[j] = jnp.maximum(mx_ref[j], parts[0])

    def store_two(c2, carry):
        store(2 * c2, 2, False)
        return carry

    lax.fori_loop(0, n_before // 2, store_two, 0)

    @pl.when(n_before % 2 == 1)
    def _():
        store(n_before - 1, 1, False)

    store(n_before, 1, True)

    row_max = [jnp.max(mx_ref[j], axis=1, keepdims=True) for j in range(n_heads)]
    shift = [jnp.broadcast_to(m, (tq, tk)) for m in row_max]
    acc_ref[...] = jnp.zeros(acc_ref.shape, F32)

    def accum(c, n):
        for j in range(n_heads):
            p = [jnp.exp2(s_ref[j, c + t] - shift[j]).astype(BF16) for t in range(n)]
            p = p[0] if n == 1 else jnp.concatenate(p, axis=1)
            acc_ref[j] += jnp.dot(p, values(c, n, j), preferred_element_type=F32)

    def accum_two(c2, carry):
        accum(2 * c2, 2)
        return carry

    lax.fori_loop(0, (n_before + 1) // 2, accum_two, 0)

    @pl.when(n_before % 2 == 0)
    def _():
        accum(n_before, 1)

    return row_max


def _causal_attn_kernel(*refs, fox, tq):
    if fox:
        q_ref, k_ref, v_ref, fc_ref, fr_ref, o_ref, s_ref, mx_ref, acc_ref = refs
    else:
        q_ref, k_ref, v_ref, o_ref, s_ref, mx_ref, acc_ref = refs
    tk = tq
    hp = pl.program_id(1)
    i = pl.program_id(2)
    if fox:
        qs = _split_pair(q_ref[0])
        lane = lax.broadcasted_iota(I32, (1, LANES), 1)
        fcol = [jnp.sum(jnp.where(lane == 2 * hp + j, fc_ref[0], 0.0), axis=1, keepdims=True) for j in range(2)]
    else:
        qs = (q_ref[0, :, :LANES], q_ref[0, :, LANES:])
    causal = (lax.broadcasted_iota(I32, (tq, tk), 1) <= lax.broadcasted_iota(I32, (tq, tk), 0))

    def logits(c, n, j, diagonal, _):
        start = pl.multiple_of(c * tk, tk)
        if fox:
            s = _nt_dot(qs[j], k_ref[0, pl.ds(start, n * tk), :])
            s = (s + fcol[j]) - fr_ref[0, pl.ds(2 * hp + j, 1), pl.ds(start, n * tk)]
        else:
            s = _nt_dot(qs[j], k_ref[0, pl.ds(start, n * tk), j * LANES:(j + 1) * LANES])
        return jnp.where(causal, s, NEG) if diagonal else s

    def values(c, n, j):
        return _values_with_ones(v_ref[0, pl.ds(pl.multiple_of(c * tk, tk), n * tk), :], j)

    _two_pass_attention(i, logits, values, s_ref, mx_ref, acc_ref, 2)
    o_ref[0] = _pair_output(acc_ref[0], acc_ref[1]).astype(o_ref.dtype)


def _causal_attn_call(arrs, fox, b, s, tq):
    scratch = [pltpu.VMEM((2, s // tq, tq, tq), F32), pltpu.VMEM((2, tq, LANES), F32),
               pltpu.VMEM((2, tq, LANES), F32)]
    if fox:
        qkv, fcol, frow = arrs
        operands = (qkv, qkv, qkv, fcol, frow)
        in_specs = [pl.BlockSpec((1, tq, LANES), lambda bi, hp, i: (bi, i, hp)),
                    pl.BlockSpec((1, s, LANES), lambda bi, hp, i: (bi, 0, 2 + hp)),
                    pl.BlockSpec((1, s, LANES), lambda bi, hp, i: (bi, 0, 4 + hp)),
                    pl.BlockSpec((1, tq, LANES), lambda bi, hp, i: (bi, i, 0)),
                    pl.BlockSpec((1, 8, s), lambda bi, hp, i: (bi, 0, 0))]
    else:
        operands = arrs
        in_specs = [pl.BlockSpec((1, tq, 2 * LANES), lambda bi, hp, i: (bi, i, hp)),
                    pl.BlockSpec((1, s, 2 * LANES), lambda bi, hp, i: (bi, 0, hp)),
                    pl.BlockSpec((1, s, LANES), lambda bi, hp, i: (bi, 0, hp))]
    return pl.pallas_call(
        functools.partial(_causal_attn_kernel, fox=fox, tq=tq),
        grid=(b, 2, s // tq),
        in_specs=in_specs,
        out_specs=pl.BlockSpec((1, tq, LANES), lambda bi, hp, i: (bi, i, hp)),
        out_shape=jax.ShapeDtypeStruct((b, s, 2 * LANES), BF16),
        scratch_shapes=scratch,
        compiler_params=pltpu.CompilerParams(dimension_semantics=("arbitrary",) * 3, vmem_limit_bytes=VMEM_LIMIT),
        name="attn_fox" if fox else "attn_mla",
    )(*operands)


def _alibi_slope(head):
    return lax.shift_left(jnp.int32(1), 7 - head).astype(F32) * (LOG2E * 2.0 ** -8)


def _swa_kernel(sink_ref, q_ref, k_ref, v_ref, o_ref, *, tq):
    hp = pl.program_id(1)
    i = pl.program_id(2)
    band = 2 * WINDOW
    rel = lax.broadcasted_iota(I32, (WINDOW, band), 0) - lax.broadcasted_iota(I32, (WINDOW, band), 1)
    for r in range(tq // WINDOW):
        q_start = i * tq + r * WINDOW
        k_start = pl.multiple_of(jnp.maximum(q_start - WINDOW, 0), WINDOW)
        kb = k_ref[0, pl.ds(k_start, band), :]
        vb = v_ref[0, pl.ds(k_start, band), :]
        dist = rel + (q_start - k_start)
        valid = (dist >= 0) & (dist < WINDOW)
        distf = dist.astype(F32)
        qs = _split_pair(q_ref[0, r * WINDOW:(r + 1) * WINDOW, :])
        acc, sink_term = [], []
        for j in range(2):
            sink = sink_ref[2 * hp + j] * LOG2E
            s = jnp.where(valid, _nt_dot(qs[j], kb) - _alibi_slope(2 * hp + j) * distf, NEG)
            m = jnp.maximum(jnp.max(s, axis=1, keepdims=True), sink)
            p = jnp.exp2(s - m).astype(BF16)
            acc.append(jnp.dot(p, _values_with_ones(vb, j), preferred_element_type=F32))
            sink_term.append(jnp.exp2(sink - m))
        o_ref[0, r * WINDOW:(r + 1) * WINDOW, :] = _pair_output(acc[0], acc[1], *sink_term).astype(o_ref.dtype)


def _swa_call(qkv, sinks, b, s, tq):
    return pl.pallas_call(
        functools.partial(_swa_kernel, tq=tq),
        grid=(b, 2, s // tq),
        in_specs=[pl.BlockSpec(memory_space=pltpu.SMEM),
                  pl.BlockSpec((1, tq, LANES), lambda bi, hp, i: (bi, i, hp)),
                  pl.BlockSpec((1, s, LANES), lambda bi, hp, i: (bi, 0, 2 + hp)),
                  pl.BlockSpec((1, s, LANES), lambda bi, hp, i: (bi, 0, 4 + hp))],
        out_specs=pl.BlockSpec((1, tq, LANES), lambda bi, hp, i: (bi, i, hp)),
        out_shape=jax.ShapeDtypeStruct((b, s, 2 * LANES), BF16),
        compiler_params=pltpu.CompilerParams(dimension_semantics=("arbitrary",) * 3, vmem_limit_bytes=VMEM_LIMIT),
        name="attn_swa",
    )(sinks, qkv, qkv, qkv)


def _dsa_kernel(q_ref, k_ref, v_ref, qi_ref, ki_ref, wi_ref, o_ref, keys_ref, thr_ref, s_ref, mx_ref, acc_ref,
                *, seq, topk, qb):
    kc = LANES
    kw = 2 * kc
    i = pl.program_id(1)
    nwide = (i + 1) * (qb // kw)
    lane = lax.broadcasted_iota(I32, (1, LANES), 1)
    key_row = lax.broadcasted_iota(I32, (kc, qb), 0)
    query_pos = i * qb + lax.broadcasted_iota(I32, (1, qb), 1)

    qi = qi_ref[0]
    wi_t = wi_ref[0].T
    qms, wrows = [], []
    for hd in range(IDX_HEADS):
        g, r = divmod(hd, LANES // IDX_DIM)
        sel = (lane >= r * IDX_DIM) & (lane < (r + 1) * IDX_DIM)
        qg = qi[:, g * LANES:(g + 1) * LANES]
        qms.append(jnp.where(sel, qg, jnp.zeros_like(qg)))
        wrows.append(wi_t[_WI_LANE + hd:_WI_LANE + hd + 1, :])
    q_all = jnp.concatenate(qms, axis=0)
    key_minus_query = (lax.broadcasted_iota(I32, (kw, qb), 0) - lax.broadcasted_iota(I32, (kw, qb), 1))

    def score_chunk(c, carry):
        start = pl.multiple_of(c * kw, kw)
        logit = _nt_dot(ki_ref[0, pl.ds(start, kw), :], q_all)
        sc = wrows[0] * jnp.maximum(logit[:, :qb], 0.0)
        for hd in range(1, IDX_HEADS):
            sc = sc + wrows[hd] * jnp.maximum(logit[:, hd * qb:(hd + 1) * qb], 0.0)
        sc = jnp.where(key_minus_query <= i * qb - start, sc, -jnp.inf)
        sc = jnp.where(sc == 0.0, 0.0, sc)
        bits = pltpu.bitcast(sc, I32)
        keys = bits ^ ((bits >> 31) & 0x7FFFFFFF)
        keys_ref[2 * c] = keys[:kc]
        keys_ref[2 * c + 1] = keys[kc:]
        return carry

    lax.fori_loop(0, nwide, score_chunk, 0)

    keep_all = jnp.full((1, qb), KEY_NEG_INF + 1, I32)
    thr_ref[...] = jnp.broadcast_to(keep_all, thr_ref.shape)

    def count(pred):
        def body(c2, acc):
            parts = []
            for c in (2 * c2, 2 * c2 + 1):
                hit = jnp.where(pred(keys_ref[c], c), 1.0, 0.0)
                parts += [hit[r * SUBLANES:(r + 1) * SUBLANES] for r in range(kc // SUBLANES)]
            while len(parts) > 1:
                parts = [a + b for a, b in zip(parts[::2], parts[1::2])]
            return acc + parts[0]
        acc = lax.fori_loop(0, nwide, body, jnp.zeros((SUBLANES, qb), F32))
        return jnp.sum(acc, axis=0, keepdims=True)

    @pl.when((i + 1) * qb > topk)
    def _():
        kf = float(topk)

        def bit_step(it, carry):
            thr, cnt_thr = carry
            cand = thr + lax.shift_left(jnp.int32(1), 31 - it)
            cnt = count(lambda kk, c: kk >= cand)
            ok = cnt >= kf
            return jnp.where(ok, cand, thr), jnp.where(ok, cnt, cnt_thr)

        thr0 = jnp.full((1, qb), INT_MIN, I32)
        thr, cnt_thr = lax.fori_loop(0, 32, bit_step, (thr0, jnp.full((1, qb), float(seq), F32)))
        thr_ref[...] = jnp.broadcast_to(jnp.where(query_pos < topk, keep_all, thr), thr_ref.shape)

        @pl.when(jnp.max(cnt_thr) > kf)
        def _():
            need = kf - count(lambda kk, c: kk > thr)

            def idx_step(it, pos):
                cand = pos + lax.shift_left(jnp.int32(1), (seq.bit_length() - 2) - it)
                cnt = count(lambda kk, c: (kk == thr) & (key_row + c * kc < cand))
                return jnp.where(cnt < need, cand, pos)

            pos = lax.fori_loop(0, seq.bit_length() - 1, idx_step, jnp.zeros((1, qb), I32))

            def demote(c, carry):
                kk = keys_ref[c]
                keys_ref[c] = jnp.where((kk == thr) & (key_row + c * kc > pos), kk - 1, kk)
                return carry

            lax.fori_loop(0, 2 * nwide, demote, 0)

    q = q_ref[0]
    qs = _split_pair(q[:, :LANES]) + _split_pair(q[:, LANES:])
    thr_row = thr_ref[0:1, :]
    rel = (lax.broadcasted_iota(I32, (qb, 2 * kw), 0) - lax.broadcasted_iota(I32, (qb, 2 * kw), 1))

    def shared(c, n):
        halves = [jnp.where(keys_ref[2 * c + h] >= thr_row, 0.0, NEG).T for h in range(2 * n)]
        distf = (rel[:, :n * kw] + (i * qb - c * kw)).astype(F32)
        return jnp.concatenate(halves, axis=1), distf

    def logits(c, n, hd, diagonal, ctx):
        bias, distf = ctx
        start = pl.multiple_of(c * kw, kw)
        s = _nt_dot(qs[hd], k_ref[0, pl.ds(start, n * kw), :]) - (LOG2E * 2.0 ** -(N_HEADS + hd + 1)) * distf
        return s + bias

    def values(c, n, hd):
        return _values_with_ones(v_ref[0, pl.ds(pl.multiple_of(c * kw, kw), n * kw), :], hd % 2)

    _two_pass_attention(nwide - 1, logits, values, s_ref, mx_ref, acc_ref, N_HEADS, shared)
    o_ref[0] = jnp.concatenate([_pair_output(acc_ref[0], acc_ref[1]), _pair_output(acc_ref[2], acc_ref[3])],
                               axis=1).astype(o_ref.dtype)


def _dsa_call(od3, small3, b, s, topk, qb):
    kc = LANES
    scratch = [pltpu.VMEM((s // kc, kc, qb), I32), pltpu.VMEM((SUBLANES, qb), I32),
               pltpu.VMEM((N_HEADS, s // (2 * kc), qb, 2 * kc), F32),
               pltpu.VMEM((N_HEADS, qb, LANES), F32), pltpu.VMEM((N_HEADS, qb, LANES), F32)]
    return pl.pallas_call(
        functools.partial(_dsa_kernel, seq=s, topk=topk, qb=qb),
        grid=(b, s // qb),
        in_specs=[pl.BlockSpec((1, qb, 2 * LANES), lambda bi, i: (bi, i, 0)),
                  pl.BlockSpec((1, s, LANES), lambda bi, i: (bi, 0, 2)),
                  pl.BlockSpec((1, s, LANES), lambda bi, i: (bi, 0, 3)),
                  pl.BlockSpec((1, qb, 2 * LANES), lambda bi, i: (bi, i, 2)),
                  pl.BlockSpec((1, s, LANES), lambda bi, i: (bi, 0, 6)),
                  pl.BlockSpec((1, qb, LANES), lambda bi, i: (bi, i, 0))],
        out_specs=pl.BlockSpec((1, qb, 2 * LANES), lambda bi, i: (bi, i, 0)),
        out_shape=jax.ShapeDtypeStruct((b, s, 2 * LANES), BF16),
        scratch_shapes=scratch,
        compiler_params=pltpu.CompilerParams(dimension_semantics=("arbitrary",) * 2, vmem_limit_bytes=VMEM_LIMIT),
        name="attn_dsa",
    )(od3, od3, od3, od3, od3, small3)


def _outproj_kernel(x_ref, oa_ref, ob_ref, oc_ref, od_ref, w_ref, mod_ref, g2_ref, wr_ref, br_ref,
                    xo_ref, h_ref, gt_ref):
    mix = None
    for n, o_ref in enumerate((oa_ref, ob_ref, oc_ref, od_ref)):
        part = jnp.dot(o_ref[...], w_ref[n * 2 * LANES:(n + 1) * 2 * LANES, :], preferred_element_type=F32)
        mix = part if mix is None else mix + part
    xn = x_ref[...] + mod_ref[0, 2:3, :] * mix
    xo_ref[...] = xn
    h = _rms(xn, g2_ref[...]) * (1.0 + mod_ref[0, 4:5, :]) + mod_ref[0, 3:4, :]
    h_ref[...] = h.astype(BF16)

    logits = lax.dot_general(wr_ref[...], h, (((1,), (1,)), ((), ())), precision=lax.Precision.HIGHEST,
                             preferred_element_type=F32)
    score = 1.0 / (1.0 + jnp.exp(-logits))
    biased = score + br_ref[...]
    srow = [score[e:e + 1, :] for e in range(N_EXPERTS)]
    brow = [biased[e:e + 1, :] for e in range(N_EXPERTS)]
    per = N_EXPERTS // N_GROUPS
    best_v = best_g = None
    for g in range(N_GROUPS):
        r = brow[g * per:(g + 1) * per]
        top2 = None
        for a in range(per):
            for c in range(a + 1, per):
                top2 = r[a] + r[c] if top2 is None else jnp.maximum(top2, r[a] + r[c])
        if g == 0:
            best_v, best_g = top2, jnp.zeros_like(top2, dtype=I32)
        else:
            up = top2 > best_v
            best_v = jnp.where(up, top2, best_v)
            best_g = jnp.where(up, g, best_g)
    cand = [jnp.where(best_g == e // per, brow[e], -jnp.inf) for e in range(N_EXPERTS)]

    def first_max(vals):
        v, idx = vals[0], jnp.zeros_like(best_g)
        for e in range(1, N_EXPERTS):
            up = vals[e] > v
            v = jnp.where(up, vals[e], v)
            idx = jnp.where(up, e, idx)
        return idx

    i1 = first_max(cand)
    i2 = first_max([jnp.where(i1 == e, -jnp.inf, cand[e]) for e in range(N_EXPERTS)])
    s1 = sum(jnp.where(i1 == e, srow[e], 0.0) for e in range(N_EXPERTS))
    s2 = sum(jnp.where(i2 == e, srow[e], 0.0) for e in range(N_EXPERTS))
    den = s1 + s2
    for e in range(N_EXPERTS):
        gt_ref[e:e + 1, :] = jnp.where(i1 == e, s1 / den, jnp.where(i2 == e, s2 / den, 0.0))


def _outproj_call(x2, outs, w_out, mod_l, g2, wr_t, br, seq, tm):
    t, d = x2.shape
    tpb = seq // tm
    row = lambda n: pl.BlockSpec((tm, n), lambda i: (i, 0))
    const = lambda shape: pl.BlockSpec(shape, lambda i: (0,) * len(shape))
    return pl.pallas_call(
        _outproj_kernel,
        grid=(t // tm,),
        in_specs=[row(d)] + [row(2 * LANES)] * 4 + [
            const(w_out.shape), pl.BlockSpec((1, 6, d), lambda i: (i // tpb, 0, 0)), const((1, d)),
            const(wr_t.shape), const(br.shape)],
        out_specs=[row(d), row(d), pl.BlockSpec((N_EXPERTS, tm), lambda i: (0, i))],
        out_shape=[jax.ShapeDtypeStruct((t, d), F32), jax.ShapeDtypeStruct((t, d), BF16),
                   jax.ShapeDtypeStruct((N_EXPERTS, t), F32)],
        compiler_params=pltpu.CompilerParams(dimension_semantics=("arbitrary",), vmem_limit_bytes=VMEM_LIMIT),
        name="outproj_router",
    )(x2, *outs, w_out, mod_l, g2, wr_t, br)


def _moe_kernel(h_ref, g_ref, x_ref, mod_ref, gf_ref, wgu_ref, wd_ref, o_ref, acc_ref, *, final):
    e = pl.program_id(1)

    @pl.when(e == 0)
    def _():
        acc_ref[...] = jnp.zeros(acc_ref.shape, F32)

    gu = jnp.dot(h_ref[...], wgu_ref[0], preferred_element_type=F32)
    gate, up = gu[:, :D_EXPERT], gu[:, D_EXPERT:]
    hid = (gate / (1.0 + jnp.exp(-gate)) * up).astype(BF16)
    y = jnp.dot(hid, wd_ref[0], preferred_element_type=F32)
    lane = lax.broadcasted_iota(I32, (1, N_EXPERTS), 1)
    ge = jnp.sum(jnp.where(lane == e, g_ref[...], 0.0), axis=1, keepdims=True)
    acc_ref[...] += ge * y

    @pl.when(e == N_EXPERTS - 1)
    def _():
        xn = x_ref[...] + mod_ref[0, 5:6, :] * acc_ref[...]
        o_ref[...] = _rms(xn, gf_ref[...]) if final else xn


def _moe_call(h2, gates, x2, mod_l, gf, wgu, wd, seq, tm, final):
    t, d = x2.shape
    tpb = seq // tm
    return pl.pallas_call(
        functools.partial(_moe_kernel, final=final),
        grid=(t // tm, N_EXPERTS),
        in_specs=[pl.BlockSpec((tm, d), lambda i, e: (i, 0)),
                  pl.BlockSpec((tm, N_EXPERTS), lambda i, e: (i, 0)),
                  pl.BlockSpec((tm, d), lambda i, e: (i, 0)),
                  pl.BlockSpec((1, 6, d), lambda i, e: (i // tpb, 0, 0)),
                  pl.BlockSpec((1, d), lambda i, e: (0, 0)),
                  pl.BlockSpec((1, d, 2 * D_EXPERT), lambda i, e: (e, 0, 0)),
                  pl.BlockSpec((1, D_EXPERT, d), lambda i, e: (e, 0, 0))],
        out_specs=pl.BlockSpec((tm, d), lambda i, e: (i, 0)),
        out_shape=jax.ShapeDtypeStruct((t, d), F32),
        scratch_shapes=[pltpu.VMEM((tm, d), F32)],
        compiler_params=pltpu.CompilerParams(dimension_semantics=("arbitrary",) * 2, vmem_limit_bytes=VMEM_LIMIT),
        name="moe",
    )(h2, gates, x2, mod_l, gf, wgu, wd)


def _swap_half(w):
    half = w.shape[1] // 2
    return jnp.concatenate([-w[:, half:], w[:, :half]], axis=1)


def _layer_weights(w_in, w_q_up, w_kv_up):
    d = w_in.shape[0]
    pts = np.cumsum(IN_SIZES)[:-1].tolist()
    (a_q, a_k, a_v, a_f, b_cq, b_ckv, b_kr, c_q, c_k, c_v,
     d_q, d_k, d_v, d_qi, d_ki, d_wi) = jnp.split(w_in, pts, axis=1)
    qs = HEAD_DIM ** -0.5
    dup = lambda w: jnp.concatenate([w[:, :HEAD_DIM], w[:, :HEAD_DIM], w[:, HEAD_DIM:], w[:, HEAD_DIM:]], axis=1)
    small = jnp.concatenate([
        a_f, d_wi * ((IDX_HEADS * IDX_DIM) ** -0.5),
        jnp.zeros((d, _KR_LANE - _WI_LANE - IDX_HEADS), F32),
        b_kr, _swap_half(b_kr), jnp.zeros((d, LANES - _KR_LANE - 2 * MLA_ROPE), F32)], axis=1)
    w_all = jnp.concatenate([
        a_q * qs, a_k, a_v,
        c_q * qs, dup(c_k), dup(c_v),
        d_q * qs, d_k, d_k, d_v, d_v, d_qi, d_ki, d_ki, d_ki, d_ki,
        small, b_cq, b_ckv], axis=1).astype(BF16)

    per_q = MLA_NOPE + MLA_ROPE
    wq = []
    for hd in range(N_HEADS):
        blk = w_q_up[:, hd * per_q:(hd + 1) * per_q]
        rot = blk[:, MLA_NOPE:]
        wq += [blk[:, :MLA_NOPE], rot, _swap_half(rot)]
    wq = jnp.concatenate(wq, axis=1).astype(BF16)

    place = np.zeros((LANES, LANES), np.float32)
    place[_KR_LANE + np.arange(MLA_ROPE), MLA_NOPE + np.arange(MLA_ROPE)] = 1.0
    place = jnp.asarray(place)
    kcols, vcols = [], []
    for hd in range(N_HEADS):
        blk = w_kv_up[:, hd * 2 * HEAD_DIM:(hd + 1) * 2 * HEAD_DIM]
        knope = jnp.concatenate([blk[:, :MLA_NOPE], jnp.zeros((MLA_KV_RANK, LANES - MLA_NOPE), F32)], axis=1)
        kcols.append(jnp.concatenate([knope, place], axis=0))
        vcols.append(jnp.concatenate([blk[:, MLA_NOPE:], jnp.zeros((LANES, HEAD_DIM), F32)], axis=0))
    wkv = jnp.concatenate(kcols + vcols, axis=1).astype(BF16)
    return w_all, wq, wkv


def _rope_tables(seq):
    half = MLA_ROPE // 2
    inv = ROPE_THETA ** (-jnp.arange(half, dtype=F32) / half)
    ang = jnp.arange(seq, dtype=F32)[:, None] * inv[None, :]
    cos = jnp.tile(jnp.cos(ang), (1, 2))
    sin = jnp.tile(jnp.sin(ang), (1, 2))
    scale = LOG2E * (MLA_NOPE + MLA_ROPE) ** -0.5
    z = lambda n: jnp.zeros((seq, n), F32)
    tab_q = jnp.concatenate([jnp.full((seq, MLA_NOPE), scale, F32), cos * scale, z(MLA_ROPE)], axis=1)
    tab_qs = jnp.concatenate([z(MLA_NOPE), sin * scale, z(MLA_ROPE)], axis=1)
    tab_k = jnp.concatenate([z(_KR_LANE), cos, z(LANES - _KR_LANE - MLA_ROPE)], axis=1)
    tab_ks = jnp.concatenate([z(_KR_LANE), sin, z(LANES - _KR_LANE - MLA_ROPE)], axis=1)
    return jnp.stack([tab_q, tab_qs, tab_k, tab_ks])


def kernel(x, c, w_ada, b_ada, g_norm1, w_in, b_forget, g_q_mla, w_q_up, g_kv_mla, w_kv_up, sinks, w_out,
           g_norm2, w_router, b_router, w_gate, w_up, w_down, g_final):
    b, s, d = x.shape
    depth = w_in.shape[0]
    t = b * s
    topk = min(TOPK_MAX, s // 4)
    tm = min(1024, s)
    tq = min(512, s)
    tq_swa = min(512, s)
    tm_moe = min(1024, s)
    qb_dsa = 2 * LANES
    assert s % tm_moe == 0 and s % tm == 0 and s % tq == 0 and s % qb_dsa == 0

    mod = _ada_call(c, w_ada, b_ada).reshape(depth, b, 6, d)
    tabs = _rope_tables(s)
    wr_t = w_router.T
    br = b_router.reshape(N_EXPERTS, 1)
    x2 = x.reshape(t, d)
    for l in range(depth):
        w_all, wq, wkv = _layer_weights(w_in[l], w_q_up[l], w_kv_up[l])
        oa, oc, od, osm, oqb, okb, ovb = _inproj_call(
            x2, mod[l], g_norm1[l].reshape(1, d), w_all, wq, wkv,
            g_q_mla[l].reshape(1, -1), g_kv_mla[l].reshape(1, -1), tabs, s, tm)
        small3 = osm.reshape(b, s, LANES)
        bf = jnp.zeros((1, LANES), F32).at[0, :N_HEADS].set(b_forget[l])
        fcum = _fox_cumsum_call(small3, bf)
        frow = jnp.swapaxes(fcum[:, :, :8], 1, 2)
        out_a = _causal_attn_call((oa.reshape(b, s, -1), fcum, frow), True, b, s, tq)
        out_b = _causal_attn_call((oqb.reshape(b, s, -1), okb.reshape(b, s, -1), ovb.reshape(b, s, -1)),
                                  False, b, s, tq)
        out_c = _swa_call(oc.reshape(b, s, -1), sinks[l], b, s, tq_swa)
        out_d = _dsa_call(od.reshape(b, s, -1), small3, b, s, topk, qb_dsa)
        outs = [o.reshape(t, 2 * LANES) for o in (out_a, out_b, out_c, out_d)]
        x2, h2, gates_t = _outproj_call(x2, outs, w_out[l].astype(BF16), mod[l], g_norm2[l].reshape(1, d),
                                        wr_t, br, s, tm)
        wgu = jnp.concatenate([w_gate[l], w_up[l]], axis=2).astype(BF16)
        x2 = _moe_call(h2, gates_t.T, x2, mod[l], g_final.reshape(1, d), wgu, w_down[l].astype(BF16),
                       s, tm_moe, l == depth - 1)
    return x2.reshape(b, s, d)
```

```python
import functools

import jax
import jax.numpy as jnp
import numpy as np
from jax import lax
from jax.experimental import pallas as pl
from jax.experimental.pallas import tpu as pltpu

F32 = jnp.float32
BF16 = jnp.bfloat16
I32 = jnp.int32
I16 = jnp.int16
I16_MIN, I16_MAX = -(2 ** 15), 2 ** 15 - 1

EPS = 1e-6
HEAD_DIM = 64
LANES = 128
SUBLANES = 8
N_HEADS = 4
MLA_Q_RANK = 256
MLA_KV_RANK = 128
MLA_NOPE = 64
MLA_ROPE = 32
ROPE_THETA = 10000.0
WINDOW = 128
IDX_HEADS = 8
IDX_DIM = 32
TOPK_MAX = 256
N_EXPERTS = 16
N_GROUPS = 4
D_EXPERT = 256
IN_SIZES = (256, 256, 256, 4, 256, 128, 32, 256, 128, 128, 256, 64, 64, 256, 32, 8)

LOG2E = 1.4426950408889634
NEG = -1e30
INT_MIN = -(2 ** 31)
KEY_NEG_INF = INT_MIN + 0x7FFFFF
VMEM_LIMIT = 56 * 1024 * 1024

_CA, _CC, _CD, _CS, _CQ, _CKV, _CEND = 0, 768, 1536, 2432, 2560, 2816, 2944
_F_LANE, _WI_LANE, _KR_LANE = 0, 4, 32


def _nt_dot(a, b):
    return lax.dot_general(a, b, (((1,), (1,)), ((), ())), preferred_element_type=F32)


def _rms(x, g):
    return x * lax.rsqrt(jnp.mean(x * x, axis=-1, keepdims=True) + EPS) * g


def _ada_kernel(c_ref, w_ref, b_ref, o_ref):
    c = c_ref[...]
    act = (c / (1.0 + jnp.exp(-c))).astype(BF16)
    o_ref[0] = jnp.dot(act, w_ref[0].astype(BF16), preferred_element_type=F32) + b_ref[0]


def _ada_call(c, w_ada, b_ada):
    depth, d, n = w_ada.shape
    bsz = c.shape[0]
    tn = 1024
    return pl.pallas_call(
        _ada_kernel,
        grid=(depth, n // tn),
        in_specs=[pl.BlockSpec((bsz, d), lambda l, j: (0, 0)),
                  pl.BlockSpec((1, d, tn), lambda l, j: (l, 0, j)),
                  pl.BlockSpec((1, 1, tn), lambda l, j: (l, 0, j))],
        out_specs=pl.BlockSpec((1, bsz, tn), lambda l, j: (l, 0, j)),
        out_shape=jax.ShapeDtypeStruct((depth, bsz, n), F32),
        compiler_params=pltpu.CompilerParams(dimension_semantics=("arbitrary", "arbitrary"),
                                             vmem_limit_bytes=VMEM_LIMIT),
        name="adaln",
    )(c, w_ada, b_ada.reshape(depth, 1, n))


def _inproj_kernel(x_ref, mod_ref, g1_ref, w_ref, wq_ref, wkv_ref, gq_ref, gkv_ref, tab_ref,
                   oa_ref, oc_ref, od_ref, os_ref, oqb_ref, okb_ref, ovb_ref):
    h = _rms(x_ref[...], g1_ref[...]) * (1.0 + mod_ref[0, 1:2, :]) + mod_ref[0, 0:1, :]
    h = h.astype(BF16)

    def proj(lo, hi):
        return jnp.dot(h, w_ref[:, lo:hi], preferred_element_type=F32)

    q_width = N_HEADS * HEAD_DIM
    for o_ref, lo, hi in ((oa_ref, _CA, _CC), (oc_ref, _CC, _CD), (od_ref, _CD, _CS)):
        o_ref[:, :q_width] = (proj(lo, lo + q_width) * LOG2E).astype(BF16)
        o_ref[:, q_width:] = proj(lo + q_width, hi).astype(BF16)
    small = proj(_CS, _CQ)
    os_ref[...] = small

    cq = _rms(proj(_CQ, _CKV), gq_ref[...]).astype(BF16)
    qf = jnp.dot(cq, wq_ref[...], preferred_element_type=F32)
    tab_q, tab_qs, tab_k, tab_ks = tab_ref[0], tab_ref[1], tab_ref[2], tab_ref[3]
    for hd in range(N_HEADS):
        qg = qf[:, hd * LANES:(hd + 1) * LANES]
        qr = qg * tab_q + pltpu.roll(qg, LANES - MLA_ROPE, axis=1) * tab_qs
        oqb_ref[:, hd * LANES:(hd + 1) * LANES] = qr.astype(BF16)

    ckv = _rms(proj(_CKV, _CEND), gkv_ref[...]).astype(BF16)
    kr = (small * tab_k + pltpu.roll(small, LANES - MLA_ROPE, axis=1) * tab_ks).astype(BF16)
    kvf = jnp.dot(jnp.concatenate([ckv, kr], axis=1), wkv_ref[...], preferred_element_type=F32)
    okb_ref[...] = kvf[:, :4 * LANES].astype(BF16)
    ovb_ref[...] = kvf[:, 4 * LANES:].astype(BF16)


def _inproj_call(x2, mod_l, g1, w_all, wq, wkv, gq, gkv, tabs, seq, tm):
    t, d = x2.shape
    tpb = seq // tm
    row = lambda n: pl.BlockSpec((tm, n), lambda i: (i, 0))
    const = lambda shape: pl.BlockSpec(shape, lambda i: (0,) * len(shape))
    widths = (768, 768, 896, 128, 512, 512, 256)
    dtypes = (BF16, BF16, BF16, F32, BF16, BF16, BF16)
    return pl.pallas_call(
        _inproj_kernel,
        grid=(t // tm,),
        in_specs=[row(d),
                  pl.BlockSpec((1, 6, d), lambda i: (i // tpb, 0, 0)),
                  const((1, d)), const(w_all.shape), const(wq.shape), const(wkv.shape),
                  const((1, MLA_Q_RANK)), const((1, MLA_KV_RANK)),
                  pl.BlockSpec((4, tm, LANES), lambda i: (0, i % tpb, 0))],
        out_specs=[row(n) for n in widths],
        out_shape=[jax.ShapeDtypeStruct((t, n), dt) for n, dt in zip(widths, dtypes)],
        compiler_params=pltpu.CompilerParams(dimension_semantics=("arbitrary",), vmem_limit_bytes=VMEM_LIMIT),
        name="inproj",
    )(x2, mod_l, g1, w_all, wq, wkv, gq, gkv, tabs)


def _fox_cumsum_kernel(s_ref, b_ref, o_ref):
    z = s_ref[0] + b_ref[...]
    lf = jnp.minimum(z, 0.0) - jnp.log(1.0 + jnp.exp(-jnp.abs(z)))
    n = lf.shape[0]
    row = lax.broadcasted_iota(I32, lf.shape, 0)
    d = 1
    while d < n:
        lf = lf + jnp.where(row >= d, pltpu.roll(lf, d, axis=0), 0.0)
        d *= 2
    o_ref[0] = lf * LOG2E


def _fox_cumsum_call(small3, bf):
    b, s, _ = small3.shape
    return pl.pallas_call(
        _fox_cumsum_kernel,
        grid=(b,),
        in_specs=[pl.BlockSpec((1, s, LANES), lambda i: (i, 0, 0)),
                  pl.BlockSpec((1, LANES), lambda i: (0, 0))],
        out_specs=pl.BlockSpec((1, s, LANES), lambda i: (i, 0, 0)),
        out_shape=jax.ShapeDtypeStruct((b, s, LANES), F32),
        compiler_params=pltpu.CompilerParams(dimension_semantics=("arbitrary",), vmem_limit_bytes=VMEM_LIMIT),
        name="fox_cumsum",
    )(small3, bf)


def _lane_is_low():
    return lax.broadcasted_iota(I32, (1, LANES), 1) < HEAD_DIM


def _split_pair(q):
    low = _lane_is_low()
    zero = jnp.zeros_like(q)
    return jnp.where(low, q, zero), jnp.where(low, zero, q)


def _values_with_ones(v, j):
    low = _lane_is_low()
    return jnp.where(low if j == 0 else ~low, v, jnp.ones_like(v))


def _pair_output(a0, a1, extra0=None, extra1=None):
    l0 = pltpu.roll(a0, HEAD_DIM, axis=1)
    l1 = pltpu.roll(a1, HEAD_DIM, axis=1)
    if extra0 is not None:
        l0, l1 = l0 + extra0, l1 + extra1
    return jnp.where(_lane_is_low(), a0 / l0, a1 / l1)


def _two_pass_attention(n_before, logits, values, s_ref, mx_ref, acc_ref, n_heads, shared=None):
    tq, tk = s_ref.shape[-2:]
    mx_ref[...] = jnp.full(mx_ref.shape, NEG, F32)

    def store(c, n, diagonal):
        ctx = shared(c, n) if shared is not None else None
        for j in range(n_heads):
            s = logits(c, n, j, diagonal, ctx)
            for t in range(n):
                s_ref[j, c + t] = s[:, t * tk:(t + 1) * tk]
            parts = [s[:, blk * LANES:(blk + 1) * LANES] for blk in range(n * tk // LANES)]
            while len(parts) > 1:
                parts = [jnp.maximum(a, b) for a, b in zip(parts[::2], parts[1::2])]
            mx_ref[j] = jnp.maximum(mx_ref[j], parts[0])

    def store_two(c2, carry):
        store(2 * c2, 2, False)
        return carry

    lax.fori_loop(0, n_before // 2, store_two, 0)

    @pl.when(n_before % 2 == 1)
    def _():
        store(n_before - 1, 1, False)

    store(n_before, 1, True)

    row_max = [jnp.max(mx_ref[j], axis=1, keepdims=True) for j in range(n_heads)]
    shift = [jnp.broadcast_to(m, (tq, tk)) for m in row_max]
    acc_ref[...] = jnp.zeros(acc_ref.shape, F32)

    def accum(c, n):
        for j in range(n_heads):
            p = [jnp.exp2(s_ref[j, c + t] - shift[j]).astype(BF16) for t in range(n)]
            p = p[0] if n == 1 else jnp.concatenate(p, axis=1)
            acc_ref[j] += jnp.dot(p, values(c, n, j), preferred_element_type=F32)

    def accum_two(c2, carry):
        accum(2 * c2, 2)
        return carry

    lax.fori_loop(0, (n_before + 1) // 2, accum_two, 0)

    @pl.when(n_before % 2 == 0)
    def _():
        accum(n_before, 1)

    return row_max


def _causal_attn_kernel(*refs, fox, tq):
    if fox:
        q_ref, k_ref, v_ref, fc_ref, fr_ref, o_ref, s_ref, mx_ref, acc_ref = refs
    else:
        q_ref, k_ref, v_ref, o_ref, s_ref, mx_ref, acc_ref = refs
    tk = tq
    hp = pl.program_id(1)
    i = pl.program_id(2)
    if fox:
        qs = _split_pair(q_ref[0])
        lane = lax.broadcasted_iota(I32, (1, LANES), 1)
        fcol = [jnp.sum(jnp.where(lane == 2 * hp + j, fc_ref[0], 0.0), axis=1, keepdims=True) for j in range(2)]
    else:
        qs = (q_ref[0, :, :LANES], q_ref[0, :, LANES:])
    causal = (lax.broadcasted_iota(I32, (tq, tk), 1) <= lax.broadcasted_iota(I32, (tq, tk), 0))

    def logits(c, n, j, diagonal, _):
        start = pl.multiple_of(c * tk, tk)
        if fox:
            s = _nt_dot(qs[j], k_ref[0, pl.ds(start, n * tk), :])
            s = (s + fcol[j]) - fr_ref[0, pl.ds(2 * hp + j, 1), pl.ds(start, n * tk)]
        else:
            s = _nt_dot(qs[j], k_ref[0, pl.ds(start, n * tk), j * LANES:(j + 1) * LANES])
        return jnp.where(causal, s, NEG) if diagonal else s

    def values(c, n, j):
        return _values_with_ones(v_ref[0, pl.ds(pl.multiple_of(c * tk, tk), n * tk), :], j)

    _two_pass_attention(i, logits, values, s_ref, mx_ref, acc_ref, 2)
    o_ref[0] = _pair_output(acc_ref[0], acc_ref[1]).astype(o_ref.dtype)


def _causal_attn_call(arrs, fox, b, s, tq):
    scratch = [pltpu.VMEM((2, s // tq, tq, tq), F32), pltpu.VMEM((2, tq, LANES), F32),
               pltpu.VMEM((2, tq, LANES), F32)]
    if fox:
        qkv, fcol, frow = arrs
        operands = (qkv, qkv, qkv, fcol, frow)
        in_specs = [pl.BlockSpec((1, tq, LANES), lambda bi, hp, i: (bi, i, hp)),
                    pl.BlockSpec((1, s, LANES), lambda bi, hp, i: (bi, 0, 2 + hp)),
                    pl.BlockSpec((1, s, LANES), lambda bi, hp, i: (bi, 0, 4 + hp)),
                    pl.BlockSpec((1, tq, LANES), lambda bi, hp, i: (bi, i, 0)),
                    pl.BlockSpec((1, 8, s), lambda bi, hp, i: (bi, 0, 0))]
    else:
        operands = arrs
        in_specs = [pl.BlockSpec((1, tq, 2 * LANES), lambda bi, hp, i: (bi, i, hp)),
                    pl.BlockSpec((1, s, 2 * LANES), lambda bi, hp, i: (bi, 0, hp)),
                    pl.BlockSpec((1, s, LANES), lambda bi, hp, i: (bi, 0, hp))]
    return pl.pallas_call(
        functools.partial(_causal_attn_kernel, fox=fox, tq=tq),
        grid=(b, 2, s // tq),
        in_specs=in_specs,
        out_specs=pl.BlockSpec((1, tq, LANES), lambda bi, hp, i: (bi, i, hp)),
        out_shape=jax.ShapeDtypeStruct((b, s, 2 * LANES), BF16),
        scratch_shapes=scratch,
        compiler_params=pltpu.CompilerParams(dimension_semantics=("arbitrary",) * 3, vmem_limit_bytes=VMEM_LIMIT),
        name="attn_fox" if fox else "attn_mla",
    )(*operands)


def _alibi_slope(head):
    return lax.shift_left(jnp.int32(1), 7 - head).astype(F32) * (LOG2E * 2.0 ** -8)


def _swa_kernel(sink_ref, q_ref, k_ref, v_ref, o_ref, *, tq):
    hp = pl.program_id(1)
    i = pl.program_id(2)
    band = 2 * WINDOW
    rel = lax.broadcasted_iota(I32, (WINDOW, band), 0) - lax.broadcasted_iota(I32, (WINDOW, band), 1)
    for r in range(tq // WINDOW):
        q_start = i * tq + r * WINDOW
        k_start = pl.multiple_of(jnp.maximum(q_start - WINDOW, 0), WINDOW)
        kb = k_ref[0, pl.ds(k_start, band), :]
        vb = v_ref[0, pl.ds(k_start, band), :]
        dist = rel + (q_start - k_start)
        valid = (dist >= 0) & (dist < WINDOW)
        distf = dist.astype(F32)
        qs = _split_pair(q_ref[0, r * WINDOW:(r + 1) * WINDOW, :])
        acc, sink_term = [], []
        for j in range(2):
            sink = sink_ref[2 * hp + j] * LOG2E
            s = jnp.where(valid, _nt_dot(qs[j], kb) - _alibi_slope(2 * hp + j) * distf, NEG)
            m = jnp.maximum(jnp.max(s, axis=1, keepdims=True), sink)
            p = jnp.exp2(s - m).astype(BF16)
            acc.append(jnp.dot(p, _values_with_ones(vb, j), preferred_element_type=F32))
            sink_term.append(jnp.exp2(sink - m))
        o_ref[0, r * WINDOW:(r + 1) * WINDOW, :] = _pair_output(acc[0], acc[1], *sink_term).astype(o_ref.dtype)


def _swa_call(qkv, sinks, b, s, tq):
    return pl.pallas_call(
        functools.partial(_swa_kernel, tq=tq),
        grid=(b, 2, s // tq),
        in_specs=[pl.BlockSpec(memory_space=pltpu.SMEM),
                  pl.BlockSpec((1, tq, LANES), lambda bi, hp, i: (bi, i, hp)),
                  pl.BlockSpec((1, s, LANES), lambda bi, hp, i: (bi, 0, 2 + hp)),
                  pl.BlockSpec((1, s, LANES), lambda bi, hp, i: (bi, 0, 4 + hp))],
        out_specs=pl.BlockSpec((1, tq, LANES), lambda bi, hp, i: (bi, i, hp)),
        out_shape=jax.ShapeDtypeStruct((b, s, 2 * LANES), BF16),
        compiler_params=pltpu.CompilerParams(dimension_semantics=("arbitrary",) * 3, vmem_limit_bytes=VMEM_LIMIT),
        name="attn_swa",
    )(sinks, qkv, qkv, qkv)


def _dsa_kernel(q_ref, k_ref, v_ref, qi_ref, ki_ref, wi_ref, o_ref, keys_ref, hi_ref, lo_ref, thr_ref,
                s_ref, mx_ref, acc_ref, *, seq, topk, qb):
    kc = LANES
    kw = 2 * kc
    i = pl.program_id(1)
    nwide = (i + 1) * (qb // kw)
    lane = lax.broadcasted_iota(I32, (1, LANES), 1)
    key_row = lax.broadcasted_iota(I32, (kc, qb), 0)
    query_pos = i * qb + lax.broadcasted_iota(I32, (1, qb), 1)

    qi = qi_ref[0]
    wi_t = wi_ref[0].T
    qms, wrows = [], []
    for hd in range(IDX_HEADS):
        g, r = divmod(hd, LANES // IDX_DIM)
        sel = (lane >= r * IDX_DIM) & (lane < (r + 1) * IDX_DIM)
        qg = qi[:, g * LANES:(g + 1) * LANES]
        qms.append(jnp.where(sel, qg, jnp.zeros_like(qg)))
        wrows.append(wi_t[_WI_LANE + hd:_WI_LANE + hd + 1, :])
    q_all = jnp.concatenate(qms, axis=0)
    key_minus_query = (lax.broadcasted_iota(I32, (kw, qb), 0) - lax.broadcasted_iota(I32, (kw, qb), 1))

    def score_chunk(c, carry):
        start = pl.multiple_of(c * kw, kw)
        logit = _nt_dot(ki_ref[0, pl.ds(start, kw), :], q_all)
        sc = wrows[0] * jnp.maximum(logit[:, :qb], 0.0)
        for hd in range(1, IDX_HEADS):
            sc = sc + wrows[hd] * jnp.maximum(logit[:, hd * qb:(hd + 1) * qb], 0.0)
        sc = jnp.where(key_minus_query <= i * qb - start, sc, -jnp.inf)
        sc = jnp.where(sc == 0.0, 0.0, sc)
        bits = pltpu.bitcast(sc, I32)
        keys = bits ^ ((bits >> 31) & 0x7FFFFFFF)
        keys_ref[2 * c] = keys[:kc]
        keys_ref[2 * c + 1] = keys[kc:]
        hi = (keys >> 16).astype(I16)
        lo = ((keys & 0xFFFF) + I16_MIN).astype(I16)
        hi_ref[2 * c] = hi[:kc]
        hi_ref[2 * c + 1] = hi[kc:]
        lo_ref[2 * c] = lo[:kc]
        lo_ref[2 * c + 1] = lo[kc:]
        return carry

    lax.fori_loop(0, nwide, score_chunk, 0)

    keep_all = jnp.full((1, qb), KEY_NEG_INF + 1, I32)
    thr_ref[...] = jnp.broadcast_to(keep_all, thr_ref.shape)

    def count(pred):
        def body(c2, acc):
            parts = []
            for c in (2 * c2, 2 * c2 + 1):
                hit = jnp.where(pred(keys_ref[c], c), 1.0, 0.0)
                parts += [hit[r * SUBLANES:(r + 1) * SUBLANES] for r in range(kc // SUBLANES)]
            while len(parts) > 1:
                parts = [a + b for a, b in zip(parts[::2], parts[1::2])]
            return acc + parts[0]
        acc = lax.fori_loop(0, nwide, body, jnp.zeros((SUBLANES, qb), F32))
        return jnp.sum(acc, axis=0, keepdims=True)

    @pl.when((i + 1) * qb > topk)
    def _():
        kf = float(topk)

        pack = 2 * SUBLANES

        def count16(plane_ref, cand):
            cand16 = cand.astype(I16)

            def body(c2, acc):
                parts = []
                for c in (2 * c2, 2 * c2 + 1):
                    hit = jnp.where(plane_ref[c] >= cand16, jnp.int16(1), jnp.int16(0))
                    parts += [hit[r * pack:(r + 1) * pack] for r in range(kc // pack)]
                while len(parts) > 1:
                    parts = [a + b for a, b in zip(parts[::2], parts[1::2])]
                return acc + parts[0]

            acc = lax.fori_loop(0, nwide, body, jnp.zeros((pack, qb), I16))
            return jnp.sum(acc.astype(F32), axis=0, keepdims=True)

        def search16(plane_ref, cnt_at_floor):
            def bit_step(it, carry):
                val, cnt_val = carry
                cand = val + lax.shift_left(jnp.int32(1), 15 - it)
                cnt = count16(plane_ref, cand)
                ok = cnt >= kf
                return jnp.where(ok, cand, val), jnp.where(ok, cnt, cnt_val)

            return lax.fori_loop(0, 16, bit_step, (jnp.full((1, qb), I16_MIN, I32), cnt_at_floor))

        thr_hi, cnt_hi = search16(hi_ref, jnp.full((1, qb), float(seq), F32))
        thr_hi16 = thr_hi.astype(I16)

        def pin(c, carry):
            h = hi_ref[c]
            lo_ref[c] = jnp.where(h > thr_hi16, jnp.int16(I16_MAX),
                                  jnp.where(h == thr_hi16, lo_ref[c], jnp.int16(I16_MIN)))
            return carry

        lax.fori_loop(0, 2 * nwide, pin, 0)
        thr_lo, cnt_thr = search16(lo_ref, cnt_hi)
        thr = thr_hi * 65536 + (thr_lo - I16_MIN)
        thr_ref[...] = jnp.broadcast_to(jnp.where(query_pos < topk, keep_all, thr), thr_ref.shape)

        @pl.when(jnp.max(cnt_thr) > kf)
        def _():
            need = kf - count(lambda kk, c: kk > thr)

            def idx_step(it, pos):
                cand = pos + lax.shift_left(jnp.int32(1), (seq.bit_length() - 2) - it)
                cnt = count(lambda kk, c: (kk == thr) & (key_row + c * kc < cand))
                return jnp.where(cnt < need, cand, pos)

            pos = lax.fori_loop(0, seq.bit_length() - 1, idx_step, jnp.zeros((1, qb), I32))

            def demote(c, carry):
                kk = keys_ref[c]
                keys_ref[c] = jnp.where((kk == thr) & (key_row + c * kc > pos), kk - 1, kk)
                return carry

            lax.fori_loop(0, 2 * nwide, demote, 0)

    q = q_ref[0]
    qs = _split_pair(q[:, :LANES]) + _split_pair(q[:, LANES:])
    thr_row = thr_ref[0:1, :]
    rel = (lax.broadcasted_iota(I32, (qb, 2 * kw), 0) - lax.broadcasted_iota(I32, (qb, 2 * kw), 1))

    def shared(c, n):
        halves = [jnp.where(keys_ref[2 * c + h] >= thr_row, 0.0, NEG).T for h in range(2 * n)]
        distf = (rel[:, :n * kw] + (i * qb - c * kw)).astype(F32)
        return jnp.concatenate(halves, axis=1), distf

    def logits(c, n, hd, diagonal, ctx):
        bias, distf = ctx
        start = pl.multiple_of(c * kw, kw)
        s = _nt_dot(qs[hd], k_ref[0, pl.ds(start, n * kw), :]) - (LOG2E * 2.0 ** -(N_HEADS + hd + 1)) * distf
        return s + bias

    def values(c, n, hd):
        return _values_with_ones(v_ref[0, pl.ds(pl.multiple_of(c * kw, kw), n * kw), :], hd % 2)

    _two_pass_attention(nwide - 1, logits, values, s_ref, mx_ref, acc_ref, N_HEADS, shared)
    o_ref[0] = jnp.concatenate([_pair_output(acc_ref[0], acc_ref[1]), _pair_output(acc_ref[2], acc_ref[3])],
                               axis=1).astype(o_ref.dtype)


def _dsa_call(od3, small3, b, s, topk, qb):
    kc = LANES
    scratch = [pltpu.VMEM((s // kc, kc, qb), I32), pltpu.VMEM((s // kc, kc, qb), I16),
               pltpu.VMEM((s // kc, kc, qb), I16), pltpu.VMEM((SUBLANES, qb), I32),
               pltpu.VMEM((N_HEADS, s // (2 * kc), qb, 2 * kc), F32),
               pltpu.VMEM((N_HEADS, qb, LANES), F32), pltpu.VMEM((N_HEADS, qb, LANES), F32)]
    return pl.pallas_call(
        functools.partial(_dsa_kernel, seq=s, topk=topk, qb=qb),
        grid=(b, s // qb),
        in_specs=[pl.BlockSpec((1, qb, 2 * LANES), lambda bi, i: (bi, i, 0)),
                  pl.BlockSpec((1, s, LANES), lambda bi, i: (bi, 0, 2)),
                  pl.BlockSpec((1, s, LANES), lambda bi, i: (bi, 0, 3)),
                  pl.BlockSpec((1, qb, 2 * LANES), lambda bi, i: (bi, i, 2)),
                  pl.BlockSpec((1, s, LANES), lambda bi, i: (bi, 0, 6)),
                  pl.BlockSpec((1, qb, LANES), lambda bi, i: (bi, i, 0))],
        out_specs=pl.BlockSpec((1, qb, 2 * LANES), lambda bi, i: (bi, i, 0)),
        out_shape=jax.ShapeDtypeStruct((b, s, 2 * LANES), BF16),
        scratch_shapes=scratch,
        compiler_params=pltpu.CompilerParams(dimension_semantics=("arbitrary",) * 2, vmem_limit_bytes=VMEM_LIMIT),
        name="attn_dsa",
    )(od3, od3, od3, od3, od3, small3)


def _outproj_kernel(x_ref, oa_ref, ob_ref, oc_ref, od_ref, w_ref, mod_ref, g2_ref, wr_ref, br_ref,
                    xo_ref, h_ref, gt_ref):
    mix = None
    for n, o_ref in enumerate((oa_ref, ob_ref, oc_ref, od_ref)):
        part = jnp.dot(o_ref[...], w_ref[n * 2 * LANES:(n + 1) * 2 * LANES, :], preferred_element_type=F32)
        mix = part if mix is None else mix + part
    xn = x_ref[...] + mod_ref[0, 2:3, :] * mix
    xo_ref[...] = xn
    h = _rms(xn, g2_ref[...]) * (1.0 + mod_ref[0, 4:5, :]) + mod_ref[0, 3:4, :]
    h_ref[...] = h.astype(BF16)

    logits = lax.dot_general(wr_ref[...], h, (((1,), (1,)), ((), ())), precision=lax.Precision.HIGHEST,
                             preferred_element_type=F32)
    score = 1.0 / (1.0 + jnp.exp(-logits))
    biased = score + br_ref[...]
    srow = [score[e:e + 1, :] for e in range(N_EXPERTS)]
    brow = [biased[e:e + 1, :] for e in range(N_EXPERTS)]
    per = N_EXPERTS // N_GROUPS
    best_v = best_g = None
    for g in range(N_GROUPS):
        r = brow[g * per:(g + 1) * per]
        top2 = None
        for a in range(per):
            for c in range(a + 1, per):
                top2 = r[a] + r[c] if top2 is None else jnp.maximum(top2, r[a] + r[c])
        if g == 0:
            best_v, best_g = top2, jnp.zeros_like(top2, dtype=I32)
        else:
            up = top2 > best_v
            best_v = jnp.where(up, top2, best_v)
            best_g = jnp.where(up, g, best_g)
    cand = [jnp.where(best_g == e // per, brow[e], -jnp.inf) for e in range(N_EXPERTS)]

    def first_max(vals):
        v, idx = vals[0], jnp.zeros_like(best_g)
        for e in range(1, N_EXPERTS):
            up = vals[e] > v
            v = jnp.where(up, vals[e], v)
            idx = jnp.where(up, e, idx)
        return idx

    i1 = first_max(cand)
    i2 = first_max([jnp.where(i1 == e, -jnp.inf, cand[e]) for e in range(N_EXPERTS)])
    s1 = sum(jnp.where(i1 == e, srow[e], 0.0) for e in range(N_EXPERTS))
    s2 = sum(jnp.where(i2 == e, srow[e], 0.0) for e in range(N_EXPERTS))
    den = s1 + s2
    for e in range(N_EXPERTS):
        gt_ref[e:e + 1, :] = jnp.where(i1 == e, s1 / den, jnp.where(i2 == e, s2 / den, 0.0))


def _outproj_call(x2, outs, w_out, mod_l, g2, wr_t, br, seq, tm):
    t, d = x2.shape
    tpb = seq // tm
    row = lambda n: pl.BlockSpec((tm, n), lambda i: (i, 0))
    const = lambda shape: pl.BlockSpec(shape, lambda i: (0,) * len(shape))
    return pl.pallas_call(
        _outproj_kernel,
        grid=(t // tm,),
        in_specs=[row(d)] + [row(2 * LANES)] * 4 + [
            const(w_out.shape), pl.BlockSpec((1, 6, d), lambda i: (i // tpb, 0, 0)), const((1, d)),
            const(wr_t.shape), const(br.shape)],
        out_specs=[row(d), row(d), pl.BlockSpec((N_EXPERTS, tm), lambda i: (0, i))],
        out_shape=[jax.ShapeDtypeStruct((t, d), F32), jax.ShapeDtypeStruct((t, d), BF16),
                   jax.ShapeDtypeStruct((N_EXPERTS, t), F32)],
        compiler_params=pltpu.CompilerParams(dimension_semantics=("arbitrary",), vmem_limit_bytes=VMEM_LIMIT),
        name="outproj_router",
    )(x2, *outs, w_out, mod_l, g2, wr_t, br)


def _moe_kernel(h_ref, g_ref, x_ref, mod_ref, gf_ref, wgu_ref, wd_ref, o_ref, acc_ref, *, final):
    e = pl.program_id(1)

    @pl.when(e == 0)
    def _():
        acc_ref[...] = jnp.zeros(acc_ref.shape, F32)

    gu = jnp.dot(h_ref[...], wgu_ref[0], preferred_element_type=F32)
    gate, up = gu[:, :D_EXPERT], gu[:, D_EXPERT:]
    hid = (gate / (1.0 + jnp.exp(-gate)) * up).astype(BF16)
    y = jnp.dot(hid, wd_ref[0], preferred_element_type=F32)
    lane = lax.broadcasted_iota(I32, (1, N_EXPERTS), 1)
    ge = jnp.sum(jnp.where(lane == e, g_ref[...], 0.0), axis=1, keepdims=True)
    acc_ref[...] += ge * y

    @pl.when(e == N_EXPERTS - 1)
    def _():
        xn = x_ref[...] + mod_ref[0, 5:6, :] * acc_ref[...]
        o_ref[...] = _rms(xn, gf_ref[...]) if final else xn


def _moe_call(h2, gates, x2, mod_l, gf, wgu, wd, seq, tm, final):
    t, d = x2.shape
    tpb = seq // tm
    return pl.pallas_call(
        functools.partial(_moe_kernel, final=final),
        grid=(t // tm, N_EXPERTS),
        in_specs=[pl.BlockSpec((tm, d), lambda i, e: (i, 0)),
                  pl.BlockSpec((tm, N_EXPERTS), lambda i, e: (i, 0)),
                  pl.BlockSpec((tm, d), lambda i, e: (i, 0)),
                  pl.BlockSpec((1, 6, d), lambda i, e: (i // tpb, 0, 0)),
                  pl.BlockSpec((1, d), lambda i, e: (0, 0)),
                  pl.BlockSpec((1, d, 2 * D_EXPERT), lambda i, e: (e, 0, 0)),
                  pl.BlockSpec((1, D_EXPERT, d), lambda i, e: (e, 0, 0))],
        out_specs=pl.BlockSpec((tm, d), lambda i, e: (i, 0)),
        out_shape=jax.ShapeDtypeStruct((t, d), F32),
        scratch_shapes=[pltpu.VMEM((tm, d), F32)],
        compiler_params=pltpu.CompilerParams(dimension_semantics=("arbitrary",) * 2, vmem_limit_bytes=VMEM_LIMIT),
        name="moe",
    )(h2, gates, x2, mod_l, gf, wgu, wd)


def _swap_half(w):
    half = w.shape[1] // 2
    return jnp.concatenate([-w[:, half:], w[:, :half]], axis=1)


def _layer_weights(w_in, w_q_up, w_kv_up):
    d = w_in.shape[0]
    pts = np.cumsum(IN_SIZES)[:-1].tolist()
    (a_q, a_k, a_v, a_f, b_cq, b_ckv, b_kr, c_q, c_k, c_v,
     d_q, d_k, d_v, d_qi, d_ki, d_wi) = jnp.split(w_in, pts, axis=1)
    qs = HEAD_DIM ** -0.5
    dup = lambda w: jnp.concatenate([w[:, :HEAD_DIM], w[:, :HEAD_DIM], w[:, HEAD_DIM:], w[:, HEAD_DIM:]], axis=1)
    small = jnp.concatenate([
        a_f, d_wi * ((IDX_HEADS * IDX_DIM) ** -0.5),
        jnp.zeros((d, _KR_LANE - _WI_LANE - IDX_HEADS), F32),
        b_kr, _swap_half(b_kr), jnp.zeros((d, LANES - _KR_LANE - 2 * MLA_ROPE), F32)], axis=1)
    w_all = jnp.concatenate([
        a_q * qs, a_k, a_v,
        c_q * qs, dup(c_k), dup(c_v),
        d_q * qs, d_k, d_k, d_v, d_v, d_qi, d_ki, d_ki, d_ki, d_ki,
        small, b_cq, b_ckv], axis=1).astype(BF16)

    per_q = MLA_NOPE + MLA_ROPE
    wq = []
    for hd in range(N_HEADS):
        blk = w_q_up[:, hd * per_q:(hd + 1) * per_q]
        rot = blk[:, MLA_NOPE:]
        wq += [blk[:, :MLA_NOPE], rot, _swap_half(rot)]
    wq = jnp.concatenate(wq, axis=1).astype(BF16)

    place = np.zeros((LANES, LANES), np.float32)
    place[_KR_LANE + np.arange(MLA_ROPE), MLA_NOPE + np.arange(MLA_ROPE)] = 1.0
    place = jnp.asarray(place)
    kcols, vcols = [], []
    for hd in range(N_HEADS):
        blk = w_kv_up[:, hd * 2 * HEAD_DIM:(hd + 1) * 2 * HEAD_DIM]
        knope = jnp.concatenate([blk[:, :MLA_NOPE], jnp.zeros((MLA_KV_RANK, LANES - MLA_NOPE), F32)], axis=1)
        kcols.append(jnp.concatenate([knope, place], axis=0))
        vcols.append(jnp.concatenate([blk[:, MLA_NOPE:], jnp.zeros((LANES, HEAD_DIM), F32)], axis=0))
    wkv = jnp.concatenate(kcols + vcols, axis=1).astype(BF16)
    return w_all, wq, wkv


def _rope_tables(seq):
    half = MLA_ROPE // 2
    inv = ROPE_THETA ** (-jnp.arange(half, dtype=F32) / half)
    ang = jnp.arange(seq, dtype=F32)[:, None] * inv[None, :]
    cos = jnp.tile(jnp.cos(ang), (1, 2))
    sin = jnp.tile(jnp.sin(ang), (1, 2))
    scale = LOG2E * (MLA_NOPE + MLA_ROPE) ** -0.5
    z = lambda n: jnp.zeros((seq, n), F32)
    tab_q = jnp.concatenate([jnp.full((seq, MLA_NOPE), scale, F32), cos * scale, z(MLA_ROPE)], axis=1)
    tab_qs = jnp.concatenate([z(MLA_NOPE), sin * scale, z(MLA_ROPE)], axis=1)
    tab_k = jnp.concatenate([z(_KR_LANE), cos, z(LANES - _KR_LANE - MLA_ROPE)], axis=1)
    tab_ks = jnp.concatenate([z(_KR_LANE), sin, z(LANES - _KR_LANE - MLA_ROPE)], axis=1)
    return jnp.stack([tab_q, tab_qs, tab_k, tab_ks])


def kernel(x, c, w_ada, b_ada, g_norm1, w_in, b_forget, g_q_mla, w_q_up, g_kv_mla, w_kv_up, sinks, w_out,
           g_norm2, w_router, b_router, w_gate, w_up, w_down, g_final):
    b, s, d = x.shape
    depth = w_in.shape[0]
    t = b * s
    topk = min(TOPK_MAX, s // 4)
    tm = min(1024, s)
    tq = min(512, s)
    tq_swa = min(512, s)
    tm_moe = min(1024, s)
    qb_dsa = 2 * LANES
    assert s % tm_moe == 0 and s % tm == 0 and s % tq == 0 and s % qb_dsa == 0

    mod = _ada_call(c, w_ada, b_ada).reshape(depth, b, 6, d)
    tabs = _rope_tables(s)
    wr_t = w_router.T
    br = b_router.reshape(N_EXPERTS, 1)
    x2 = x.reshape(t, d)
    for l in range(depth):
        w_all, wq, wkv = _layer_weights(w_in[l], w_q_up[l], w_kv_up[l])
        oa, oc, od, osm, oqb, okb, ovb = _inproj_call(
            x2, mod[l], g_norm1[l].reshape(1, d), w_all, wq, wkv,
            g_q_mla[l].reshape(1, -1), g_kv_mla[l].reshape(1, -1), tabs, s, tm)
        small3 = osm.reshape(b, s, LANES)
        bf = jnp.zeros((1, LANES), F32).at[0, :N_HEADS].set(b_forget[l])
        fcum = _fox_cumsum_call(small3, bf)
        frow = jnp.swapaxes(fcum[:, :, :8], 1, 2)
        out_a = _causal_attn_call((oa.reshape(b, s, -1), fcum, frow), True, b, s, tq)
        out_b = _causal_attn_call((oqb.reshape(b, s, -1), okb.reshape(b, s, -1), ovb.reshape(b, s, -1)),
                                  False, b, s, tq)
        out_c = _swa_call(oc.reshape(b, s, -1), sinks[l], b, s, tq_swa)
        out_d = _dsa_call(od.reshape(b, s, -1), small3, b, s, topk, qb_dsa)
        outs = [o.reshape(t, 2 * LANES) for o in (out_a, out_b, out_c, out_d)]
        x2, h2, gates_t = _outproj_call(x2, outs, w_out[l].astype(BF16), mod[l], g_norm2[l].reshape(1, d),
                                        wr_t, br, s, tm)
        wgu = jnp.concatenate([w_gate[l], w_up[l]], axis=2).astype(BF16)
        x2 = _moe_call(h2, gates_t.T, x2, mod[l], g_final.reshape(1, d), wgu, w_down[l].astype(BF16),
                       s, tm_moe, l == depth - 1)
    return x2.reshape(b, s, d)
```

```python
import functools

import jax
import jax.numpy as jnp
import numpy as np
from jax import lax
from jax.experimental import pallas as pl
from jax.experimental.pallas import tpu as pltpu

F32 = jnp.float32
BF16 = jnp.bfloat16
I32 = jnp.int32
I16 = jnp.int16
I16_MIN, I16_MAX = -(2 ** 15), 2 ** 15 - 1

EPS = 1e-6
HEAD_DIM = 64
LANES = 128
SUBLANES = 8
N_HEADS = 4
MLA_Q_RANK = 256
MLA_KV_RANK = 128
MLA_NOPE = 64
MLA_ROPE = 32
ROPE_THETA = 10000.0
WINDOW = 128
IDX_HEADS = 8
IDX_DIM = 32
TOPK_MAX = 256
N_EXPERTS = 16
N_GROUPS = 4
N_PAIRS = 6
N_CLASSES = N_GROUPS * N_PAIRS
PAIR_LOW = (0, 0, 0, 1, 1, 2)
PAIR_HIGH = (1, 2, 3, 2, 3, 3)
D_EXPERT = 256
MOE_ROWS = 256
PERM_ROWS = 512
IN_SIZES = (256, 256, 256, 4, 256, 128, 32, 256, 128, 128, 256, 64, 64, 256, 32, 8)

LOG2E = 1.4426950408889634
NEG = -1e30
INT_MIN = -(2 ** 31)
KEY_NEG_INF = INT_MIN + 0x7FFFFF
VMEM_LIMIT = 56 * 1024 * 1024

_CA, _CC, _CD, _CS, _CQ, _CKV, _CEND = 0, 768, 1536, 2432, 2560, 2816, 2944
_F_LANE, _WI_LANE, _KR_LANE = 0, 4, 32


def _nt_dot(a, b):
    return lax.dot_general(a, b, (((1,), (1,)), ((), ())), preferred_element_type=F32)


def _rms(x, g):
    return x * lax.rsqrt(jnp.mean(x * x, axis=-1, keepdims=True) + EPS) * g


def _ada_kernel(c_ref, w_ref, b_ref, o_ref):
    c = c_ref[...]
    act = (c / (1.0 + jnp.exp(-c))).astype(BF16)
    o_ref[0] = jnp.dot(act, w_ref[0].astype(BF16), preferred_element_type=F32) + b_ref[0]


def _ada_call(c, w_ada, b_ada):
    depth, d, n = w_ada.shape
    bsz = c.shape[0]
    tn = 1024
    return pl.pallas_call(
        _ada_kernel,
        grid=(depth, n // tn),
        in_specs=[pl.BlockSpec((bsz, d), lambda l, j: (0, 0)),
                  pl.BlockSpec((1, d, tn), lambda l, j: (l, 0, j)),
                  pl.BlockSpec((1, 1, tn), lambda l, j: (l, 0, j))],
        out_specs=pl.BlockSpec((1, bsz, tn), lambda l, j: (l, 0, j)),
        out_shape=jax.ShapeDtypeStruct((depth, bsz, n), F32),
        compiler_params=pltpu.CompilerParams(dimension_semantics=("arbitrary", "arbitrary"),
                                             vmem_limit_bytes=VMEM_LIMIT),
        name="adaln",
    )(c, w_ada, b_ada.reshape(depth, 1, n))


def _inproj_kernel(x_ref, mod_ref, g1_ref, w_ref, wq_ref, wkv_ref, gq_ref, gkv_ref, tab_ref,
                   oa_ref, oc_ref, od_ref, os_ref, oqb_ref, okb_ref, ovb_ref):
    h = _rms(x_ref[...], g1_ref[...]) * (1.0 + mod_ref[0, 1:2, :]) + mod_ref[0, 0:1, :]
    h = h.astype(BF16)

    def proj(lo, hi):
        return jnp.dot(h, w_ref[:, lo:hi], preferred_element_type=F32)

    q_width = N_HEADS * HEAD_DIM
    for o_ref, lo, hi in ((oa_ref, _CA, _CC), (oc_ref, _CC, _CD), (od_ref, _CD, _CS)):
        o_ref[:, :q_width] = (proj(lo, lo + q_width) * LOG2E).astype(BF16)
        o_ref[:, q_width:] = proj(lo + q_width, hi).astype(BF16)
    small = proj(_CS, _CQ)
    os_ref[...] = small

    cq = _rms(proj(_CQ, _CKV), gq_ref[...]).astype(BF16)
    qf = jnp.dot(cq, wq_ref[...], preferred_element_type=F32)
    tab_q, tab_qs, tab_k, tab_ks = tab_ref[0], tab_ref[1], tab_ref[2], tab_ref[3]
    for hd in range(N_HEADS):
        qg = qf[:, hd * LANES:(hd + 1) * LANES]
        qr = qg * tab_q + pltpu.roll(qg, LANES - MLA_ROPE, axis=1) * tab_qs
        oqb_ref[:, hd * LANES:(hd + 1) * LANES] = qr.astype(BF16)

    ckv = _rms(proj(_CKV, _CEND), gkv_ref[...]).astype(BF16)
    kr = (small * tab_k + pltpu.roll(small, LANES - MLA_ROPE, axis=1) * tab_ks).astype(BF16)
    kvf = jnp.dot(jnp.concatenate([ckv, kr], axis=1), wkv_ref[...], preferred_element_type=F32)
    okb_ref[...] = kvf[:, :4 * LANES].astype(BF16)
    ovb_ref[...] = kvf[:, 4 * LANES:].astype(BF16)


def _inproj_call(x2, mod_l, g1, w_all, wq, wkv, gq, gkv, tabs, seq, tm):
    t, d = x2.shape
    tpb = seq // tm
    row = lambda n: pl.BlockSpec((tm, n), lambda i: (i, 0))
    const = lambda shape: pl.BlockSpec(shape, lambda i: (0,) * len(shape))
    widths = (768, 768, 896, 128, 512, 512, 256)
    dtypes = (BF16, BF16, BF16, F32, BF16, BF16, BF16)
    return pl.pallas_call(
        _inproj_kernel,
        grid=(t // tm,),
        in_specs=[row(d),
                  pl.BlockSpec((1, 6, d), lambda i: (i // tpb, 0, 0)),
                  const((1, d)), const(w_all.shape), const(wq.shape), const(wkv.shape),
                  const((1, MLA_Q_RANK)), const((1, MLA_KV_RANK)),
                  pl.BlockSpec((4, tm, LANES), lambda i: (0, i % tpb, 0))],
        out_specs=[row(n) for n in widths],
        out_shape=[jax.ShapeDtypeStruct((t, n), dt) for n, dt in zip(widths, dtypes)],
        compiler_params=pltpu.CompilerParams(dimension_semantics=("arbitrary",), vmem_limit_bytes=VMEM_LIMIT),
        name="inproj",
    )(x2, mod_l, g1, w_all, wq, wkv, gq, gkv, tabs)


def _fox_cumsum_kernel(s_ref, b_ref, o_ref):
    z = s_ref[0] + b_ref[...]
    lf = jnp.minimum(z, 0.0) - jnp.log(1.0 + jnp.exp(-jnp.abs(z)))
    n = lf.shape[0]
    row = lax.broadcasted_iota(I32, lf.shape, 0)
    d = 1
    while d < n:
        lf = lf + jnp.where(row >= d, pltpu.roll(lf, d, axis=0), 0.0)
        d *= 2
    o_ref[0] = lf * LOG2E


def _fox_cumsum_call(small3, bf):
    b, s, _ = small3.shape
    return pl.pallas_call(
        _fox_cumsum_kernel,
        grid=(b,),
        in_specs=[pl.BlockSpec((1, s, LANES), lambda i: (i, 0, 0)),
                  pl.BlockSpec((1, LANES), lambda i: (0, 0))],
        out_specs=pl.BlockSpec((1, s, LANES), lambda i: (i, 0, 0)),
        out_shape=jax.ShapeDtypeStruct((b, s, LANES), F32),
        compiler_params=pltpu.CompilerParams(dimension_semantics=("arbitrary",), vmem_limit_bytes=VMEM_LIMIT),
        name="fox_cumsum",
    )(small3, bf)


def _lane_is_low():
    return lax.broadcasted_iota(I32, (1, LANES), 1) < HEAD_DIM


def _split_pair(q):
    low = _lane_is_low()
    zero = jnp.zeros_like(q)
    return jnp.where(low, q, zero), jnp.where(low, zero, q)


def _values_with_ones(v, j):
    low = _lane_is_low()
    return jnp.where(low if j == 0 else ~low, v, jnp.ones_like(v))


def _pair_output(a0, a1, extra0=None, extra1=None):
    l0 = pltpu.roll(a0, HEAD_DIM, axis=1)
    l1 = pltpu.roll(a1, HEAD_DIM, axis=1)
    if extra0 is not None:
        l0, l1 = l0 + extra0, l1 + extra1
    return jnp.where(_lane_is_low(), a0 / l0, a1 / l1)


def _two_pass_attention(n_before, logits, values, s_ref, mx_ref, acc_ref, n_heads, shared=None):
    tq, tk = s_ref.shape[-2:]
    mx_ref[...] = jnp.full(mx_ref.shape, NEG, F32)

    def store(c, n, diagonal):
        ctx = shared(c, n) if shared is not None else None
        for j in range(n_heads):
            s = logits(c, n, j, diagonal, ctx)
            for t in range(n):
                s_ref[j, c + t] = s[:, t * tk:(t + 1) * tk]
            parts = [s[:, blk * LANES:(blk + 1) * LANES] for blk in range(n * tk // LANES)]
            while len(parts) > 1:
                parts = [jnp.maximum(a, b) for a, b in zip(parts[::2], parts[1::2])]
            mx_ref[j] = jnp.maximum(mx_ref[j], parts[0])

    def store_two(c2, carry):
        store(2 * c2, 2, False)
        return carry

    lax.fori_loop(0, n_before // 2, store_two, 0)

    @pl.when(n_before % 2 == 1)
    def _():
        store(n_before - 1, 1, False)

    store(n_before, 1, True)

    row_max = [jnp.max(mx_ref[j], axis=1, keepdims=True) for j in range(n_heads)]
    shift = [jnp.broadcast_to(m, (tq, tk)) for m in row_max]
    acc_ref[...] = jnp.zeros(acc_ref.shape, F32)

    def accum(c, n):
        for j in range(n_heads):
            p = [jnp.exp2(s_ref[j, c + t] - shift[j]).astype(BF16) for t in range(n)]
            p = p[0] if n == 1 else jnp.concatenate(p, axis=1)
            acc_ref[j] += jnp.dot(p, values(c, n, j), preferred_element_type=F32)

    def accum_two(c2, carry):
        accum(2 * c2, 2)
        return carry

    lax.fori_loop(0, (n_before + 1) // 2, accum_two, 0)

    @pl.when(n_before % 2 == 0)
    def _():
        accum(n_before, 1)

    return row_max


def _causal_attn_kernel(*refs, fox, tq):
    if fox:
        q_ref, k_ref, v_ref, fc_ref, fr_ref, o_ref, s_ref, mx_ref, acc_ref = refs
    else:
        q_ref, k_ref, v_ref, o_ref, s_ref, mx_ref, acc_ref = refs
    tk = tq
    hp = pl.program_id(1)
    i = pl.program_id(2)
    if fox:
        qs = _split_pair(q_ref[0])
        lane = lax.broadcasted_iota(I32, (1, LANES), 1)
        fcol = [jnp.sum(jnp.where(lane == 2 * hp + j, fc_ref[0], 0.0), axis=1, keepdims=True) for j in range(2)]
    else:
        qs = (q_ref[0, :, :LANES], q_ref[0, :, LANES:])
    causal = (lax.broadcasted_iota(I32, (tq, tk), 1) <= lax.broadcasted_iota(I32, (tq, tk), 0))

    def logits(c, n, j, diagonal, _):
        start = pl.multiple_of(c * tk, tk)
        if fox:
            s = _nt_dot(qs[j], k_ref[0, pl.ds(start, n * tk), :])
            s = (s + fcol[j]) - fr_ref[0, pl.ds(2 * hp + j, 1), pl.ds(start, n * tk)]
        else:
            s = _nt_dot(qs[j], k_ref[0, pl.ds(start, n * tk), j * LANES:(j + 1) * LANES])
        return jnp.where(causal, s, NEG) if diagonal else s

    def values(c, n, j):
        return _values_with_ones(v_ref[0, pl.ds(pl.multiple_of(c * tk, tk), n * tk), :], j)

    _two_pass_attention(i, logits, values, s_ref, mx_ref, acc_ref, 2)
    o_ref[0] = _pair_output(acc_ref[0], acc_ref[1]).astype(o_ref.dtype)


def _causal_attn_call(arrs, fox, b, s, tq):
    scratch = [pltpu.VMEM((2, s // tq, tq, tq), F32), pltpu.VMEM((2, tq, LANES), F32),
               pltpu.VMEM((2, tq, LANES), F32)]
    if fox:
        qkv, fcol, frow = arrs
        operands = (qkv, qkv, qkv, fcol, frow)
        in_specs = [pl.BlockSpec((1, tq, LANES), lambda bi, hp, i: (bi, i, hp)),
                    pl.BlockSpec((1, s, LANES), lambda bi, hp, i: (bi, 0, 2 + hp)),
                    pl.BlockSpec((1, s, LANES), lambda bi, hp, i: (bi, 0, 4 + hp)),
                    pl.BlockSpec((1, tq, LANES), lambda bi, hp, i: (bi, i, 0)),
                    pl.BlockSpec((1, 8, s), lambda bi, hp, i: (bi, 0, 0))]
    else:
        operands = arrs
        in_specs = [pl.BlockSpec((1, tq, 2 * LANES), lambda bi, hp, i: (bi, i, hp)),
                    pl.BlockSpec((1, s, 2 * LANES), lambda bi, hp, i: (bi, 0, hp)),
                    pl.BlockSpec((1, s, LANES), lambda bi, hp, i: (bi, 0, hp))]
    return pl.pallas_call(
        functools.partial(_causal_attn_kernel, fox=fox, tq=tq),
        grid=(b, 2, s // tq),
        in_specs=in_specs,
        out_specs=pl.BlockSpec((1, tq, LANES), lambda bi, hp, i: (bi, i, hp)),
        out_shape=jax.ShapeDtypeStruct((b, s, 2 * LANES), BF16),
        scratch_shapes=scratch,
        compiler_params=pltpu.CompilerParams(dimension_semantics=("arbitrary",) * 3, vmem_limit_bytes=VMEM_LIMIT),
        name="attn_fox" if fox else "attn_mla",
    )(*operands)


def _alibi_slope(head):
    return lax.shift_left(jnp.int32(1), 7 - head).astype(F32) * (LOG2E * 2.0 ** -8)


def _swa_kernel(sink_ref, q_ref, k_ref, v_ref, o_ref, *, tq):
    hp = pl.program_id(1)
    i = pl.program_id(2)
    band = 2 * WINDOW
    rel = lax.broadcasted_iota(I32, (WINDOW, band), 0) - lax.broadcasted_iota(I32, (WINDOW, band), 1)
    for r in range(tq // WINDOW):
        q_start = i * tq + r * WINDOW
        k_start = pl.multiple_of(jnp.maximum(q_start - WINDOW, 0), WINDOW)
        kb = k_ref[0, pl.ds(k_start, band), :]
        vb = v_ref[0, pl.ds(k_start, band), :]
        dist = rel + (q_start - k_start)
        valid = (dist >= 0) & (dist < WINDOW)
        distf = dist.astype(F32)
        qs = _split_pair(q_ref[0, r * WINDOW:(r + 1) * WINDOW, :])
        acc, sink_term = [], []
        for j in range(2):
            sink = sink_ref[2 * hp + j] * LOG2E
            s = jnp.where(valid, _nt_dot(qs[j], kb) - _alibi_slope(2 * hp + j) * distf, NEG)
            m = jnp.maximum(jnp.max(s, axis=1, keepdims=True), sink)
            p = jnp.exp2(s - m).astype(BF16)
            acc.append(jnp.dot(p, _values_with_ones(vb, j), preferred_element_type=F32))
            sink_term.append(jnp.exp2(sink - m))
        o_ref[0, r * WINDOW:(r + 1) * WINDOW, :] = _pair_output(acc[0], acc[1], *sink_term).astype(o_ref.dtype)


def _swa_call(qkv, sinks, b, s, tq):
    return pl.pallas_call(
        functools.partial(_swa_kernel, tq=tq),
        grid=(b, 2, s // tq),
        in_specs=[pl.BlockSpec(memory_space=pltpu.SMEM),
                  pl.BlockSpec((1, tq, LANES), lambda bi, hp, i: (bi, i, hp)),
                  pl.BlockSpec((1, s, LANES), lambda bi, hp, i: (bi, 0, 2 + hp)),
                  pl.BlockSpec((1, s, LANES), lambda bi, hp, i: (bi, 0, 4 + hp))],
        out_specs=pl.BlockSpec((1, tq, LANES), lambda bi, hp, i: (bi, i, hp)),
        out_shape=jax.ShapeDtypeStruct((b, s, 2 * LANES), BF16),
        compiler_params=pltpu.CompilerParams(dimension_semantics=("arbitrary",) * 3, vmem_limit_bytes=VMEM_LIMIT),
        name="attn_swa",
    )(sinks, qkv, qkv, qkv)


def _dsa_kernel(q_ref, k_ref, v_ref, qi_ref, ki_ref, wi_ref, o_ref, keys_ref, hi_ref, lo_ref, thr_ref,
                s_ref, mx_ref, acc_ref, *, seq, topk, qb):
    kc = LANES
    kw = 2 * kc
    i = pl.program_id(1)
    nwide = (i + 1) * (qb // kw)
    lane = lax.broadcasted_iota(I32, (1, LANES), 1)
    key_row = lax.broadcasted_iota(I32, (kc, qb), 0)
    query_pos = i * qb + lax.broadcasted_iota(I32, (1, qb), 1)

    qi = qi_ref[0]
    wi_t = wi_ref[0].T
    qms, wrows = [], []
    for hd in range(IDX_HEADS):
        g, r = divmod(hd, LANES // IDX_DIM)
        sel = (lane >= r * IDX_DIM) & (lane < (r + 1) * IDX_DIM)
        qg = qi[:, g * LANES:(g + 1) * LANES]
        qms.append(jnp.where(sel, qg, jnp.zeros_like(qg)))
        wrows.append(wi_t[_WI_LANE + hd:_WI_LANE + hd + 1, :])
    q_all = jnp.concatenate(qms, axis=0)
    key_minus_query = (lax.broadcasted_iota(I32, (kw, qb), 0) - lax.broadcasted_iota(I32, (kw, qb), 1))

    def score_chunk(c, carry):
        start = pl.multiple_of(c * kw, kw)
        logit = _nt_dot(ki_ref[0, pl.ds(start, kw), :], q_all)
        sc = wrows[0] * jnp.maximum(logit[:, :qb], 0.0)
        for hd in range(1, IDX_HEADS):
            sc = sc + wrows[hd] * jnp.maximum(logit[:, hd * qb:(hd + 1) * qb], 0.0)
        sc = jnp.where(key_minus_query <= i * qb - start, sc, -jnp.inf)
        sc = jnp.where(sc == 0.0, 0.0, sc)
        bits = pltpu.bitcast(sc, I32)
        keys = bits ^ ((bits >> 31) & 0x7FFFFFFF)
        keys_ref[2 * c] = keys[:kc]
        keys_ref[2 * c + 1] = keys[kc:]
        hi = (keys >> 16).astype(I16)
        lo = ((keys & 0xFFFF) + I16_MIN).astype(I16)
        hi_ref[2 * c] = hi[:kc]
        hi_ref[2 * c + 1] = hi[kc:]
        lo_ref[2 * c] = lo[:kc]
        lo_ref[2 * c + 1] = lo[kc:]
        return carry

    lax.fori_loop(0, nwide, score_chunk, 0)

    keep_all = jnp.full((1, qb), KEY_NEG_INF + 1, I32)
    thr_ref[...] = jnp.broadcast_to(keep_all, thr_ref.shape)

    def count(pred):
        def body(c2, acc):
            parts = []
            for c in (2 * c2, 2 * c2 + 1):
                hit = jnp.where(pred(keys_ref[c], c), 1.0, 0.0)
                parts += [hit[r * SUBLANES:(r + 1) * SUBLANES] for r in range(kc // SUBLANES)]
            while len(parts) > 1:
                parts = [a + b for a, b in zip(parts[::2], parts[1::2])]
            return acc + parts[0]
        acc = lax.fori_loop(0, nwide, body, jnp.zeros((SUBLANES, qb), F32))
        return jnp.sum(acc, axis=0, keepdims=True)

    @pl.when((i + 1) * qb > topk)
    def _():
        kf = float(topk)

        pack = 2 * SUBLANES

        def count16(plane_ref, cand):
            cand16 = cand.astype(I16)

            def body(c2, acc):
                parts = []
                for c in (2 * c2, 2 * c2 + 1):
                    hit = jnp.where(plane_ref[c] >= cand16, jnp.int16(1), jnp.int16(0))
                    parts += [hit[r * pack:(r + 1) * pack] for r in range(kc // pack)]
                while len(parts) > 1:
                    parts = [a + b for a, b in zip(parts[::2], parts[1::2])]
                return acc + parts[0]

            acc = lax.fori_loop(0, nwide, body, jnp.zeros((pack, qb), I16))
            return jnp.sum(acc.astype(F32), axis=0, keepdims=True)

        def search16(plane_ref, cnt_at_floor):
            def bit_step(it, carry):
                val, cnt_val = carry
                cand = val + lax.shift_left(jnp.int32(1), 15 - it)
                cnt = count16(plane_ref, cand)
                ok = cnt >= kf
                return jnp.where(ok, cand, val), jnp.where(ok, cnt, cnt_val)

            return lax.fori_loop(0, 16, bit_step, (jnp.full((1, qb), I16_MIN, I32), cnt_at_floor))

        thr_hi, cnt_hi = search16(hi_ref, jnp.full((1, qb), float(seq), F32))
        thr_hi16 = thr_hi.astype(I16)

        def pin(c, carry):
            h = hi_ref[c]
            lo_ref[c] = jnp.where(h > thr_hi16, jnp.int16(I16_MAX),
                                  jnp.where(h == thr_hi16, lo_ref[c], jnp.int16(I16_MIN)))
            return carry

        lax.fori_loop(0, 2 * nwide, pin, 0)
        thr_lo, cnt_thr = search16(lo_ref, cnt_hi)
        thr = thr_hi * 65536 + (thr_lo - I16_MIN)
        thr_ref[...] = jnp.broadcast_to(jnp.where(query_pos < topk, keep_all, thr), thr_ref.shape)

        @pl.when(jnp.max(cnt_thr) > kf)
        def _():
            need = kf - count(lambda kk, c: kk > thr)

            def idx_step(it, pos):
                cand = pos + lax.shift_left(jnp.int32(1), (seq.bit_length() - 2) - it)
                cnt = count(lambda kk, c: (kk == thr) & (key_row + c * kc < cand))
                return jnp.where(cnt < need, cand, pos)

            pos = lax.fori_loop(0, seq.bit_length() - 1, idx_step, jnp.zeros((1, qb), I32))

            def demote(c, carry):
                kk = keys_ref[c]
                keys_ref[c] = jnp.where((kk == thr) & (key_row + c * kc > pos), kk - 1, kk)
                return carry

            lax.fori_loop(0, 2 * nwide, demote, 0)

    q = q_ref[0]
    qs = _split_pair(q[:, :LANES]) + _split_pair(q[:, LANES:])
    thr_row = thr_ref[0:1, :]
    rel = (lax.broadcasted_iota(I32, (qb, 2 * kw), 0) - lax.broadcasted_iota(I32, (qb, 2 * kw), 1))

    def shared(c, n):
        halves = [jnp.where(keys_ref[2 * c + h] >= thr_row, 0.0, NEG).T for h in range(2 * n)]
        distf = (rel[:, :n * kw] + (i * qb - c * kw)).astype(F32)
        return jnp.concatenate(halves, axis=1), distf

    def logits(c, n, hd, diagonal, ctx):
        bias, distf = ctx
        start = pl.multiple_of(c * kw, kw)
        s = _nt_dot(qs[hd], k_ref[0, pl.ds(start, n * kw), :]) - (LOG2E * 2.0 ** -(N_HEADS + hd + 1)) * distf
        return s + bias

    def values(c, n, hd):
        return _values_with_ones(v_ref[0, pl.ds(pl.multiple_of(c * kw, kw), n * kw), :], hd % 2)

    _two_pass_attention(nwide - 1, logits, values, s_ref, mx_ref, acc_ref, N_HEADS, shared)
    o_ref[0] = jnp.concatenate([_pair_output(acc_ref[0], acc_ref[1]), _pair_output(acc_ref[2], acc_ref[3])],
                               axis=1).astype(o_ref.dtype)


def _dsa_call(od3, small3, b, s, topk, qb):
    kc = LANES
    scratch = [pltpu.VMEM((s // kc, kc, qb), I32), pltpu.VMEM((s // kc, kc, qb), I16),
               pltpu.VMEM((s // kc, kc, qb), I16), pltpu.VMEM((SUBLANES, qb), I32),
               pltpu.VMEM((N_HEADS, s // (2 * kc), qb, 2 * kc), F32),
               pltpu.VMEM((N_HEADS, qb, LANES), F32), pltpu.VMEM((N_HEADS, qb, LANES), F32)]
    return pl.pallas_call(
        functools.partial(_dsa_kernel, seq=s, topk=topk, qb=qb),
        grid=(b, s // qb),
        in_specs=[pl.BlockSpec((1, qb, 2 * LANES), lambda bi, i: (bi, i, 0)),
                  pl.BlockSpec((1, s, LANES), lambda bi, i: (bi, 0, 2)),
                  pl.BlockSpec((1, s, LANES), lambda bi, i: (bi, 0, 3)),
                  pl.BlockSpec((1, qb, 2 * LANES), lambda bi, i: (bi, i, 2)),
                  pl.BlockSpec((1, s, LANES), lambda bi, i: (bi, 0, 6)),
                  pl.BlockSpec((1, qb, LANES), lambda bi, i: (bi, i, 0))],
        out_specs=pl.BlockSpec((1, qb, 2 * LANES), lambda bi, i: (bi, i, 0)),
        out_shape=jax.ShapeDtypeStruct((b, s, 2 * LANES), BF16),
        scratch_shapes=scratch,
        compiler_params=pltpu.CompilerParams(dimension_semantics=("arbitrary",) * 2, vmem_limit_bytes=VMEM_LIMIT),
        name="attn_dsa",
    )(od3, od3, od3, od3, od3, small3)


def _outproj_kernel(x_ref, oa_ref, ob_ref, oc_ref, od_ref, w_ref, mod_ref, g2_ref, wr_ref, br_ref,
                    xo_ref, h_ref, rt_ref):
    mix = None
    for n, o_ref in enumerate((oa_ref, ob_ref, oc_ref, od_ref)):
        part = jnp.dot(o_ref[...], w_ref[n * 2 * LANES:(n + 1) * 2 * LANES, :], preferred_element_type=F32)
        mix = part if mix is None else mix + part
    xn = x_ref[...] + mod_ref[0, 2:3, :] * mix
    xo_ref[...] = xn
    h = _rms(xn, g2_ref[...]) * (1.0 + mod_ref[0, 4:5, :]) + mod_ref[0, 3:4, :]
    h_ref[...] = h

    logits = lax.dot_general(wr_ref[...], h, (((1,), (1,)), ((), ())), precision=lax.Precision.HIGHEST,
                             preferred_element_type=F32)
    score = 1.0 / (1.0 + jnp.exp(-logits))
    biased = score + br_ref[...]
    srow = [score[e:e + 1, :] for e in range(N_EXPERTS)]
    brow = [biased[e:e + 1, :] for e in range(N_EXPERTS)]
    per = N_EXPERTS // N_GROUPS
    best_v = best_g = None
    for g in range(N_GROUPS):
        r = brow[g * per:(g + 1) * per]
        top2 = None
        for a in range(per):
            for c in range(a + 1, per):
                top2 = r[a] + r[c] if top2 is None else jnp.maximum(top2, r[a] + r[c])
        if g == 0:
            best_v, best_g = top2, jnp.zeros_like(top2, dtype=I32)
        else:
            up = top2 > best_v
            best_v = jnp.where(up, top2, best_v)
            best_g = jnp.where(up, g, best_g)
    cand = [jnp.where(best_g == e // per, brow[e], -jnp.inf) for e in range(N_EXPERTS)]

    def first_max(vals):
        v, idx = vals[0], jnp.zeros_like(best_g)
        for e in range(1, N_EXPERTS):
            up = vals[e] > v
            v = jnp.where(up, vals[e], v)
            idx = jnp.where(up, e, idx)
        return idx

    i1 = first_max(cand)
    i2 = first_max([jnp.where(i1 == e, -jnp.inf, cand[e]) for e in range(N_EXPERTS)])
    s1 = sum(jnp.where(i1 == e, srow[e], 0.0) for e in range(N_EXPERTS))
    s2 = sum(jnp.where(i2 == e, srow[e], 0.0) for e in range(N_EXPERTS))
    den = s1 + s2
    first_low = i1 < i2
    la = jnp.minimum(i1, i2) - per * best_g
    lb = jnp.maximum(i1, i2) - per * best_g
    pair = jnp.where(la == 0, lb - 1, jnp.where(la == 1, lb + 1, 5))
    rt_ref[...] = jnp.zeros(rt_ref.shape, F32)
    rt_ref[0:1, :] = (best_g * N_PAIRS + pair).astype(F32)
    rt_ref[1:2, :] = jnp.where(first_low, s1, s2) / den
    rt_ref[2:3, :] = jnp.where(first_low, s2, s1) / den


def _outproj_call(x2, outs, w_out, mod_l, g2, wr_t, br, seq, tm):
    t, d = x2.shape
    tpb = seq // tm
    row = lambda n: pl.BlockSpec((tm, n), lambda i: (i, 0))
    const = lambda shape: pl.BlockSpec(shape, lambda i: (0,) * len(shape))
    return pl.pallas_call(
        _outproj_kernel,
        grid=(t // tm,),
        in_specs=[row(d)] + [row(2 * LANES)] * 4 + [
            const(w_out.shape), pl.BlockSpec((1, 6, d), lambda i: (i // tpb, 0, 0)), const((1, d)),
            const(wr_t.shape), const(br.shape)],
        out_specs=[row(d), row(d), pl.BlockSpec((SUBLANES, tm), lambda i: (0, i))],
        out_shape=[jax.ShapeDtypeStruct((t, d), F32), jax.ShapeDtypeStruct((t, d), F32),
                   jax.ShapeDtypeStruct((SUBLANES, t), F32)],
        compiler_params=pltpu.CompilerParams(dimension_semantics=("arbitrary",), vmem_limit_bytes=VMEM_LIMIT),
        name="outproj_router",
    )(x2, *outs, w_out, mod_l, g2, wr_t, br)


def _routing_tables(cls, w_pair, n_steps):
    t = cls.shape[0]
    onehot = (cls[:, None] == jnp.arange(N_CLASSES, dtype=I32)[None, :]).astype(I32)
    upto = jnp.cumsum(onehot, axis=0)
    rank = jnp.sum((upto - onehot) * onehot, axis=1)
    padded = (upto[-1] + MOE_ROWS - 1) // MOE_ROWS * MOE_ROWS
    ends = jnp.cumsum(padded)
    pos = jnp.sum(onehot * (ends - padded)[None, :], axis=1) + rank
    step_cls = jnp.sum((jnp.arange(n_steps, dtype=I32) * MOE_ROWS)[:, None] >= ends[None, :], axis=1)
    valid = (step_cls < N_CLASSES).astype(I32)
    step_cls = jnp.minimum(step_cls, N_CLASSES - 1)
    base = (step_cls // N_PAIRS) * (N_EXPERTS // N_GROUPS)
    e_low = base + jnp.asarray(PAIR_LOW, I32)[step_cls % N_PAIRS]
    e_high = base + jnp.asarray(PAIR_HIGH, I32)[step_cls % N_PAIRS]
    w_sorted = jnp.zeros((n_steps * MOE_ROWS, 2), F32).at[pos].set(w_pair)
    return pos.reshape(t // PERM_ROWS, 1, PERM_ROWS), e_low, e_high, valid, w_sorted


def _row_copy(src_ref, src_row, dst_ref, dst_row, sem):
    return pltpu.make_async_copy(src_ref.at[pl.ds(src_row, 1)], dst_ref.at[pl.ds(dst_row, 1)], sem)


def _rows_done(src_ref, dst_ref, n, sem):
    pltpu.make_async_copy(src_ref.at[pl.ds(0, n)], dst_ref.at[pl.ds(0, n)], sem).wait()


def _scatter_rows_kernel(pos_ref, src_ref, init_ref, dst_ref, sem):
    del init_ref
    first = pl.program_id(0) * PERM_ROWS

    def issue(r, carry):
        _row_copy(src_ref, first + r, dst_ref, pos_ref[0, 0, r], sem).start()
        return carry

    lax.fori_loop(0, PERM_ROWS, issue, 0)
    _rows_done(src_ref, dst_ref, PERM_ROWS, sem)


def _scatter_rows_call(pos3, src, n_sorted):
    t, d = src.shape
    return pl.pallas_call(
        _scatter_rows_kernel,
        grid=(t // PERM_ROWS,),
        in_specs=[pl.BlockSpec((1, 1, PERM_ROWS), lambda j: (j, 0, 0), memory_space=pltpu.SMEM),
                  pl.BlockSpec(memory_space=pl.ANY), pl.BlockSpec(memory_space=pl.ANY)],
        out_specs=pl.BlockSpec(memory_space=pl.ANY),
        out_shape=jax.ShapeDtypeStruct((n_sorted, d), src.dtype),
        scratch_shapes=[pltpu.SemaphoreType.DMA(())],
        input_output_aliases={2: 0},
        compiler_params=pltpu.CompilerParams(dimension_semantics=("arbitrary",), has_side_effects=True),
        name="moe_sort_rows",
    )(pos3, src, jnp.zeros((n_sorted, d), src.dtype))


def _moe_kernel(e_low_ref, e_high_ref, valid_ref, h_ref, w_ref, wgu_low_ref, wgu_high_ref, wd_low_ref, wd_high_ref,
                o_ref):
    del e_low_ref, e_high_ref
    step = pl.program_id(0)

    @pl.when(valid_ref[step] == 0)
    def _():
        o_ref[...] = jnp.zeros(o_ref.shape, F32)

    @pl.when(valid_ref[step] != 0)
    def _():
        h = h_ref[...].astype(BF16)

        def expert(wgu_ref, wd_ref):
            gu = jnp.dot(h, wgu_ref[0], preferred_element_type=F32)
            gate, up = gu[:, :D_EXPERT], gu[:, D_EXPERT:]
            hid = (gate / (1.0 + jnp.exp(-gate)) * up).astype(BF16)
            return jnp.dot(hid, wd_ref[0], preferred_element_type=F32)

        o_ref[...] = (w_ref[:, 0:1] * expert(wgu_low_ref, wd_low_ref)
                      + w_ref[:, 1:2] * expert(wgu_high_ref, wd_high_ref))


def _moe_call(hs, w_sorted, e_low, e_high, valid, wgu, wd):
    n_sorted, d = hs.shape
    grid_spec = pltpu.PrefetchScalarGridSpec(
        num_scalar_prefetch=3,
        grid=(n_sorted // MOE_ROWS,),
        in_specs=[pl.BlockSpec((MOE_ROWS, d), lambda j, lo, hi, ok: (j, 0)),
                  pl.BlockSpec((MOE_ROWS, 2), lambda j, lo, hi, ok: (j, 0)),
                  pl.BlockSpec((1, d, 2 * D_EXPERT), lambda j, lo, hi, ok: (lo[j], 0, 0)),
                  pl.BlockSpec((1, d, 2 * D_EXPERT), lambda j, lo, hi, ok: (hi[j], 0, 0)),
                  pl.BlockSpec((1, D_EXPERT, d), lambda j, lo, hi, ok: (lo[j], 0, 0)),
                  pl.BlockSpec((1, D_EXPERT, d), lambda j, lo, hi, ok: (hi[j], 0, 0))],
        out_specs=pl.BlockSpec((MOE_ROWS, d), lambda j, lo, hi, ok: (j, 0)))
    return pl.pallas_call(
        _moe_kernel,
        grid_spec=grid_spec,
        out_shape=jax.ShapeDtypeStruct((n_sorted, d), F32),
        compiler_params=pltpu.CompilerParams(dimension_semantics=("arbitrary",), vmem_limit_bytes=VMEM_LIMIT),
        name="moe",
    )(e_low, e_high, valid, hs, w_sorted, wgu, wgu, wd, wd)


def _gather_residual_kernel(pos_ref, ys_ref, x_ref, mod_ref, gf_ref, o_ref, rows_ref, sem, *, final):
    def issue(r, carry):
        _row_copy(ys_ref, pos_ref[0, 0, r], rows_ref, r, sem).start()
        return carry

    lax.fori_loop(0, PERM_ROWS, issue, 0)
    _rows_done(ys_ref, rows_ref, PERM_ROWS, sem)
    xn = x_ref[...] + mod_ref[0, 5:6, :] * rows_ref[...]
    o_ref[...] = _rms(xn, gf_ref[...]) if final else xn


def _gather_residual_call(pos3, ys, x2, mod_l, gf, seq, final):
    t, d = x2.shape
    tpb = seq // PERM_ROWS
    return pl.pallas_call(
        functools.partial(_gather_residual_kernel, final=final),
        grid=(t // PERM_ROWS,),
        in_specs=[pl.BlockSpec((1, 1, PERM_ROWS), lambda j: (j, 0, 0), memory_space=pltpu.SMEM),
                  pl.BlockSpec(memory_space=pl.ANY),
                  pl.BlockSpec((PERM_ROWS, d), lambda j: (j, 0)),
                  pl.BlockSpec((1, 6, d), lambda j: (j // tpb, 0, 0)),
                  pl.BlockSpec((1, d), lambda j: (0, 0))],
        out_specs=pl.BlockSpec((PERM_ROWS, d), lambda j: (j, 0)),
        out_shape=jax.ShapeDtypeStruct((t, d), F32),
        scratch_shapes=[pltpu.VMEM((PERM_ROWS, d), F32), pltpu.SemaphoreType.DMA(())],
        compiler_params=pltpu.CompilerParams(dimension_semantics=("arbitrary",), vmem_limit_bytes=VMEM_LIMIT),
        name="moe_unsort_residual",
    )(pos3, ys, x2, mod_l, gf)


def _swap_half(w):
    half = w.shape[1] // 2
    return jnp.concatenate([-w[:, half:], w[:, :half]], axis=1)


def _layer_weights(w_in, w_q_up, w_kv_up):
    d = w_in.shape[0]
    pts = np.cumsum(IN_SIZES)[:-1].tolist()
    (a_q, a_k, a_v, a_f, b_cq, b_ckv, b_kr, c_q, c_k, c_v,
     d_q, d_k, d_v, d_qi, d_ki, d_wi) = jnp.split(w_in, pts, axis=1)
    qs = HEAD_DIM ** -0.5
    dup = lambda w: jnp.concatenate([w[:, :HEAD_DIM], w[:, :HEAD_DIM], w[:, HEAD_DIM:], w[:, HEAD_DIM:]], axis=1)
    small = jnp.concatenate([
        a_f, d_wi * ((IDX_HEADS * IDX_DIM) ** -0.5),
        jnp.zeros((d, _KR_LANE - _WI_LANE - IDX_HEADS), F32),
        b_kr, _swap_half(b_kr), jnp.zeros((d, LANES - _KR_LANE - 2 * MLA_ROPE), F32)], axis=1)
    w_all = jnp.concatenate([
        a_q * qs, a_k, a_v,
        c_q * qs, dup(c_k), dup(c_v),
        d_q * qs, d_k, d_k, d_v, d_v, d_qi, d_ki, d_ki, d_ki, d_ki,
        small, b_cq, b_ckv], axis=1).astype(BF16)

    per_q = MLA_NOPE + MLA_ROPE
    wq = []
    for hd in range(N_HEADS):
        blk = w_q_up[:, hd * per_q:(hd + 1) * per_q]
        rot = blk[:, MLA_NOPE:]
        wq += [blk[:, :MLA_NOPE], rot, _swap_half(rot)]
    wq = jnp.concatenate(wq, axis=1).astype(BF16)

    place = np.zeros((LANES, LANES), np.float32)
    place[_KR_LANE + np.arange(MLA_ROPE), MLA_NOPE + np.arange(MLA_ROPE)] = 1.0
    place = jnp.asarray(place)
    kcols, vcols = [], []
    for hd in range(N_HEADS):
        blk = w_kv_up[:, hd * 2 * HEAD_DIM:(hd + 1) * 2 * HEAD_DIM]
        knope = jnp.concatenate([blk[:, :MLA_NOPE], jnp.zeros((MLA_KV_RANK, LANES - MLA_NOPE), F32)], axis=1)
        kcols.append(jnp.concatenate([knope, place], axis=0))
        vcols.append(jnp.concatenate([blk[:, MLA_NOPE:], jnp.zeros((LANES, HEAD_DIM), F32)], axis=0))
    wkv = jnp.concatenate(kcols + vcols, axis=1).astype(BF16)
    return w_all, wq, wkv


def _rope_tables(seq):
    half = MLA_ROPE // 2
    inv = ROPE_THETA ** (-jnp.arange(half, dtype=F32) / half)
    ang = jnp.arange(seq, dtype=F32)[:, None] * inv[None, :]
    cos = jnp.tile(jnp.cos(ang), (1, 2))
    sin = jnp.tile(jnp.sin(ang), (1, 2))
    scale = LOG2E * (MLA_NOPE + MLA_ROPE) ** -0.5
    z = lambda n: jnp.zeros((seq, n), F32)
    tab_q = jnp.concatenate([jnp.full((seq, MLA_NOPE), scale, F32), cos * scale, z(MLA_ROPE)], axis=1)
    tab_qs = jnp.concatenate([z(MLA_NOPE), sin * scale, z(MLA_ROPE)], axis=1)
    tab_k = jnp.concatenate([z(_KR_LANE), cos, z(LANES - _KR_LANE - MLA_ROPE)], axis=1)
    tab_ks = jnp.concatenate([z(_KR_LANE), sin, z(LANES - _KR_LANE - MLA_ROPE)], axis=1)
    return jnp.stack([tab_q, tab_qs, tab_k, tab_ks])


def kernel(x, c, w_ada, b_ada, g_norm1, w_in, b_forget, g_q_mla, w_q_up, g_kv_mla, w_kv_up, sinks, w_out,
           g_norm2, w_router, b_router, w_gate, w_up, w_down, g_final):
    b, s, d = x.shape
    depth = w_in.shape[0]
    t = b * s
    topk = min(TOPK_MAX, s // 4)
    tm = min(1024, s)
    tq = min(512, s)
    tq_swa = min(512, s)
    qb_dsa = 2 * LANES
    n_steps = t // MOE_ROWS + N_CLASSES
    assert s % tm == 0 and s % tq == 0 and s % qb_dsa == 0 and s % PERM_ROWS == 0

    mod = _ada_call(c, w_ada, b_ada).reshape(depth, b, 6, d)
    tabs = _rope_tables(s)
    wr_t = w_router.T
    br = b_router.reshape(N_EXPERTS, 1)
    x2 = x.reshape(t, d)
    for l in range(depth):
        w_all, wq, wkv = _layer_weights(w_in[l], w_q_up[l], w_kv_up[l])
        oa, oc, od, osm, oqb, okb, ovb = _inproj_call(
            x2, mod[l], g_norm1[l].reshape(1, d), w_all, wq, wkv,
            g_q_mla[l].reshape(1, -1), g_kv_mla[l].reshape(1, -1), tabs, s, tm)
        small3 = osm.reshape(b, s, LANES)
        bf = jnp.zeros((1, LANES), F32).at[0, :N_HEADS].set(b_forget[l])
        fcum = _fox_cumsum_call(small3, bf)
        frow = jnp.swapaxes(fcum[:, :, :8], 1, 2)
        out_a = _causal_attn_call((oa.reshape(b, s, -1), fcum, frow), True, b, s, tq)
        out_b = _causal_attn_call((oqb.reshape(b, s, -1), okb.reshape(b, s, -1), ovb.reshape(b, s, -1)),
                                  False, b, s, tq)
        out_c = _swa_call(oc.reshape(b, s, -1), sinks[l], b, s, tq_swa)
        out_d = _dsa_call(od.reshape(b, s, -1), small3, b, s, topk, qb_dsa)
        outs = [o.reshape(t, 2 * LANES) for o in (out_a, out_b, out_c, out_d)]
        x2, h2, route = _outproj_call(x2, outs, w_out[l].astype(BF16), mod[l], g_norm2[l].reshape(1, d),
                                      wr_t, br, s, tm)
        pos3, e_low, e_high, valid, w_sorted = _routing_tables(route[0].astype(I32), route[1:3].T, n_steps)
        wgu = jnp.concatenate([w_gate[l], w_up[l]], axis=2).astype(BF16)
        hs = _scatter_rows_call(pos3, h2, n_steps * MOE_ROWS)
        ys = _moe_call(hs, w_sorted, e_low, e_high, valid, wgu, w_down[l].astype(BF16))
        x2 = _gather_residual_call(pos3, ys, x2, mod[l], g_final.reshape(1, d), s, l == depth - 1)
    return x2.reshape(b, s, d)
```

```python
import functools

import jax
import jax.numpy as jnp
import numpy as np
from jax import lax
from jax.experimental import pallas as pl
from jax.experimental.pallas import tpu as pltpu

F32 = jnp.float32
BF16 = jnp.bfloat16
I32 = jnp.int32
I16 = jnp.int16
I16_MIN, I16_MAX = -(2 ** 15), 2 ** 15 - 1

EPS = 1e-6
HEAD_DIM = 64
LANES = 128
SUBLANES = 8
N_HEADS = 4
MLA_Q_RANK = 256
MLA_KV_RANK = 128
MLA_NOPE = 64
MLA_ROPE = 32
ROPE_THETA = 10000.0
WINDOW = 128
IDX_HEADS = 8
IDX_DIM = 32
TOPK_MAX = 256
N_EXPERTS = 16
N_GROUPS = 4
N_PAIRS = 6
N_CLASSES = N_GROUPS * N_PAIRS
PAIR_LOW = (0, 0, 0, 1, 1, 2)
PAIR_HIGH = (1, 2, 3, 2, 3, 3)
D_EXPERT = 256
MOE_ROWS = 256
PERM_ROWS = 512
ROW_UNROLL = 8
IN_SIZES = (256, 256, 256, 4, 256, 128, 32, 256, 128, 128, 256, 64, 64, 256, 32, 8)

LOG2E = 1.4426950408889634
NEG = -1e30
INT_MIN = -(2 ** 31)
KEY_NEG_INF = INT_MIN + 0x7FFFFF
VMEM_LIMIT = 56 * 1024 * 1024

_CA, _CC, _CD, _CS, _CQ, _CKV, _CEND = 0, 768, 1536, 2432, 2560, 2816, 2944
_F_LANE, _WI_LANE, _KR_LANE = 0, 4, 32


def _nt_dot(a, b):
    return lax.dot_general(a, b, (((1,), (1,)), ((), ())), preferred_element_type=F32)


def _rms(x, g):
    return x * lax.rsqrt(jnp.mean(x * x, axis=-1, keepdims=True) + EPS) * g


def _ada_kernel(c_ref, w_ref, b_ref, o_ref):
    c = c_ref[...]
    act = (c / (1.0 + jnp.exp(-c))).astype(BF16)
    o_ref[0] = jnp.dot(act, w_ref[0].astype(BF16), preferred_element_type=F32) + b_ref[0]


def _ada_call(c, w_ada, b_ada):
    depth, d, n = w_ada.shape
    bsz = c.shape[0]
    tn = 1024
    return pl.pallas_call(
        _ada_kernel,
        grid=(depth, n // tn),
        in_specs=[pl.BlockSpec((bsz, d), lambda l, j: (0, 0)),
                  pl.BlockSpec((1, d, tn), lambda l, j: (l, 0, j)),
                  pl.BlockSpec((1, 1, tn), lambda l, j: (l, 0, j))],
        out_specs=pl.BlockSpec((1, bsz, tn), lambda l, j: (l, 0, j)),
        out_shape=jax.ShapeDtypeStruct((depth, bsz, n), F32),
        compiler_params=pltpu.CompilerParams(dimension_semantics=("arbitrary", "arbitrary"),
                                             vmem_limit_bytes=VMEM_LIMIT),
        name="adaln",
    )(c, w_ada, b_ada.reshape(depth, 1, n))


def _inproj_kernel(x_ref, mod_ref, g1_ref, w_ref, wq_ref, wkv_ref, gq_ref, gkv_ref, tab_ref,
                   oa_ref, oc_ref, od_ref, os_ref, oqb_ref, okb_ref, ovb_ref):
    h = _rms(x_ref[...], g1_ref[...]) * (1.0 + mod_ref[0, 1:2, :]) + mod_ref[0, 0:1, :]
    h = h.astype(BF16)

    def proj(lo, hi):
        return jnp.dot(h, w_ref[:, lo:hi], preferred_element_type=F32)

    q_width = N_HEADS * HEAD_DIM
    for o_ref, lo, hi in ((oa_ref, _CA, _CC), (oc_ref, _CC, _CD), (od_ref, _CD, _CS)):
        o_ref[:, :q_width] = (proj(lo, lo + q_width) * LOG2E).astype(BF16)
        o_ref[:, q_width:] = proj(lo + q_width, hi).astype(BF16)
    small = proj(_CS, _CQ)
    os_ref[...] = small

    cq = _rms(proj(_CQ, _CKV), gq_ref[...]).astype(BF16)
    qf = jnp.dot(cq, wq_ref[...], preferred_element_type=F32)
    tab_q, tab_qs, tab_k, tab_ks = tab_ref[0], tab_ref[1], tab_ref[2], tab_ref[3]
    for hd in range(N_HEADS):
        qg = qf[:, hd * LANES:(hd + 1) * LANES]
        qr = qg * tab_q + pltpu.roll(qg, LANES - MLA_ROPE, axis=1) * tab_qs
        oqb_ref[:, hd * LANES:(hd + 1) * LANES] = qr.astype(BF16)

    ckv = _rms(proj(_CKV, _CEND), gkv_ref[...]).astype(BF16)
    kr = (small * tab_k + pltpu.roll(small, LANES - MLA_ROPE, axis=1) * tab_ks).astype(BF16)
    kvf = jnp.dot(jnp.concatenate([ckv, kr], axis=1), wkv_ref[...], preferred_element_type=F32)
    okb_ref[...] = kvf[:, :4 * LANES].astype(BF16)
    ovb_ref[...] = kvf[:, 4 * LANES:].astype(BF16)


def _inproj_call(x2, mod_l, g1, w_all, wq, wkv, gq, gkv, tabs, seq, tm):
    t, d = x2.shape
    tpb = seq // tm
    row = lambda n: pl.BlockSpec((tm, n), lambda i: (i, 0))
    const = lambda shape: pl.BlockSpec(shape, lambda i: (0,) * len(shape))
    widths = (768, 768, 896, 128, 512, 512, 256)
    dtypes = (BF16, BF16, BF16, F32, BF16, BF16, BF16)
    return pl.pallas_call(
        _inproj_kernel,
        grid=(t // tm,),
        in_specs=[row(d),
                  pl.BlockSpec((1, 6, d), lambda i: (i // tpb, 0, 0)),
                  const((1, d)), const(w_all.shape), const(wq.shape), const(wkv.shape),
                  const((1, MLA_Q_RANK)), const((1, MLA_KV_RANK)),
                  pl.BlockSpec((4, tm, LANES), lambda i: (0, i % tpb, 0))],
        out_specs=[row(n) for n in widths],
        out_shape=[jax.ShapeDtypeStruct((t, n), dt) for n, dt in zip(widths, dtypes)],
        compiler_params=pltpu.CompilerParams(dimension_semantics=("arbitrary",), vmem_limit_bytes=VMEM_LIMIT),
        name="inproj",
    )(x2, mod_l, g1, w_all, wq, wkv, gq, gkv, tabs)


def _fox_cumsum_kernel(s_ref, b_ref, o_ref):
    z = s_ref[0] + b_ref[...]
    lf = jnp.minimum(z, 0.0) - jnp.log(1.0 + jnp.exp(-jnp.abs(z)))
    n = lf.shape[0]
    row = lax.broadcasted_iota(I32, lf.shape, 0)
    d = 1
    while d < n:
        lf = lf + jnp.where(row >= d, pltpu.roll(lf, d, axis=0), 0.0)
        d *= 2
    o_ref[0] = lf * LOG2E


def _fox_cumsum_call(small3, bf):
    b, s, _ = small3.shape
    return pl.pallas_call(
        _fox_cumsum_kernel,
        grid=(b,),
        in_specs=[pl.BlockSpec((1, s, LANES), lambda i: (i, 0, 0)),
                  pl.BlockSpec((1, LANES), lambda i: (0, 0))],
        out_specs=pl.BlockSpec((1, s, LANES), lambda i: (i, 0, 0)),
        out_shape=jax.ShapeDtypeStruct((b, s, LANES), F32),
        compiler_params=pltpu.CompilerParams(dimension_semantics=("arbitrary",), vmem_limit_bytes=VMEM_LIMIT),
        name="fox_cumsum",
    )(small3, bf)


def _lane_is_low():
    return lax.broadcasted_iota(I32, (1, LANES), 1) < HEAD_DIM


def _split_pair(q):
    low = _lane_is_low()
    zero = jnp.zeros_like(q)
    return jnp.where(low, q, zero), jnp.where(low, zero, q)


def _values_with_ones(v, j):
    low = _lane_is_low()
    return jnp.where(low if j == 0 else ~low, v, jnp.ones_like(v))


def _pair_output(a0, a1, extra0=None, extra1=None):
    l0 = pltpu.roll(a0, HEAD_DIM, axis=1)
    l1 = pltpu.roll(a1, HEAD_DIM, axis=1)
    if extra0 is not None:
        l0, l1 = l0 + extra0, l1 + extra1
    return jnp.where(_lane_is_low(), a0 / l0, a1 / l1)


def _two_pass_attention(n_before, logits, values, s_ref, mx_ref, acc_ref, n_heads, shared=None):
    tq, tk = s_ref.shape[-2:]
    mx_ref[...] = jnp.full(mx_ref.shape, NEG, F32)

    def store(c, n, diagonal):
        ctx = shared(c, n) if shared is not None else None
        for j in range(n_heads):
            s = logits(c, n, j, diagonal, ctx)
            for t in range(n):
                s_ref[j, c + t] = s[:, t * tk:(t + 1) * tk]
            parts = [s[:, blk * LANES:(blk + 1) * LANES] for blk in range(n * tk // LANES)]
            while len(parts) > 1:
                parts = [jnp.maximum(a, b) for a, b in zip(parts[::2], parts[1::2])]
            mx_ref[j] = jnp.maximum(mx_ref[j], parts[0])

    def store_two(c2, carry):
        store(2 * c2, 2, False)
        return carry

    lax.fori_loop(0, n_before // 2, store_two, 0)

    @pl.when(n_before % 2 == 1)
    def _():
        store(n_before - 1, 1, False)

    store(n_before, 1, True)

    row_max = [jnp.max(mx_ref[j], axis=1, keepdims=True) for j in range(n_heads)]
    shift = [jnp.broadcast_to(m, (tq, tk)) for m in row_max]
    acc_ref[...] = jnp.zeros(acc_ref.shape, F32)

    def accum(c, n):
        for j in range(n_heads):
            p = [jnp.exp2(s_ref[j, c + t] - shift[j]).astype(BF16) for t in range(n)]
            p = p[0] if n == 1 else jnp.concatenate(p, axis=1)
            acc_ref[j] += jnp.dot(p, values(c, n, j), preferred_element_type=F32)

    def accum_two(c2, carry):
        accum(2 * c2, 2)
        return carry

    lax.fori_loop(0, (n_before + 1) // 2, accum_two, 0)

    @pl.when(n_before % 2 == 0)
    def _():
        accum(n_before, 1)

    return row_max


def _causal_attn_kernel(*refs, fox, tq):
    if fox:
        q_ref, k_ref, v_ref, fc_ref, fr_ref, o_ref, s_ref, mx_ref, acc_ref = refs
    else:
        q_ref, k_ref, v_ref, o_ref, s_ref, mx_ref, acc_ref = refs
    tk = tq
    hp = pl.program_id(1)
    i = pl.program_id(2)
    if fox:
        qs = _split_pair(q_ref[0])
        lane = lax.broadcasted_iota(I32, (1, LANES), 1)
        fcol = [jnp.sum(jnp.where(lane == 2 * hp + j, fc_ref[0], 0.0), axis=1, keepdims=True) for j in range(2)]
    else:
        qs = (q_ref[0, :, :LANES], q_ref[0, :, LANES:])
    causal = (lax.broadcasted_iota(I32, (tq, tk), 1) <= lax.broadcasted_iota(I32, (tq, tk), 0))

    def logits(c, n, j, diagonal, _):
        start = pl.multiple_of(c * tk, tk)
        if fox:
            s = _nt_dot(qs[j], k_ref[0, pl.ds(start, n * tk), :])
            s = (s + fcol[j]) - fr_ref[0, pl.ds(2 * hp + j, 1), pl.ds(start, n * tk)]
        else:
            s = _nt_dot(qs[j], k_ref[0, pl.ds(start, n * tk), j * LANES:(j + 1) * LANES])
        return jnp.where(causal, s, NEG) if diagonal else s

    def values(c, n, j):
        return _values_with_ones(v_ref[0, pl.ds(pl.multiple_of(c * tk, tk), n * tk), :], j)

    _two_pass_attention(i, logits, values, s_ref, mx_ref, acc_ref, 2)
    o_ref[0] = _pair_output(acc_ref[0], acc_ref[1]).astype(o_ref.dtype)


def _causal_attn_call(arrs, fox, b, s, tq):
    scratch = [pltpu.VMEM((2, s // tq, tq, tq), F32), pltpu.VMEM((2, tq, LANES), F32),
               pltpu.VMEM((2, tq, LANES), F32)]
    if fox:
        qkv, fcol, frow = arrs
        operands = (qkv, qkv, qkv, fcol, frow)
        in_specs = [pl.BlockSpec((1, tq, LANES), lambda bi, hp, i: (bi, i, hp)),
                    pl.BlockSpec((1, s, LANES), lambda bi, hp, i: (bi, 0, 2 + hp)),
                    pl.BlockSpec((1, s, LANES), lambda bi, hp, i: (bi, 0, 4 + hp)),
                    pl.BlockSpec((1, tq, LANES), lambda bi, hp, i: (bi, i, 0)),
                    pl.BlockSpec((1, 8, s), lambda bi, hp, i: (bi, 0, 0))]
    else:
        operands = arrs
        in_specs = [pl.BlockSpec((1, tq, 2 * LANES), lambda bi, hp, i: (bi, i, hp)),
                    pl.BlockSpec((1, s, 2 * LANES), lambda bi, hp, i: (bi, 0, hp)),
                    pl.BlockSpec((1, s, LANES), lambda bi, hp, i: (bi, 0, hp))]
    return pl.pallas_call(
        functools.partial(_causal_attn_kernel, fox=fox, tq=tq),
        grid=(b, 2, s // tq),
        in_specs=in_specs,
        out_specs=pl.BlockSpec((1, tq, LANES), lambda bi, hp, i: (bi, i, hp)),
        out_shape=jax.ShapeDtypeStruct((b, s, 2 * LANES), BF16),
        scratch_shapes=scratch,
        compiler_params=pltpu.CompilerParams(dimension_semantics=("arbitrary",) * 3, vmem_limit_bytes=VMEM_LIMIT),
        name="attn_fox" if fox else "attn_mla",
    )(*operands)


def _alibi_slope(head):
    return lax.shift_left(jnp.int32(1), 7 - head).astype(F32) * (LOG2E * 2.0 ** -8)


def _swa_kernel(sink_ref, q_ref, k_ref, v_ref, o_ref, *, tq):
    hp = pl.program_id(1)
    i = pl.program_id(2)
    band = 2 * WINDOW
    rel = lax.broadcasted_iota(I32, (WINDOW, band), 0) - lax.broadcasted_iota(I32, (WINDOW, band), 1)
    for r in range(tq // WINDOW):
        q_start = i * tq + r * WINDOW
        k_start = pl.multiple_of(jnp.maximum(q_start - WINDOW, 0), WINDOW)
        kb = k_ref[0, pl.ds(k_start, band), :]
        vb = v_ref[0, pl.ds(k_start, band), :]
        dist = rel + (q_start - k_start)
        valid = (dist >= 0) & (dist < WINDOW)
        distf = dist.astype(F32)
        qs = _split_pair(q_ref[0, r * WINDOW:(r + 1) * WINDOW, :])
        acc, sink_term = [], []
        for j in range(2):
            sink = sink_ref[2 * hp + j] * LOG2E
            s = jnp.where(valid, _nt_dot(qs[j], kb) - _alibi_slope(2 * hp + j) * distf, NEG)
            m = jnp.maximum(jnp.max(s, axis=1, keepdims=True), sink)
            p = jnp.exp2(s - m).astype(BF16)
            acc.append(jnp.dot(p, _values_with_ones(vb, j), preferred_element_type=F32))
            sink_term.append(jnp.exp2(sink - m))
        o_ref[0, r * WINDOW:(r + 1) * WINDOW, :] = _pair_output(acc[0], acc[1], *sink_term).astype(o_ref.dtype)


def _swa_call(qkv, sinks, b, s, tq):
    return pl.pallas_call(
        functools.partial(_swa_kernel, tq=tq),
        grid=(b, 2, s // tq),
        in_specs=[pl.BlockSpec(memory_space=pltpu.SMEM),
                  pl.BlockSpec((1, tq, LANES), lambda bi, hp, i: (bi, i, hp)),
                  pl.BlockSpec((1, s, LANES), lambda bi, hp, i: (bi, 0, 2 + hp)),
                  pl.BlockSpec((1, s, LANES), lambda bi, hp, i: (bi, 0, 4 + hp))],
        out_specs=pl.BlockSpec((1, tq, LANES), lambda bi, hp, i: (bi, i, hp)),
        out_shape=jax.ShapeDtypeStruct((b, s, 2 * LANES), BF16),
        compiler_params=pltpu.CompilerParams(dimension_semantics=("arbitrary",) * 3, vmem_limit_bytes=VMEM_LIMIT),
        name="attn_swa",
    )(sinks, qkv, qkv, qkv)


def _dsa_kernel(q_ref, k_ref, v_ref, qi_ref, ki_ref, wi_ref, o_ref, keys_ref, hi_ref, lo_ref, thr_ref,
                s_ref, mx_ref, acc_ref, *, seq, topk, qb):
    kc = LANES
    kw = 2 * kc
    i = pl.program_id(1)
    nwide = (i + 1) * (qb // kw)
    lane = lax.broadcasted_iota(I32, (1, LANES), 1)
    key_row = lax.broadcasted_iota(I32, (kc, qb), 0)
    query_pos = i * qb + lax.broadcasted_iota(I32, (1, qb), 1)

    qi = qi_ref[0]
    wi_t = wi_ref[0].T
    qms, wrows = [], []
    for hd in range(IDX_HEADS):
        g, r = divmod(hd, LANES // IDX_DIM)
        sel = (lane >= r * IDX_DIM) & (lane < (r + 1) * IDX_DIM)
        qg = qi[:, g * LANES:(g + 1) * LANES]
        qms.append(jnp.where(sel, qg, jnp.zeros_like(qg)))
        wrows.append(wi_t[_WI_LANE + hd:_WI_LANE + hd + 1, :])
    q_all = jnp.concatenate(qms, axis=0)
    key_minus_query = (lax.broadcasted_iota(I32, (kw, qb), 0) - lax.broadcasted_iota(I32, (kw, qb), 1))

    def score_chunk(c, carry):
        start = pl.multiple_of(c * kw, kw)
        logit = _nt_dot(ki_ref[0, pl.ds(start, kw), :], q_all)
        sc = wrows[0] * jnp.maximum(logit[:, :qb], 0.0)
        for hd in range(1, IDX_HEADS):
            sc = sc + wrows[hd] * jnp.maximum(logit[:, hd * qb:(hd + 1) * qb], 0.0)
        sc = jnp.where(key_minus_query <= i * qb - start, sc, -jnp.inf)
        sc = jnp.where(sc == 0.0, 0.0, sc)
        bits = pltpu.bitcast(sc, I32)
        keys = bits ^ ((bits >> 31) & 0x7FFFFFFF)
        keys_ref[2 * c] = keys[:kc]
        keys_ref[2 * c + 1] = keys[kc:]
        hi = (keys >> 16).astype(I16)
        lo = ((keys & 0xFFFF) + I16_MIN).astype(I16)
        hi_ref[2 * c] = hi[:kc]
        hi_ref[2 * c + 1] = hi[kc:]
        lo_ref[2 * c] = lo[:kc]
        lo_ref[2 * c + 1] = lo[kc:]
        return carry

    lax.fori_loop(0, nwide, score_chunk, 0)

    keep_all = jnp.full((1, qb), KEY_NEG_INF + 1, I32)
    thr_ref[...] = jnp.broadcast_to(keep_all, thr_ref.shape)

    def count(pred):
        def body(c2, acc):
            parts = []
            for c in (2 * c2, 2 * c2 + 1):
                hit = jnp.where(pred(keys_ref[c], c), 1.0, 0.0)
                parts += [hit[r * SUBLANES:(r + 1) * SUBLANES] for r in range(kc // SUBLANES)]
            while len(parts) > 1:
                parts = [a + b for a, b in zip(parts[::2], parts[1::2])]
            return acc + parts[0]
        acc = lax.fori_loop(0, nwide, body, jnp.zeros((SUBLANES, qb), F32))
        return jnp.sum(acc, axis=0, keepdims=True)

    @pl.when((i + 1) * qb > topk)
    def _():
        kf = float(topk)

        pack = 2 * SUBLANES

        def count16(plane_ref, cand):
            cand16 = cand.astype(I16)

            def body(c2, acc):
                parts = []
                for c in (2 * c2, 2 * c2 + 1):
                    hit = jnp.where(plane_ref[c] >= cand16, jnp.int16(1), jnp.int16(0))
                    parts += [hit[r * pack:(r + 1) * pack] for r in range(kc // pack)]
                while len(parts) > 1:
                    parts = [a + b for a, b in zip(parts[::2], parts[1::2])]
                return acc + parts[0]

            acc = lax.fori_loop(0, nwide, body, jnp.zeros((pack, qb), I16))
            return jnp.sum(acc.astype(F32), axis=0, keepdims=True)

        def search16(plane_ref, cnt_at_floor):
            def bit_step(it, carry):
                val, cnt_val = carry
                cand = val + lax.shift_left(jnp.int32(1), 15 - it)
                cnt = count16(plane_ref, cand)
                ok = cnt >= kf
                return jnp.where(ok, cand, val), jnp.where(ok, cnt, cnt_val)

            return lax.fori_loop(0, 16, bit_step, (jnp.full((1, qb), I16_MIN, I32), cnt_at_floor))

        thr_hi, cnt_hi = search16(hi_ref, jnp.full((1, qb), float(seq), F32))
        thr_hi16 = thr_hi.astype(I16)

        def pin(c, carry):
            h = hi_ref[c]
            lo_ref[c] = jnp.where(h > thr_hi16, jnp.int16(I16_MAX),
                                  jnp.where(h == thr_hi16, lo_ref[c], jnp.int16(I16_MIN)))
            return carry

        lax.fori_loop(0, 2 * nwide, pin, 0)
        thr_lo, cnt_thr = search16(lo_ref, cnt_hi)
        thr = thr_hi * 65536 + (thr_lo - I16_MIN)
        thr_ref[...] = jnp.broadcast_to(jnp.where(query_pos < topk, keep_all, thr), thr_ref.shape)

        @pl.when(jnp.max(cnt_thr) > kf)
        def _():
            need = kf - count(lambda kk, c: kk > thr)

            def idx_step(it, pos):
                cand = pos + lax.shift_left(jnp.int32(1), (seq.bit_length() - 2) - it)
                cnt = count(lambda kk, c: (kk == thr) & (key_row + c * kc < cand))
                return jnp.where(cnt < need, cand, pos)

            pos = lax.fori_loop(0, seq.bit_length() - 1, idx_step, jnp.zeros((1, qb), I32))

            def demote(c, carry):
                kk = keys_ref[c]
                keys_ref[c] = jnp.where((kk == thr) & (key_row + c * kc > pos), kk - 1, kk)
                return carry

            lax.fori_loop(0, 2 * nwide, demote, 0)

    q = q_ref[0]
    qs = _split_pair(q[:, :LANES]) + _split_pair(q[:, LANES:])
    thr_row = thr_ref[0:1, :]
    rel = (lax.broadcasted_iota(I32, (qb, 2 * kw), 0) - lax.broadcasted_iota(I32, (qb, 2 * kw), 1))

    def shared(c, n):
        halves = [jnp.where(keys_ref[2 * c + h] >= thr_row, 0.0, NEG).T for h in range(2 * n)]
        distf = (rel[:, :n * kw] + (i * qb - c * kw)).astype(F32)
        return jnp.concatenate(halves, axis=1), distf

    def logits(c, n, hd, diagonal, ctx):
        bias, distf = ctx
        start = pl.multiple_of(c * kw, kw)
        s = _nt_dot(qs[hd], k_ref[0, pl.ds(start, n * kw), :]) - (LOG2E * 2.0 ** -(N_HEADS + hd + 1)) * distf
        return s + bias

    def values(c, n, hd):
        return _values_with_ones(v_ref[0, pl.ds(pl.multiple_of(c * kw, kw), n * kw), :], hd % 2)

    _two_pass_attention(nwide - 1, logits, values, s_ref, mx_ref, acc_ref, N_HEADS, shared)
    o_ref[0] = jnp.concatenate([_pair_output(acc_ref[0], acc_ref[1]), _pair_output(acc_ref[2], acc_ref[3])],
                               axis=1).astype(o_ref.dtype)


def _dsa_call(od3, small3, b, s, topk, qb):
    kc = LANES
    scratch = [pltpu.VMEM((s // kc, kc, qb), I32), pltpu.VMEM((s // kc, kc, qb), I16),
               pltpu.VMEM((s // kc, kc, qb), I16), pltpu.VMEM((SUBLANES, qb), I32),
               pltpu.VMEM((N_HEADS, s // (2 * kc), qb, 2 * kc), F32),
               pltpu.VMEM((N_HEADS, qb, LANES), F32), pltpu.VMEM((N_HEADS, qb, LANES), F32)]
    return pl.pallas_call(
        functools.partial(_dsa_kernel, seq=s, topk=topk, qb=qb),
        grid=(b, s // qb),
        in_specs=[pl.BlockSpec((1, qb, 2 * LANES), lambda bi, i: (bi, i, 0)),
                  pl.BlockSpec((1, s, LANES), lambda bi, i: (bi, 0, 2)),
                  pl.BlockSpec((1, s, LANES), lambda bi, i: (bi, 0, 3)),
                  pl.BlockSpec((1, qb, 2 * LANES), lambda bi, i: (bi, i, 2)),
                  pl.BlockSpec((1, s, LANES), lambda bi, i: (bi, 0, 6)),
                  pl.BlockSpec((1, qb, LANES), lambda bi, i: (bi, i, 0))],
        out_specs=pl.BlockSpec((1, qb, 2 * LANES), lambda bi, i: (bi, i, 0)),
        out_shape=jax.ShapeDtypeStruct((b, s, 2 * LANES), BF16),
        scratch_shapes=scratch,
        compiler_params=pltpu.CompilerParams(dimension_semantics=("arbitrary",) * 2, vmem_limit_bytes=VMEM_LIMIT),
        name="attn_dsa",
    )(od3, od3, od3, od3, od3, small3)


def _outproj_kernel(x_ref, oa_ref, ob_ref, oc_ref, od_ref, w_ref, mod_ref, g2_ref, wr_ref, br_ref,
                    xo_ref, h_ref, rt_ref):
    d_model = x_ref.shape[1]
    mix = None
    for n, o_ref in enumerate((oa_ref, ob_ref, oc_ref, od_ref)):
        part = jnp.dot(o_ref[...], w_ref[n * 2 * LANES:(n + 1) * 2 * LANES, :], preferred_element_type=F32)
        mix = part if mix is None else mix + part
    xn = x_ref[...] + mod_ref[0, 2:3, :] * mix
    xo_ref[...] = xn
    h = _rms(xn, g2_ref[...]) * (1.0 + mod_ref[0, 4:5, :]) + mod_ref[0, 3:4, :]
    h_ref[:, :d_model] = h

    logits = lax.dot_general(wr_ref[...], h, (((1,), (1,)), ((), ())), precision=lax.Precision.HIGHEST,
                             preferred_element_type=F32)
    score = 1.0 / (1.0 + jnp.exp(-logits))
    biased = score + br_ref[...]
    srow = [score[e:e + 1, :] for e in range(N_EXPERTS)]
    brow = [biased[e:e + 1, :] for e in range(N_EXPERTS)]
    per = N_EXPERTS // N_GROUPS
    best_v = best_g = None
    for g in range(N_GROUPS):
        r = brow[g * per:(g + 1) * per]
        top2 = None
        for a in range(per):
            for c in range(a + 1, per):
                top2 = r[a] + r[c] if top2 is None else jnp.maximum(top2, r[a] + r[c])
        if g == 0:
            best_v, best_g = top2, jnp.zeros_like(top2, dtype=I32)
        else:
            up = top2 > best_v
            best_v = jnp.where(up, top2, best_v)
            best_g = jnp.where(up, g, best_g)
    cand = [jnp.where(best_g == e // per, brow[e], -jnp.inf) for e in range(N_EXPERTS)]

    def first_max(vals):
        v, idx = vals[0], jnp.zeros_like(best_g)
        for e in range(1, N_EXPERTS):
            up = vals[e] > v
            v = jnp.where(up, vals[e], v)
            idx = jnp.where(up, e, idx)
        return idx

    i1 = first_max(cand)
    i2 = first_max([jnp.where(i1 == e, -jnp.inf, cand[e]) for e in range(N_EXPERTS)])
    s1 = sum(jnp.where(i1 == e, srow[e], 0.0) for e in range(N_EXPERTS))
    s2 = sum(jnp.where(i2 == e, srow[e], 0.0) for e in range(N_EXPERTS))
    den = s1 + s2
    first_low = i1 < i2
    la = jnp.minimum(i1, i2) - per * best_g
    lb = jnp.maximum(i1, i2) - per * best_g
    pair = jnp.where(la == 0, lb - 1, jnp.where(la == 1, lb + 1, 5))
    route = jnp.concatenate([(best_g * N_PAIRS + pair).astype(F32), jnp.where(first_low, s1, s2) / den,
                             jnp.where(first_low, s2, s1) / den, jnp.zeros((SUBLANES - 3, den.shape[1]), F32)], axis=0)
    rt_ref[...] = route
    pad = jnp.zeros((LANES - SUBLANES, den.shape[1]), F32)
    h_ref[:, d_model:] = jnp.concatenate([route, pad], axis=0).T


def _outproj_call(x2, outs, w_out, mod_l, g2, wr_t, br, seq, tm):
    t, d = x2.shape
    tpb = seq // tm
    row = lambda n: pl.BlockSpec((tm, n), lambda i: (i, 0))
    const = lambda shape: pl.BlockSpec(shape, lambda i: (0,) * len(shape))
    return pl.pallas_call(
        _outproj_kernel,
        grid=(t // tm,),
        in_specs=[row(d)] + [row(2 * LANES)] * 4 + [
            const(w_out.shape), pl.BlockSpec((1, 6, d), lambda i: (i // tpb, 0, 0)), const((1, d)),
            const(wr_t.shape), const(br.shape)],
        out_specs=[row(d), row(d + LANES), pl.BlockSpec((SUBLANES, tm), lambda i: (0, i))],
        out_shape=[jax.ShapeDtypeStruct((t, d), F32), jax.ShapeDtypeStruct((t, d + LANES), F32),
                   jax.ShapeDtypeStruct((SUBLANES, t), F32)],
        compiler_params=pltpu.CompilerParams(dimension_semantics=("arbitrary",), vmem_limit_bytes=VMEM_LIMIT),
        name="outproj_router",
    )(x2, *outs, w_out, mod_l, g2, wr_t, br)


def _routing_tables(cls, n_steps):
    t = cls.shape[0]
    onehot = (cls[:, None] == jnp.arange(N_CLASSES, dtype=I32)[None, :]).astype(I32)
    upto = jnp.cumsum(onehot, axis=0)
    rank = jnp.sum((upto - onehot) * onehot, axis=1)
    padded = (upto[-1] + MOE_ROWS - 1) // MOE_ROWS * MOE_ROWS
    ends = jnp.cumsum(padded)
    pos = jnp.sum(onehot * (ends - padded)[None, :], axis=1) + rank
    step_cls = jnp.sum((jnp.arange(n_steps, dtype=I32) * MOE_ROWS)[:, None] >= ends[None, :], axis=1)
    valid = (step_cls < N_CLASSES).astype(I32)
    step_cls = jnp.minimum(step_cls, N_CLASSES - 1)
    base = (step_cls // N_PAIRS) * (N_EXPERTS // N_GROUPS)
    e_low = base + jnp.asarray(PAIR_LOW, I32)[step_cls % N_PAIRS]
    e_high = base + jnp.asarray(PAIR_HIGH, I32)[step_cls % N_PAIRS]
    return pos.reshape(t // PERM_ROWS, 1, PERM_ROWS), e_low, e_high, valid


def _row_copy(src_ref, src_row, dst_ref, dst_row, sem):
    return pltpu.make_async_copy(src_ref.at[pl.ds(src_row, 1)], dst_ref.at[pl.ds(dst_row, 1)], sem)


def _issue_rows(copy_row):
    def issue(r8, carry):
        for u in range(ROW_UNROLL):
            copy_row(r8 * ROW_UNROLL + u).start()
        return carry

    lax.fori_loop(0, PERM_ROWS // ROW_UNROLL, issue, 0)


def _rows_done(src_ref, dst_ref, sem):
    pltpu.make_async_copy(src_ref.at[pl.ds(0, PERM_ROWS)], dst_ref.at[pl.ds(0, PERM_ROWS)], sem).wait()


def _scatter_rows_kernel(pos_ref, src_ref, init_ref, dst_ref, sem):
    del init_ref
    _issue_rows(lambda r: _row_copy(src_ref, r, dst_ref, pos_ref[0, 0, r], sem))
    _rows_done(src_ref, dst_ref, sem)


def _scatter_rows_call(pos3, src, n_sorted):
    t, d = src.shape
    return pl.pallas_call(
        _scatter_rows_kernel,
        grid=(t // PERM_ROWS,),
        in_specs=[pl.BlockSpec((1, 1, PERM_ROWS), lambda j: (j, 0, 0), memory_space=pltpu.SMEM),
                  pl.BlockSpec((PERM_ROWS, d), lambda j: (j, 0)), pl.BlockSpec(memory_space=pl.ANY)],
        out_specs=pl.BlockSpec(memory_space=pl.ANY),
        out_shape=jax.ShapeDtypeStruct((n_sorted, d), src.dtype),
        scratch_shapes=[pltpu.SemaphoreType.DMA(())],
        input_output_aliases={2: 0},
        compiler_params=pltpu.CompilerParams(dimension_semantics=("arbitrary",), has_side_effects=True),
        name="moe_sort_rows",
    )(pos3, src, jnp.zeros((n_sorted, d), src.dtype))


def _moe_kernel(e_low_ref, e_high_ref, valid_ref, h_ref, wgu_low_ref, wgu_high_ref, wd_low_ref, wd_high_ref, o_ref):
    del e_low_ref, e_high_ref
    step = pl.program_id(0)
    d_model = o_ref.shape[1]

    @pl.when(valid_ref[step] == 0)
    def _():
        o_ref[...] = jnp.zeros(o_ref.shape, F32)

    @pl.when(valid_ref[step] != 0)
    def _():
        h = h_ref[:, :d_model].astype(BF16)
        w_low = h_ref[:, d_model + 1:d_model + 2]
        w_high = h_ref[:, d_model + 2:d_model + 3]

        def expert(wgu_ref, wd_ref):
            gu = jnp.dot(h, wgu_ref[0], preferred_element_type=F32)
            gate, up = gu[:, :D_EXPERT], gu[:, D_EXPERT:]
            hid = (gate / (1.0 + jnp.exp(-gate)) * up).astype(BF16)
            return jnp.dot(hid, wd_ref[0], preferred_element_type=F32)

        o_ref[...] = w_low * expert(wgu_low_ref, wd_low_ref) + w_high * expert(wgu_high_ref, wd_high_ref)


def _moe_call(hs, e_low, e_high, valid, wgu, wd):
    n_sorted = hs.shape[0]
    d = wd.shape[2]
    grid_spec = pltpu.PrefetchScalarGridSpec(
        num_scalar_prefetch=3,
        grid=(n_sorted // MOE_ROWS,),
        in_specs=[pl.BlockSpec((MOE_ROWS, hs.shape[1]), lambda j, lo, hi, ok: (j, 0)),
                  pl.BlockSpec((1, d, 2 * D_EXPERT), lambda j, lo, hi, ok: (lo[j], 0, 0)),
                  pl.BlockSpec((1, d, 2 * D_EXPERT), lambda j, lo, hi, ok: (hi[j], 0, 0)),
                  pl.BlockSpec((1, D_EXPERT, d), lambda j, lo, hi, ok: (lo[j], 0, 0)),
                  pl.BlockSpec((1, D_EXPERT, d), lambda j, lo, hi, ok: (hi[j], 0, 0))],
        out_specs=pl.BlockSpec((MOE_ROWS, d), lambda j, lo, hi, ok: (j, 0)))
    return pl.pallas_call(
        _moe_kernel,
        grid_spec=grid_spec,
        out_shape=jax.ShapeDtypeStruct((n_sorted, d), F32),
        compiler_params=pltpu.CompilerParams(dimension_semantics=("arbitrary",), vmem_limit_bytes=VMEM_LIMIT),
        name="moe",
    )(e_low, e_high, valid, hs, wgu, wgu, wd, wd)


def _gather_residual_kernel(pos_ref, pos_next_ref, ys_ref, x_ref, mod_ref, gf_ref, o_ref, rows_ref, sems, *, final):
    j = pl.program_id(0)
    slot = j % 2

    def request(index_ref, into):
        _issue_rows(lambda r: _row_copy(ys_ref, index_ref[0, 0, r], rows_ref.at[into], r, sems.at[into]))

    @pl.when(j == 0)
    def _():
        request(pos_ref, slot)

    @pl.when(j + 1 < pl.num_programs(0))
    def _():
        request(pos_next_ref, 1 - slot)

    _rows_done(ys_ref, rows_ref.at[slot], sems.at[slot])
    xn = x_ref[...] + mod_ref[0, 5:6, :] * rows_ref[slot]
    o_ref[...] = _rms(xn, gf_ref[...]) if final else xn


def _gather_residual_call(pos3, ys, x2, mod_l, gf, seq, final):
    t, d = x2.shape
    tpb = seq // PERM_ROWS
    last = t // PERM_ROWS - 1
    return pl.pallas_call(
        functools.partial(_gather_residual_kernel, final=final),
        grid=(t // PERM_ROWS,),
        in_specs=[pl.BlockSpec((1, 1, PERM_ROWS), lambda j: (j, 0, 0), memory_space=pltpu.SMEM),
                  pl.BlockSpec((1, 1, PERM_ROWS), lambda j: (jnp.minimum(j + 1, last), 0, 0),
                               memory_space=pltpu.SMEM),
                  pl.BlockSpec(memory_space=pl.ANY),
                  pl.BlockSpec((PERM_ROWS, d), lambda j: (j, 0)),
                  pl.BlockSpec((1, 6, d), lambda j: (j // tpb, 0, 0)),
                  pl.BlockSpec((1, d), lambda j: (0, 0))],
        out_specs=pl.BlockSpec((PERM_ROWS, d), lambda j: (j, 0)),
        out_shape=jax.ShapeDtypeStruct((t, d), F32),
        scratch_shapes=[pltpu.VMEM((2, PERM_ROWS, d), F32), pltpu.SemaphoreType.DMA((2,))],
        compiler_params=pltpu.CompilerParams(dimension_semantics=("arbitrary",), vmem_limit_bytes=VMEM_LIMIT),
        name="moe_unsort_residual",
    )(pos3, pos3, ys, x2, mod_l, gf)


def _swap_half(w):
    half = w.shape[1] // 2
    return jnp.concatenate([-w[:, half:], w[:, :half]], axis=1)


def _layer_weights(w_in, w_q_up, w_kv_up):
    d = w_in.shape[0]
    pts = np.cumsum(IN_SIZES)[:-1].tolist()
    (a_q, a_k, a_v, a_f, b_cq, b_ckv, b_kr, c_q, c_k, c_v,
     d_q, d_k, d_v, d_qi, d_ki, d_wi) = jnp.split(w_in, pts, axis=1)
    qs = HEAD_DIM ** -0.5
    dup = lambda w: jnp.concatenate([w[:, :HEAD_DIM], w[:, :HEAD_DIM], w[:, HEAD_DIM:], w[:, HEAD_DIM:]], axis=1)
    small = jnp.concatenate([
        a_f, d_wi * ((IDX_HEADS * IDX_DIM) ** -0.5),
        jnp.zeros((d, _KR_LANE - _WI_LANE - IDX_HEADS), F32),
        b_kr, _swap_half(b_kr), jnp.zeros((d, LANES - _KR_LANE - 2 * MLA_ROPE), F32)], axis=1)
    w_all = jnp.concatenate([
        a_q * qs, a_k, a_v,
        c_q * qs, dup(c_k), dup(c_v),
        d_q * qs, d_k, d_k, d_v, d_v, d_qi, d_ki, d_ki, d_ki, d_ki,
        small, b_cq, b_ckv], axis=1).astype(BF16)

    per_q = MLA_NOPE + MLA_ROPE
    wq = []
    for hd in range(N_HEADS):
        blk = w_q_up[:, hd * per_q:(hd + 1) * per_q]
        rot = blk[:, MLA_NOPE:]
        wq += [blk[:, :MLA_NOPE], rot, _swap_half(rot)]
    wq = jnp.concatenate(wq, axis=1).astype(BF16)

    place = np.zeros((LANES, LANES), np.float32)
    place[_KR_LANE + np.arange(MLA_ROPE), MLA_NOPE + np.arange(MLA_ROPE)] = 1.0
    place = jnp.asarray(place)
    kcols, vcols = [], []
    for hd in range(N_HEADS):
        blk = w_kv_up[:, hd * 2 * HEAD_DIM:(hd + 1) * 2 * HEAD_DIM]
        knope = jnp.concatenate([blk[:, :MLA_NOPE], jnp.zeros((MLA_KV_RANK, LANES - MLA_NOPE), F32)], axis=1)
        kcols.append(jnp.concatenate([knope, place], axis=0))
        vcols.append(jnp.concatenate([blk[:, MLA_NOPE:], jnp.zeros((LANES, HEAD_DIM), F32)], axis=0))
    wkv = jnp.concatenate(kcols + vcols, axis=1).astype(BF16)
    return w_all, wq, wkv


def _rope_tables(seq):
    half = MLA_ROPE // 2
    inv = ROPE_THETA ** (-jnp.arange(half, dtype=F32) / half)
    ang = jnp.arange(seq, dtype=F32)[:, None] * inv[None, :]
    cos = jnp.tile(jnp.cos(ang), (1, 2))
    sin = jnp.tile(jnp.sin(ang), (1, 2))
    scale = LOG2E * (MLA_NOPE + MLA_ROPE) ** -0.5
    z = lambda n: jnp.zeros((seq, n), F32)
    tab_q = jnp.concatenate([jnp.full((seq, MLA_NOPE), scale, F32), cos * scale, z(MLA_ROPE)], axis=1)
    tab_qs = jnp.concatenate([z(MLA_NOPE), sin * scale, z(MLA_ROPE)], axis=1)
    tab_k = jnp.concatenate([z(_KR_LANE), cos, z(LANES - _KR_LANE - MLA_ROPE)], axis=1)
    tab_ks = jnp.concatenate([z(_KR_LANE), sin, z(LANES - _KR_LANE - MLA_ROPE)], axis=1)
    return jnp.stack([tab_q, tab_qs, tab_k, tab_ks])


def kernel(x, c, w_ada, b_ada, g_norm1, w_in, b_forget, g_q_mla, w_q_up, g_kv_mla, w_kv_up, sinks, w_out,
           g_norm2, w_router, b_router, w_gate, w_up, w_down, g_final):
    b, s, d = x.shape
    depth = w_in.shape[0]
    t = b * s
    topk = min(TOPK_MAX, s // 4)
    tm = min(1024, s)
    tq = min(512, s)
    tq_swa = min(512, s)
    qb_dsa = 2 * LANES
    n_steps = t // MOE_ROWS + N_CLASSES
    assert s % tm == 0 and s % tq == 0 and s % qb_dsa == 0 and s % PERM_ROWS == 0

    mod = _ada_call(c, w_ada, b_ada).reshape(depth, b, 6, d)
    tabs = _rope_tables(s)
    wr_t = w_router.T
    br = b_router.reshape(N_EXPERTS, 1)
    x2 = x.reshape(t, d)
    for l in range(depth):
        w_all, wq, wkv = _layer_weights(w_in[l], w_q_up[l], w_kv_up[l])
        oa, oc, od, osm, oqb, okb, ovb = _inproj_call(
            x2, mod[l], g_norm1[l].reshape(1, d), w_all, wq, wkv,
            g_q_mla[l].reshape(1, -1), g_kv_mla[l].reshape(1, -1), tabs, s, tm)
        small3 = osm.reshape(b, s, LANES)
        bf = jnp.zeros((1, LANES), F32).at[0, :N_HEADS].set(b_forget[l])
        fcum = _fox_cumsum_call(small3, bf)
        frow = jnp.swapaxes(fcum[:, :, :8], 1, 2)
        out_a = _causal_attn_call((oa.reshape(b, s, -1), fcum, frow), True, b, s, tq)
        out_b = _causal_attn_call((oqb.reshape(b, s, -1), okb.reshape(b, s, -1), ovb.reshape(b, s, -1)),
                                  False, b, s, tq)
        out_c = _swa_call(oc.reshape(b, s, -1), sinks[l], b, s, tq_swa)
        out_d = _dsa_call(od.reshape(b, s, -1), small3, b, s, topk, qb_dsa)
        outs = [o.reshape(t, 2 * LANES) for o in (out_a, out_b, out_c, out_d)]
        x2, h2, route = _outproj_call(x2, outs, w_out[l].astype(BF16), mod[l], g_norm2[l].reshape(1, d),
                                      wr_t, br, s, tm)
        pos3, e_low, e_high, valid = _routing_tables(route[0].astype(I32), n_steps)
        wgu = jnp.concatenate([w_gate[l], w_up[l]], axis=2).astype(BF16)
        hs = _scatter_rows_call(pos3, h2, n_steps * MOE_ROWS)
        ys = _moe_call(hs, e_low, e_high, valid, wgu, w_down[l].astype(BF16))
        x2 = _gather_residual_call(pos3, ys, x2, mod[l], g_final.reshape(1, d), s, l == depth - 1)
    return x2.reshape(b, s, d)
```

```python
import functools

import jax
import jax.numpy as jnp
import numpy as np
from jax import lax
from jax.experimental import pallas as pl
from jax.experimental.pallas import tpu as pltpu

F32 = jnp.float32
BF16 = jnp.bfloat16
I32 = jnp.int32
I16 = jnp.int16
I16_MIN, I16_MAX = -(2 ** 15), 2 ** 15 - 1

EPS = 1e-6
HEAD_DIM = 64
LANES = 128
SUBLANES = 8
N_HEADS = 4
MLA_Q_RANK = 256
MLA_KV_RANK = 128
MLA_NOPE = 64
MLA_ROPE = 32
ROPE_THETA = 10000.0
WINDOW = 128
IDX_HEADS = 8
IDX_DIM = 32
TOPK_MAX = 256
N_EXPERTS = 16
N_GROUPS = 4
N_PAIRS = 6
N_CLASSES = N_GROUPS * N_PAIRS
PAIR_LOW = (0, 0, 0, 1, 1, 2)
PAIR_HIGH = (1, 2, 3, 2, 3, 3)
D_EXPERT = 256
MOE_ROWS = 256
PERM_ROWS = 512
ROW_UNROLL = 8
IN_SIZES = (256, 256, 256, 4, 256, 128, 32, 256, 128, 128, 256, 64, 64, 256, 32, 8)

LOG2E = 1.4426950408889634
NEG = -1e30
INT_MIN = -(2 ** 31)
KEY_NEG_INF = INT_MIN + 0x7FFFFF
VMEM_LIMIT = 56 * 1024 * 1024

_CA, _CC, _CD, _CS, _CQ, _CKV, _CEND = 0, 768, 1280, 1920, 2048, 2304, 2432
_F_LANE, _WI_LANE, _KR_LANE = 0, 4, 32


def _nt_dot(a, b):
    return lax.dot_general(a, b, (((1,), (1,)), ((), ())), preferred_element_type=F32)


def _rms(x, g):
    return x * lax.rsqrt(jnp.mean(x * x, axis=-1, keepdims=True) + EPS) * g


def _ada_kernel(c_ref, w_ref, b_ref, o_ref):
    c = c_ref[...]
    act = (c / (1.0 + jnp.exp(-c))).astype(BF16)
    o_ref[0] = jnp.dot(act, w_ref[0].astype(BF16), preferred_element_type=F32) + b_ref[0]


def _ada_call(c, w_ada, b_ada):
    depth, d, n = w_ada.shape
    bsz = c.shape[0]
    tn = 1024
    return pl.pallas_call(
        _ada_kernel,
        grid=(depth, n // tn),
        in_specs=[pl.BlockSpec((bsz, d), lambda l, j: (0, 0)),
                  pl.BlockSpec((1, d, tn), lambda l, j: (l, 0, j)),
                  pl.BlockSpec((1, 1, tn), lambda l, j: (l, 0, j))],
        out_specs=pl.BlockSpec((1, bsz, tn), lambda l, j: (l, 0, j)),
        out_shape=jax.ShapeDtypeStruct((depth, bsz, n), F32),
        compiler_params=pltpu.CompilerParams(dimension_semantics=("arbitrary", "arbitrary"),
                                             vmem_limit_bytes=VMEM_LIMIT),
        name="adaln",
    )(c, w_ada, b_ada.reshape(depth, 1, n))


def _inproj_kernel(x_ref, mod_ref, g1_ref, w_ref, wq_ref, wkv_ref, gq_ref, gkv_ref, tab_ref,
                   oa_ref, oc_ref, od_ref, os_ref, oqb_ref, okb_ref, ovb_ref):
    h = _rms(x_ref[...], g1_ref[...]) * (1.0 + mod_ref[0, 1:2, :]) + mod_ref[0, 0:1, :]
    h = h.astype(BF16)

    def proj(lo, hi):
        return jnp.dot(h, w_ref[:, lo:hi], preferred_element_type=F32)

    q_width = N_HEADS * HEAD_DIM
    low = _lane_is_low()

    def both_halves(pair):
        swapped = pltpu.roll(pair, HEAD_DIM, axis=1)
        return jnp.where(low, pair, swapped), jnp.where(low, swapped, pair)

    pa = proj(_CA, _CC)
    oa_ref[:, :q_width] = (pa[:, :q_width] * LOG2E).astype(BF16)
    oa_ref[:, q_width:] = pa[:, q_width:].astype(BF16)

    pc = proj(_CC, _CD)
    oc_ref[:, :q_width] = (pc[:, :q_width] * LOG2E).astype(BF16)
    for n in range(2):
        tiles = both_halves(pc[:, q_width + n * LANES:q_width + (n + 1) * LANES])
        for g in range(2):
            lo_col = q_width + (2 * n + g) * LANES
            oc_ref[:, lo_col:lo_col + LANES] = tiles[g].astype(BF16)

    pd = proj(_CD, _CS)
    small = proj(_CS, _CQ)
    os_ref[...] = small
    od_ref[:, :q_width] = (pd[:, :q_width] * LOG2E).astype(BF16)
    k_twice, v_twice = both_halves(pd[:, 2 * q_width:])
    od_ref[:, q_width:q_width + LANES] = k_twice.astype(BF16)
    od_ref[:, q_width + LANES:2 * q_width] = v_twice.astype(BF16)
    od_ref[:, 2 * q_width:3 * q_width] = pd[:, q_width:2 * q_width].astype(BF16)
    lane = lax.broadcasted_iota(I32, (1, LANES), 1)
    ki4 = small
    for n in range(1, LANES // IDX_DIM):
        ki4 = jnp.where(lane < LANES - n * IDX_DIM, pltpu.roll(small, LANES - n * IDX_DIM, axis=1), ki4)
    od_ref[:, 3 * q_width:] = ki4.astype(BF16)

    cq = _rms(proj(_CQ, _CKV), gq_ref[...]).astype(BF16)
    qf = jnp.dot(cq, wq_ref[...], preferred_element_type=F32)
    tab_q, tab_qs, tab_k, tab_ks = tab_ref[0], tab_ref[1], tab_ref[2], tab_ref[3]
    for hd in range(N_HEADS):
        qg = qf[:, hd * LANES:(hd + 1) * LANES]
        qr = qg * tab_q + pltpu.roll(qg, LANES - MLA_ROPE, axis=1) * tab_qs
        oqb_ref[:, hd * LANES:(hd + 1) * LANES] = qr.astype(BF16)

    ckv = _rms(proj(_CKV, _CEND), gkv_ref[...]).astype(BF16)
    kr = (small * tab_k + pltpu.roll(small, LANES - MLA_ROPE, axis=1) * tab_ks).astype(BF16)
    kvf = jnp.dot(jnp.concatenate([ckv, kr], axis=1), wkv_ref[...], preferred_element_type=F32)
    okb_ref[...] = kvf[:, :4 * LANES].astype(BF16)
    ovb_ref[...] = kvf[:, 4 * LANES:].astype(BF16)


def _inproj_call(x2, mod_l, g1, w_all, wq, wkv, gq, gkv, tabs, seq, tm):
    t, d = x2.shape
    tpb = seq // tm
    row = lambda n: pl.BlockSpec((tm, n), lambda i: (i, 0))
    const = lambda shape: pl.BlockSpec(shape, lambda i: (0,) * len(shape))
    widths = (768, 768, 896, 128, 512, 512, 256)
    dtypes = (BF16, BF16, BF16, F32, BF16, BF16, BF16)
    return pl.pallas_call(
        _inproj_kernel,
        grid=(t // tm,),
        in_specs=[row(d),
                  pl.BlockSpec((1, 6, d), lambda i: (i // tpb, 0, 0)),
                  const((1, d)), const(w_all.shape), const(wq.shape), const(wkv.shape),
                  const((1, MLA_Q_RANK)), const((1, MLA_KV_RANK)),
                  pl.BlockSpec((4, tm, LANES), lambda i: (0, i % tpb, 0))],
        out_specs=[row(n) for n in widths],
        out_shape=[jax.ShapeDtypeStruct((t, n), dt) for n, dt in zip(widths, dtypes)],
        compiler_params=pltpu.CompilerParams(dimension_semantics=("arbitrary",), vmem_limit_bytes=VMEM_LIMIT),
        name="inproj",
    )(x2, mod_l, g1, w_all, wq, wkv, gq, gkv, tabs)


def _fox_cumsum_kernel(s_ref, b_ref, o_ref):
    z = s_ref[0] + b_ref[...]
    lf = jnp.minimum(z, 0.0) - jnp.log(1.0 + jnp.exp(-jnp.abs(z)))
    n = lf.shape[0]
    row = lax.broadcasted_iota(I32, lf.shape, 0)
    d = 1
    while d < n:
        lf = lf + jnp.where(row >= d, pltpu.roll(lf, d, axis=0), 0.0)
        d *= 2
    o_ref[0] = lf * LOG2E


def _fox_cumsum_call(small3, bf):
    b, s, _ = small3.shape
    return pl.pallas_call(
        _fox_cumsum_kernel,
        grid=(b,),
        in_specs=[pl.BlockSpec((1, s, LANES), lambda i: (i, 0, 0)),
                  pl.BlockSpec((1, LANES), lambda i: (0, 0))],
        out_specs=pl.BlockSpec((1, s, LANES), lambda i: (i, 0, 0)),
        out_shape=jax.ShapeDtypeStruct((b, s, LANES), F32),
        compiler_params=pltpu.CompilerParams(dimension_semantics=("arbitrary",), vmem_limit_bytes=VMEM_LIMIT),
        name="fox_cumsum",
    )(small3, bf)


def _lane_is_low():
    return lax.broadcasted_iota(I32, (1, LANES), 1) < HEAD_DIM


def _split_pair(q):
    low = _lane_is_low()
    zero = jnp.zeros_like(q)
    return jnp.where(low, q, zero), jnp.where(low, zero, q)


def _values_with_ones(v, j):
    low = _lane_is_low()
    return jnp.where(low if j == 0 else ~low, v, jnp.ones_like(v))


def _pair_output(a0, a1, extra0=None, extra1=None):
    l0 = pltpu.roll(a0, HEAD_DIM, axis=1)
    l1 = pltpu.roll(a1, HEAD_DIM, axis=1)
    if extra0 is not None:
        l0, l1 = l0 + extra0, l1 + extra1
    return jnp.where(_lane_is_low(), a0 / l0, a1 / l1)


def _two_pass_attention(n_before, logits, values, s_ref, mx_ref, acc_ref, n_heads, shared=None):
    tq, tk = s_ref.shape[-2:]
    mx_ref[...] = jnp.full(mx_ref.shape, NEG, F32)

    def store(c, n, diagonal):
        ctx = shared(c, n) if shared is not None else None
        for j in range(n_heads):
            s = logits(c, n, j, diagonal, ctx)
            for t in range(n):
                s_ref[j, c + t] = s[:, t * tk:(t + 1) * tk]
            parts = [s[:, blk * LANES:(blk + 1) * LANES] for blk in range(n * tk // LANES)]
            while len(parts) > 1:
                parts = [jnp.maximum(a, b) for a, b in zip(parts[::2], parts[1::2])]
            mx_ref[j] = jnp.maximum(mx_ref[j], parts[0])

    def store_two(c2, carry):
        store(2 * c2, 2, False)
        return carry

    lax.fori_loop(0, n_before // 2, store_two, 0)

    @pl.when(n_before % 2 == 1)
    def _():
        store(n_before - 1, 1, False)

    store(n_before, 1, True)

    row_max = [jnp.max(mx_ref[j], axis=1, keepdims=True) for j in range(n_heads)]
    shift = [jnp.broadcast_to(m, (tq, tk)) for m in row_max]
    acc_ref[...] = jnp.zeros(acc_ref.shape, F32)

    def accum(c, n):
        for j in range(n_heads):
            p = [jnp.exp2(s_ref[j, c + t] - shift[j]).astype(BF16) for t in range(n)]
            p = p[0] if n == 1 else jnp.concatenate(p, axis=1)
            acc_ref[j] += jnp.dot(p, values(c, n, j), preferred_element_type=F32)

    def accum_two(c2, carry):
        accum(2 * c2, 2)
        return carry

    lax.fori_loop(0, (n_before + 1) // 2, accum_two, 0)

    @pl.when(n_before % 2 == 0)
    def _():
        accum(n_before, 1)

    return row_max


def _causal_attn_kernel(*refs, fox, tq):
    if fox:
        q_ref, k_ref, v_ref, fc_ref, fr_ref, o_ref, s_ref, mx_ref, acc_ref = refs
    else:
        q_ref, k_ref, v_ref, o_ref, s_ref, mx_ref, acc_ref = refs
    tk = tq
    hp = pl.program_id(1)
    i = pl.program_id(2)
    if fox:
        qs = _split_pair(q_ref[0])
        lane = lax.broadcasted_iota(I32, (1, LANES), 1)
        fcol = [jnp.sum(jnp.where(lane == 2 * hp + j, fc_ref[0], 0.0), axis=1, keepdims=True) for j in range(2)]
    else:
        qs = (q_ref[0, :, :LANES], q_ref[0, :, LANES:])
    causal = (lax.broadcasted_iota(I32, (tq, tk), 1) <= lax.broadcasted_iota(I32, (tq, tk), 0))

    def logits(c, n, j, diagonal, _):
        start = pl.multiple_of(c * tk, tk)
        if fox:
            s = _nt_dot(qs[j], k_ref[0, pl.ds(start, n * tk), :])
            s = (s + fcol[j]) - fr_ref[0, pl.ds(2 * hp + j, 1), pl.ds(start, n * tk)]
        else:
            s = _nt_dot(qs[j], k_ref[0, pl.ds(start, n * tk), j * LANES:(j + 1) * LANES])
        return jnp.where(causal, s, NEG) if diagonal else s

    def values(c, n, j):
        return _values_with_ones(v_ref[0, pl.ds(pl.multiple_of(c * tk, tk), n * tk), :], j)

    _two_pass_attention(i, logits, values, s_ref, mx_ref, acc_ref, 2)
    o_ref[0] = _pair_output(acc_ref[0], acc_ref[1]).astype(o_ref.dtype)


def _causal_attn_call(arrs, fox, b, s, tq):
    scratch = [pltpu.VMEM((2, s // tq, tq, tq), F32), pltpu.VMEM((2, tq, LANES), F32),
               pltpu.VMEM((2, tq, LANES), F32)]
    if fox:
        qkv, fcol, frow = arrs
        operands = (qkv, qkv, qkv, fcol, frow)
        in_specs = [pl.BlockSpec((1, tq, LANES), lambda bi, hp, i: (bi, i, hp)),
                    pl.BlockSpec((1, s, LANES), lambda bi, hp, i: (bi, 0, 2 + hp)),
                    pl.BlockSpec((1, s, LANES), lambda bi, hp, i: (bi, 0, 4 + hp)),
                    pl.BlockSpec((1, tq, LANES), lambda bi, hp, i: (bi, i, 0)),
                    pl.BlockSpec((1, 8, s), lambda bi, hp, i: (bi, 0, 0))]
    else:
        operands = arrs
        in_specs = [pl.BlockSpec((1, tq, 2 * LANES), lambda bi, hp, i: (bi, i, hp)),
                    pl.BlockSpec((1, s, 2 * LANES), lambda bi, hp, i: (bi, 0, hp)),
                    pl.BlockSpec((1, s, LANES), lambda bi, hp, i: (bi, 0, hp))]
    return pl.pallas_call(
        functools.partial(_causal_attn_kernel, fox=fox, tq=tq),
        grid=(b, 2, s // tq),
        in_specs=in_specs,
        out_specs=pl.BlockSpec((1, tq, LANES), lambda bi, hp, i: (bi, i, hp)),
        out_shape=jax.ShapeDtypeStruct((b, s, 2 * LANES), BF16),
        scratch_shapes=scratch,
        compiler_params=pltpu.CompilerParams(dimension_semantics=("arbitrary",) * 3, vmem_limit_bytes=VMEM_LIMIT),
        name="attn_fox" if fox else "attn_mla",
    )(*operands)


def _alibi_slope(head):
    return lax.shift_left(jnp.int32(1), 7 - head).astype(F32) * (LOG2E * 2.0 ** -8)


def _swa_kernel(sink_ref, q_ref, k_ref, v_ref, o_ref, *, tq):
    hp = pl.program_id(1)
    i = pl.program_id(2)
    band = 2 * WINDOW
    rel = lax.broadcasted_iota(I32, (WINDOW, band), 0) - lax.broadcasted_iota(I32, (WINDOW, band), 1)
    for r in range(tq // WINDOW):
        q_start = i * tq + r * WINDOW
        k_start = pl.multiple_of(jnp.maximum(q_start - WINDOW, 0), WINDOW)
        kb = k_ref[0, pl.ds(k_start, band), :]
        vb = v_ref[0, pl.ds(k_start, band), :]
        dist = rel + (q_start - k_start)
        valid = (dist >= 0) & (dist < WINDOW)
        distf = dist.astype(F32)
        qs = _split_pair(q_ref[0, r * WINDOW:(r + 1) * WINDOW, :])
        acc, sink_term = [], []
        for j in range(2):
            sink = sink_ref[2 * hp + j] * LOG2E
            s = jnp.where(valid, _nt_dot(qs[j], kb) - _alibi_slope(2 * hp + j) * distf, NEG)
            m = jnp.maximum(jnp.max(s, axis=1, keepdims=True), sink)
            p = jnp.exp2(s - m).astype(BF16)
            acc.append(jnp.dot(p, _values_with_ones(vb, j), preferred_element_type=F32))
            sink_term.append(jnp.exp2(sink - m))
        o_ref[0, r * WINDOW:(r + 1) * WINDOW, :] = _pair_output(acc[0], acc[1], *sink_term).astype(o_ref.dtype)


def _swa_call(qkv, sinks, b, s, tq):
    return pl.pallas_call(
        functools.partial(_swa_kernel, tq=tq),
        grid=(b, 2, s // tq),
        in_specs=[pl.BlockSpec(memory_space=pltpu.SMEM),
                  pl.BlockSpec((1, tq, LANES), lambda bi, hp, i: (bi, i, hp)),
                  pl.BlockSpec((1, s, LANES), lambda bi, hp, i: (bi, 0, 2 + hp)),
                  pl.BlockSpec((1, s, LANES), lambda bi, hp, i: (bi, 0, 4 + hp))],
        out_specs=pl.BlockSpec((1, tq, LANES), lambda bi, hp, i: (bi, i, hp)),
        out_shape=jax.ShapeDtypeStruct((b, s, 2 * LANES), BF16),
        compiler_params=pltpu.CompilerParams(dimension_semantics=("arbitrary",) * 3, vmem_limit_bytes=VMEM_LIMIT),
        name="attn_swa",
    )(sinks, qkv, qkv, qkv)


def _dsa_kernel(q_ref, k_ref, v_ref, qi_ref, ki_ref, wi_ref, o_ref, keys_ref, hi_ref, lo_ref, thr_ref,
                s_ref, mx_ref, acc_ref, *, seq, topk, qb):
    kc = LANES
    kw = 2 * kc
    i = pl.program_id(1)
    nwide = (i + 1) * (qb // kw)
    lane = lax.broadcasted_iota(I32, (1, LANES), 1)
    key_row = lax.broadcasted_iota(I32, (kc, qb), 0)
    query_pos = i * qb + lax.broadcasted_iota(I32, (1, qb), 1)

    qi = qi_ref[0]
    wi_t = wi_ref[0].T
    qms, wrows = [], []
    for hd in range(IDX_HEADS):
        g, r = divmod(hd, LANES // IDX_DIM)
        sel = (lane >= r * IDX_DIM) & (lane < (r + 1) * IDX_DIM)
        qg = qi[:, g * LANES:(g + 1) * LANES]
        qms.append(jnp.where(sel, qg, jnp.zeros_like(qg)))
        wrows.append(wi_t[_WI_LANE + hd:_WI_LANE + hd + 1, :])
    q_all = jnp.concatenate(qms, axis=0)
    key_minus_query = (lax.broadcasted_iota(I32, (kw, qb), 0) - lax.broadcasted_iota(I32, (kw, qb), 1))

    def score_chunk(c, carry):
        start = pl.multiple_of(c * kw, kw)
        logit = _nt_dot(ki_ref[0, pl.ds(start, kw), :], q_all)
        sc = wrows[0] * jnp.maximum(logit[:, :qb], 0.0)
        for hd in range(1, IDX_HEADS):
            sc = sc + wrows[hd] * jnp.maximum(logit[:, hd * qb:(hd + 1) * qb], 0.0)
        sc = jnp.where(key_minus_query <= i * qb - start, sc, -jnp.inf)
        sc = jnp.where(sc == 0.0, 0.0, sc)
        bits = pltpu.bitcast(sc, I32)
        keys = bits ^ ((bits >> 31) & 0x7FFFFFFF)
        keys_ref[2 * c] = keys[:kc]
        keys_ref[2 * c + 1] = keys[kc:]
        hi = (keys >> 16).astype(I16)
        lo = ((keys & 0xFFFF) + I16_MIN).astype(I16)
        hi_ref[2 * c] = hi[:kc]
        hi_ref[2 * c + 1] = hi[kc:]
        lo_ref[2 * c] = lo[:kc]
        lo_ref[2 * c + 1] = lo[kc:]
        return carry

    lax.fori_loop(0, nwide, score_chunk, 0)

    keep_all = jnp.full((1, qb), KEY_NEG_INF + 1, I32)
    thr_ref[...] = jnp.broadcast_to(keep_all, thr_ref.shape)

    def count(pred):
        def body(c2, acc):
            parts = []
            for c in (2 * c2, 2 * c2 + 1):
                hit = jnp.where(pred(keys_ref[c], c), 1.0, 0.0)
                parts += [hit[r * SUBLANES:(r + 1) * SUBLANES] for r in range(kc // SUBLANES)]
            while len(parts) > 1:
                parts = [a + b for a, b in zip(parts[::2], parts[1::2])]
            return acc + parts[0]
        acc = lax.fori_loop(0, nwide, body, jnp.zeros((SUBLANES, qb), F32))
        return jnp.sum(acc, axis=0, keepdims=True)

    @pl.when((i + 1) * qb > topk)
    def _():
        kf = float(topk)

        pack = 2 * SUBLANES

        def count16(plane_ref, cand):
            cand16 = cand.astype(I16)

            def body(c2, acc):
                parts = []
                for c in (2 * c2, 2 * c2 + 1):
                    hit = jnp.where(plane_ref[c] >= cand16, jnp.int16(1), jnp.int16(0))
                    parts += [hit[r * pack:(r + 1) * pack] for r in range(kc // pack)]
                while len(parts) > 1:
                    parts = [a + b for a, b in zip(parts[::2], parts[1::2])]
                return acc + parts[0]

            acc = lax.fori_loop(0, nwide, body, jnp.zeros((pack, qb), I16))
            return jnp.sum(acc.astype(F32), axis=0, keepdims=True)

        def search16(plane_ref, cnt_at_floor):
            def bit_step(it, carry):
                val, cnt_val = carry
                cand = val + lax.shift_left(jnp.int32(1), 15 - it)
                cnt = count16(plane_ref, cand)
                ok = cnt >= kf
                return jnp.where(ok, cand, val), jnp.where(ok, cnt, cnt_val)

            return lax.fori_loop(0, 16, bit_step, (jnp.full((1, qb), I16_MIN, I32), cnt_at_floor))

        thr_hi, cnt_hi = search16(hi_ref, jnp.full((1, qb), float(seq), F32))
        thr_hi16 = thr_hi.astype(I16)

        def pin(c, carry):
            h = hi_ref[c]
            lo_ref[c] = jnp.where(h > thr_hi16, jnp.int16(I16_MAX),
                                  jnp.where(h == thr_hi16, lo_ref[c], jnp.int16(I16_MIN)))
            return carry

        lax.fori_loop(0, 2 * nwide, pin, 0)
        thr_lo, cnt_thr = search16(lo_ref, cnt_hi)
        thr = thr_hi * 65536 + (thr_lo - I16_MIN)
        thr_ref[...] = jnp.broadcast_to(jnp.where(query_pos < topk, keep_all, thr), thr_ref.shape)

        @pl.when(jnp.max(cnt_thr) > kf)
        def _():
            need = kf - count(lambda kk, c: kk > thr)

            def idx_step(it, pos):
                cand = pos + lax.shift_left(jnp.int32(1), (seq.bit_length() - 2) - it)
                cnt = count(lambda kk, c: (kk == thr) & (key_row + c * kc < cand))
                return jnp.where(cnt < need, cand, pos)

            pos = lax.fori_loop(0, seq.bit_length() - 1, idx_step, jnp.zeros((1, qb), I32))

            def demote(c, carry):
                kk = keys_ref[c]
                keys_ref[c] = jnp.where((kk == thr) & (key_row + c * kc > pos), kk - 1, kk)
                return carry

            lax.fori_loop(0, 2 * nwide, demote, 0)

    q = q_ref[0]
    qs = _split_pair(q[:, :LANES]) + _split_pair(q[:, LANES:])
    thr_row = thr_ref[0:1, :]
    key_col = lax.broadcasted_iota(I32, (1, 2 * kw), 1)

    def shared(c, n):
        halves = [jnp.where(keys_ref[2 * c + h] >= thr_row, 0.0, NEG).T for h in range(2 * n)]
        key_off = (key_col[:, :n * kw] + (c * kw - i * qb)).astype(F32)
        return jnp.concatenate(halves, axis=1), key_off

    def logits(c, n, hd, diagonal, ctx):
        bias, key_off = ctx
        start = pl.multiple_of(c * kw, kw)
        slope = LOG2E * 2.0 ** -(N_HEADS + hd + 1)
        return _nt_dot(qs[hd], k_ref[0, pl.ds(start, n * kw), :]) + (bias + slope * key_off)

    def values(c, n, hd):
        return _values_with_ones(v_ref[0, pl.ds(pl.multiple_of(c * kw, kw), n * kw), :], hd % 2)

    _two_pass_attention(nwide - 1, logits, values, s_ref, mx_ref, acc_ref, N_HEADS, shared)
    o_ref[0] = jnp.concatenate([_pair_output(acc_ref[0], acc_ref[1]), _pair_output(acc_ref[2], acc_ref[3])],
                               axis=1).astype(o_ref.dtype)


def _dsa_call(od3, small3, b, s, topk, qb):
    kc = LANES
    scratch = [pltpu.VMEM((s // kc, kc, qb), I32), pltpu.VMEM((s // kc, kc, qb), I16),
               pltpu.VMEM((s // kc, kc, qb), I16), pltpu.VMEM((SUBLANES, qb), I32),
               pltpu.VMEM((N_HEADS, s // (2 * kc), qb, 2 * kc), F32),
               pltpu.VMEM((N_HEADS, qb, LANES), F32), pltpu.VMEM((N_HEADS, qb, LANES), F32)]
    return pl.pallas_call(
        functools.partial(_dsa_kernel, seq=s, topk=topk, qb=qb),
        grid=(b, s // qb),
        in_specs=[pl.BlockSpec((1, qb, 2 * LANES), lambda bi, i: (bi, i, 0)),
                  pl.BlockSpec((1, s, LANES), lambda bi, i: (bi, 0, 2)),
                  pl.BlockSpec((1, s, LANES), lambda bi, i: (bi, 0, 3)),
                  pl.BlockSpec((1, qb, 2 * LANES), lambda bi, i: (bi, i, 2)),
                  pl.BlockSpec((1, s, LANES), lambda bi, i: (bi, 0, 6)),
                  pl.BlockSpec((1, qb, LANES), lambda bi, i: (bi, i, 0))],
        out_specs=pl.BlockSpec((1, qb, 2 * LANES), lambda bi, i: (bi, i, 0)),
        out_shape=jax.ShapeDtypeStruct((b, s, 2 * LANES), BF16),
        scratch_shapes=scratch,
        compiler_params=pltpu.CompilerParams(dimension_semantics=("arbitrary",) * 2, vmem_limit_bytes=VMEM_LIMIT),
        name="attn_dsa",
    )(od3, od3, od3, od3, od3, small3)


def _outproj_kernel(x_ref, oa_ref, ob_ref, oc_ref, od_ref, w_ref, mod_ref, g2_ref, wr_ref, br_ref,
                    xo_ref, h_ref, rt_ref):
    d_model = x_ref.shape[1]
    merged = jnp.concatenate([oa_ref[...], ob_ref[...], oc_ref[...], od_ref[...]], axis=1)
    mix = jnp.dot(merged, w_ref[...], preferred_element_type=F32)
    xn = x_ref[...] + mod_ref[0, 2:3, :] * mix
    xo_ref[...] = xn
    h = _rms(xn, g2_ref[...]) * (1.0 + mod_ref[0, 4:5, :]) + mod_ref[0, 3:4, :]
    h_ref[:, :d_model] = h

    h_hi = h.astype(BF16)
    h_lo = (h - h_hi.astype(F32)).astype(BF16)
    logits = _nt_dot(wr_ref[0], h_hi) + (_nt_dot(wr_ref[0], h_lo) + _nt_dot(wr_ref[1], h_hi))
    score = 1.0 / (1.0 + jnp.exp(-logits))
    biased = score + br_ref[...]
    srow = [score[e:e + 1, :] for e in range(N_EXPERTS)]
    brow = [biased[e:e + 1, :] for e in range(N_EXPERTS)]
    per = N_EXPERTS // N_GROUPS
    best_v = best_g = None
    for g in range(N_GROUPS):
        r = brow[g * per:(g + 1) * per]
        top2 = None
        for a in range(per):
            for c in range(a + 1, per):
                top2 = r[a] + r[c] if top2 is None else jnp.maximum(top2, r[a] + r[c])
        if g == 0:
            best_v, best_g = top2, jnp.zeros_like(top2, dtype=I32)
        else:
            up = top2 > best_v
            best_v = jnp.where(up, top2, best_v)
            best_g = jnp.where(up, g, best_g)
    cand = [jnp.where(best_g == e // per, brow[e], -jnp.inf) for e in range(N_EXPERTS)]

    def first_max(vals):
        v, idx = vals[0], jnp.zeros_like(best_g)
        for e in range(1, N_EXPERTS):
            up = vals[e] > v
            v = jnp.where(up, vals[e], v)
            idx = jnp.where(up, e, idx)
        return idx

    i1 = first_max(cand)
    i2 = first_max([jnp.where(i1 == e, -jnp.inf, cand[e]) for e in range(N_EXPERTS)])
    s1 = sum(jnp.where(i1 == e, srow[e], 0.0) for e in range(N_EXPERTS))
    s2 = sum(jnp.where(i2 == e, srow[e], 0.0) for e in range(N_EXPERTS))
    den = s1 + s2
    first_low = i1 < i2
    la = jnp.minimum(i1, i2) - per * best_g
    lb = jnp.maximum(i1, i2) - per * best_g
    pair = jnp.where(la == 0, lb - 1, jnp.where(la == 1, lb + 1, 5))
    route = jnp.concatenate([(best_g * N_PAIRS + pair).astype(F32), jnp.where(first_low, s1, s2) / den,
                             jnp.where(first_low, s2, s1) / den, jnp.zeros((SUBLANES - 3, den.shape[1]), F32)], axis=0)
    rt_ref[...] = route
    pad = jnp.zeros((LANES - SUBLANES, den.shape[1]), F32)
    h_ref[:, d_model:] = jnp.concatenate([route, pad], axis=0).T


def _outproj_call(x2, outs, w_out, mod_l, g2, wr_t, br, seq, tm):
    t, d = x2.shape
    tpb = seq // tm
    row = lambda n: pl.BlockSpec((tm, n), lambda i: (i, 0))
    const = lambda shape: pl.BlockSpec(shape, lambda i: (0,) * len(shape))
    return pl.pallas_call(
        _outproj_kernel,
        grid=(t // tm,),
        in_specs=[row(d)] + [row(2 * LANES)] * 4 + [
            const(w_out.shape), pl.BlockSpec((1, 6, d), lambda i: (i // tpb, 0, 0)), const((1, d)),
            const(wr_t.shape), const(br.shape)],
        out_specs=[row(d), row(d + LANES), pl.BlockSpec((SUBLANES, tm), lambda i: (0, i))],
        out_shape=[jax.ShapeDtypeStruct((t, d), F32), jax.ShapeDtypeStruct((t, d + LANES), F32),
                   jax.ShapeDtypeStruct((SUBLANES, t), F32)],
        compiler_params=pltpu.CompilerParams(dimension_semantics=("arbitrary",), vmem_limit_bytes=VMEM_LIMIT),
        name="outproj_router",
    )(x2, *outs, w_out, mod_l, g2, wr_t, br)


def _routing_tables(cls, n_steps):
    t = cls.shape[0]
    onehot = (cls[:, None] == jnp.arange(N_CLASSES, dtype=I32)[None, :]).astype(I32)
    upto = jnp.cumsum(onehot, axis=0)
    rank = jnp.sum((upto - onehot) * onehot, axis=1)
    padded = (upto[-1] + MOE_ROWS - 1) // MOE_ROWS * MOE_ROWS
    ends = jnp.cumsum(padded)
    pos = jnp.sum(onehot * (ends - padded)[None, :], axis=1) + rank
    step_cls = jnp.sum((jnp.arange(n_steps, dtype=I32) * MOE_ROWS)[:, None] >= ends[None, :], axis=1)
    valid = (step_cls < N_CLASSES).astype(I32)
    step_cls = jnp.minimum(step_cls, N_CLASSES - 1)
    base = (step_cls // N_PAIRS) * (N_EXPERTS // N_GROUPS)
    e_low = base + jnp.asarray(PAIR_LOW, I32)[step_cls % N_PAIRS]
    e_high = base + jnp.asarray(PAIR_HIGH, I32)[step_cls % N_PAIRS]
    return pos.reshape(t // PERM_ROWS, 1, PERM_ROWS), e_low, e_high, valid


def _row_copy(src_ref, src_row, dst_ref, dst_row, sem):
    return pltpu.make_async_copy(src_ref.at[pl.ds(src_row, 1)], dst_ref.at[pl.ds(dst_row, 1)], sem)


def _issue_rows(copy_row):
    def issue(r8, carry):
        for u in range(ROW_UNROLL):
            copy_row(r8 * ROW_UNROLL + u).start()
        return carry

    lax.fori_loop(0, PERM_ROWS // ROW_UNROLL, issue, 0)


def _rows_done(src_ref, dst_ref, sem):
    pltpu.make_async_copy(src_ref.at[pl.ds(0, PERM_ROWS)], dst_ref.at[pl.ds(0, PERM_ROWS)], sem).wait()


def _scatter_rows_kernel(pos_ref, src_ref, init_ref, dst_ref, sem):
    del init_ref
    _issue_rows(lambda r: _row_copy(src_ref, r, dst_ref, pos_ref[0, 0, r], sem))
    _rows_done(src_ref, dst_ref, sem)


def _scatter_rows_call(pos3, src, n_sorted):
    t, d = src.shape
    return pl.pallas_call(
        _scatter_rows_kernel,
        grid=(t // PERM_ROWS,),
        in_specs=[pl.BlockSpec((1, 1, PERM_ROWS), lambda j: (j, 0, 0), memory_space=pltpu.SMEM),
                  pl.BlockSpec((PERM_ROWS, d), lambda j: (j, 0)), pl.BlockSpec(memory_space=pl.ANY)],
        out_specs=pl.BlockSpec(memory_space=pl.ANY),
        out_shape=jax.ShapeDtypeStruct((n_sorted, d), src.dtype),
        scratch_shapes=[pltpu.SemaphoreType.DMA(())],
        input_output_aliases={2: 0},
        compiler_params=pltpu.CompilerParams(dimension_semantics=("arbitrary",), has_side_effects=True),
        name="moe_sort_rows",
    )(pos3, src, jnp.zeros((n_sorted, d), src.dtype))


def _moe_kernel(e_low_ref, e_high_ref, valid_ref, h_ref, wgu_low_ref, wgu_high_ref, wd_low_ref, wd_high_ref, o_ref):
    del e_low_ref, e_high_ref
    step = pl.program_id(0)
    d_model = o_ref.shape[1]

    @pl.when(valid_ref[step] == 0)
    def _():
        o_ref[...] = jnp.zeros(o_ref.shape, F32)

    @pl.when(valid_ref[step] != 0)
    def _():
        h = h_ref[:, :d_model].astype(BF16)
        w_low = h_ref[:, d_model + 1:d_model + 2]
        w_high = h_ref[:, d_model + 2:d_model + 3]

        def expert(wgu_ref, wd_ref):
            gu = jnp.dot(h, wgu_ref[0], preferred_element_type=F32)
            gate, up = gu[:, :D_EXPERT], gu[:, D_EXPERT:]
            hid = (gate / (1.0 + jnp.exp(-gate)) * up).astype(BF16)
            return jnp.dot(hid, wd_ref[0], preferred_element_type=F32)

        o_ref[...] = w_low * expert(wgu_low_ref, wd_low_ref) + w_high * expert(wgu_high_ref, wd_high_ref)


def _moe_call(hs, e_low, e_high, valid, wgu, wd):
    n_sorted = hs.shape[0]
    d = wd.shape[2]
    grid_spec = pltpu.PrefetchScalarGridSpec(
        num_scalar_prefetch=3,
        grid=(n_sorted // MOE_ROWS,),
        in_specs=[pl.BlockSpec((MOE_ROWS, hs.shape[1]), lambda j, lo, hi, ok: (j, 0)),
                  pl.BlockSpec((1, d, 2 * D_EXPERT), lambda j, lo, hi, ok: (lo[j], 0, 0)),
                  pl.BlockSpec((1, d, 2 * D_EXPERT), lambda j, lo, hi, ok: (hi[j], 0, 0)),
                  pl.BlockSpec((1, D_EXPERT, d), lambda j, lo, hi, ok: (lo[j], 0, 0)),
                  pl.BlockSpec((1, D_EXPERT, d), lambda j, lo, hi, ok: (hi[j], 0, 0))],
        out_specs=pl.BlockSpec((MOE_ROWS, d), lambda j, lo, hi, ok: (j, 0)))
    return pl.pallas_call(
        _moe_kernel,
        grid_spec=grid_spec,
        out_shape=jax.ShapeDtypeStruct((n_sorted, d), F32),
        compiler_params=pltpu.CompilerParams(dimension_semantics=("arbitrary",), vmem_limit_bytes=VMEM_LIMIT),
        name="moe",
    )(e_low, e_high, valid, hs, wgu, wgu, wd, wd)


def _gather_residual_kernel(pos_ref, pos_next_ref, ys_ref, x_ref, mod_ref, gf_ref, o_ref, rows_ref, sems, *, final):
    j = pl.program_id(0)
    slot = j % 2

    def request(index_ref, into):
        _issue_rows(lambda r: _row_copy(ys_ref, index_ref[0, 0, r], rows_ref.at[into], r, sems.at[into]))

    @pl.when(j == 0)
    def _():
        request(pos_ref, slot)

    @pl.when(j + 1 < pl.num_programs(0))
    def _():
        request(pos_next_ref, 1 - slot)

    _rows_done(ys_ref, rows_ref.at[slot], sems.at[slot])
    xn = x_ref[...] + mod_ref[0, 5:6, :] * rows_ref[slot]
    o_ref[...] = _rms(xn, gf_ref[...]) if final else xn


def _gather_residual_call(pos3, ys, x2, mod_l, gf, seq, final):
    t, d = x2.shape
    tpb = seq // PERM_ROWS
    last = t // PERM_ROWS - 1
    return pl.pallas_call(
        functools.partial(_gather_residual_kernel, final=final),
        grid=(t // PERM_ROWS,),
        in_specs=[pl.BlockSpec((1, 1, PERM_ROWS), lambda j: (j, 0, 0), memory_space=pltpu.SMEM),
                  pl.BlockSpec((1, 1, PERM_ROWS), lambda j: (jnp.minimum(j + 1, last), 0, 0),
                               memory_space=pltpu.SMEM),
                  pl.BlockSpec(memory_space=pl.ANY),
                  pl.BlockSpec((PERM_ROWS, d), lambda j: (j, 0)),
                  pl.BlockSpec((1, 6, d), lambda j: (j // tpb, 0, 0)),
                  pl.BlockSpec((1, d), lambda j: (0, 0))],
        out_specs=pl.BlockSpec((PERM_ROWS, d), lambda j: (j, 0)),
        out_shape=jax.ShapeDtypeStruct((t, d), F32),
        scratch_shapes=[pltpu.VMEM((2, PERM_ROWS, d), F32), pltpu.SemaphoreType.DMA((2,))],
        compiler_params=pltpu.CompilerParams(dimension_semantics=("arbitrary",), vmem_limit_bytes=VMEM_LIMIT),
        name="moe_unsort_residual",
    )(pos3, pos3, ys, x2, mod_l, gf)


def _swap_half(w):
    half = w.shape[1] // 2
    return jnp.concatenate([-w[:, half:], w[:, :half]], axis=1)


def _layer_weights(w_in, w_q_up, w_kv_up):
    d = w_in.shape[0]
    pts = np.cumsum(IN_SIZES)[:-1].tolist()
    (a_q, a_k, a_v, a_f, b_cq, b_ckv, b_kr, c_q, c_k, c_v,
     d_q, d_k, d_v, d_qi, d_ki, d_wi) = jnp.split(w_in, pts, axis=1)
    qs = HEAD_DIM ** -0.5
    small = jnp.concatenate([
        a_f, d_wi * ((IDX_HEADS * IDX_DIM) ** -0.5),
        jnp.zeros((d, _KR_LANE - _WI_LANE - IDX_HEADS), F32),
        b_kr, _swap_half(b_kr), d_ki], axis=1)
    w_all = jnp.concatenate([
        a_q * qs, a_k, a_v,
        c_q * qs, c_k, c_v,
        d_q * qs, d_qi, d_k, d_v,
        small, b_cq, b_ckv], axis=1).astype(BF16)

    per_q = MLA_NOPE + MLA_ROPE
    wq = []
    for hd in range(N_HEADS):
        blk = w_q_up[:, hd * per_q:(hd + 1) * per_q]
        rot = blk[:, MLA_NOPE:]
        wq += [blk[:, :MLA_NOPE], rot, _swap_half(rot)]
    wq = jnp.concatenate(wq, axis=1).astype(BF16)

    place = np.zeros((LANES, LANES), np.float32)
    place[_KR_LANE + np.arange(MLA_ROPE), MLA_NOPE + np.arange(MLA_ROPE)] = 1.0
    place = jnp.asarray(place)
    kcols, vcols = [], []
    for hd in range(N_HEADS):
        blk = w_kv_up[:, hd * 2 * HEAD_DIM:(hd + 1) * 2 * HEAD_DIM]
        knope = jnp.concatenate([blk[:, :MLA_NOPE], jnp.zeros((MLA_KV_RANK, LANES - MLA_NOPE), F32)], axis=1)
        kcols.append(jnp.concatenate([knope, place], axis=0))
        vcols.append(jnp.concatenate([blk[:, MLA_NOPE:], jnp.zeros((LANES, HEAD_DIM), F32)], axis=0))
    wkv = jnp.concatenate(kcols + vcols, axis=1).astype(BF16)
    return w_all, wq, wkv


def _rope_tables(seq):
    half = MLA_ROPE // 2
    inv = ROPE_THETA ** (-jnp.arange(half, dtype=F32) / half)
    ang = jnp.arange(seq, dtype=F32)[:, None] * inv[None, :]
    cos = jnp.tile(jnp.cos(ang), (1, 2))
    sin = jnp.tile(jnp.sin(ang), (1, 2))
    scale = LOG2E * (MLA_NOPE + MLA_ROPE) ** -0.5
    z = lambda n: jnp.zeros((seq, n), F32)
    tab_q = jnp.concatenate([jnp.full((seq, MLA_NOPE), scale, F32), cos * scale, z(MLA_ROPE)], axis=1)
    tab_qs = jnp.concatenate([z(MLA_NOPE), sin * scale, z(MLA_ROPE)], axis=1)
    tab_k = jnp.concatenate([z(_KR_LANE), cos, z(LANES - _KR_LANE - MLA_ROPE)], axis=1)
    tab_ks = jnp.concatenate([z(_KR_LANE), sin, z(LANES - _KR_LANE - MLA_ROPE)], axis=1)
    return jnp.stack([tab_q, tab_qs, tab_k, tab_ks])


def kernel(x, c, w_ada, b_ada, g_norm1, w_in, b_forget, g_q_mla, w_q_up, g_kv_mla, w_kv_up, sinks, w_out,
           g_norm2, w_router, b_router, w_gate, w_up, w_down, g_final):
    b, s, d = x.shape
    depth = w_in.shape[0]
    t = b * s
    topk = min(TOPK_MAX, s // 4)
    tm = min(1024, s)
    tq = min(512, s)
    tq_swa = min(512, s)
    qb_dsa = 2 * LANES
    n_steps = t // MOE_ROWS + N_CLASSES
    assert s % tm == 0 and s % tq == 0 and s % qb_dsa == 0 and s % PERM_ROWS == 0

    mod = _ada_call(c, w_ada, b_ada).reshape(depth, b, 6, d)
    tabs = _rope_tables(s)
    wr_hi = w_router.T.astype(BF16)
    wr_t = jnp.stack([wr_hi, (w_router.T - wr_hi.astype(F32)).astype(BF16)])
    br = b_router.reshape(N_EXPERTS, 1)
    x2 = x.reshape(t, d)
    for l in range(depth):
        w_all, wq, wkv = _layer_weights(w_in[l], w_q_up[l], w_kv_up[l])
        oa, oc, od, osm, oqb, okb, ovb = _inproj_call(
            x2, mod[l], g_norm1[l].reshape(1, d), w_all, wq, wkv,
            g_q_mla[l].reshape(1, -1), g_kv_mla[l].reshape(1, -1), tabs, s, tm)
        small3 = osm.reshape(b, s, LANES)
        bf = jnp.zeros((1, LANES), F32).at[0, :N_HEADS].set(b_forget[l])
        fcum = _fox_cumsum_call(small3, bf)
        frow = jnp.swapaxes(fcum[:, :, :8], 1, 2)
        out_a = _causal_attn_call((oa.reshape(b, s, -1), fcum, frow), True, b, s, tq)
        out_b = _causal_attn_call((oqb.reshape(b, s, -1), okb.reshape(b, s, -1), ovb.reshape(b, s, -1)),
                                  False, b, s, tq)
        out_c = _swa_call(oc.reshape(b, s, -1), sinks[l], b, s, tq_swa)
        out_d = _dsa_call(od.reshape(b, s, -1), small3, b, s, topk, qb_dsa)
        outs = [o.reshape(t, 2 * LANES) for o in (out_a, out_b, out_c, out_d)]
        x2, h2, route = _outproj_call(x2, outs, w_out[l].astype(BF16), mod[l], g_norm2[l].reshape(1, d),
                                      wr_t, br, s, tm)
        pos3, e_low, e_high, valid = _routing_tables(route[0].astype(I32), n_steps)
        wgu = jnp.concatenate([w_gate[l], w_up[l]], axis=2).astype(BF16)
        hs = _scatter_rows_call(pos3, h2, n_steps * MOE_ROWS)
        ys = _moe_call(hs, e_low, e_high, valid, wgu, w_down[l].astype(BF16))
        x2 = _gather_residual_call(pos3, ys, x2, mod[l], g_final.reshape(1, d), s, l == depth - 1)
    return x2.reshape(b, s, d)
```

```python
import functools

import jax
import jax.numpy as jnp
import numpy as np
from jax import lax
from jax.experimental import pallas as pl
from jax.experimental.pallas import tpu as pltpu

F32 = jnp.float32
BF16 = jnp.bfloat16
I32 = jnp.int32
I16 = jnp.int16
I16_MIN, I16_MAX = -(2 ** 15), 2 ** 15 - 1

EPS = 1e-6
HEAD_DIM = 64
LANES = 128
SUBLANES = 8
N_HEADS = 4
MLA_Q_RANK = 256
MLA_KV_RANK = 128
MLA_NOPE = 64
MLA_ROPE = 32
ROPE_THETA = 10000.0
WINDOW = 128
IDX_HEADS = 8
IDX_DIM = 32
TOPK_MAX = 256
N_EXPERTS = 16
N_GROUPS = 4
N_PAIRS = 6
N_CLASSES = N_GROUPS * N_PAIRS
PAIR_LOW = (0, 0, 0, 1, 1, 2)
PAIR_HIGH = (1, 2, 3, 2, 3, 3)
D_EXPERT = 256
MOE_ROWS = 512
PERM_ROWS = 512
ROW_UNROLL = 8
IN_SIZES = (256, 256, 256, 4, 256, 128, 32, 256, 128, 128, 256, 64, 64, 256, 32, 8)

LOG2E = 1.4426950408889634
NEG = -1e30
INT_MIN = -(2 ** 31)
KEY_NEG_INF = INT_MIN + 0x7FFFFF
VMEM_LIMIT = 56 * 1024 * 1024

_CA, _CC, _CD, _CS, _CQ, _CKV, _CEND = 0, 768, 1280, 1920, 2048, 2304, 2432
_F_LANE, _WI_LANE, _KR_LANE = 0, 4, 32


def _nt_dot(a, b):
    return lax.dot_general(a, b, (((1,), (1,)), ((), ())), preferred_element_type=F32)


def _rms(x, g):
    return x * lax.rsqrt(jnp.mean(x * x, axis=-1, keepdims=True) + EPS) * g


def _ada_kernel(c_ref, w_ref, b_ref, o_ref):
    c = c_ref[...]
    act = (c / (1.0 + jnp.exp(-c))).astype(BF16)
    o_ref[0] = jnp.dot(act, w_ref[0].astype(BF16), preferred_element_type=F32) + b_ref[0]


def _ada_call(c, w_ada, b_ada):
    depth, d, n = w_ada.shape
    bsz = c.shape[0]
    tn = 1024
    return pl.pallas_call(
        _ada_kernel,
        grid=(depth, n // tn),
        in_specs=[pl.BlockSpec((bsz, d), lambda l, j: (0, 0)),
                  pl.BlockSpec((1, d, tn), lambda l, j: (l, 0, j)),
                  pl.BlockSpec((1, 1, tn), lambda l, j: (l, 0, j))],
        out_specs=pl.BlockSpec((1, bsz, tn), lambda l, j: (l, 0, j)),
        out_shape=jax.ShapeDtypeStruct((depth, bsz, n), F32),
        compiler_params=pltpu.CompilerParams(dimension_semantics=("arbitrary", "arbitrary"),
                                             vmem_limit_bytes=VMEM_LIMIT),
        name="adaln",
    )(c, w_ada, b_ada.reshape(depth, 1, n))


def _inproj_kernel(x_ref, mod_ref, g1_ref, w_ref, wq_ref, wkv_ref, gq_ref, gkv_ref, tab_ref,
                   oa_ref, oc_ref, od_ref, os_ref, oqb_ref, okb_ref, ovb_ref):
    h = _rms(x_ref[...], g1_ref[...]) * (1.0 + mod_ref[0, 1:2, :]) + mod_ref[0, 0:1, :]
    h = h.astype(BF16)

    def proj(lo, hi):
        return jnp.dot(h, w_ref[:, lo:hi], preferred_element_type=F32)

    q_width = N_HEADS * HEAD_DIM
    low = _lane_is_low()

    def both_halves(pair):
        swapped = pltpu.roll(pair, HEAD_DIM, axis=1)
        return jnp.where(low, pair, swapped), jnp.where(low, swapped, pair)

    pa = proj(_CA, _CC)
    oa_ref[:, :q_width] = (pa[:, :q_width] * LOG2E).astype(BF16)
    oa_ref[:, q_width:] = pa[:, q_width:].astype(BF16)

    pc = proj(_CC, _CD)
    oc_ref[:, :q_width] = (pc[:, :q_width] * LOG2E).astype(BF16)
    for n in range(2):
        tiles = both_halves(pc[:, q_width + n * LANES:q_width + (n + 1) * LANES])
        for g in range(2):
            lo_col = q_width + (2 * n + g) * LANES
            oc_ref[:, lo_col:lo_col + LANES] = tiles[g].astype(BF16)

    pd = proj(_CD, _CS)
    small = proj(_CS, _CQ)
    os_ref[...] = small
    od_ref[:, :q_width] = (pd[:, :q_width] * LOG2E).astype(BF16)
    k_twice, v_twice = both_halves(pd[:, 2 * q_width:])
    od_ref[:, q_width:q_width + LANES] = k_twice.astype(BF16)
    od_ref[:, q_width + LANES:2 * q_width] = v_twice.astype(BF16)
    od_ref[:, 2 * q_width:3 * q_width] = pd[:, q_width:2 * q_width].astype(BF16)
    lane = lax.broadcasted_iota(I32, (1, LANES), 1)
    ki4 = small
    for n in range(1, LANES // IDX_DIM):
        ki4 = jnp.where(lane < LANES - n * IDX_DIM, pltpu.roll(small, LANES - n * IDX_DIM, axis=1), ki4)
    od_ref[:, 3 * q_width:] = ki4.astype(BF16)

    cq = _rms(proj(_CQ, _CKV), gq_ref[...]).astype(BF16)
    qf = jnp.dot(cq, wq_ref[...], preferred_element_type=F32)
    tab_q, tab_qs, tab_k, tab_ks = tab_ref[0], tab_ref[1], tab_ref[2], tab_ref[3]
    for hd in range(N_HEADS):
        qg = qf[:, hd * LANES:(hd + 1) * LANES]
        qr = qg * tab_q + pltpu.roll(qg, LANES - MLA_ROPE, axis=1) * tab_qs
        oqb_ref[:, hd * LANES:(hd + 1) * LANES] = qr.astype(BF16)

    ckv = _rms(proj(_CKV, _CEND), gkv_ref[...]).astype(BF16)
    kr = (small * tab_k + pltpu.roll(small, LANES - MLA_ROPE, axis=1) * tab_ks).astype(BF16)
    kvf = jnp.dot(jnp.concatenate([ckv, kr], axis=1), wkv_ref[...], preferred_element_type=F32)
    okb_ref[...] = kvf[:, :4 * LANES].astype(BF16)
    ovb_ref[...] = kvf[:, 4 * LANES:].astype(BF16)


def _inproj_call(x2, mod_l, g1, w_all, wq, wkv, gq, gkv, tabs, seq, tm):
    t, d = x2.shape
    tpb = seq // tm
    row = lambda n: pl.BlockSpec((tm, n), lambda i: (i, 0))
    const = lambda shape: pl.BlockSpec(shape, lambda i: (0,) * len(shape))
    widths = (768, 768, 896, 128, 512, 512, 256)
    dtypes = (BF16, BF16, BF16, F32, BF16, BF16, BF16)
    return pl.pallas_call(
        _inproj_kernel,
        grid=(t // tm,),
        in_specs=[row(d),
                  pl.BlockSpec((1, 6, d), lambda i: (i // tpb, 0, 0)),
                  const((1, d)), const(w_all.shape), const(wq.shape), const(wkv.shape),
                  const((1, MLA_Q_RANK)), const((1, MLA_KV_RANK)),
                  pl.BlockSpec((4, tm, LANES), lambda i: (0, i % tpb, 0))],
        out_specs=[row(n) for n in widths],
        out_shape=[jax.ShapeDtypeStruct((t, n), dt) for n, dt in zip(widths, dtypes)],
        compiler_params=pltpu.CompilerParams(dimension_semantics=("arbitrary",), vmem_limit_bytes=VMEM_LIMIT),
        name="inproj",
    )(x2, mod_l, g1, w_all, wq, wkv, gq, gkv, tabs)


def _fox_cumsum_kernel(s_ref, b_ref, o_ref):
    z = s_ref[0] + b_ref[...]
    lf = jnp.minimum(z, 0.0) - jnp.log(1.0 + jnp.exp(-jnp.abs(z)))
    n = lf.shape[0]
    row = lax.broadcasted_iota(I32, lf.shape, 0)
    d = 1
    while d < n:
        lf = lf + jnp.where(row >= d, pltpu.roll(lf, d, axis=0), 0.0)
        d *= 2
    o_ref[0] = lf * LOG2E


def _fox_cumsum_call(small3, bf):
    b, s, _ = small3.shape
    return pl.pallas_call(
        _fox_cumsum_kernel,
        grid=(b,),
        in_specs=[pl.BlockSpec((1, s, LANES), lambda i: (i, 0, 0)),
                  pl.BlockSpec((1, LANES), lambda i: (0, 0))],
        out_specs=pl.BlockSpec((1, s, LANES), lambda i: (i, 0, 0)),
        out_shape=jax.ShapeDtypeStruct((b, s, LANES), F32),
        compiler_params=pltpu.CompilerParams(dimension_semantics=("arbitrary",), vmem_limit_bytes=VMEM_LIMIT),
        name="fox_cumsum",
    )(small3, bf)


def _lane_is_low():
    return lax.broadcasted_iota(I32, (1, LANES), 1) < HEAD_DIM


def _split_pair(q):
    low = _lane_is_low()
    zero = jnp.zeros_like(q)
    return jnp.where(low, q, zero), jnp.where(low, zero, q)


def _values_with_ones(v, j):
    low = _lane_is_low()
    return jnp.where(low if j == 0 else ~low, v, jnp.ones_like(v))


def _pair_output(a0, a1, extra0=None, extra1=None):
    l0 = pltpu.roll(a0, HEAD_DIM, axis=1)
    l1 = pltpu.roll(a1, HEAD_DIM, axis=1)
    if extra0 is not None:
        l0, l1 = l0 + extra0, l1 + extra1
    return jnp.where(_lane_is_low(), a0 / l0, a1 / l1)


def _two_pass_attention(n_before, logits, values, s_ref, mx_ref, acc_ref, n_heads, shared=None, diagonal_mask=True):
    tq, tk = s_ref.shape[-2:]
    mx_ref[...] = jnp.full(mx_ref.shape, NEG, F32)

    def store(c, n, diagonal):
        ctx = shared(c, n) if shared is not None else None
        for j in range(n_heads):
            s = logits(c, n, j, diagonal, ctx)
            for t in range(n):
                s_ref[j, c + t] = s[:, t * tk:(t + 1) * tk]
            parts = [s[:, blk * LANES:(blk + 1) * LANES] for blk in range(n * tk // LANES)]
            while len(parts) > 1:
                parts = [jnp.maximum(a, b) for a, b in zip(parts[::2], parts[1::2])]
            mx_ref[j] = jnp.maximum(mx_ref[j], parts[0])

    def store_two(c2, carry):
        store(2 * c2, 2, False)
        return carry

    if diagonal_mask:
        lax.fori_loop(0, n_before // 2, store_two, 0)

        @pl.when(n_before % 2 == 1)
        def _():
            store(n_before - 1, 1, False)

        store(n_before, 1, True)
    else:
        lax.fori_loop(0, (n_before + 1) // 2, store_two, 0)

        @pl.when(n_before % 2 == 0)
        def _():
            store(n_before, 1, False)

    row_max = [jnp.max(mx_ref[j], axis=1, keepdims=True) for j in range(n_heads)]
    shift = [jnp.broadcast_to(m, (tq, tk)) for m in row_max]
    acc_ref[...] = jnp.zeros(acc_ref.shape, F32)

    def accum(c, n):
        for j in range(n_heads):
            p = [jnp.exp2(s_ref[j, c + t] - shift[j]).astype(BF16) for t in range(n)]
            p = p[0] if n == 1 else jnp.concatenate(p, axis=1)
            acc_ref[j] += jnp.dot(p, values(c, n, j), preferred_element_type=F32)

    def accum_two(c2, carry):
        accum(2 * c2, 2)
        return carry

    lax.fori_loop(0, (n_before + 1) // 2, accum_two, 0)

    @pl.when(n_before % 2 == 0)
    def _():
        accum(n_before, 1)

    return row_max


def _causal_attn_kernel(*refs, fox, tq):
    if fox:
        q_ref, k_ref, v_ref, fc_ref, fr_ref, o_ref, s_ref, mx_ref, acc_ref = refs
    else:
        q_ref, k_ref, v_ref, o_ref, s_ref, mx_ref, acc_ref = refs
    tk = tq
    hp = pl.program_id(1)
    i = pl.program_id(2)
    if fox:
        qs = _split_pair(q_ref[0])
        lane = lax.broadcasted_iota(I32, (1, LANES), 1)
        fcol = [jnp.sum(jnp.where(lane == 2 * hp + j, fc_ref[0], 0.0), axis=1, keepdims=True) for j in range(2)]
    else:
        qs = (q_ref[0, :, :LANES], q_ref[0, :, LANES:])
    causal = (lax.broadcasted_iota(I32, (tq, tk), 1) <= lax.broadcasted_iota(I32, (tq, tk), 0))

    def logits(c, n, j, diagonal, _):
        start = pl.multiple_of(c * tk, tk)
        if fox:
            s = _nt_dot(qs[j], k_ref[0, pl.ds(start, n * tk), :])
            s = (s + fcol[j]) - fr_ref[0, pl.ds(2 * hp + j, 1), pl.ds(start, n * tk)]
        else:
            s = _nt_dot(qs[j], k_ref[0, pl.ds(start, n * tk), j * LANES:(j + 1) * LANES])
        return jnp.where(causal, s, NEG) if diagonal else s

    def values(c, n, j):
        return _values_with_ones(v_ref[0, pl.ds(pl.multiple_of(c * tk, tk), n * tk), :], j)

    _two_pass_attention(i, logits, values, s_ref, mx_ref, acc_ref, 2)
    o_ref[0] = _pair_output(acc_ref[0], acc_ref[1]).astype(o_ref.dtype)


def _causal_attn_call(arrs, fox, b, s, tq):
    scratch = [pltpu.VMEM((2, s // tq, tq, tq), F32), pltpu.VMEM((2, tq, LANES), F32),
               pltpu.VMEM((2, tq, LANES), F32)]
    if fox:
        qkv, fcol, frow = arrs
        operands = (qkv, qkv, qkv, fcol, frow)
        in_specs = [pl.BlockSpec((1, tq, LANES), lambda bi, hp, i: (bi, i, hp)),
                    pl.BlockSpec((1, s, LANES), lambda bi, hp, i: (bi, 0, 2 + hp)),
                    pl.BlockSpec((1, s, LANES), lambda bi, hp, i: (bi, 0, 4 + hp)),
                    pl.BlockSpec((1, tq, LANES), lambda bi, hp, i: (bi, i, 0)),
                    pl.BlockSpec((1, 8, s), lambda bi, hp, i: (bi, 0, 0))]
    else:
        operands = arrs
        in_specs = [pl.BlockSpec((1, tq, 2 * LANES), lambda bi, hp, i: (bi, i, hp)),
                    pl.BlockSpec((1, s, 2 * LANES), lambda bi, hp, i: (bi, 0, hp)),
                    pl.BlockSpec((1, s, LANES), lambda bi, hp, i: (bi, 0, hp))]
    return pl.pallas_call(
        functools.partial(_causal_attn_kernel, fox=fox, tq=tq),
        grid=(b, 2, s // tq),
        in_specs=in_specs,
        out_specs=pl.BlockSpec((1, tq, LANES), lambda bi, hp, i: (bi, i, hp)),
        out_shape=jax.ShapeDtypeStruct((b, s, 2 * LANES), BF16),
        scratch_shapes=scratch,
        compiler_params=pltpu.CompilerParams(dimension_semantics=("arbitrary",) * 3, vmem_limit_bytes=VMEM_LIMIT),
        name="attn_fox" if fox else "attn_mla",
    )(*operands)


def _alibi_slope(head):
    return lax.shift_left(jnp.int32(1), 7 - head).astype(F32) * (LOG2E * 2.0 ** -8)


def _swa_kernel(sink_ref, q_ref, k_ref, v_ref, o_ref, *, tq):
    hp = pl.program_id(1)
    i = pl.program_id(2)
    band = 2 * WINDOW
    rel = lax.broadcasted_iota(I32, (WINDOW, band), 0) - lax.broadcasted_iota(I32, (WINDOW, band), 1)
    for r in range(tq // WINDOW):
        q_start = i * tq + r * WINDOW
        k_start = pl.multiple_of(jnp.maximum(q_start - WINDOW, 0), WINDOW)
        kb = k_ref[0, pl.ds(k_start, band), :]
        vb = v_ref[0, pl.ds(k_start, band), :]
        dist = rel + (q_start - k_start)
        valid = (dist >= 0) & (dist < WINDOW)
        distf = dist.astype(F32)
        qs = _split_pair(q_ref[0, r * WINDOW:(r + 1) * WINDOW, :])
        acc, sink_term = [], []
        for j in range(2):
            sink = sink_ref[2 * hp + j] * LOG2E
            s = jnp.where(valid, _nt_dot(qs[j], kb) - _alibi_slope(2 * hp + j) * distf, NEG)
            m = jnp.maximum(jnp.max(s, axis=1, keepdims=True), sink)
            p = jnp.exp2(s - m).astype(BF16)
            acc.append(jnp.dot(p, _values_with_ones(vb, j), preferred_element_type=F32))
            sink_term.append(jnp.exp2(sink - m))
        o_ref[0, r * WINDOW:(r + 1) * WINDOW, :] = _pair_output(acc[0], acc[1], *sink_term).astype(o_ref.dtype)


def _swa_call(qkv, sinks, b, s, tq):
    return pl.pallas_call(
        functools.partial(_swa_kernel, tq=tq),
        grid=(b, 2, s // tq),
        in_specs=[pl.BlockSpec(memory_space=pltpu.SMEM),
                  pl.BlockSpec((1, tq, LANES), lambda bi, hp, i: (bi, i, hp)),
                  pl.BlockSpec((1, s, LANES), lambda bi, hp, i: (bi, 0, 2 + hp)),
                  pl.BlockSpec((1, s, LANES), lambda bi, hp, i: (bi, 0, 4 + hp))],
        out_specs=pl.BlockSpec((1, tq, LANES), lambda bi, hp, i: (bi, i, hp)),
        out_shape=jax.ShapeDtypeStruct((b, s, 2 * LANES), BF16),
        compiler_params=pltpu.CompilerParams(dimension_semantics=("arbitrary",) * 3, vmem_limit_bytes=VMEM_LIMIT),
        name="attn_swa",
    )(sinks, qkv, qkv, qkv)


def _dsa_kernel(q_ref, k_ref, v_ref, qi_ref, ki_ref, wi_ref, o_ref, keys_ref, hi_ref, lo_ref, thr_ref,
                s_ref, mx_ref, acc_ref, *, seq, topk, qb):
    kc = LANES
    kw = 2 * kc
    i = pl.program_id(1)
    nwide = (i + 1) * (qb // kw)
    lane = lax.broadcasted_iota(I32, (1, LANES), 1)
    key_row = lax.broadcasted_iota(I32, (kc, qb), 0)
    query_pos = i * qb + lax.broadcasted_iota(I32, (1, qb), 1)

    qi = qi_ref[0]
    wi_t = wi_ref[0].T
    qms, wrows = [], []
    for hd in range(IDX_HEADS):
        g, r = divmod(hd, LANES // IDX_DIM)
        sel = (lane >= r * IDX_DIM) & (lane < (r + 1) * IDX_DIM)
        qg = qi[:, g * LANES:(g + 1) * LANES]
        qms.append(jnp.where(sel, qg, jnp.zeros_like(qg)))
        wrows.append(wi_t[_WI_LANE + hd:_WI_LANE + hd + 1, :])
    q_all = jnp.concatenate(qms, axis=0)
    key_minus_query = (lax.broadcasted_iota(I32, (kw, qb), 0) - lax.broadcasted_iota(I32, (kw, qb), 1))

    def score_chunk(c, carry):
        start = pl.multiple_of(c * kw, kw)
        logit = _nt_dot(ki_ref[0, pl.ds(start, kw), :], q_all)
        sc = wrows[0] * jnp.maximum(logit[:, :qb], 0.0)
        for hd in range(1, IDX_HEADS):
            sc = sc + wrows[hd] * jnp.maximum(logit[:, hd * qb:(hd + 1) * qb], 0.0)
        sc = jnp.where(key_minus_query <= i * qb - start, sc, -jnp.inf)
        sc = jnp.where(sc == 0.0, 0.0, sc)
        bits = pltpu.bitcast(sc, I32)
        keys = bits ^ ((bits >> 31) & 0x7FFFFFFF)
        keys_ref[2 * c] = keys[:kc]
        keys_ref[2 * c + 1] = keys[kc:]
        hi = (keys >> 16).astype(I16)
        lo = ((keys & 0xFFFF) + I16_MIN).astype(I16)
        hi_ref[2 * c] = hi[:kc]
        hi_ref[2 * c + 1] = hi[kc:]
        lo_ref[2 * c] = lo[:kc]
        lo_ref[2 * c + 1] = lo[kc:]
        return carry

    lax.fori_loop(0, nwide, score_chunk, 0)

    keep_all = jnp.full((1, qb), KEY_NEG_INF + 1, I32)
    thr_ref[...] = jnp.broadcast_to(keep_all, thr_ref.shape)

    def count(pred):
        def body(c2, acc):
            parts = []
            for c in (2 * c2, 2 * c2 + 1):
                hit = jnp.where(pred(keys_ref[c], c), 1.0, 0.0)
                parts += [hit[r * SUBLANES:(r + 1) * SUBLANES] for r in range(kc // SUBLANES)]
            while len(parts) > 1:
                parts = [a + b for a, b in zip(parts[::2], parts[1::2])]
            return acc + parts[0]
        acc = lax.fori_loop(0, nwide, body, jnp.zeros((SUBLANES, qb), F32))
        return jnp.sum(acc, axis=0, keepdims=True)

    @pl.when((i + 1) * qb > topk)
    def _():
        kf = float(topk)

        pack = 2 * SUBLANES

        def count16(plane_ref, cand):
            cand16 = cand.astype(I16)

            def body(c2, acc):
                parts = []
                for c in (2 * c2, 2 * c2 + 1):
                    hit = jnp.where(plane_ref[c] >= cand16, jnp.int16(1), jnp.int16(0))
                    parts += [hit[r * pack:(r + 1) * pack] for r in range(kc // pack)]
                while len(parts) > 1:
                    parts = [a + b for a, b in zip(parts[::2], parts[1::2])]
                return acc + parts[0]

            acc = lax.fori_loop(0, nwide, body, jnp.zeros((pack, qb), I16))
            return jnp.sum(acc.astype(F32), axis=0, keepdims=True)

        def search16(plane_ref, cnt_at_floor):
            def bit_step(it, carry):
                val, cnt_val = carry
                cand = val + lax.shift_left(jnp.int32(1), 15 - it)
                cnt = count16(plane_ref, cand)
                ok = cnt >= kf
                return jnp.where(ok, cand, val), jnp.where(ok, cnt, cnt_val)

            return lax.fori_loop(0, 16, bit_step, (jnp.full((1, qb), I16_MIN, I32), cnt_at_floor))

        thr_hi, cnt_hi = search16(hi_ref, jnp.full((1, qb), float(seq), F32))
        thr_hi16 = thr_hi.astype(I16)

        def pin(c, carry):
            h = hi_ref[c]
            lo_ref[c] = jnp.where(h > thr_hi16, jnp.int16(I16_MAX),
                                  jnp.where(h == thr_hi16, lo_ref[c], jnp.int16(I16_MIN)))
            return carry

        lax.fori_loop(0, 2 * nwide, pin, 0)
        thr_lo, cnt_thr = search16(lo_ref, cnt_hi)
        thr = thr_hi * 65536 + (thr_lo - I16_MIN)
        thr_ref[...] = jnp.broadcast_to(jnp.where(query_pos < topk, keep_all, thr), thr_ref.shape)

        @pl.when(jnp.max(cnt_thr) > kf)
        def _():
            need = kf - count(lambda kk, c: kk > thr)

            def idx_step(it, pos):
                cand = pos + lax.shift_left(jnp.int32(1), (seq.bit_length() - 2) - it)
                cnt = count(lambda kk, c: (kk == thr) & (key_row + c * kc < cand))
                return jnp.where(cnt < need, cand, pos)

            pos = lax.fori_loop(0, seq.bit_length() - 1, idx_step, jnp.zeros((1, qb), I32))

            def demote(c, carry):
                kk = keys_ref[c]
                keys_ref[c] = jnp.where((kk == thr) & (key_row + c * kc > pos), kk - 1, kk)
                return carry

            lax.fori_loop(0, 2 * nwide, demote, 0)

    q = q_ref[0]
    qs = _split_pair(q[:, :LANES]) + _split_pair(q[:, LANES:])
    thr_row = thr_ref[0:1, :]
    key_col = lax.broadcasted_iota(I32, (1, 2 * kw), 1)

    def shared(c, n):
        halves = [jnp.where(keys_ref[2 * c + h] >= thr_row, 0.0, NEG).T for h in range(2 * n)]
        key_off = (key_col[:, :n * kw] + (c * kw - i * qb)).astype(F32)
        return jnp.concatenate(halves, axis=1), key_off

    def logits(c, n, hd, diagonal, ctx):
        bias, key_off = ctx
        start = pl.multiple_of(c * kw, kw)
        slope = LOG2E * 2.0 ** -(N_HEADS + hd + 1)
        return _nt_dot(qs[hd], k_ref[0, pl.ds(start, n * kw), :]) + (bias + slope * key_off)

    def values(c, n, hd):
        return _values_with_ones(v_ref[0, pl.ds(pl.multiple_of(c * kw, kw), n * kw), :], hd % 2)

    _two_pass_attention(nwide - 1, logits, values, s_ref, mx_ref, acc_ref, N_HEADS, shared, diagonal_mask=False)
    o_ref[0] = jnp.concatenate([_pair_output(acc_ref[0], acc_ref[1]), _pair_output(acc_ref[2], acc_ref[3])],
                               axis=1).astype(o_ref.dtype)


def _dsa_call(od3, small3, b, s, topk, qb):
    kc = LANES
    scratch = [pltpu.VMEM((s // kc, kc, qb), I32), pltpu.VMEM((s // kc, kc, qb), I16),
               pltpu.VMEM((s // kc, kc, qb), I16), pltpu.VMEM((SUBLANES, qb), I32),
               pltpu.VMEM((N_HEADS, s // (2 * kc), qb, 2 * kc), F32),
               pltpu.VMEM((N_HEADS, qb, LANES), F32), pltpu.VMEM((N_HEADS, qb, LANES), F32)]
    return pl.pallas_call(
        functools.partial(_dsa_kernel, seq=s, topk=topk, qb=qb),
        grid=(b, s // qb),
        in_specs=[pl.BlockSpec((1, qb, 2 * LANES), lambda bi, i: (bi, i, 0)),
                  pl.BlockSpec((1, s, LANES), lambda bi, i: (bi, 0, 2)),
                  pl.BlockSpec((1, s, LANES), lambda bi, i: (bi, 0, 3)),
                  pl.BlockSpec((1, qb, 2 * LANES), lambda bi, i: (bi, i, 2)),
                  pl.BlockSpec((1, s, LANES), lambda bi, i: (bi, 0, 6)),
                  pl.BlockSpec((1, qb, LANES), lambda bi, i: (bi, i, 0))],
        out_specs=pl.BlockSpec((1, qb, 2 * LANES), lambda bi, i: (bi, i, 0)),
        out_shape=jax.ShapeDtypeStruct((b, s, 2 * LANES), BF16),
        scratch_shapes=scratch,
        compiler_params=pltpu.CompilerParams(dimension_semantics=("arbitrary",) * 2, vmem_limit_bytes=VMEM_LIMIT),
        name="attn_dsa",
    )(od3, od3, od3, od3, od3, small3)


def _outproj_kernel(x_ref, oa_ref, ob_ref, oc_ref, od_ref, w_ref, mod_ref, g2_ref, wr_ref, br_ref,
                    xo_ref, h_ref, rt_ref):
    d_model = x_ref.shape[1]
    merged = jnp.concatenate([oa_ref[...], ob_ref[...], oc_ref[...], od_ref[...]], axis=1)
    mix = jnp.dot(merged, w_ref[...], preferred_element_type=F32)
    xn = x_ref[...] + mod_ref[0, 2:3, :] * mix
    xo_ref[...] = xn
    h = _rms(xn, g2_ref[...]) * (1.0 + mod_ref[0, 4:5, :]) + mod_ref[0, 3:4, :]
    h_ref[:, :d_model] = h

    h_hi = h.astype(BF16)
    h_lo = (h - h_hi.astype(F32)).astype(BF16)
    logits = _nt_dot(wr_ref[0], h_hi) + (_nt_dot(wr_ref[0], h_lo) + _nt_dot(wr_ref[1], h_hi))
    score = 1.0 / (1.0 + jnp.exp(-logits))
    biased = score + br_ref[...]
    srow = [score[e:e + 1, :] for e in range(N_EXPERTS)]
    brow = [biased[e:e + 1, :] for e in range(N_EXPERTS)]
    per = N_EXPERTS // N_GROUPS
    best_v = best_g = None
    for g in range(N_GROUPS):
        r = brow[g * per:(g + 1) * per]
        top2 = None
        for a in range(per):
            for c in range(a + 1, per):
                top2 = r[a] + r[c] if top2 is None else jnp.maximum(top2, r[a] + r[c])
        if g == 0:
            best_v, best_g = top2, jnp.zeros_like(top2, dtype=I32)
        else:
            up = top2 > best_v
            best_v = jnp.where(up, top2, best_v)
            best_g = jnp.where(up, g, best_g)
    cand = [jnp.where(best_g == e // per, brow[e], -jnp.inf) for e in range(N_EXPERTS)]

    def first_max(vals):
        v, idx = vals[0], jnp.zeros_like(best_g)
        for e in range(1, N_EXPERTS):
            up = vals[e] > v
            v = jnp.where(up, vals[e], v)
            idx = jnp.where(up, e, idx)
        return idx

    i1 = first_max(cand)
    i2 = first_max([jnp.where(i1 == e, -jnp.inf, cand[e]) for e in range(N_EXPERTS)])
    s1 = sum(jnp.where(i1 == e, srow[e], 0.0) for e in range(N_EXPERTS))
    s2 = sum(jnp.where(i2 == e, srow[e], 0.0) for e in range(N_EXPERTS))
    den = s1 + s2
    first_low = i1 < i2
    la = jnp.minimum(i1, i2) - per * best_g
    lb = jnp.maximum(i1, i2) - per * best_g
    pair = jnp.where(la == 0, lb - 1, jnp.where(la == 1, lb + 1, 5))
    route = jnp.concatenate([(best_g * N_PAIRS + pair).astype(F32), jnp.where(first_low, s1, s2) / den,
                             jnp.where(first_low, s2, s1) / den, jnp.zeros((SUBLANES - 3, den.shape[1]), F32)], axis=0)
    rt_ref[...] = route
    pad = jnp.zeros((LANES - SUBLANES, den.shape[1]), F32)
    h_ref[:, d_model:] = jnp.concatenate([route, pad], axis=0).T


def _outproj_call(x2, outs, w_out, mod_l, g2, wr_t, br, seq, tm):
    t, d = x2.shape
    tpb = seq // tm
    row = lambda n: pl.BlockSpec((tm, n), lambda i: (i, 0))
    const = lambda shape: pl.BlockSpec(shape, lambda i: (0,) * len(shape))
    return pl.pallas_call(
        _outproj_kernel,
        grid=(t // tm,),
        in_specs=[row(d)] + [row(2 * LANES)] * 4 + [
            const(w_out.shape), pl.BlockSpec((1, 6, d), lambda i: (i // tpb, 0, 0)), const((1, d)),
            const(wr_t.shape), const(br.shape)],
        out_specs=[row(d), row(d + LANES), pl.BlockSpec((SUBLANES, tm), lambda i: (0, i))],
        out_shape=[jax.ShapeDtypeStruct((t, d), F32), jax.ShapeDtypeStruct((t, d + LANES), F32),
                   jax.ShapeDtypeStruct((SUBLANES, t), F32)],
        compiler_params=pltpu.CompilerParams(dimension_semantics=("arbitrary",), vmem_limit_bytes=VMEM_LIMIT),
        name="outproj_router",
    )(x2, *outs, w_out, mod_l, g2, wr_t, br)


def _routing_tables(cls, n_steps):
    t = cls.shape[0]
    onehot = (cls[:, None] == jnp.arange(N_CLASSES, dtype=I32)[None, :]).astype(I32)
    upto = jnp.cumsum(onehot, axis=0)
    rank = jnp.sum((upto - onehot) * onehot, axis=1)
    padded = (upto[-1] + MOE_ROWS - 1) // MOE_ROWS * MOE_ROWS
    ends = jnp.cumsum(padded)
    pos = jnp.sum(onehot * (ends - padded)[None, :], axis=1) + rank
    step_cls = jnp.sum((jnp.arange(n_steps, dtype=I32) * MOE_ROWS)[:, None] >= ends[None, :], axis=1)
    valid = (step_cls < N_CLASSES).astype(I32)
    step_cls = jnp.minimum(step_cls, N_CLASSES - 1)
    base = (step_cls // N_PAIRS) * (N_EXPERTS // N_GROUPS)
    e_low = base + jnp.asarray(PAIR_LOW, I32)[step_cls % N_PAIRS]
    e_high = base + jnp.asarray(PAIR_HIGH, I32)[step_cls % N_PAIRS]
    return pos.reshape(t // PERM_ROWS, 1, PERM_ROWS), e_low, e_high, valid


def _row_copy(src_ref, src_row, dst_ref, dst_row, sem):
    return pltpu.make_async_copy(src_ref.at[pl.ds(src_row, 1)], dst_ref.at[pl.ds(dst_row, 1)], sem)


def _issue_rows(copy_row):
    def issue(r8, carry):
        for u in range(ROW_UNROLL):
            copy_row(r8 * ROW_UNROLL + u).start()
        return carry

    lax.fori_loop(0, PERM_ROWS // ROW_UNROLL, issue, 0)


def _rows_done(src_ref, dst_ref, sem):
    pltpu.make_async_copy(src_ref.at[pl.ds(0, PERM_ROWS)], dst_ref.at[pl.ds(0, PERM_ROWS)], sem).wait()


def _scatter_rows_kernel(pos_ref, src_ref, init_ref, dst_ref, sem):
    del init_ref
    _issue_rows(lambda r: _row_copy(src_ref, r, dst_ref, pos_ref[0, 0, r], sem))
    _rows_done(src_ref, dst_ref, sem)


def _scatter_rows_call(pos3, src, n_sorted):
    t, d = src.shape
    return pl.pallas_call(
        _scatter_rows_kernel,
        grid=(t // PERM_ROWS,),
        in_specs=[pl.BlockSpec((1, 1, PERM_ROWS), lambda j: (j, 0, 0), memory_space=pltpu.SMEM),
                  pl.BlockSpec((PERM_ROWS, d), lambda j: (j, 0)), pl.BlockSpec(memory_space=pl.ANY)],
        out_specs=pl.BlockSpec(memory_space=pl.ANY),
        out_shape=jax.ShapeDtypeStruct((n_sorted, d), src.dtype),
        scratch_shapes=[pltpu.SemaphoreType.DMA(())],
        input_output_aliases={2: 0},
        compiler_params=pltpu.CompilerParams(dimension_semantics=("arbitrary",), has_side_effects=True),
        name="moe_sort_rows",
    )(pos3, src, jnp.zeros((n_sorted, d), src.dtype))


def _moe_kernel(e_low_ref, e_high_ref, valid_ref, h_ref, wgu_low_ref, wgu_high_ref, wd_low_ref, wd_high_ref, o_ref):
    del e_low_ref, e_high_ref
    step = pl.program_id(0)
    d_model = o_ref.shape[1]

    @pl.when(valid_ref[step] == 0)
    def _():
        o_ref[...] = jnp.zeros(o_ref.shape, F32)

    @pl.when(valid_ref[step] != 0)
    def _():
        h = h_ref[:, :d_model].astype(BF16)
        w_low = h_ref[:, d_model + 1:d_model + 2]
        w_high = h_ref[:, d_model + 2:d_model + 3]

        def expert(wgu_ref, wd_ref):
            gu = jnp.dot(h, wgu_ref[0], preferred_element_type=F32)
            gate, up = gu[:, :D_EXPERT], gu[:, D_EXPERT:]
            hid = (gate / (1.0 + jnp.exp(-gate)) * up).astype(BF16)
            return jnp.dot(hid, wd_ref[0], preferred_element_type=F32)

        o_ref[...] = w_low * expert(wgu_low_ref, wd_low_ref) + w_high * expert(wgu_high_ref, wd_high_ref)


def _moe_call(hs, e_low, e_high, valid, wgu, wd):
    n_sorted = hs.shape[0]
    d = wd.shape[2]
    grid_spec = pltpu.PrefetchScalarGridSpec(
        num_scalar_prefetch=3,
        grid=(n_sorted // MOE_ROWS,),
        in_specs=[pl.BlockSpec((MOE_ROWS, hs.shape[1]), lambda j, lo, hi, ok: (j, 0)),
                  pl.BlockSpec((1, d, 2 * D_EXPERT), lambda j, lo, hi, ok: (lo[j], 0, 0)),
                  pl.BlockSpec((1, d, 2 * D_EXPERT), lambda j, lo, hi, ok: (hi[j], 0, 0)),
                  pl.BlockSpec((1, D_EXPERT, d), lambda j, lo, hi, ok: (lo[j], 0, 0)),
                  pl.BlockSpec((1, D_EXPERT, d), lambda j, lo, hi, ok: (hi[j], 0, 0))],
        out_specs=pl.BlockSpec((MOE_ROWS, d), lambda j, lo, hi, ok: (j, 0)))
    return pl.pallas_call(
        _moe_kernel,
        grid_spec=grid_spec,
        out_shape=jax.ShapeDtypeStruct((n_sorted, d), F32),
        compiler_params=pltpu.CompilerParams(dimension_semantics=("arbitrary",), vmem_limit_bytes=VMEM_LIMIT),
        name="moe",
    )(e_low, e_high, valid, hs, wgu, wgu, wd, wd)


def _gather_residual_kernel(pos_ref, pos_next_ref, ys_ref, x_ref, mod_ref, gf_ref, o_ref, rows_ref, sems, *, final):
    j = pl.program_id(0)
    slot = j % 2

    def request(index_ref, into):
        _issue_rows(lambda r: _row_copy(ys_ref, index_ref[0, 0, r], rows_ref.at[into], r, sems.at[into]))

    @pl.when(j == 0)
    def _():
        request(pos_ref, slot)

    @pl.when(j + 1 < pl.num_programs(0))
    def _():
        request(pos_next_ref, 1 - slot)

    _rows_done(ys_ref, rows_ref.at[slot], sems.at[slot])
    xn = x_ref[...] + mod_ref[0, 5:6, :] * rows_ref[slot]
    o_ref[...] = _rms(xn, gf_ref[...]) if final else xn


def _gather_residual_call(pos3, ys, x2, mod_l, gf, seq, final):
    t, d = x2.shape
    tpb = seq // PERM_ROWS
    last = t // PERM_ROWS - 1
    return pl.pallas_call(
        functools.partial(_gather_residual_kernel, final=final),
        grid=(t // PERM_ROWS,),
        in_specs=[pl.BlockSpec((1, 1, PERM_ROWS), lambda j: (j, 0, 0), memory_space=pltpu.SMEM),
                  pl.BlockSpec((1, 1, PERM_ROWS), lambda j: (jnp.minimum(j + 1, last), 0, 0),
                               memory_space=pltpu.SMEM),
                  pl.BlockSpec(memory_space=pl.ANY),
                  pl.BlockSpec((PERM_ROWS, d), lambda j: (j, 0)),
                  pl.BlockSpec((1, 6, d), lambda j: (j // tpb, 0, 0)),
                  pl.BlockSpec((1, d), lambda j: (0, 0))],
        out_specs=pl.BlockSpec((PERM_ROWS, d), lambda j: (j, 0)),
        out_shape=jax.ShapeDtypeStruct((t, d), F32),
        scratch_shapes=[pltpu.VMEM((2, PERM_ROWS, d), F32), pltpu.SemaphoreType.DMA((2,))],
        compiler_params=pltpu.CompilerParams(dimension_semantics=("arbitrary",), vmem_limit_bytes=VMEM_LIMIT),
        name="moe_unsort_residual",
    )(pos3, pos3, ys, x2, mod_l, gf)


def _swap_half(w):
    half = w.shape[1] // 2
    return jnp.concatenate([-w[:, half:], w[:, :half]], axis=1)


def _layer_weights(w_in, w_q_up, w_kv_up):
    d = w_in.shape[0]
    pts = np.cumsum(IN_SIZES)[:-1].tolist()
    (a_q, a_k, a_v, a_f, b_cq, b_ckv, b_kr, c_q, c_k, c_v,
     d_q, d_k, d_v, d_qi, d_ki, d_wi) = jnp.split(w_in, pts, axis=1)
    qs = HEAD_DIM ** -0.5
    small = jnp.concatenate([
        a_f, d_wi * ((IDX_HEADS * IDX_DIM) ** -0.5),
        jnp.zeros((d, _KR_LANE - _WI_LANE - IDX_HEADS), F32),
        b_kr, _swap_half(b_kr), d_ki], axis=1)
    w_all = jnp.concatenate([
        a_q * qs, a_k, a_v,
        c_q * qs, c_k, c_v,
        d_q * qs, d_qi, d_k, d_v,
        small, b_cq, b_ckv], axis=1).astype(BF16)

    per_q = MLA_NOPE + MLA_ROPE
    wq = []
    for hd in range(N_HEADS):
        blk = w_q_up[:, hd * per_q:(hd + 1) * per_q]
        rot = blk[:, MLA_NOPE:]
        wq += [blk[:, :MLA_NOPE], rot, _swap_half(rot)]
    wq = jnp.concatenate(wq, axis=1).astype(BF16)

    place = np.zeros((LANES, LANES), np.float32)
    place[_KR_LANE + np.arange(MLA_ROPE), MLA_NOPE + np.arange(MLA_ROPE)] = 1.0
    place = jnp.asarray(place)
    kcols, vcols = [], []
    for hd in range(N_HEADS):
        blk = w_kv_up[:, hd * 2 * HEAD_DIM:(hd + 1) * 2 * HEAD_DIM]
        knope = jnp.concatenate([blk[:, :MLA_NOPE], jnp.zeros((MLA_KV_RANK, LANES - MLA_NOPE), F32)], axis=1)
        kcols.append(jnp.concatenate([knope, place], axis=0))
        vcols.append(jnp.concatenate([blk[:, MLA_NOPE:], jnp.zeros((LANES, HEAD_DIM), F32)], axis=0))
    wkv = jnp.concatenate(kcols + vcols, axis=1).astype(BF16)
    return w_all, wq, wkv


def _rope_tables(seq):
    half = MLA_ROPE // 2
    inv = ROPE_THETA ** (-jnp.arange(half, dtype=F32) / half)
    ang = jnp.arange(seq, dtype=F32)[:, None] * inv[None, :]
    cos = jnp.tile(jnp.cos(ang), (1, 2))
    sin = jnp.tile(jnp.sin(ang), (1, 2))
    scale = LOG2E * (MLA_NOPE + MLA_ROPE) ** -0.5
    z = lambda n: jnp.zeros((seq, n), F32)
    tab_q = jnp.concatenate([jnp.full((seq, MLA_NOPE), scale, F32), cos * scale, z(MLA_ROPE)], axis=1)
    tab_qs = jnp.concatenate([z(MLA_NOPE), sin * scale, z(MLA_ROPE)], axis=1)
    tab_k = jnp.concatenate([z(_KR_LANE), cos, z(LANES - _KR_LANE - MLA_ROPE)], axis=1)
    tab_ks = jnp.concatenate([z(_KR_LANE), sin, z(LANES - _KR_LANE - MLA_ROPE)], axis=1)
    return jnp.stack([tab_q, tab_qs, tab_k, tab_ks])


def kernel(x, c, w_ada, b_ada, g_norm1, w_in, b_forget, g_q_mla, w_q_up, g_kv_mla, w_kv_up, sinks, w_out,
           g_norm2, w_router, b_router, w_gate, w_up, w_down, g_final):
    b, s, d = x.shape
    depth = w_in.shape[0]
    t = b * s
    topk = min(TOPK_MAX, s // 4)
    tm = min(1024, s)
    tq = min(512, s)
    tq_swa = min(512, s)
    qb_dsa = min(4 * LANES, s)
    n_steps = t // MOE_ROWS + N_CLASSES
    assert s % tm == 0 and s % tq == 0 and s % qb_dsa == 0 and s % PERM_ROWS == 0

    mod = _ada_call(c, w_ada, b_ada).reshape(depth, b, 6, d)
    tabs = _rope_tables(s)
    wr_hi = w_router.T.astype(BF16)
    wr_t = jnp.stack([wr_hi, (w_router.T - wr_hi.astype(F32)).astype(BF16)])
    br = b_router.reshape(N_EXPERTS, 1)
    x2 = x.reshape(t, d)
    for l in range(depth):
        w_all, wq, wkv = _layer_weights(w_in[l], w_q_up[l], w_kv_up[l])
        oa, oc, od, osm, oqb, okb, ovb = _inproj_call(
            x2, mod[l], g_norm1[l].reshape(1, d), w_all, wq, wkv,
            g_q_mla[l].reshape(1, -1), g_kv_mla[l].reshape(1, -1), tabs, s, tm)
        small3 = osm.reshape(b, s, LANES)
        bf = jnp.zeros((1, LANES), F32).at[0, :N_HEADS].set(b_forget[l])
        fcum = _fox_cumsum_call(small3, bf)
        frow = jnp.swapaxes(fcum[:, :, :8], 1, 2)
        out_a = _causal_attn_call((oa.reshape(b, s, -1), fcum, frow), True, b, s, tq)
        out_b = _causal_attn_call((oqb.reshape(b, s, -1), okb.reshape(b, s, -1), ovb.reshape(b, s, -1)),
                                  False, b, s, tq)
        out_c = _swa_call(oc.reshape(b, s, -1), sinks[l], b, s, tq_swa)
        out_d = _dsa_call(od.reshape(b, s, -1), small3, b, s, topk, qb_dsa)
        outs = [o.reshape(t, 2 * LANES) for o in (out_a, out_b, out_c, out_d)]
        x2, h2, route = _outproj_call(x2, outs, w_out[l].astype(BF16), mod[l], g_norm2[l].reshape(1, d),
                                      wr_t, br, s, tm)
        pos3, e_low, e_high, valid = _routing_tables(route[0].astype(I32), n_steps)
        wgu = jnp.concatenate([w_gate[l], w_up[l]], axis=2).astype(BF16)
        hs = _scatter_rows_call(pos3, h2, n_steps * MOE_ROWS)
        ys = _moe_call(hs, e_low, e_high, valid, wgu, w_down[l].astype(BF16))
        x2 = _gather_residual_call(pos3, ys, x2, mod[l], g_final.reshape(1, d), s, l == depth - 1)
    return x2.reshape(b, s, d)
```

```python
import functools

import jax
import jax.numpy as jnp
import numpy as np
from jax import lax
from jax.experimental import pallas as pl
from jax.experimental.pallas import tpu as pltpu

F32 = jnp.float32
BF16 = jnp.bfloat16
I32 = jnp.int32
I16 = jnp.int16
I16_MIN, I16_MAX = -(2 ** 15), 2 ** 15 - 1

EPS = 1e-6
HEAD_DIM = 64
LANES = 128
SUBLANES = 8
N_HEADS = 4
MLA_Q_RANK = 256
MLA_KV_RANK = 128
MLA_NOPE = 64
MLA_ROPE = 32
ROPE_THETA = 10000.0
WINDOW = 128
IDX_HEADS = 8
IDX_DIM = 32
TOPK_MAX = 256
N_EXPERTS = 16
N_GROUPS = 4
N_PAIRS = 6
N_CLASSES = N_GROUPS * N_PAIRS
PAIR_LOW = (0, 0, 0, 1, 1, 2)
PAIR_HIGH = (1, 2, 3, 2, 3, 3)
D_EXPERT = 256
MOE_ROWS = 512
PERM_ROWS = 512
ROW_UNROLL = 8
IN_SIZES = (256, 256, 256, 4, 256, 128, 32, 256, 128, 128, 256, 64, 64, 256, 32, 8)

LOG2E = 1.4426950408889634
NEG = -1e30
INT_MIN = -(2 ** 31)
KEY_NEG_INF = INT_MIN + 0x7FFFFF
VMEM_LIMIT = 56 * 1024 * 1024

_CA, _CC, _CD, _CS, _CQ, _CKV, _CEND = 0, 768, 1280, 1920, 2048, 2304, 2432
_F_LANE, _WI_LANE, _KR_LANE = 0, 4, 32


def _nt_dot(a, b):
    return lax.dot_general(a, b, (((1,), (1,)), ((), ())), preferred_element_type=F32)


def _rms(x, g):
    return x * lax.rsqrt(jnp.mean(x * x, axis=-1, keepdims=True) + EPS) * g


def _ada_kernel(c_ref, w_ref, b_ref, o_ref):
    c = c_ref[...]
    act = (c / (1.0 + jnp.exp(-c))).astype(BF16)
    o_ref[0] = jnp.dot(act, w_ref[0].astype(BF16), preferred_element_type=F32) + b_ref[0]


def _ada_call(c, w_ada, b_ada):
    depth, d, n = w_ada.shape
    bsz = c.shape[0]
    tn = 1024
    return pl.pallas_call(
        _ada_kernel,
        grid=(depth, n // tn),
        in_specs=[pl.BlockSpec((bsz, d), lambda l, j: (0, 0)),
                  pl.BlockSpec((1, d, tn), lambda l, j: (l, 0, j)),
                  pl.BlockSpec((1, 1, tn), lambda l, j: (l, 0, j))],
        out_specs=pl.BlockSpec((1, bsz, tn), lambda l, j: (l, 0, j)),
        out_shape=jax.ShapeDtypeStruct((depth, bsz, n), F32),
        compiler_params=pltpu.CompilerParams(dimension_semantics=("arbitrary", "arbitrary"),
                                             vmem_limit_bytes=VMEM_LIMIT),
        name="adaln",
    )(c, w_ada, b_ada.reshape(depth, 1, n))


def _inproj_kernel(x_ref, mod_ref, g1_ref, w_ref, wq_ref, wkv_ref, gq_ref, gkv_ref, tab_ref,
                   oa_ref, oc_ref, od_ref, os_ref, oqb_ref, okb_ref, ovb_ref):
    h = _rms(x_ref[...], g1_ref[...]) * (1.0 + mod_ref[0, 1:2, :]) + mod_ref[0, 0:1, :]
    h = h.astype(BF16)

    def proj(lo, hi):
        return jnp.dot(h, w_ref[:, lo:hi], preferred_element_type=F32)

    q_width = N_HEADS * HEAD_DIM
    low = _lane_is_low()

    def both_halves(pair):
        swapped = pltpu.roll(pair, HEAD_DIM, axis=1)
        return jnp.where(low, pair, swapped), jnp.where(low, swapped, pair)

    pa = proj(_CA, _CC)
    oa_ref[:, :q_width] = (pa[:, :q_width] * LOG2E).astype(BF16)
    oa_ref[:, q_width:] = pa[:, q_width:].astype(BF16)

    pc = proj(_CC, _CD)
    oc_ref[:, :q_width] = (pc[:, :q_width] * LOG2E).astype(BF16)
    for n in range(2):
        tiles = both_halves(pc[:, q_width + n * LANES:q_width + (n + 1) * LANES])
        for g in range(2):
            lo_col = q_width + (2 * n + g) * LANES
            oc_ref[:, lo_col:lo_col + LANES] = tiles[g].astype(BF16)

    pd = proj(_CD, _CS)
    small = proj(_CS, _CQ)
    os_ref[...] = small
    od_ref[:, :q_width] = (pd[:, :q_width] * LOG2E).astype(BF16)
    k_twice, v_twice = both_halves(pd[:, 2 * q_width:])
    od_ref[:, q_width:q_width + LANES] = k_twice.astype(BF16)
    od_ref[:, q_width + LANES:2 * q_width] = v_twice.astype(BF16)
    od_ref[:, 2 * q_width:3 * q_width] = pd[:, q_width:2 * q_width].astype(BF16)
    lane = lax.broadcasted_iota(I32, (1, LANES), 1)
    ki4 = small
    for n in range(1, LANES // IDX_DIM):
        ki4 = jnp.where(lane < LANES - n * IDX_DIM, pltpu.roll(small, LANES - n * IDX_DIM, axis=1), ki4)
    od_ref[:, 3 * q_width:] = ki4.astype(BF16)

    cq = _rms(proj(_CQ, _CKV), gq_ref[...]).astype(BF16)
    qf = jnp.dot(cq, wq_ref[...], preferred_element_type=F32)
    tab_q, tab_qs, tab_k, tab_ks = tab_ref[0], tab_ref[1], tab_ref[2], tab_ref[3]
    for hd in range(N_HEADS):
        qg = qf[:, hd * LANES:(hd + 1) * LANES]
        qr = qg * tab_q + pltpu.roll(qg, LANES - MLA_ROPE, axis=1) * tab_qs
        oqb_ref[:, hd * LANES:(hd + 1) * LANES] = qr.astype(BF16)

    ckv = _rms(proj(_CKV, _CEND), gkv_ref[...]).astype(BF16)
    kr = (small * tab_k + pltpu.roll(small, LANES - MLA_ROPE, axis=1) * tab_ks).astype(BF16)
    kvf = jnp.dot(jnp.concatenate([ckv, kr], axis=1), wkv_ref[...], preferred_element_type=F32)
    okb_ref[...] = kvf[:, :4 * LANES].astype(BF16)
    ovb_ref[...] = kvf[:, 4 * LANES:].astype(BF16)


def _inproj_call(x2, mod_l, g1, w_all, wq, wkv, gq, gkv, tabs, seq, tm):
    t, d = x2.shape
    tpb = seq // tm
    row = lambda n: pl.BlockSpec((tm, n), lambda i: (i, 0))
    const = lambda shape: pl.BlockSpec(shape, lambda i: (0,) * len(shape))
    widths = (768, 768, 896, 128, 512, 512, 256)
    dtypes = (BF16, BF16, BF16, F32, BF16, BF16, BF16)
    return pl.pallas_call(
        _inproj_kernel,
        grid=(t // tm,),
        in_specs=[row(d),
                  pl.BlockSpec((1, 6, d), lambda i: (i // tpb, 0, 0)),
                  const((1, d)), const(w_all.shape), const(wq.shape), const(wkv.shape),
                  const((1, MLA_Q_RANK)), const((1, MLA_KV_RANK)),
                  pl.BlockSpec((4, tm, LANES), lambda i: (0, i % tpb, 0))],
        out_specs=[row(n) for n in widths],
        out_shape=[jax.ShapeDtypeStruct((t, n), dt) for n, dt in zip(widths, dtypes)],
        compiler_params=pltpu.CompilerParams(dimension_semantics=("arbitrary",), vmem_limit_bytes=VMEM_LIMIT),
        name="inproj",
    )(x2, mod_l, g1, w_all, wq, wkv, gq, gkv, tabs)


def _fox_cumsum_kernel(s_ref, b_ref, o_ref):
    z = s_ref[0] + b_ref[...]
    lf = jnp.minimum(z, 0.0) - jnp.log(1.0 + jnp.exp(-jnp.abs(z)))
    n = lf.shape[0]
    row = lax.broadcasted_iota(I32, lf.shape, 0)
    d = 1
    while d < n:
        lf = lf + jnp.where(row >= d, pltpu.roll(lf, d, axis=0), 0.0)
        d *= 2
    o_ref[0] = lf * LOG2E


def _fox_cumsum_call(small3, bf):
    b, s, _ = small3.shape
    return pl.pallas_call(
        _fox_cumsum_kernel,
        grid=(b,),
        in_specs=[pl.BlockSpec((1, s, LANES), lambda i: (i, 0, 0)),
                  pl.BlockSpec((1, LANES), lambda i: (0, 0))],
        out_specs=pl.BlockSpec((1, s, LANES), lambda i: (i, 0, 0)),
        out_shape=jax.ShapeDtypeStruct((b, s, LANES), F32),
        compiler_params=pltpu.CompilerParams(dimension_semantics=("arbitrary",), vmem_limit_bytes=VMEM_LIMIT),
        name="fox_cumsum",
    )(small3, bf)


def _lane_is_low():
    return lax.broadcasted_iota(I32, (1, LANES), 1) < HEAD_DIM


def _split_pair(q):
    low = _lane_is_low()
    zero = jnp.zeros_like(q)
    return jnp.where(low, q, zero), jnp.where(low, zero, q)


def _values_with_ones(v, j):
    low = _lane_is_low()
    return jnp.where(low if j == 0 else ~low, v, jnp.ones_like(v))


def _pair_output(a0, a1, extra0=None, extra1=None):
    l0 = pltpu.roll(a0, HEAD_DIM, axis=1)
    l1 = pltpu.roll(a1, HEAD_DIM, axis=1)
    if extra0 is not None:
        l0, l1 = l0 + extra0, l1 + extra1
    return jnp.where(_lane_is_low(), a0 / l0, a1 / l1)


def _two_pass_attention(n_before, logits, values, s_ref, mx_ref, acc_ref, n_heads, shared=None, diagonal_mask=True):
    tq, tk = s_ref.shape[-2:]
    mx_ref[...] = jnp.full(mx_ref.shape, NEG, F32)

    def store(c, n, diagonal):
        ctx = shared(c, n) if shared is not None else None
        for j in range(n_heads):
            s = logits(c, n, j, diagonal, ctx)
            for t in range(n):
                s_ref[j, c + t] = s[:, t * tk:(t + 1) * tk]
            parts = [s[:, blk * LANES:(blk + 1) * LANES] for blk in range(n * tk // LANES)]
            while len(parts) > 1:
                parts = [jnp.maximum(a, b) for a, b in zip(parts[::2], parts[1::2])]
            mx_ref[j] = jnp.maximum(mx_ref[j], parts[0])

    def store_two(c2, carry):
        store(2 * c2, 2, False)
        return carry

    if diagonal_mask:
        lax.fori_loop(0, n_before // 2, store_two, 0)

        @pl.when(n_before % 2 == 1)
        def _():
            store(n_before - 1, 1, False)

        store(n_before, 1, True)
    else:
        lax.fori_loop(0, (n_before + 1) // 2, store_two, 0)

        @pl.when(n_before % 2 == 0)
        def _():
            store(n_before, 1, False)

    row_max = [jnp.max(mx_ref[j], axis=1, keepdims=True) for j in range(n_heads)]
    shift = [jnp.broadcast_to(m, (tq, tk)) for m in row_max]
    acc_ref[...] = jnp.zeros(acc_ref.shape, F32)

    def accum(c, n):
        for j in range(n_heads):
            p = [jnp.exp2(s_ref[j, c + t] - shift[j]).astype(BF16) for t in range(n)]
            p = p[0] if n == 1 else jnp.concatenate(p, axis=1)
            acc_ref[j] += jnp.dot(p, values(c, n, j), preferred_element_type=F32)

    def accum_two(c2, carry):
        accum(2 * c2, 2)
        return carry

    lax.fori_loop(0, (n_before + 1) // 2, accum_two, 0)

    @pl.when(n_before % 2 == 0)
    def _():
        accum(n_before, 1)

    return row_max


def _causal_attn_kernel(*refs, fox, tq):
    if fox:
        q_ref, k_ref, v_ref, fc_ref, fr_ref, o_ref, s_ref, mx_ref, acc_ref = refs
    else:
        q_ref, k_ref, v_ref, o_ref, s_ref, mx_ref, acc_ref = refs
    tk = tq
    hp = pl.program_id(1)
    i = pl.program_id(2)
    if fox:
        qs = _split_pair(q_ref[0])
        lane = lax.broadcasted_iota(I32, (1, LANES), 1)
        fcol = [jnp.sum(jnp.where(lane == 2 * hp + j, fc_ref[0], 0.0), axis=1, keepdims=True) for j in range(2)]
    else:
        qs = (q_ref[0, :, :LANES], q_ref[0, :, LANES:])
    causal = (lax.broadcasted_iota(I32, (tq, tk), 1) <= lax.broadcasted_iota(I32, (tq, tk), 0))

    def logits(c, n, j, diagonal, _):
        start = pl.multiple_of(c * tk, tk)
        if fox:
            s = _nt_dot(qs[j], k_ref[0, pl.ds(start, n * tk), :])
            s = (s + fcol[j]) - fr_ref[0, pl.ds(2 * hp + j, 1), pl.ds(start, n * tk)]
        else:
            s = _nt_dot(qs[j], k_ref[0, pl.ds(start, n * tk), j * LANES:(j + 1) * LANES])
        return jnp.where(causal, s, NEG) if diagonal else s

    def values(c, n, j):
        return _values_with_ones(v_ref[0, pl.ds(pl.multiple_of(c * tk, tk), n * tk), :], j)

    _two_pass_attention(i, logits, values, s_ref, mx_ref, acc_ref, 2)
    o_ref[0] = _pair_output(acc_ref[0], acc_ref[1]).astype(o_ref.dtype)


def _causal_attn_call(arrs, fox, b, s, tq):
    scratch = [pltpu.VMEM((2, s // tq, tq, tq), F32), pltpu.VMEM((2, tq, LANES), F32),
               pltpu.VMEM((2, tq, LANES), F32)]
    if fox:
        qkv, fcol, frow = arrs
        operands = (qkv, qkv, qkv, fcol, frow)
        in_specs = [pl.BlockSpec((1, tq, LANES), lambda bi, hp, i: (bi, i, hp)),
                    pl.BlockSpec((1, s, LANES), lambda bi, hp, i: (bi, 0, 2 + hp)),
                    pl.BlockSpec((1, s, LANES), lambda bi, hp, i: (bi, 0, 4 + hp)),
                    pl.BlockSpec((1, tq, LANES), lambda bi, hp, i: (bi, i, 0)),
                    pl.BlockSpec((1, 8, s), lambda bi, hp, i: (bi, 0, 0))]
    else:
        operands = arrs
        in_specs = [pl.BlockSpec((1, tq, 2 * LANES), lambda bi, hp, i: (bi, i, hp)),
                    pl.BlockSpec((1, s, 2 * LANES), lambda bi, hp, i: (bi, 0, hp)),
                    pl.BlockSpec((1, s, LANES), lambda bi, hp, i: (bi, 0, hp))]
    return pl.pallas_call(
        functools.partial(_causal_attn_kernel, fox=fox, tq=tq),
        grid=(b, 2, s // tq),
        in_specs=in_specs,
        out_specs=pl.BlockSpec((1, tq, LANES), lambda bi, hp, i: (bi, i, hp)),
        out_shape=jax.ShapeDtypeStruct((b, s, 2 * LANES), BF16),
        scratch_shapes=scratch,
        compiler_params=pltpu.CompilerParams(dimension_semantics=("arbitrary",) * 3, vmem_limit_bytes=VMEM_LIMIT),
        name="attn_fox" if fox else "attn_mla",
    )(*operands)


def _alibi_slope(head):
    return lax.shift_left(jnp.int32(1), 7 - head).astype(F32) * (LOG2E * 2.0 ** -8)


def _swa_kernel(sink_ref, q_ref, k_ref, v_ref, o_ref, *, tq):
    hp = pl.program_id(1)
    i = pl.program_id(2)
    band = 2 * WINDOW
    rel = lax.broadcasted_iota(I32, (WINDOW, band), 0) - lax.broadcasted_iota(I32, (WINDOW, band), 1)
    for r in range(tq // WINDOW):
        q_start = i * tq + r * WINDOW
        k_start = pl.multiple_of(jnp.maximum(q_start - WINDOW, 0), WINDOW)
        kb = k_ref[0, pl.ds(k_start, band), :]
        vb = v_ref[0, pl.ds(k_start, band), :]
        dist = rel + (q_start - k_start)
        valid = (dist >= 0) & (dist < WINDOW)
        distf = dist.astype(F32)
        qs = _split_pair(q_ref[0, r * WINDOW:(r + 1) * WINDOW, :])
        acc, sink_term = [], []
        for j in range(2):
            sink = sink_ref[2 * hp + j] * LOG2E
            s = jnp.where(valid, _nt_dot(qs[j], kb) - _alibi_slope(2 * hp + j) * distf, NEG)
            m = jnp.maximum(jnp.max(s, axis=1, keepdims=True), sink)
            p = jnp.exp2(s - m).astype(BF16)
            acc.append(jnp.dot(p, _values_with_ones(vb, j), preferred_element_type=F32))
            sink_term.append(jnp.exp2(sink - m))
        o_ref[0, r * WINDOW:(r + 1) * WINDOW, :] = _pair_output(acc[0], acc[1], *sink_term).astype(o_ref.dtype)


def _swa_call(qkv, sinks, b, s, tq):
    return pl.pallas_call(
        functools.partial(_swa_kernel, tq=tq),
        grid=(b, 2, s // tq),
        in_specs=[pl.BlockSpec(memory_space=pltpu.SMEM),
                  pl.BlockSpec((1, tq, LANES), lambda bi, hp, i: (bi, i, hp)),
                  pl.BlockSpec((1, s, LANES), lambda bi, hp, i: (bi, 0, 2 + hp)),
                  pl.BlockSpec((1, s, LANES), lambda bi, hp, i: (bi, 0, 4 + hp))],
        out_specs=pl.BlockSpec((1, tq, LANES), lambda bi, hp, i: (bi, i, hp)),
        out_shape=jax.ShapeDtypeStruct((b, s, 2 * LANES), BF16),
        compiler_params=pltpu.CompilerParams(dimension_semantics=("arbitrary",) * 3, vmem_limit_bytes=VMEM_LIMIT),
        name="attn_swa",
    )(sinks, qkv, qkv, qkv)


def _dsa_kernel(q_ref, k_ref, v_ref, qi_ref, ki_ref, wi_ref, o_ref, keys_ref, hi_ref, lo_ref, thr_ref,
                s_ref, mx_ref, acc_ref, *, seq, topk, qb):
    kc = LANES
    kw = 2 * kc
    i = pl.program_id(1)
    nwide = (i + 1) * (qb // kw)
    lane = lax.broadcasted_iota(I32, (1, LANES), 1)
    key_row = lax.broadcasted_iota(I32, (kc, qb), 0)
    query_pos = i * qb + lax.broadcasted_iota(I32, (1, qb), 1)

    qi = qi_ref[0]
    wi_t = wi_ref[0].T
    qms, wrows = [], []
    for hd in range(IDX_HEADS):
        g, r = divmod(hd, LANES // IDX_DIM)
        sel = (lane >= r * IDX_DIM) & (lane < (r + 1) * IDX_DIM)
        qg = qi[:, g * LANES:(g + 1) * LANES]
        qms.append(jnp.where(sel, qg, jnp.zeros_like(qg)))
        wrows.append(wi_t[_WI_LANE + hd:_WI_LANE + hd + 1, :])
    q_all = jnp.concatenate(qms, axis=0)
    key_minus_query = (lax.broadcasted_iota(I32, (kw, qb), 0) - lax.broadcasted_iota(I32, (kw, qb), 1))

    def score_chunk(c, carry):
        start = pl.multiple_of(c * kw, kw)
        logit = _nt_dot(ki_ref[0, pl.ds(start, kw), :], q_all)
        sc = wrows[0] * jnp.maximum(logit[:, :qb], 0.0)
        for hd in range(1, IDX_HEADS):
            sc = sc + wrows[hd] * jnp.maximum(logit[:, hd * qb:(hd + 1) * qb], 0.0)
        sc = jnp.where(key_minus_query <= i * qb - start, sc, -jnp.inf)
        sc = jnp.where(sc == 0.0, 0.0, sc)
        bits = pltpu.bitcast(sc, I32)
        keys = bits ^ ((bits >> 31) & 0x7FFFFFFF)
        keys_ref[2 * c] = keys[:kc]
        keys_ref[2 * c + 1] = keys[kc:]
        hi = (keys >> 16).astype(I16)
        lo = ((keys & 0xFFFF) + I16_MIN).astype(I16)
        hi_ref[2 * c] = hi[:kc]
        hi_ref[2 * c + 1] = hi[kc:]
        lo_ref[2 * c] = lo[:kc]
        lo_ref[2 * c + 1] = lo[kc:]
        return carry

    lax.fori_loop(0, nwide, score_chunk, 0)

    keep_all = jnp.full((1, qb), KEY_NEG_INF + 1, I32)
    thr_ref[...] = jnp.broadcast_to(keep_all, thr_ref.shape)

    def count(pred):
        def body(c2, acc):
            parts = []
            for c in (2 * c2, 2 * c2 + 1):
                hit = jnp.where(pred(keys_ref[c], c), 1.0, 0.0)
                parts += [hit[r * SUBLANES:(r + 1) * SUBLANES] for r in range(kc // SUBLANES)]
            while len(parts) > 1:
                parts = [a + b for a, b in zip(parts[::2], parts[1::2])]
            return acc + parts[0]
        acc = lax.fori_loop(0, nwide, body, jnp.zeros((SUBLANES, qb), F32))
        return jnp.sum(acc, axis=0, keepdims=True)

    @pl.when((i + 1) * qb > topk)
    def _():
        kf = float(topk)

        pack = 2 * SUBLANES

        def count16(plane_ref, cand):
            cand16 = cand.astype(I16)

            def body(c2, acc):
                parts = []
                for c in (2 * c2, 2 * c2 + 1):
                    hit = jnp.where(plane_ref[c] >= cand16, jnp.int16(1), jnp.int16(0))
                    parts += [hit[r * pack:(r + 1) * pack] for r in range(kc // pack)]
                while len(parts) > 1:
                    parts = [a + b for a, b in zip(parts[::2], parts[1::2])]
                return acc + parts[0]

            acc = lax.fori_loop(0, nwide, body, jnp.zeros((pack, qb), I16))
            return jnp.sum(acc.astype(F32), axis=0, keepdims=True)

        def search16(plane_ref, cnt_at_floor):
            def bit_step(it, carry):
                val, cnt_val = carry
                cand = val + lax.shift_left(jnp.int32(1), 15 - it)
                cnt = count16(plane_ref, cand)
                ok = cnt >= kf
                return jnp.where(ok, cand, val), jnp.where(ok, cnt, cnt_val)

            return lax.fori_loop(0, 16, bit_step, (jnp.full((1, qb), I16_MIN, I32), cnt_at_floor))

        thr_hi, cnt_hi = search16(hi_ref, jnp.full((1, qb), float(seq), F32))
        thr_hi16 = thr_hi.astype(I16)

        def pin(c, carry):
            h = hi_ref[c]
            lo_ref[c] = jnp.where(h > thr_hi16, jnp.int16(I16_MAX),
                                  jnp.where(h == thr_hi16, lo_ref[c], jnp.int16(I16_MIN)))
            return carry

        lax.fori_loop(0, 2 * nwide, pin, 0)
        thr_lo, cnt_thr = search16(lo_ref, cnt_hi)
        thr = thr_hi * 65536 + (thr_lo - I16_MIN)
        thr_ref[...] = jnp.broadcast_to(jnp.where(query_pos < topk, keep_all, thr), thr_ref.shape)

        @pl.when(jnp.max(cnt_thr) > kf)
        def _():
            need = kf - count(lambda kk, c: kk > thr)

            def idx_step(it, pos):
                cand = pos + lax.shift_left(jnp.int32(1), (seq.bit_length() - 2) - it)
                cnt = count(lambda kk, c: (kk == thr) & (key_row + c * kc < cand))
                return jnp.where(cnt < need, cand, pos)

            pos = lax.fori_loop(0, seq.bit_length() - 1, idx_step, jnp.zeros((1, qb), I32))

            def demote(c, carry):
                kk = keys_ref[c]
                keys_ref[c] = jnp.where((kk == thr) & (key_row + c * kc > pos), kk - 1, kk)
                return carry

            lax.fori_loop(0, 2 * nwide, demote, 0)

    q = q_ref[0]
    qs = _split_pair(q[:, :LANES]) + _split_pair(q[:, LANES:])
    thr_row = thr_ref[0:1, :]
    key_col = lax.broadcasted_iota(I32, (1, 2 * kw), 1)

    def shared(c, n):
        halves = [jnp.where(keys_ref[2 * c + h] >= thr_row, 0.0, NEG).T for h in range(2 * n)]
        key_off = (key_col[:, :n * kw] + (c * kw - i * qb)).astype(F32)
        return jnp.concatenate(halves, axis=1), key_off

    def logits(c, n, hd, diagonal, ctx):
        bias, key_off = ctx
        start = pl.multiple_of(c * kw, kw)
        slope = LOG2E * 2.0 ** -(N_HEADS + hd + 1)
        return _nt_dot(qs[hd], k_ref[0, pl.ds(start, n * kw), :]) + (bias + slope * key_off)

    def values(c, n, hd):
        return _values_with_ones(v_ref[0, pl.ds(pl.multiple_of(c * kw, kw), n * kw), :], hd % 2)

    _two_pass_attention(nwide - 1, logits, values, s_ref, mx_ref, acc_ref, N_HEADS, shared, diagonal_mask=False)
    o_ref[0] = jnp.concatenate([_pair_output(acc_ref[0], acc_ref[1]), _pair_output(acc_ref[2], acc_ref[3])],
                               axis=1).astype(o_ref.dtype)


def _dsa_call(od3, small3, b, s, topk, qb):
    kc = LANES
    scratch = [pltpu.VMEM((s // kc, kc, qb), I32), pltpu.VMEM((s // kc, kc, qb), I16),
               pltpu.VMEM((s // kc, kc, qb), I16), pltpu.VMEM((SUBLANES, qb), I32),
               pltpu.VMEM((N_HEADS, s // (2 * kc), qb, 2 * kc), F32),
               pltpu.VMEM((N_HEADS, qb, LANES), F32), pltpu.VMEM((N_HEADS, qb, LANES), F32)]
    return pl.pallas_call(
        functools.partial(_dsa_kernel, seq=s, topk=topk, qb=qb),
        grid=(b, s // qb),
        in_specs=[pl.BlockSpec((1, qb, 2 * LANES), lambda bi, i: (bi, i, 0)),
                  pl.BlockSpec((1, s, LANES), lambda bi, i: (bi, 0, 2)),
                  pl.BlockSpec((1, s, LANES), lambda bi, i: (bi, 0, 3)),
                  pl.BlockSpec((1, qb, 2 * LANES), lambda bi, i: (bi, i, 2)),
                  pl.BlockSpec((1, s, LANES), lambda bi, i: (bi, 0, 6)),
                  pl.BlockSpec((1, qb, LANES), lambda bi, i: (bi, i, 0))],
        out_specs=pl.BlockSpec((1, qb, 2 * LANES), lambda bi, i: (bi, i, 0)),
        out_shape=jax.ShapeDtypeStruct((b, s, 2 * LANES), BF16),
        scratch_shapes=scratch,
        compiler_params=pltpu.CompilerParams(dimension_semantics=("arbitrary",) * 2, vmem_limit_bytes=VMEM_LIMIT),
        name="attn_dsa",
    )(od3, od3, od3, od3, od3, small3)


def _outproj_kernel(x_ref, oa_ref, ob_ref, oc_ref, od_ref, w_ref, mod_ref, g2_ref, wr_ref, br_ref,
                    xo_ref, h_ref, rt_ref):
    d_model = x_ref.shape[1]
    merged = jnp.concatenate([oa_ref[...], ob_ref[...], oc_ref[...], od_ref[...]], axis=1)
    mix = jnp.dot(merged, w_ref[...], preferred_element_type=F32)
    xn = x_ref[...] + mod_ref[0, 2:3, :] * mix
    xo_ref[...] = xn
    h = _rms(xn, g2_ref[...]) * (1.0 + mod_ref[0, 4:5, :]) + mod_ref[0, 3:4, :]
    h_ref[:, :d_model] = h

    h_hi = h.astype(BF16)
    h_lo = (h - h_hi.astype(F32)).astype(BF16)
    logits = _nt_dot(wr_ref[0], h_hi) + (_nt_dot(wr_ref[0], h_lo) + _nt_dot(wr_ref[1], h_hi))
    score = 1.0 / (1.0 + jnp.exp(-logits))
    biased = score + br_ref[...]
    srow = [score[e:e + 1, :] for e in range(N_EXPERTS)]
    brow = [biased[e:e + 1, :] for e in range(N_EXPERTS)]
    per = N_EXPERTS // N_GROUPS
    best_v = best_g = None
    for g in range(N_GROUPS):
        r = brow[g * per:(g + 1) * per]
        top2 = None
        for a in range(per):
            for c in range(a + 1, per):
                top2 = r[a] + r[c] if top2 is None else jnp.maximum(top2, r[a] + r[c])
        if g == 0:
            best_v, best_g = top2, jnp.zeros_like(top2, dtype=I32)
        else:
            up = top2 > best_v
            best_v = jnp.where(up, top2, best_v)
            best_g = jnp.where(up, g, best_g)
    cand = [jnp.where(best_g == e // per, brow[e], -jnp.inf) for e in range(N_EXPERTS)]

    def first_max(vals):
        v, idx = vals[0], jnp.zeros_like(best_g)
        for e in range(1, N_EXPERTS):
            up = vals[e] > v
            v = jnp.where(up, vals[e], v)
            idx = jnp.where(up, e, idx)
        return idx

    i1 = first_max(cand)
    i2 = first_max([jnp.where(i1 == e, -jnp.inf, cand[e]) for e in range(N_EXPERTS)])
    s1 = sum(jnp.where(i1 == e, srow[e], 0.0) for e in range(N_EXPERTS))
    s2 = sum(jnp.where(i2 == e, srow[e], 0.0) for e in range(N_EXPERTS))
    den = s1 + s2
    first_low = i1 < i2
    la = jnp.minimum(i1, i2) - per * best_g
    lb = jnp.maximum(i1, i2) - per * best_g
    pair = jnp.where(la == 0, lb - 1, jnp.where(la == 1, lb + 1, 5))
    route = jnp.concatenate([(best_g * N_PAIRS + pair).astype(F32), jnp.where(first_low, s1, s2) / den,
                             jnp.where(first_low, s2, s1) / den, jnp.zeros((SUBLANES - 3, den.shape[1]), F32)], axis=0)
    rt_ref[...] = route
    pad = jnp.zeros((LANES - SUBLANES, den.shape[1]), F32)
    h_ref[:, d_model:] = jnp.concatenate([route, pad], axis=0).T


def _outproj_call(x2, outs, w_out, mod_l, g2, wr_t, br, seq, tm):
    t, d = x2.shape
    tpb = seq // tm
    row = lambda n: pl.BlockSpec((tm, n), lambda i: (i, 0))
    const = lambda shape: pl.BlockSpec(shape, lambda i: (0,) * len(shape))
    return pl.pallas_call(
        _outproj_kernel,
        grid=(t // tm,),
        in_specs=[row(d)] + [row(2 * LANES)] * 4 + [
            const(w_out.shape), pl.BlockSpec((1, 6, d), lambda i: (i // tpb, 0, 0)), const((1, d)),
            const(wr_t.shape), const(br.shape)],
        out_specs=[row(d), row(d + LANES), pl.BlockSpec((SUBLANES, tm), lambda i: (0, i))],
        out_shape=[jax.ShapeDtypeStruct((t, d), F32), jax.ShapeDtypeStruct((t, d + LANES), F32),
                   jax.ShapeDtypeStruct((SUBLANES, t), F32)],
        compiler_params=pltpu.CompilerParams(dimension_semantics=("arbitrary",), vmem_limit_bytes=VMEM_LIMIT),
        name="outproj_router",
    )(x2, *outs, w_out, mod_l, g2, wr_t, br)


def _routing_tables(cls, n_steps):
    t = cls.shape[0]
    onehot = (cls[:, None] == jnp.arange(N_CLASSES, dtype=I32)[None, :]).astype(I32)
    upto = jnp.cumsum(onehot, axis=0)
    rank = jnp.sum((upto - onehot) * onehot, axis=1)
    padded = (upto[-1] + MOE_ROWS - 1) // MOE_ROWS * MOE_ROWS
    ends = jnp.cumsum(padded)
    pos = jnp.sum(onehot * (ends - padded)[None, :], axis=1) + rank
    step_cls = jnp.sum((jnp.arange(n_steps, dtype=I32) * MOE_ROWS)[:, None] >= ends[None, :], axis=1)
    valid = (step_cls < N_CLASSES).astype(I32)
    step_cls = jnp.minimum(step_cls, N_CLASSES - 1)
    base = (step_cls // N_PAIRS) * (N_EXPERTS // N_GROUPS)
    e_low = base + jnp.asarray(PAIR_LOW, I32)[step_cls % N_PAIRS]
    e_high = base + jnp.asarray(PAIR_HIGH, I32)[step_cls % N_PAIRS]
    return pos.reshape(t // PERM_ROWS, 1, PERM_ROWS), e_low, e_high, valid


def _row_copy(src_ref, src_row, dst_ref, dst_row, sem):
    return pltpu.make_async_copy(src_ref.at[pl.ds(src_row, 1)], dst_ref.at[pl.ds(dst_row, 1)], sem)


def _issue_rows(copy_row):
    def issue(r8, carry):
        for u in range(ROW_UNROLL):
            copy_row(r8 * ROW_UNROLL + u).start()
        return carry

    lax.fori_loop(0, PERM_ROWS // ROW_UNROLL, issue, 0)


def _rows_done(src_ref, dst_ref, sem):
    pltpu.make_async_copy(src_ref.at[pl.ds(0, PERM_ROWS)], dst_ref.at[pl.ds(0, PERM_ROWS)], sem).wait()


def _scatter_rows_kernel(pos_ref, src_ref, init_ref, dst_ref, sem):
    del init_ref
    _issue_rows(lambda r: _row_copy(src_ref, r, dst_ref, pos_ref[0, 0, r], sem))
    _rows_done(src_ref, dst_ref, sem)


def _scatter_rows_call(pos3, src, n_sorted):
    t, d = src.shape
    return pl.pallas_call(
        _scatter_rows_kernel,
        grid=(t // PERM_ROWS,),
        in_specs=[pl.BlockSpec((1, 1, PERM_ROWS), lambda j: (j, 0, 0), memory_space=pltpu.SMEM),
                  pl.BlockSpec((PERM_ROWS, d), lambda j: (j, 0)), pl.BlockSpec(memory_space=pl.ANY)],
        out_specs=pl.BlockSpec(memory_space=pl.ANY),
        out_shape=jax.ShapeDtypeStruct((n_sorted, d), src.dtype),
        scratch_shapes=[pltpu.SemaphoreType.DMA(())],
        input_output_aliases={2: 0},
        compiler_params=pltpu.CompilerParams(dimension_semantics=("arbitrary",), has_side_effects=True),
        name="moe_sort_rows",
    )(pos3, src, jnp.zeros((n_sorted, d), src.dtype))


def _moe_kernel(e_low_ref, e_high_ref, valid_ref, h_ref, wgu_low_ref, wgu_high_ref, wd_low_ref, wd_high_ref, o_ref):
    del e_low_ref, e_high_ref
    step = pl.program_id(0)
    d_model = o_ref.shape[1]

    @pl.when(valid_ref[step] == 0)
    def _():
        o_ref[...] = jnp.zeros(o_ref.shape, F32)

    @pl.when(valid_ref[step] != 0)
    def _():
        h = h_ref[:, :d_model].astype(BF16)
        w_low = h_ref[:, d_model + 1:d_model + 2]
        w_high = h_ref[:, d_model + 2:d_model + 3]

        def expert(wgu_ref, wd_ref):
            gu = jnp.dot(h, wgu_ref[0], preferred_element_type=F32)
            gate, up = gu[:, :D_EXPERT], gu[:, D_EXPERT:]
            hid = (gate / (1.0 + jnp.exp(-gate)) * up).astype(BF16)
            return jnp.dot(hid, wd_ref[0], preferred_element_type=F32)

        o_ref[...] = w_low * expert(wgu_low_ref, wd_low_ref) + w_high * expert(wgu_high_ref, wd_high_ref)


def _moe_call(hs, e_low, e_high, valid, wgu, wd):
    n_sorted = hs.shape[0]
    d = wd.shape[2]
    grid_spec = pltpu.PrefetchScalarGridSpec(
        num_scalar_prefetch=3,
        grid=(n_sorted // MOE_ROWS,),
        in_specs=[pl.BlockSpec((MOE_ROWS, hs.shape[1]), lambda j, lo, hi, ok: (j, 0)),
                  pl.BlockSpec((1, d, 2 * D_EXPERT), lambda j, lo, hi, ok: (lo[j], 0, 0)),
                  pl.BlockSpec((1, d, 2 * D_EXPERT), lambda j, lo, hi, ok: (hi[j], 0, 0)),
                  pl.BlockSpec((1, D_EXPERT, d), lambda j, lo, hi, ok: (lo[j], 0, 0)),
                  pl.BlockSpec((1, D_EXPERT, d), lambda j, lo, hi, ok: (hi[j], 0, 0))],
        out_specs=pl.BlockSpec((MOE_ROWS, d), lambda j, lo, hi, ok: (j, 0)))
    return pl.pallas_call(
        _moe_kernel,
        grid_spec=grid_spec,
        out_shape=jax.ShapeDtypeStruct((n_sorted, d), F32),
        compiler_params=pltpu.CompilerParams(dimension_semantics=("arbitrary",), vmem_limit_bytes=VMEM_LIMIT),
        name="moe",
    )(e_low, e_high, valid, hs, wgu, wgu, wd, wd)


def _gather_residual_kernel(pos_ref, pos_next_ref, ys_ref, x_ref, mod_ref, gf_ref, o_ref, rows_ref, sems, *, final):
    j = pl.program_id(0)
    slot = j % 2

    def request(index_ref, into):
        _issue_rows(lambda r: _row_copy(ys_ref, index_ref[0, 0, r], rows_ref.at[into], r, sems.at[into]))

    @pl.when(j == 0)
    def _():
        request(pos_ref, slot)

    @pl.when(j + 1 < pl.num_programs(0))
    def _():
        request(pos_next_ref, 1 - slot)

    _rows_done(ys_ref, rows_ref.at[slot], sems.at[slot])
    xn = x_ref[...] + mod_ref[0, 5:6, :] * rows_ref[slot]
    o_ref[...] = _rms(xn, gf_ref[...]) if final else xn


def _gather_residual_call(pos3, ys, x2, mod_l, gf, seq, final):
    t, d = x2.shape
    tpb = seq // PERM_ROWS
    last = t // PERM_ROWS - 1
    return pl.pallas_call(
        functools.partial(_gather_residual_kernel, final=final),
        grid=(t // PERM_ROWS,),
        in_specs=[pl.BlockSpec((1, 1, PERM_ROWS), lambda j: (j, 0, 0), memory_space=pltpu.SMEM),
                  pl.BlockSpec((1, 1, PERM_ROWS), lambda j: (jnp.minimum(j + 1, last), 0, 0),
                               memory_space=pltpu.SMEM),
                  pl.BlockSpec(memory_space=pl.ANY),
                  pl.BlockSpec((PERM_ROWS, d), lambda j: (j, 0)),
                  pl.BlockSpec((1, 6, d), lambda j: (j // tpb, 0, 0)),
                  pl.BlockSpec((1, d), lambda j: (0, 0))],
        out_specs=pl.BlockSpec((PERM_ROWS, d), lambda j: (j, 0)),
        out_shape=jax.ShapeDtypeStruct((t, d), F32),
        scratch_shapes=[pltpu.VMEM((2, PERM_ROWS, d), F32), pltpu.SemaphoreType.DMA((2,))],
        compiler_params=pltpu.CompilerParams(dimension_semantics=("arbitrary",), vmem_limit_bytes=VMEM_LIMIT),
        name="moe_unsort_residual",
    )(pos3, pos3, ys, x2, mod_l, gf)


def _swap_half(w):
    half = w.shape[1] // 2
    return jnp.concatenate([-w[:, half:], w[:, :half]], axis=1)


def _layer_weights(w_in, w_q_up, w_kv_up):
    d = w_in.shape[0]
    pts = np.cumsum(IN_SIZES)[:-1].tolist()
    (a_q, a_k, a_v, a_f, b_cq, b_ckv, b_kr, c_q, c_k, c_v,
     d_q, d_k, d_v, d_qi, d_ki, d_wi) = jnp.split(w_in, pts, axis=1)
    qs = HEAD_DIM ** -0.5
    small = jnp.concatenate([
        a_f, d_wi * ((IDX_HEADS * IDX_DIM) ** -0.5),
        jnp.zeros((d, _KR_LANE - _WI_LANE - IDX_HEADS), F32),
        b_kr, _swap_half(b_kr), d_ki], axis=1)
    w_all = jnp.concatenate([
        a_q * qs, a_k, a_v,
        c_q * qs, c_k, c_v,
        d_q * qs, d_qi, d_k, d_v,
        small, b_cq, b_ckv], axis=1).astype(BF16)

    per_q = MLA_NOPE + MLA_ROPE
    wq = []
    for hd in range(N_HEADS):
        blk = w_q_up[:, hd * per_q:(hd + 1) * per_q]
        rot = blk[:, MLA_NOPE:]
        wq += [blk[:, :MLA_NOPE], rot, _swap_half(rot)]
    wq = jnp.concatenate(wq, axis=1).astype(BF16)

    place = np.zeros((LANES, LANES), np.float32)
    place[_KR_LANE + np.arange(MLA_ROPE), MLA_NOPE + np.arange(MLA_ROPE)] = 1.0
    place = jnp.asarray(place)
    kcols, vcols = [], []
    for hd in range(N_HEADS):
        blk = w_kv_up[:, hd * 2 * HEAD_DIM:(hd + 1) * 2 * HEAD_DIM]
        knope = jnp.concatenate([blk[:, :MLA_NOPE], jnp.zeros((MLA_KV_RANK, LANES - MLA_NOPE), F32)], axis=1)
        kcols.append(jnp.concatenate([knope, place], axis=0))
        vcols.append(jnp.concatenate([blk[:, MLA_NOPE:], jnp.zeros((LANES, HEAD_DIM), F32)], axis=0))
    wkv = jnp.concatenate(kcols + vcols, axis=1).astype(BF16)
    return w_all, wq, wkv


def _rope_tables(seq):
    half = MLA_ROPE // 2
    inv = ROPE_THETA ** (-jnp.arange(half, dtype=F32) / half)
    ang = jnp.arange(seq, dtype=F32)[:, None] * inv[None, :]
    cos = jnp.tile(jnp.cos(ang), (1, 2))
    sin = jnp.tile(jnp.sin(ang), (1, 2))
    scale = LOG2E * (MLA_NOPE + MLA_ROPE) ** -0.5
    z = lambda n: jnp.zeros((seq, n), F32)
    tab_q = jnp.concatenate([jnp.full((seq, MLA_NOPE), scale, F32), cos * scale, z(MLA_ROPE)], axis=1)
    tab_qs = jnp.concatenate([z(MLA_NOPE), sin * scale, z(MLA_ROPE)], axis=1)
    tab_k = jnp.concatenate([z(_KR_LANE), cos, z(LANES - _KR_LANE - MLA_ROPE)], axis=1)
    tab_ks = jnp.concatenate([z(_KR_LANE), sin, z(LANES - _KR_LANE - MLA_ROPE)], axis=1)
    return jnp.stack([tab_q, tab_qs, tab_k, tab_ks])


def kernel(x, c, w_ada, b_ada, g_norm1, w_in, b_forget, g_q_mla, w_q_up, g_kv_mla, w_kv_up, sinks, w_out,
           g_norm2, w_router, b_router, w_gate, w_up, w_down, g_final):
    b, s, d = x.shape
    depth = w_in.shape[0]
    t = b * s
    topk = min(TOPK_MAX, s // 4)
    tm = min(1024, s)
    tq = min(512, s)
    tq_swa = min(512, s)
    qb_dsa = 2 * LANES
    n_steps = t // MOE_ROWS + N_CLASSES
    assert s % tm == 0 and s % tq == 0 and s % qb_dsa == 0 and s % PERM_ROWS == 0

    mod = _ada_call(c, w_ada, b_ada).reshape(depth, b, 6, d)
    tabs = _rope_tables(s)
    wr_hi = w_router.T.astype(BF16)
    wr_t = jnp.stack([wr_hi, (w_router.T - wr_hi.astype(F32)).astype(BF16)])
    br = b_router.reshape(N_EXPERTS, 1)
    x2 = x.reshape(t, d)
    for l in range(depth):
        w_all, wq, wkv = _layer_weights(w_in[l], w_q_up[l], w_kv_up[l])
        oa, oc, od, osm, oqb, okb, ovb = _inproj_call(
            x2, mod[l], g_norm1[l].reshape(1, d), w_all, wq, wkv,
            g_q_mla[l].reshape(1, -1), g_kv_mla[l].reshape(1, -1), tabs, s, tm)
        small3 = osm.reshape(b, s, LANES)
        bf = jnp.zeros((1, LANES), F32).at[0, :N_HEADS].set(b_forget[l])
        fcum = _fox_cumsum_call(small3, bf)
        frow = jnp.swapaxes(fcum[:, :, :8], 1, 2)
        out_a = _causal_attn_call((oa.reshape(b, s, -1), fcum, frow), True, b, s, tq)
        out_b = _causal_attn_call((oqb.reshape(b, s, -1), okb.reshape(b, s, -1), ovb.reshape(b, s, -1)),
                                  False, b, s, tq)
        out_c = _swa_call(oc.reshape(b, s, -1), sinks[l], b, s, tq_swa)
        out_d = _dsa_call(od.reshape(b, s, -1), small3, b, s, topk, qb_dsa)
        outs = [o.reshape(t, 2 * LANES) for o in (out_a, out_b, out_c, out_d)]
        x2, h2, route = _outproj_call(x2, outs, w_out[l].astype(BF16), mod[l], g_norm2[l].reshape(1, d),
                                      wr_t, br, s, tm)
        pos3, e_low, e_high, valid = _routing_tables(route[0].astype(I32), n_steps)
        wgu = jnp.concatenate([w_gate[l], w_up[l]], axis=2).astype(BF16)
        hs = _scatter_rows_call(pos3, h2, n_steps * MOE_ROWS)
        ys = _moe_call(hs, e_low, e_high, valid, wgu, w_down[l].astype(BF16))
        x2 = _gather_residual_call(pos3, ys, x2, mod[l], g_final.reshape(1, d), s, l == depth - 1)
    return x2.reshape(b, s, d)
```

```python
import functools

import jax
import jax.numpy as jnp
import numpy as np
from jax import lax
from jax.experimental import pallas as pl
from jax.experimental.pallas import tpu as pltpu

F32 = jnp.float32
BF16 = jnp.bfloat16
I32 = jnp.int32
I16 = jnp.int16
I16_MIN, I16_MAX = -(2 ** 15), 2 ** 15 - 1

EPS = 1e-6
HEAD_DIM = 64
LANES = 128
SUBLANES = 8
N_HEADS = 4
MLA_Q_RANK = 256
MLA_KV_RANK = 128
MLA_NOPE = 64
MLA_ROPE = 32
ROPE_THETA = 10000.0
WINDOW = 128
IDX_HEADS = 8
IDX_DIM = 32
TOPK_MAX = 256
N_EXPERTS = 16
N_GROUPS = 4
N_PAIRS = 6
N_CLASSES = N_GROUPS * N_PAIRS
PAIR_LOW = (0, 0, 0, 1, 1, 2)
PAIR_HIGH = (1, 2, 3, 2, 3, 3)
D_EXPERT = 256
MOE_ROWS = 512
PERM_ROWS = 1024
ROW_UNROLL = 8
IN_SIZES = (256, 256, 256, 4, 256, 128, 32, 256, 128, 128, 256, 64, 64, 256, 32, 8)

LOG2E = 1.4426950408889634
NEG = -1e30
INT_MIN = -(2 ** 31)
KEY_NEG_INF = INT_MIN + 0x7FFFFF
VMEM_LIMIT = 56 * 1024 * 1024

_CA, _CC, _CD, _CS, _CQ, _CKV, _CEND = 0, 768, 1280, 1920, 2048, 2304, 2432
_F_LANE, _WI_LANE, _KR_LANE = 0, 4, 32


def _nt_dot(a, b):
    return lax.dot_general(a, b, (((1,), (1,)), ((), ())), preferred_element_type=F32)


def _rms(x, g):
    return x * lax.rsqrt(jnp.mean(x * x, axis=-1, keepdims=True) + EPS) * g


def _ada_kernel(c_ref, w_ref, b_ref, o_ref):
    c = c_ref[...]
    act = (c / (1.0 + jnp.exp(-c))).astype(BF16)
    o_ref[0] = jnp.dot(act, w_ref[0].astype(BF16), preferred_element_type=F32) + b_ref[0]


def _ada_call(c, w_ada, b_ada):
    depth, d, n = w_ada.shape
    bsz = c.shape[0]
    tn = 1024
    return pl.pallas_call(
        _ada_kernel,
        grid=(depth, n // tn),
        in_specs=[pl.BlockSpec((bsz, d), lambda l, j: (0, 0)),
                  pl.BlockSpec((1, d, tn), lambda l, j: (l, 0, j)),
                  pl.BlockSpec((1, 1, tn), lambda l, j: (l, 0, j))],
        out_specs=pl.BlockSpec((1, bsz, tn), lambda l, j: (l, 0, j)),
        out_shape=jax.ShapeDtypeStruct((depth, bsz, n), F32),
        compiler_params=pltpu.CompilerParams(dimension_semantics=("arbitrary", "arbitrary"),
                                             vmem_limit_bytes=VMEM_LIMIT),
        name="adaln",
    )(c, w_ada, b_ada.reshape(depth, 1, n))


def _inproj_kernel(x_ref, mod_ref, g1_ref, w_ref, wq_ref, wkv_ref, gq_ref, gkv_ref, tab_ref,
                   oa_ref, oc_ref, od_ref, os_ref, oqb_ref, okb_ref, ovb_ref):
    h = _rms(x_ref[...], g1_ref[...]) * (1.0 + mod_ref[0, 1:2, :]) + mod_ref[0, 0:1, :]
    h = h.astype(BF16)

    def proj(lo, hi):
        return jnp.dot(h, w_ref[:, lo:hi], preferred_element_type=F32)

    q_width = N_HEADS * HEAD_DIM
    low = _lane_is_low()

    def both_halves(pair):
        swapped = pltpu.roll(pair, HEAD_DIM, axis=1)
        return jnp.where(low, pair, swapped), jnp.where(low, swapped, pair)

    pa = proj(_CA, _CC)
    oa_ref[:, :q_width] = (pa[:, :q_width] * LOG2E).astype(BF16)
    oa_ref[:, q_width:] = pa[:, q_width:].astype(BF16)

    pc = proj(_CC, _CD)
    oc_ref[:, :q_width] = (pc[:, :q_width] * LOG2E).astype(BF16)
    for n in range(2):
        tiles = both_halves(pc[:, q_width + n * LANES:q_width + (n + 1) * LANES])
        for g in range(2):
            lo_col = q_width + (2 * n + g) * LANES
            oc_ref[:, lo_col:lo_col + LANES] = tiles[g].astype(BF16)

    pd = proj(_CD, _CS)
    small = proj(_CS, _CQ)
    os_ref[...] = small
    od_ref[:, :q_width] = (pd[:, :q_width] * LOG2E).astype(BF16)
    k_twice, v_twice = both_halves(pd[:, 2 * q_width:])
    od_ref[:, q_width:q_width + LANES] = k_twice.astype(BF16)
    od_ref[:, q_width + LANES:2 * q_width] = v_twice.astype(BF16)
    od_ref[:, 2 * q_width:3 * q_width] = pd[:, q_width:2 * q_width].astype(BF16)
    lane = lax.broadcasted_iota(I32, (1, LANES), 1)
    ki4 = small
    for n in range(1, LANES // IDX_DIM):
        ki4 = jnp.where(lane < LANES - n * IDX_DIM, pltpu.roll(small, LANES - n * IDX_DIM, axis=1), ki4)
    od_ref[:, 3 * q_width:] = ki4.astype(BF16)

    cq = _rms(proj(_CQ, _CKV), gq_ref[...]).astype(BF16)
    qf = jnp.dot(cq, wq_ref[...], preferred_element_type=F32)
    tab_q, tab_qs, tab_k, tab_ks = tab_ref[0], tab_ref[1], tab_ref[2], tab_ref[3]
    for hd in range(N_HEADS):
        qg = qf[:, hd * LANES:(hd + 1) * LANES]
        qr = qg * tab_q + pltpu.roll(qg, LANES - MLA_ROPE, axis=1) * tab_qs
        oqb_ref[:, hd * LANES:(hd + 1) * LANES] = qr.astype(BF16)

    ckv = _rms(proj(_CKV, _CEND), gkv_ref[...]).astype(BF16)
    kr = (small * tab_k + pltpu.roll(small, LANES - MLA_ROPE, axis=1) * tab_ks).astype(BF16)
    kvf = jnp.dot(jnp.concatenate([ckv, kr], axis=1), wkv_ref[...], preferred_element_type=F32)
    okb_ref[...] = kvf[:, :4 * LANES].astype(BF16)
    ovb_ref[...] = kvf[:, 4 * LANES:].astype(BF16)


def _inproj_call(x2, mod_l, g1, w_all, wq, wkv, gq, gkv, tabs, seq, tm):
    t, d = x2.shape
    tpb = seq // tm
    row = lambda n: pl.BlockSpec((tm, n), lambda i: (i, 0))
    const = lambda shape: pl.BlockSpec(shape, lambda i: (0,) * len(shape))
    widths = (768, 768, 896, 128, 512, 512, 256)
    dtypes = (BF16, BF16, BF16, F32, BF16, BF16, BF16)
    return pl.pallas_call(
        _inproj_kernel,
        grid=(t // tm,),
        in_specs=[row(d),
                  pl.BlockSpec((1, 6, d), lambda i: (i // tpb, 0, 0)),
                  const((1, d)), const(w_all.shape), const(wq.shape), const(wkv.shape),
                  const((1, MLA_Q_RANK)), const((1, MLA_KV_RANK)),
                  pl.BlockSpec((4, tm, LANES), lambda i: (0, i % tpb, 0))],
        out_specs=[row(n) for n in widths],
        out_shape=[jax.ShapeDtypeStruct((t, n), dt) for n, dt in zip(widths, dtypes)],
        compiler_params=pltpu.CompilerParams(dimension_semantics=("arbitrary",), vmem_limit_bytes=VMEM_LIMIT),
        name="inproj",
    )(x2, mod_l, g1, w_all, wq, wkv, gq, gkv, tabs)


def _fox_cumsum_kernel(s_ref, b_ref, o_ref):
    z = s_ref[0] + b_ref[...]
    lf = jnp.minimum(z, 0.0) - jnp.log(1.0 + jnp.exp(-jnp.abs(z)))
    n = lf.shape[0]
    row = lax.broadcasted_iota(I32, lf.shape, 0)
    d = 1
    while d < n:
        lf = lf + jnp.where(row >= d, pltpu.roll(lf, d, axis=0), 0.0)
        d *= 2
    o_ref[0] = lf * LOG2E


def _fox_cumsum_call(small3, bf):
    b, s, _ = small3.shape
    return pl.pallas_call(
        _fox_cumsum_kernel,
        grid=(b,),
        in_specs=[pl.BlockSpec((1, s, LANES), lambda i: (i, 0, 0)),
                  pl.BlockSpec((1, LANES), lambda i: (0, 0))],
        out_specs=pl.BlockSpec((1, s, LANES), lambda i: (i, 0, 0)),
        out_shape=jax.ShapeDtypeStruct((b, s, LANES), F32),
        compiler_params=pltpu.CompilerParams(dimension_semantics=("arbitrary",), vmem_limit_bytes=VMEM_LIMIT),
        name="fox_cumsum",
    )(small3, bf)


def _lane_is_low():
    return lax.broadcasted_iota(I32, (1, LANES), 1) < HEAD_DIM


def _split_pair(q):
    low = _lane_is_low()
    zero = jnp.zeros_like(q)
    return jnp.where(low, q, zero), jnp.where(low, zero, q)


def _values_with_ones(v, j):
    low = _lane_is_low()
    return jnp.where(low if j == 0 else ~low, v, jnp.ones_like(v))


def _pair_output(a0, a1, extra0=None, extra1=None):
    l0 = pltpu.roll(a0, HEAD_DIM, axis=1)
    l1 = pltpu.roll(a1, HEAD_DIM, axis=1)
    if extra0 is not None:
        l0, l1 = l0 + extra0, l1 + extra1
    return jnp.where(_lane_is_low(), a0 / l0, a1 / l1)


def _two_pass_attention(n_before, logits, values, s_ref, mx_ref, acc_ref, n_heads, shared=None, diagonal_mask=True):
    tq, tk = s_ref.shape[-2:]
    mx_ref[...] = jnp.full(mx_ref.shape, NEG, F32)

    def store(c, n, diagonal):
        ctx = shared(c, n) if shared is not None else None
        for j in range(n_heads):
            s = logits(c, n, j, diagonal, ctx)
            for t in range(n):
                s_ref[j, c + t] = s[:, t * tk:(t + 1) * tk]
            parts = [s[:, blk * LANES:(blk + 1) * LANES] for blk in range(n * tk // LANES)]
            while len(parts) > 1:
                parts = [jnp.maximum(a, b) for a, b in zip(parts[::2], parts[1::2])]
            mx_ref[j] = jnp.maximum(mx_ref[j], parts[0])

    def store_two(c2, carry):
        store(2 * c2, 2, False)
        return carry

    if diagonal_mask:
        lax.fori_loop(0, n_before // 2, store_two, 0)

        @pl.when(n_before % 2 == 1)
        def _():
            store(n_before - 1, 1, False)

        store(n_before, 1, True)
    else:
        lax.fori_loop(0, (n_before + 1) // 2, store_two, 0)

        @pl.when(n_before % 2 == 0)
        def _():
            store(n_before, 1, False)

    row_max = [jnp.max(mx_ref[j], axis=1, keepdims=True) for j in range(n_heads)]
    shift = [jnp.broadcast_to(m, (tq, tk)) for m in row_max]
    acc_ref[...] = jnp.zeros(acc_ref.shape, F32)

    def accum(c, n):
        for j in range(n_heads):
            p = [jnp.exp2(s_ref[j, c + t] - shift[j]).astype(BF16) for t in range(n)]
            p = p[0] if n == 1 else jnp.concatenate(p, axis=1)
            acc_ref[j] += jnp.dot(p, values(c, n, j), preferred_element_type=F32)

    def accum_two(c2, carry):
        accum(2 * c2, 2)
        return carry

    lax.fori_loop(0, (n_before + 1) // 2, accum_two, 0)

    @pl.when(n_before % 2 == 0)
    def _():
        accum(n_before, 1)

    return row_max


def _causal_attn_kernel(*refs, fox, tq):
    if fox:
        q_ref, k_ref, v_ref, fc_ref, fr_ref, o_ref, s_ref, mx_ref, acc_ref = refs
    else:
        q_ref, k_ref, v_ref, o_ref, s_ref, mx_ref, acc_ref = refs
    tk = tq
    hp = pl.program_id(1)
    i = pl.program_id(2)
    if fox:
        qs = _split_pair(q_ref[0])
        lane = lax.broadcasted_iota(I32, (1, LANES), 1)
        fcol = [jnp.sum(jnp.where(lane == 2 * hp + j, fc_ref[0], 0.0), axis=1, keepdims=True) for j in range(2)]
    else:
        qs = (q_ref[0, :, :LANES], q_ref[0, :, LANES:])
    causal = (lax.broadcasted_iota(I32, (tq, tk), 1) <= lax.broadcasted_iota(I32, (tq, tk), 0))

    def logits(c, n, j, diagonal, _):
        start = pl.multiple_of(c * tk, tk)
        if fox:
            s = _nt_dot(qs[j], k_ref[0, pl.ds(start, n * tk), :])
            s = (s + fcol[j]) - fr_ref[0, pl.ds(2 * hp + j, 1), pl.ds(start, n * tk)]
        else:
            s = _nt_dot(qs[j], k_ref[0, pl.ds(start, n * tk), j * LANES:(j + 1) * LANES])
        return jnp.where(causal, s, NEG) if diagonal else s

    def values(c, n, j):
        return _values_with_ones(v_ref[0, pl.ds(pl.multiple_of(c * tk, tk), n * tk), :], j)

    _two_pass_attention(i, logits, values, s_ref, mx_ref, acc_ref, 2)
    o_ref[0] = _pair_output(acc_ref[0], acc_ref[1]).astype(o_ref.dtype)


def _causal_attn_call(arrs, fox, b, s, tq):
    scratch = [pltpu.VMEM((2, s // tq, tq, tq), F32), pltpu.VMEM((2, tq, LANES), F32),
               pltpu.VMEM((2, tq, LANES), F32)]
    if fox:
        qkv, fcol, frow = arrs
        operands = (qkv, qkv, qkv, fcol, frow)
        in_specs = [pl.BlockSpec((1, tq, LANES), lambda bi, hp, i: (bi, i, hp)),
                    pl.BlockSpec((1, s, LANES), lambda bi, hp, i: (bi, 0, 2 + hp)),
                    pl.BlockSpec((1, s, LANES), lambda bi, hp, i: (bi, 0, 4 + hp)),
                    pl.BlockSpec((1, tq, LANES), lambda bi, hp, i: (bi, i, 0)),
                    pl.BlockSpec((1, 8, s), lambda bi, hp, i: (bi, 0, 0))]
    else:
        operands = arrs
        in_specs = [pl.BlockSpec((1, tq, 2 * LANES), lambda bi, hp, i: (bi, i, hp)),
                    pl.BlockSpec((1, s, 2 * LANES), lambda bi, hp, i: (bi, 0, hp)),
                    pl.BlockSpec((1, s, LANES), lambda bi, hp, i: (bi, 0, hp))]
    return pl.pallas_call(
        functools.partial(_causal_attn_kernel, fox=fox, tq=tq),
        grid=(b, 2, s // tq),
        in_specs=in_specs,
        out_specs=pl.BlockSpec((1, tq, LANES), lambda bi, hp, i: (bi, i, hp)),
        out_shape=jax.ShapeDtypeStruct((b, s, 2 * LANES), BF16),
        scratch_shapes=scratch,
        compiler_params=pltpu.CompilerParams(dimension_semantics=("arbitrary",) * 3, vmem_limit_bytes=VMEM_LIMIT),
        name="attn_fox" if fox else "attn_mla",
    )(*operands)


def _alibi_slope(head):
    return lax.shift_left(jnp.int32(1), 7 - head).astype(F32) * (LOG2E * 2.0 ** -8)


def _swa_kernel(sink_ref, q_ref, k_ref, v_ref, o_ref, *, tq):
    hp = pl.program_id(1)
    i = pl.program_id(2)
    band = 2 * WINDOW
    rel = lax.broadcasted_iota(I32, (WINDOW, band), 0) - lax.broadcasted_iota(I32, (WINDOW, band), 1)
    for r in range(tq // WINDOW):
        q_start = i * tq + r * WINDOW
        k_start = pl.multiple_of(jnp.maximum(q_start - WINDOW, 0), WINDOW)
        kb = k_ref[0, pl.ds(k_start, band), :]
        vb = v_ref[0, pl.ds(k_start, band), :]
        dist = rel + (q_start - k_start)
        valid = (dist >= 0) & (dist < WINDOW)
        distf = dist.astype(F32)
        qs = _split_pair(q_ref[0, r * WINDOW:(r + 1) * WINDOW, :])
        acc, sink_term = [], []
        for j in range(2):
            sink = sink_ref[2 * hp + j] * LOG2E
            s = jnp.where(valid, _nt_dot(qs[j], kb) - _alibi_slope(2 * hp + j) * distf, NEG)
            m = jnp.maximum(jnp.max(s, axis=1, keepdims=True), sink)
            p = jnp.exp2(s - m).astype(BF16)
            acc.append(jnp.dot(p, _values_with_ones(vb, j), preferred_element_type=F32))
            sink_term.append(jnp.exp2(sink - m))
        o_ref[0, r * WINDOW:(r + 1) * WINDOW, :] = _pair_output(acc[0], acc[1], *sink_term).astype(o_ref.dtype)


def _swa_call(qkv, sinks, b, s, tq):
    return pl.pallas_call(
        functools.partial(_swa_kernel, tq=tq),
        grid=(b, 2, s // tq),
        in_specs=[pl.BlockSpec(memory_space=pltpu.SMEM),
                  pl.BlockSpec((1, tq, LANES), lambda bi, hp, i: (bi, i, hp)),
                  pl.BlockSpec((1, s, LANES), lambda bi, hp, i: (bi, 0, 2 + hp)),
                  pl.BlockSpec((1, s, LANES), lambda bi, hp, i: (bi, 0, 4 + hp))],
        out_specs=pl.BlockSpec((1, tq, LANES), lambda bi, hp, i: (bi, i, hp)),
        out_shape=jax.ShapeDtypeStruct((b, s, 2 * LANES), BF16),
        compiler_params=pltpu.CompilerParams(dimension_semantics=("arbitrary",) * 3, vmem_limit_bytes=VMEM_LIMIT),
        name="attn_swa",
    )(sinks, qkv, qkv, qkv)


def _dsa_kernel(q_ref, k_ref, v_ref, qi_ref, ki_ref, wi_ref, o_ref, keys_ref, hi_ref, lo_ref, thr_ref,
                s_ref, mx_ref, acc_ref, *, seq, topk, qb):
    kc = LANES
    kw = 2 * kc
    i = pl.program_id(1)
    nwide = (i + 1) * (qb // kw)
    lane = lax.broadcasted_iota(I32, (1, LANES), 1)
    key_row = lax.broadcasted_iota(I32, (kc, qb), 0)
    query_pos = i * qb + lax.broadcasted_iota(I32, (1, qb), 1)

    qi = qi_ref[0]
    wi_t = wi_ref[0].T
    qms, wrows = [], []
    for hd in range(IDX_HEADS):
        g, r = divmod(hd, LANES // IDX_DIM)
        sel = (lane >= r * IDX_DIM) & (lane < (r + 1) * IDX_DIM)
        qg = qi[:, g * LANES:(g + 1) * LANES]
        qms.append(jnp.where(sel, qg, jnp.zeros_like(qg)))
        wrows.append(wi_t[_WI_LANE + hd:_WI_LANE + hd + 1, :])
    q_all = jnp.concatenate(qms, axis=0)
    key_minus_query = (lax.broadcasted_iota(I32, (kw, qb), 0) - lax.broadcasted_iota(I32, (kw, qb), 1))

    def score_chunk(c, carry):
        start = pl.multiple_of(c * kw, kw)
        logit = _nt_dot(ki_ref[0, pl.ds(start, kw), :], q_all)
        sc = wrows[0] * jnp.maximum(logit[:, :qb], 0.0)
        for hd in range(1, IDX_HEADS):
            sc = sc + wrows[hd] * jnp.maximum(logit[:, hd * qb:(hd + 1) * qb], 0.0)
        sc = jnp.where(key_minus_query <= i * qb - start, sc, -jnp.inf)
        sc = jnp.where(sc == 0.0, 0.0, sc)
        bits = pltpu.bitcast(sc, I32)
        keys = bits ^ ((bits >> 31) & 0x7FFFFFFF)
        keys_ref[2 * c] = keys[:kc]
        keys_ref[2 * c + 1] = keys[kc:]
        hi = (keys >> 16).astype(I16)
        lo = ((keys & 0xFFFF) + I16_MIN).astype(I16)
        hi_ref[2 * c] = hi[:kc]
        hi_ref[2 * c + 1] = hi[kc:]
        lo_ref[2 * c] = lo[:kc]
        lo_ref[2 * c + 1] = lo[kc:]
        return carry

    lax.fori_loop(0, nwide, score_chunk, 0)

    keep_all = jnp.full((1, qb), KEY_NEG_INF + 1, I32)
    thr_ref[...] = jnp.broadcast_to(keep_all, thr_ref.shape)

    def count(pred):
        def body(c2, acc):
            parts = []
            for c in (2 * c2, 2 * c2 + 1):
                hit = jnp.where(pred(keys_ref[c], c), 1.0, 0.0)
                parts += [hit[r * SUBLANES:(r + 1) * SUBLANES] for r in range(kc // SUBLANES)]
            while len(parts) > 1:
                parts = [a + b for a, b in zip(parts[::2], parts[1::2])]
            return acc + parts[0]
        acc = lax.fori_loop(0, nwide, body, jnp.zeros((SUBLANES, qb), F32))
        return jnp.sum(acc, axis=0, keepdims=True)

    @pl.when((i + 1) * qb > topk)
    def _():
        kf = float(topk)

        pack = 2 * SUBLANES

        def count16(plane_ref, cand):
            cand16 = cand.astype(I16)

            def body(c2, acc):
                parts = []
                for c in (2 * c2, 2 * c2 + 1):
                    hit = jnp.where(plane_ref[c] >= cand16, jnp.int16(1), jnp.int16(0))
                    parts += [hit[r * pack:(r + 1) * pack] for r in range(kc // pack)]
                while len(parts) > 1:
                    parts = [a + b for a, b in zip(parts[::2], parts[1::2])]
                return acc + parts[0]

            acc = lax.fori_loop(0, nwide, body, jnp.zeros((pack, qb), I16))
            return jnp.sum(acc.astype(F32), axis=0, keepdims=True)

        def search16(plane_ref, cnt_at_floor):
            def bit_step(it, carry):
                val, cnt_val = carry
                cand = val + lax.shift_left(jnp.int32(1), 15 - it)
                cnt = count16(plane_ref, cand)
                ok = cnt >= kf
                return jnp.where(ok, cand, val), jnp.where(ok, cnt, cnt_val)

            return lax.fori_loop(0, 16, bit_step, (jnp.full((1, qb), I16_MIN, I32), cnt_at_floor))

        thr_hi, cnt_hi = search16(hi_ref, jnp.full((1, qb), float(seq), F32))
        thr_hi16 = thr_hi.astype(I16)

        def pin(c, carry):
            h = hi_ref[c]
            lo_ref[c] = jnp.where(h > thr_hi16, jnp.int16(I16_MAX),
                                  jnp.where(h == thr_hi16, lo_ref[c], jnp.int16(I16_MIN)))
            return carry

        lax.fori_loop(0, 2 * nwide, pin, 0)
        thr_lo, cnt_thr = search16(lo_ref, cnt_hi)
        thr = thr_hi * 65536 + (thr_lo - I16_MIN)
        thr_ref[...] = jnp.broadcast_to(jnp.where(query_pos < topk, keep_all, thr), thr_ref.shape)

        @pl.when(jnp.max(cnt_thr) > kf)
        def _():
            need = kf - count(lambda kk, c: kk > thr)

            def idx_step(it, pos):
                cand = pos + lax.shift_left(jnp.int32(1), (seq.bit_length() - 2) - it)
                cnt = count(lambda kk, c: (kk == thr) & (key_row + c * kc < cand))
                return jnp.where(cnt < need, cand, pos)

            pos = lax.fori_loop(0, seq.bit_length() - 1, idx_step, jnp.zeros((1, qb), I32))

            def demote(c, carry):
                kk = keys_ref[c]
                keys_ref[c] = jnp.where((kk == thr) & (key_row + c * kc > pos), kk - 1, kk)
                return carry

            lax.fori_loop(0, 2 * nwide, demote, 0)

    q = q_ref[0]
    qs = _split_pair(q[:, :LANES]) + _split_pair(q[:, LANES:])
    thr_row = thr_ref[0:1, :]
    key_col = lax.broadcasted_iota(I32, (1, 2 * kw), 1)

    def shared(c, n):
        halves = [jnp.where(keys_ref[2 * c + h] >= thr_row, 0.0, NEG).T for h in range(2 * n)]
        key_off = (key_col[:, :n * kw] + (c * kw - i * qb)).astype(F32)
        return jnp.concatenate(halves, axis=1), key_off

    def logits(c, n, hd, diagonal, ctx):
        bias, key_off = ctx
        start = pl.multiple_of(c * kw, kw)
        slope = LOG2E * 2.0 ** -(N_HEADS + hd + 1)
        return _nt_dot(qs[hd], k_ref[0, pl.ds(start, n * kw), :]) + (bias + slope * key_off)

    def values(c, n, hd):
        return _values_with_ones(v_ref[0, pl.ds(pl.multiple_of(c * kw, kw), n * kw), :], hd % 2)

    _two_pass_attention(nwide - 1, logits, values, s_ref, mx_ref, acc_ref, N_HEADS, shared, diagonal_mask=False)
    o_ref[0] = jnp.concatenate([_pair_output(acc_ref[0], acc_ref[1]), _pair_output(acc_ref[2], acc_ref[3])],
                               axis=1).astype(o_ref.dtype)


def _dsa_call(od3, small3, b, s, topk, qb):
    kc = LANES
    scratch = [pltpu.VMEM((s // kc, kc, qb), I32), pltpu.VMEM((s // kc, kc, qb), I16),
               pltpu.VMEM((s // kc, kc, qb), I16), pltpu.VMEM((SUBLANES, qb), I32),
               pltpu.VMEM((N_HEADS, s // (2 * kc), qb, 2 * kc), F32),
               pltpu.VMEM((N_HEADS, qb, LANES), F32), pltpu.VMEM((N_HEADS, qb, LANES), F32)]
    return pl.pallas_call(
        functools.partial(_dsa_kernel, seq=s, topk=topk, qb=qb),
        grid=(b, s // qb),
        in_specs=[pl.BlockSpec((1, qb, 2 * LANES), lambda bi, i: (bi, i, 0)),
                  pl.BlockSpec((1, s, LANES), lambda bi, i: (bi, 0, 2)),
                  pl.BlockSpec((1, s, LANES), lambda bi, i: (bi, 0, 3)),
                  pl.BlockSpec((1, qb, 2 * LANES), lambda bi, i: (bi, i, 2)),
                  pl.BlockSpec((1, s, LANES), lambda bi, i: (bi, 0, 6)),
                  pl.BlockSpec((1, qb, LANES), lambda bi, i: (bi, i, 0))],
        out_specs=pl.BlockSpec((1, qb, 2 * LANES), lambda bi, i: (bi, i, 0)),
        out_shape=jax.ShapeDtypeStruct((b, s, 2 * LANES), BF16),
        scratch_shapes=scratch,
        compiler_params=pltpu.CompilerParams(dimension_semantics=("arbitrary",) * 2, vmem_limit_bytes=VMEM_LIMIT),
        name="attn_dsa",
    )(od3, od3, od3, od3, od3, small3)


def _outproj_kernel(x_ref, oa_ref, ob_ref, oc_ref, od_ref, w_ref, mod_ref, g2_ref, wr_ref, br_ref,
                    xo_ref, h_ref, rt_ref):
    d_model = x_ref.shape[1]
    merged = jnp.concatenate([oa_ref[...], ob_ref[...], oc_ref[...], od_ref[...]], axis=1)
    mix = jnp.dot(merged, w_ref[...], preferred_element_type=F32)
    xn = x_ref[...] + mod_ref[0, 2:3, :] * mix
    xo_ref[...] = xn
    h = _rms(xn, g2_ref[...]) * (1.0 + mod_ref[0, 4:5, :]) + mod_ref[0, 3:4, :]
    h_ref[:, :d_model] = h

    h_hi = h.astype(BF16)
    h_lo = (h - h_hi.astype(F32)).astype(BF16)
    logits = _nt_dot(wr_ref[0], h_hi) + (_nt_dot(wr_ref[0], h_lo) + _nt_dot(wr_ref[1], h_hi))
    score = 1.0 / (1.0 + jnp.exp(-logits))
    biased = score + br_ref[...]
    srow = [score[e:e + 1, :] for e in range(N_EXPERTS)]
    brow = [biased[e:e + 1, :] for e in range(N_EXPERTS)]
    per = N_EXPERTS // N_GROUPS
    best_v = best_g = None
    for g in range(N_GROUPS):
        r = brow[g * per:(g + 1) * per]
        top2 = None
        for a in range(per):
            for c in range(a + 1, per):
                top2 = r[a] + r[c] if top2 is None else jnp.maximum(top2, r[a] + r[c])
        if g == 0:
            best_v, best_g = top2, jnp.zeros_like(top2, dtype=I32)
        else:
            up = top2 > best_v
            best_v = jnp.where(up, top2, best_v)
            best_g = jnp.where(up, g, best_g)
    cand = [jnp.where(best_g == e // per, brow[e], -jnp.inf) for e in range(N_EXPERTS)]

    def first_max(vals):
        v, idx = vals[0], jnp.zeros_like(best_g)
        for e in range(1, N_EXPERTS):
            up = vals[e] > v
            v = jnp.where(up, vals[e], v)
            idx = jnp.where(up, e, idx)
        return idx

    i1 = first_max(cand)
    i2 = first_max([jnp.where(i1 == e, -jnp.inf, cand[e]) for e in range(N_EXPERTS)])
    s1 = sum(jnp.where(i1 == e, srow[e], 0.0) for e in range(N_EXPERTS))
    s2 = sum(jnp.where(i2 == e, srow[e], 0.0) for e in range(N_EXPERTS))
    den = s1 + s2
    first_low = i1 < i2
    la = jnp.minimum(i1, i2) - per * best_g
    lb = jnp.maximum(i1, i2) - per * best_g
    pair = jnp.where(la == 0, lb - 1, jnp.where(la == 1, lb + 1, 5))
    route = jnp.concatenate([(best_g * N_PAIRS + pair).astype(F32), jnp.where(first_low, s1, s2) / den,
                             jnp.where(first_low, s2, s1) / den, jnp.zeros((SUBLANES - 3, den.shape[1]), F32)], axis=0)
    rt_ref[...] = route
    pad = jnp.zeros((LANES - SUBLANES, den.shape[1]), F32)
    h_ref[:, d_model:] = jnp.concatenate([route, pad], axis=0).T


def _outproj_call(x2, outs, w_out, mod_l, g2, wr_t, br, seq, tm):
    t, d = x2.shape
    tpb = seq // tm
    row = lambda n: pl.BlockSpec((tm, n), lambda i: (i, 0))
    const = lambda shape: pl.BlockSpec(shape, lambda i: (0,) * len(shape))
    return pl.pallas_call(
        _outproj_kernel,
        grid=(t // tm,),
        in_specs=[row(d)] + [row(2 * LANES)] * 4 + [
            const(w_out.shape), pl.BlockSpec((1, 6, d), lambda i: (i // tpb, 0, 0)), const((1, d)),
            const(wr_t.shape), const(br.shape)],
        out_specs=[row(d), row(d + LANES), pl.BlockSpec((SUBLANES, tm), lambda i: (0, i))],
        out_shape=[jax.ShapeDtypeStruct((t, d), F32), jax.ShapeDtypeStruct((t, d + LANES), F32),
                   jax.ShapeDtypeStruct((SUBLANES, t), F32)],
        compiler_params=pltpu.CompilerParams(dimension_semantics=("arbitrary",), vmem_limit_bytes=VMEM_LIMIT),
        name="outproj_router",
    )(x2, *outs, w_out, mod_l, g2, wr_t, br)


def _routing_tables(cls, n_steps):
    t = cls.shape[0]
    onehot = (cls[:, None] == jnp.arange(N_CLASSES, dtype=I32)[None, :]).astype(I32)
    upto = jnp.cumsum(onehot, axis=0)
    rank = jnp.sum((upto - onehot) * onehot, axis=1)
    padded = (upto[-1] + MOE_ROWS - 1) // MOE_ROWS * MOE_ROWS
    ends = jnp.cumsum(padded)
    pos = jnp.sum(onehot * (ends - padded)[None, :], axis=1) + rank
    step_cls = jnp.sum((jnp.arange(n_steps, dtype=I32) * MOE_ROWS)[:, None] >= ends[None, :], axis=1)
    valid = (step_cls < N_CLASSES).astype(I32)
    step_cls = jnp.minimum(step_cls, N_CLASSES - 1)
    base = (step_cls // N_PAIRS) * (N_EXPERTS // N_GROUPS)
    e_low = base + jnp.asarray(PAIR_LOW, I32)[step_cls % N_PAIRS]
    e_high = base + jnp.asarray(PAIR_HIGH, I32)[step_cls % N_PAIRS]
    return pos.reshape(t // PERM_ROWS, 1, PERM_ROWS), e_low, e_high, valid


def _row_copy(src_ref, src_row, dst_ref, dst_row, sem):
    return pltpu.make_async_copy(src_ref.at[pl.ds(src_row, 1)], dst_ref.at[pl.ds(dst_row, 1)], sem)


def _issue_rows(copy_row):
    def issue(r8, carry):
        for u in range(ROW_UNROLL):
            copy_row(r8 * ROW_UNROLL + u).start()
        return carry

    lax.fori_loop(0, PERM_ROWS // ROW_UNROLL, issue, 0)


def _rows_done(src_ref, dst_ref, sem):
    pltpu.make_async_copy(src_ref.at[pl.ds(0, PERM_ROWS)], dst_ref.at[pl.ds(0, PERM_ROWS)], sem).wait()


def _scatter_rows_kernel(pos_ref, src_ref, init_ref, dst_ref, sem):
    del init_ref
    _issue_rows(lambda r: _row_copy(src_ref, r, dst_ref, pos_ref[0, 0, r], sem))
    _rows_done(src_ref, dst_ref, sem)


def _scatter_rows_call(pos3, src, n_sorted):
    t, d = src.shape
    return pl.pallas_call(
        _scatter_rows_kernel,
        grid=(t // PERM_ROWS,),
        in_specs=[pl.BlockSpec((1, 1, PERM_ROWS), lambda j: (j, 0, 0), memory_space=pltpu.SMEM),
                  pl.BlockSpec((PERM_ROWS, d), lambda j: (j, 0)), pl.BlockSpec(memory_space=pl.ANY)],
        out_specs=pl.BlockSpec(memory_space=pl.ANY),
        out_shape=jax.ShapeDtypeStruct((n_sorted, d), src.dtype),
        scratch_shapes=[pltpu.SemaphoreType.DMA(())],
        input_output_aliases={2: 0},
        compiler_params=pltpu.CompilerParams(dimension_semantics=("arbitrary",), has_side_effects=True),
        name="moe_sort_rows",
    )(pos3, src, jnp.zeros((n_sorted, d), src.dtype))


def _moe_kernel(e_low_ref, e_high_ref, valid_ref, h_ref, wgu_low_ref, wgu_high_ref, wd_low_ref, wd_high_ref, o_ref):
    del e_low_ref, e_high_ref
    step = pl.program_id(0)
    d_model = o_ref.shape[1]

    @pl.when(valid_ref[step] == 0)
    def _():
        o_ref[...] = jnp.zeros(o_ref.shape, F32)

    @pl.when(valid_ref[step] != 0)
    def _():
        h = h_ref[:, :d_model].astype(BF16)
        w_low = h_ref[:, d_model + 1:d_model + 2]
        w_high = h_ref[:, d_model + 2:d_model + 3]

        def expert(wgu_ref, wd_ref):
            gu = jnp.dot(h, wgu_ref[0], preferred_element_type=F32)
            gate, up = gu[:, :D_EXPERT], gu[:, D_EXPERT:]
            hid = (gate / (1.0 + jnp.exp(-gate)) * up).astype(BF16)
            return jnp.dot(hid, wd_ref[0], preferred_element_type=F32)

        o_ref[...] = w_low * expert(wgu_low_ref, wd_low_ref) + w_high * expert(wgu_high_ref, wd_high_ref)


def _moe_call(hs, e_low, e_high, valid, wgu, wd):
    n_sorted = hs.shape[0]
    d = wd.shape[2]
    grid_spec = pltpu.PrefetchScalarGridSpec(
        num_scalar_prefetch=3,
        grid=(n_sorted // MOE_ROWS,),
        in_specs=[pl.BlockSpec((MOE_ROWS, hs.shape[1]), lambda j, lo, hi, ok: (j, 0)),
                  pl.BlockSpec((1, d, 2 * D_EXPERT), lambda j, lo, hi, ok: (lo[j], 0, 0)),
                  pl.BlockSpec((1, d, 2 * D_EXPERT), lambda j, lo, hi, ok: (hi[j], 0, 0)),
                  pl.BlockSpec((1, D_EXPERT, d), lambda j, lo, hi, ok: (lo[j], 0, 0)),
                  pl.BlockSpec((1, D_EXPERT, d), lambda j, lo, hi, ok: (hi[j], 0, 0))],
        out_specs=pl.BlockSpec((MOE_ROWS, d), lambda j, lo, hi, ok: (j, 0)))
    return pl.pallas_call(
        _moe_kernel,
        grid_spec=grid_spec,
        out_shape=jax.ShapeDtypeStruct((n_sorted, d), F32),
        compiler_params=pltpu.CompilerParams(dimension_semantics=("arbitrary",), vmem_limit_bytes=VMEM_LIMIT),
        name="moe",
    )(e_low, e_high, valid, hs, wgu, wgu, wd, wd)


def _gather_residual_kernel(pos_ref, pos_next_ref, ys_ref, x_ref, mod_ref, gf_ref, o_ref, rows_ref, sems, *, final):
    j = pl.program_id(0)
    slot = j % 2

    def request(index_ref, into):
        _issue_rows(lambda r: _row_copy(ys_ref, index_ref[0, 0, r], rows_ref.at[into], r, sems.at[into]))

    @pl.when(j == 0)
    def _():
        request(pos_ref, slot)

    @pl.when(j + 1 < pl.num_programs(0))
    def _():
        request(pos_next_ref, 1 - slot)

    _rows_done(ys_ref, rows_ref.at[slot], sems.at[slot])
    xn = x_ref[...] + mod_ref[0, 5:6, :] * rows_ref[slot]
    o_ref[...] = _rms(xn, gf_ref[...]) if final else xn


def _gather_residual_call(pos3, ys, x2, mod_l, gf, seq, final):
    t, d = x2.shape
    tpb = seq // PERM_ROWS
    last = t // PERM_ROWS - 1
    return pl.pallas_call(
        functools.partial(_gather_residual_kernel, final=final),
        grid=(t // PERM_ROWS,),
        in_specs=[pl.BlockSpec((1, 1, PERM_ROWS), lambda j: (j, 0, 0), memory_space=pltpu.SMEM),
                  pl.BlockSpec((1, 1, PERM_ROWS), lambda j: (jnp.minimum(j + 1, last), 0, 0),
                               memory_space=pltpu.SMEM),
                  pl.BlockSpec(memory_space=pl.ANY),
                  pl.BlockSpec((PERM_ROWS, d), lambda j: (j, 0)),
                  pl.BlockSpec((1, 6, d), lambda j: (j // tpb, 0, 0)),
                  pl.BlockSpec((1, d), lambda j: (0, 0))],
        out_specs=pl.BlockSpec((PERM_ROWS, d), lambda j: (j, 0)),
        out_shape=jax.ShapeDtypeStruct((t, d), F32),
        scratch_shapes=[pltpu.VMEM((2, PERM_ROWS, d), F32), pltpu.SemaphoreType.DMA((2,))],
        compiler_params=pltpu.CompilerParams(dimension_semantics=("arbitrary",), vmem_limit_bytes=VMEM_LIMIT),
        name="moe_unsort_residual",
    )(pos3, pos3, ys, x2, mod_l, gf)


def _swap_half(w):
    half = w.shape[1] // 2
    return jnp.concatenate([-w[:, half:], w[:, :half]], axis=1)


def _layer_weights(w_in, w_q_up, w_kv_up):
    d = w_in.shape[0]
    pts = np.cumsum(IN_SIZES)[:-1].tolist()
    (a_q, a_k, a_v, a_f, b_cq, b_ckv, b_kr, c_q, c_k, c_v,
     d_q, d_k, d_v, d_qi, d_ki, d_wi) = jnp.split(w_in, pts, axis=1)
    qs = HEAD_DIM ** -0.5
    small = jnp.concatenate([
        a_f, d_wi * ((IDX_HEADS * IDX_DIM) ** -0.5),
        jnp.zeros((d, _KR_LANE - _WI_LANE - IDX_HEADS), F32),
        b_kr, _swap_half(b_kr), d_ki], axis=1)
    w_all = jnp.concatenate([
        a_q * qs, a_k, a_v,
        c_q * qs, c_k, c_v,
        d_q * qs, d_qi, d_k, d_v,
        small, b_cq, b_ckv], axis=1).astype(BF16)

    per_q = MLA_NOPE + MLA_ROPE
    wq = []
    for hd in range(N_HEADS):
        blk = w_q_up[:, hd * per_q:(hd + 1) * per_q]
        rot = blk[:, MLA_NOPE:]
        wq += [blk[:, :MLA_NOPE], rot, _swap_half(rot)]
    wq = jnp.concatenate(wq, axis=1).astype(BF16)

    place = np.zeros((LANES, LANES), np.float32)
    place[_KR_LANE + np.arange(MLA_ROPE), MLA_NOPE + np.arange(MLA_ROPE)] = 1.0
    place = jnp.asarray(place)
    kcols, vcols = [], []
    for hd in range(N_HEADS):
        blk = w_kv_up[:, hd * 2 * HEAD_DIM:(hd + 1) * 2 * HEAD_DIM]
        knope = jnp.concatenate([blk[:, :MLA_NOPE], jnp.zeros((MLA_KV_RANK, LANES - MLA_NOPE), F32)], axis=1)
        kcols.append(jnp.concatenate([knope, place], axis=0))
        vcols.append(jnp.concatenate([blk[:, MLA_NOPE:], jnp.zeros((LANES, HEAD_DIM), F32)], axis=0))
    wkv = jnp.concatenate(kcols + vcols, axis=1).astype(BF16)
    return w_all, wq, wkv


def _rope_tables(seq):
    half = MLA_ROPE // 2
    inv = ROPE_THETA ** (-jnp.arange(half, dtype=F32) / half)
    ang = jnp.arange(seq, dtype=F32)[:, None] * inv[None, :]
    cos = jnp.tile(jnp.cos(ang), (1, 2))
    sin = jnp.tile(jnp.sin(ang), (1, 2))
    scale = LOG2E * (MLA_NOPE + MLA_ROPE) ** -0.5
    z = lambda n: jnp.zeros((seq, n), F32)
    tab_q = jnp.concatenate([jnp.full((seq, MLA_NOPE), scale, F32), cos * scale, z(MLA_ROPE)], axis=1)
    tab_qs = jnp.concatenate([z(MLA_NOPE), sin * scale, z(MLA_ROPE)], axis=1)
    tab_k = jnp.concatenate([z(_KR_LANE), cos, z(LANES - _KR_LANE - MLA_ROPE)], axis=1)
    tab_ks = jnp.concatenate([z(_KR_LANE), sin, z(LANES - _KR_LANE - MLA_ROPE)], axis=1)
    return jnp.stack([tab_q, tab_qs, tab_k, tab_ks])


def kernel(x, c, w_ada, b_ada, g_norm1, w_in, b_forget, g_q_mla, w_q_up, g_kv_mla, w_kv_up, sinks, w_out,
           g_norm2, w_router, b_router, w_gate, w_up, w_down, g_final):
    b, s, d = x.shape
    depth = w_in.shape[0]
    t = b * s
    topk = min(TOPK_MAX, s // 4)
    tm = min(1024, s)
    tq = min(512, s)
    tq_swa = min(512, s)
    qb_dsa = 2 * LANES
    n_steps = t // MOE_ROWS + N_CLASSES
    assert s % tm == 0 and s % tq == 0 and s % qb_dsa == 0 and s % PERM_ROWS == 0

    mod = _ada_call(c, w_ada, b_ada).reshape(depth, b, 6, d)
    tabs = _rope_tables(s)
    wr_hi = w_router.T.astype(BF16)
    wr_t = jnp.stack([wr_hi, (w_router.T - wr_hi.astype(F32)).astype(BF16)])
    br = b_router.reshape(N_EXPERTS, 1)
    x2 = x.reshape(t, d)
    for l in range(depth):
        w_all, wq, wkv = _layer_weights(w_in[l], w_q_up[l], w_kv_up[l])
        oa, oc, od, osm, oqb, okb, ovb = _inproj_call(
            x2, mod[l], g_norm1[l].reshape(1, d), w_all, wq, wkv,
            g_q_mla[l].reshape(1, -1), g_kv_mla[l].reshape(1, -1), tabs, s, tm)
        small3 = osm.reshape(b, s, LANES)
        bf = jnp.zeros((1, LANES), F32).at[0, :N_HEADS].set(b_forget[l])
        fcum = _fox_cumsum_call(small3, bf)
        frow = jnp.swapaxes(fcum[:, :, :8], 1, 2)
        out_a = _causal_attn_call((oa.reshape(b, s, -1), fcum, frow), True, b, s, tq)
        out_b = _causal_attn_call((oqb.reshape(b, s, -1), okb.reshape(b, s, -1), ovb.reshape(b, s, -1)),
                                  False, b, s, tq)
        out_c = _swa_call(oc.reshape(b, s, -1), sinks[l], b, s, tq_swa)
        out_d = _dsa_call(od.reshape(b, s, -1), small3, b, s, topk, qb_dsa)
        outs = [o.reshape(t, 2 * LANES) for o in (out_a, out_b, out_c, out_d)]
        x2, h2, route = _outproj_call(x2, outs, w_out[l].astype(BF16), mod[l], g_norm2[l].reshape(1, d),
                                      wr_t, br, s, tm)
        pos3, e_low, e_high, valid = _routing_tables(route[0].astype(I32), n_steps)
        wgu = jnp.concatenate([w_gate[l], w_up[l]], axis=2).astype(BF16)
        hs = _scatter_rows_call(pos3, h2, n_steps * MOE_ROWS)
        ys = _moe_call(hs, e_low, e_high, valid, wgu, w_down[l].astype(BF16))
        x2 = _gather_residual_call(pos3, ys, x2, mod[l], g_final.reshape(1, d), s, l == depth - 1)
    return x2.reshape(b, s, d)
```

```python
import functools

import jax
import jax.numpy as jnp
import numpy as np
from jax import lax
from jax.experimental import pallas as pl
from jax.experimental.pallas import tpu as pltpu

F32 = jnp.float32
BF16 = jnp.bfloat16
I32 = jnp.int32
I16 = jnp.int16
I16_MIN, I16_MAX = -(2 ** 15), 2 ** 15 - 1

EPS = 1e-6
HEAD_DIM = 64
LANES = 128
SUBLANES = 8
N_HEADS = 4
MLA_Q_RANK = 256
MLA_KV_RANK = 128
MLA_NOPE = 64
MLA_ROPE = 32
ROPE_THETA = 10000.0
WINDOW = 128
IDX_HEADS = 8
IDX_DIM = 32
TOPK_MAX = 256
N_EXPERTS = 16
N_GROUPS = 4
N_PAIRS = 6
N_CLASSES = N_GROUPS * N_PAIRS
PAIR_LOW = (0, 0, 0, 1, 1, 2)
PAIR_HIGH = (1, 2, 3, 2, 3, 3)
D_EXPERT = 256
MOE_ROWS = 512
PERM_ROWS = 1024
ROW_UNROLL = 8
IN_SIZES = (256, 256, 256, 4, 256, 128, 32, 256, 128, 128, 256, 64, 64, 256, 32, 8)

LOG2E = 1.4426950408889634
NEG = -1e30
INT_MIN = -(2 ** 31)
KEY_NEG_INF = INT_MIN + 0x7FFFFF
VMEM_LIMIT = 56 * 1024 * 1024

_CA, _CC, _CD, _CS, _CQ, _CKV, _CEND = 0, 768, 1280, 1920, 2048, 2304, 2432
_WI_LANE, _KR_LANE = 4, 32


def _nt_dot(a, b):
    return lax.dot_general(a, b, (((1,), (1,)), ((), ())), preferred_element_type=F32)


def _rms(x, g):
    return x * lax.rsqrt(jnp.mean(x * x, axis=-1, keepdims=True) + EPS) * g


def _ada_kernel(c_ref, w_ref, b_ref, o_ref):
    c = c_ref[...]
    act = (c / (1.0 + jnp.exp(-c))).astype(BF16)
    o_ref[0] = jnp.dot(act, w_ref[0].astype(BF16), preferred_element_type=F32) + b_ref[0]


def _ada_call(c, w_ada, b_ada):
    depth, d, n = w_ada.shape
    bsz = c.shape[0]
    tn = 1024
    return pl.pallas_call(
        _ada_kernel,
        grid=(depth, n // tn),
        in_specs=[pl.BlockSpec((bsz, d), lambda l, j: (0, 0)),
                  pl.BlockSpec((1, d, tn), lambda l, j: (l, 0, j)),
                  pl.BlockSpec((1, 1, tn), lambda l, j: (l, 0, j))],
        out_specs=pl.BlockSpec((1, bsz, tn), lambda l, j: (l, 0, j)),
        out_shape=jax.ShapeDtypeStruct((depth, bsz, n), F32),
        compiler_params=pltpu.CompilerParams(dimension_semantics=("arbitrary", "arbitrary"),
                                             vmem_limit_bytes=VMEM_LIMIT),
        name="adaln",
    )(c, w_ada, b_ada.reshape(depth, 1, n))


def _inproj_kernel(x_ref, mod_ref, g1_ref, w_ref, wq_ref, wkv_ref, gq_ref, gkv_ref, tab_ref,
                   oa_ref, oc_ref, od_ref, os_ref, oqb_ref, okb_ref, ovb_ref):
    h = _rms(x_ref[...], g1_ref[...]) * (1.0 + mod_ref[0, 1:2, :]) + mod_ref[0, 0:1, :]
    h = h.astype(BF16)

    def proj(lo, hi):
        return jnp.dot(h, w_ref[:, lo:hi], preferred_element_type=F32)

    q_width = N_HEADS * HEAD_DIM
    low = _lane_is_low()

    def both_halves(pair):
        swapped = pltpu.roll(pair, HEAD_DIM, axis=1)
        return jnp.where(low, pair, swapped), jnp.where(low, swapped, pair)

    pa = proj(_CA, _CC)
    oa_ref[:, :q_width] = (pa[:, :q_width] * LOG2E).astype(BF16)
    oa_ref[:, q_width:] = pa[:, q_width:].astype(BF16)

    pc = proj(_CC, _CD)
    oc_ref[:, :q_width] = (pc[:, :q_width] * LOG2E).astype(BF16)
    for n in range(2):
        tiles = both_halves(pc[:, q_width + n * LANES:q_width + (n + 1) * LANES])
        for g in range(2):
            lo_col = q_width + (2 * n + g) * LANES
            oc_ref[:, lo_col:lo_col + LANES] = tiles[g].astype(BF16)

    pd = proj(_CD, _CS)
    small = proj(_CS, _CQ)
    os_ref[...] = small
    od_ref[:, :q_width] = (pd[:, :q_width] * LOG2E).astype(BF16)
    k_twice, v_twice = both_halves(pd[:, 2 * q_width:])
    od_ref[:, q_width:q_width + LANES] = k_twice.astype(BF16)
    od_ref[:, q_width + LANES:2 * q_width] = v_twice.astype(BF16)
    od_ref[:, 2 * q_width:3 * q_width] = pd[:, q_width:2 * q_width].astype(BF16)
    lane = lax.broadcasted_iota(I32, (1, LANES), 1)
    ki4 = small
    for n in range(1, LANES // IDX_DIM):
        ki4 = jnp.where(lane < LANES - n * IDX_DIM, pltpu.roll(small, LANES - n * IDX_DIM, axis=1), ki4)
    od_ref[:, 3 * q_width:] = ki4.astype(BF16)

    cq = _rms(proj(_CQ, _CKV), gq_ref[...]).astype(BF16)
    qf = jnp.dot(cq, wq_ref[...], preferred_element_type=F32)
    tab_q, tab_qs, tab_k, tab_ks = tab_ref[0], tab_ref[1], tab_ref[2], tab_ref[3]
    for hd in range(N_HEADS):
        qg = qf[:, hd * LANES:(hd + 1) * LANES]
        qr = qg * tab_q + pltpu.roll(qg, LANES - MLA_ROPE, axis=1) * tab_qs
        oqb_ref[:, hd * LANES:(hd + 1) * LANES] = qr.astype(BF16)

    ckv = _rms(proj(_CKV, _CEND), gkv_ref[...]).astype(BF16)
    kr = (small * tab_k + pltpu.roll(small, LANES - MLA_ROPE, axis=1) * tab_ks).astype(BF16)
    kvf = jnp.dot(jnp.concatenate([ckv, kr], axis=1), wkv_ref[...], preferred_element_type=F32)
    okb_ref[...] = kvf[:, :4 * LANES].astype(BF16)
    ovb_ref[...] = kvf[:, 4 * LANES:].astype(BF16)


def _inproj_call(x2, mod_l, g1, w_all, wq, wkv, gq, gkv, tabs, seq, tm):
    t, d = x2.shape
    tpb = seq // tm
    row = lambda n: pl.BlockSpec((tm, n), lambda i: (i, 0))
    const = lambda shape: pl.BlockSpec(shape, lambda i: (0,) * len(shape))
    widths = (768, 768, 896, 128, 512, 512, 256)
    dtypes = (BF16, BF16, BF16, F32, BF16, BF16, BF16)
    return pl.pallas_call(
        _inproj_kernel,
        grid=(t // tm,),
        in_specs=[row(d),
                  pl.BlockSpec((1, 6, d), lambda i: (i // tpb, 0, 0)),
                  const((1, d)), const(w_all.shape), const(wq.shape), const(wkv.shape),
                  const((1, MLA_Q_RANK)), const((1, MLA_KV_RANK)),
                  pl.BlockSpec((4, tm, LANES), lambda i: (0, i % tpb, 0))],
        out_specs=[row(n) for n in widths],
        out_shape=[jax.ShapeDtypeStruct((t, n), dt) for n, dt in zip(widths, dtypes)],
        compiler_params=pltpu.CompilerParams(dimension_semantics=("arbitrary",), vmem_limit_bytes=VMEM_LIMIT),
        name="inproj",
    )(x2, mod_l, g1, w_all, wq, wkv, gq, gkv, tabs)


def _fox_cumsum_kernel(s_ref, b_ref, o_ref):
    z = s_ref[0] + b_ref[...]
    lf = jnp.minimum(z, 0.0) - jnp.log(1.0 + jnp.exp(-jnp.abs(z)))
    n = lf.shape[0]
    row = lax.broadcasted_iota(I32, lf.shape, 0)
    d = 1
    while d < n:
        lf = lf + jnp.where(row >= d, pltpu.roll(lf, d, axis=0), 0.0)
        d *= 2
    o_ref[0] = lf * LOG2E


def _fox_cumsum_call(small3, bf):
    b, s, _ = small3.shape
    return pl.pallas_call(
        _fox_cumsum_kernel,
        grid=(b,),
        in_specs=[pl.BlockSpec((1, s, LANES), lambda i: (i, 0, 0)),
                  pl.BlockSpec((1, LANES), lambda i: (0, 0))],
        out_specs=pl.BlockSpec((1, s, LANES), lambda i: (i, 0, 0)),
        out_shape=jax.ShapeDtypeStruct((b, s, LANES), F32),
        compiler_params=pltpu.CompilerParams(dimension_semantics=("arbitrary",), vmem_limit_bytes=VMEM_LIMIT),
        name="fox_cumsum",
    )(small3, bf)


def _lane_is_low():
    return lax.broadcasted_iota(I32, (1, LANES), 1) < HEAD_DIM


def _split_pair(q):
    low = _lane_is_low()
    zero = jnp.zeros_like(q)
    return jnp.where(low, q, zero), jnp.where(low, zero, q)


def _values_with_ones(v, j):
    low = _lane_is_low()
    return jnp.where(low if j == 0 else ~low, v, jnp.ones_like(v))


def _pair_output(a0, a1, extra0=None, extra1=None):
    l0 = pltpu.roll(a0, HEAD_DIM, axis=1)
    l1 = pltpu.roll(a1, HEAD_DIM, axis=1)
    if extra0 is not None:
        l0, l1 = l0 + extra0, l1 + extra1
    return jnp.where(_lane_is_low(), a0 / l0, a1 / l1)


def _two_pass_attention(n_before, logits, values, s_ref, mx_ref, acc_ref, n_heads, shared=None, diagonal_mask=True):
    tq, tk = s_ref.shape[-2:]
    mx_ref[...] = jnp.full(mx_ref.shape, NEG, F32)

    def store(c, n, diagonal):
        ctx = shared(c, n) if shared is not None else None
        for j in range(n_heads):
            s = logits(c, n, j, diagonal, ctx)
            for t in range(n):
                s_ref[j, c + t] = s[:, t * tk:(t + 1) * tk]
            parts = [s[:, blk * LANES:(blk + 1) * LANES] for blk in range(n * tk // LANES)]
            while len(parts) > 1:
                parts = [jnp.maximum(a, b) for a, b in zip(parts[::2], parts[1::2])]
            mx_ref[j] = jnp.maximum(mx_ref[j], parts[0])

    def store_two(c2, carry):
        store(2 * c2, 2, False)
        return carry

    if diagonal_mask:
        lax.fori_loop(0, n_before // 2, store_two, 0)

        @pl.when(n_before % 2 == 1)
        def _():
            store(n_before - 1, 1, False)

        store(n_before, 1, True)
    else:
        lax.fori_loop(0, (n_before + 1) // 2, store_two, 0)

        @pl.when(n_before % 2 == 0)
        def _():
            store(n_before, 1, False)

    row_max = [jnp.max(mx_ref[j], axis=1, keepdims=True) for j in range(n_heads)]
    shift = [jnp.broadcast_to(m, (tq, tk)) for m in row_max]
    acc_ref[...] = jnp.zeros(acc_ref.shape, F32)

    def accum(c, n):
        for j in range(n_heads):
            p = [jnp.exp2(s_ref[j, c + t] - shift[j]).astype(BF16) for t in range(n)]
            p = p[0] if n == 1 else jnp.concatenate(p, axis=1)
            acc_ref[j] += jnp.dot(p, values(c, n, j), preferred_element_type=F32)

    def accum_two(c2, carry):
        accum(2 * c2, 2)
        return carry

    lax.fori_loop(0, (n_before + 1) // 2, accum_two, 0)

    @pl.when(n_before % 2 == 0)
    def _():
        accum(n_before, 1)

    return row_max


def _causal_attn_kernel(*refs, fox, tq):
    if fox:
        q_ref, k_ref, v_ref, fc_ref, fr_ref, o_ref, s_ref, mx_ref, acc_ref = refs
    else:
        q_ref, k_ref, v_ref, o_ref, s_ref, mx_ref, acc_ref = refs
    tk = tq
    hp = pl.program_id(1)
    i = pl.program_id(2)
    if fox:
        qs = _split_pair(q_ref[0])
        lane = lax.broadcasted_iota(I32, (1, LANES), 1)
        fcol = [jnp.sum(jnp.where(lane == 2 * hp + j, fc_ref[0], 0.0), axis=1, keepdims=True) for j in range(2)]
    else:
        qs = (q_ref[0, :, :LANES], q_ref[0, :, LANES:])
    causal = (lax.broadcasted_iota(I32, (tq, tk), 1) <= lax.broadcasted_iota(I32, (tq, tk), 0))

    def logits(c, n, j, diagonal, _):
        start = pl.multiple_of(c * tk, tk)
        if fox:
            s = _nt_dot(qs[j], k_ref[0, pl.ds(start, n * tk), :])
            s = (s + fcol[j]) - fr_ref[0, pl.ds(2 * hp + j, 1), pl.ds(start, n * tk)]
        else:
            s = _nt_dot(qs[j], k_ref[0, pl.ds(start, n * tk), j * LANES:(j + 1) * LANES])
        return jnp.where(causal, s, NEG) if diagonal else s

    def values(c, n, j):
        return _values_with_ones(v_ref[0, pl.ds(pl.multiple_of(c * tk, tk), n * tk), :], j)

    _two_pass_attention(i, logits, values, s_ref, mx_ref, acc_ref, 2)
    o_ref[0] = _pair_output(acc_ref[0], acc_ref[1]).astype(o_ref.dtype)


def _causal_attn_call(arrs, fox, b, s, tq):
    scratch = [pltpu.VMEM((2, s // tq, tq, tq), F32), pltpu.VMEM((2, tq, LANES), F32),
               pltpu.VMEM((2, tq, LANES), F32)]
    if fox:
        qkv, fcol, frow = arrs
        operands = (qkv, qkv, qkv, fcol, frow)
        in_specs = [pl.BlockSpec((1, tq, LANES), lambda bi, hp, i: (bi, i, hp)),
                    pl.BlockSpec((1, s, LANES), lambda bi, hp, i: (bi, 0, 2 + hp)),
                    pl.BlockSpec((1, s, LANES), lambda bi, hp, i: (bi, 0, 4 + hp)),
                    pl.BlockSpec((1, tq, LANES), lambda bi, hp, i: (bi, i, 0)),
                    pl.BlockSpec((1, 8, s), lambda bi, hp, i: (bi, 0, 0))]
    else:
        operands = arrs
        in_specs = [pl.BlockSpec((1, tq, 2 * LANES), lambda bi, hp, i: (bi, i, hp)),
                    pl.BlockSpec((1, s, 2 * LANES), lambda bi, hp, i: (bi, 0, hp)),
                    pl.BlockSpec((1, s, LANES), lambda bi, hp, i: (bi, 0, hp))]
    return pl.pallas_call(
        functools.partial(_causal_attn_kernel, fox=fox, tq=tq),
        grid=(b, 2, s // tq),
        in_specs=in_specs,
        out_specs=pl.BlockSpec((1, tq, LANES), lambda bi, hp, i: (bi, i, hp)),
        out_shape=jax.ShapeDtypeStruct((b, s, 2 * LANES), BF16),
        scratch_shapes=scratch,
        compiler_params=pltpu.CompilerParams(dimension_semantics=("arbitrary",) * 3, vmem_limit_bytes=VMEM_LIMIT),
        name="attn_fox" if fox else "attn_mla",
    )(*operands)


def _alibi_slope(head):
    return lax.shift_left(jnp.int32(1), 7 - head).astype(F32) * (LOG2E * 2.0 ** -8)


def _swa_kernel(sink_ref, q_ref, k_ref, v_ref, o_ref, *, tq):
    hp = pl.program_id(1)
    i = pl.program_id(2)
    band = 2 * WINDOW
    rel = lax.broadcasted_iota(I32, (WINDOW, band), 0) - lax.broadcasted_iota(I32, (WINDOW, band), 1)
    for r in range(tq // WINDOW):
        q_start = i * tq + r * WINDOW
        k_start = pl.multiple_of(jnp.maximum(q_start - WINDOW, 0), WINDOW)
        kb = k_ref[0, pl.ds(k_start, band), :]
        vb = v_ref[0, pl.ds(k_start, band), :]
        dist = rel + (q_start - k_start)
        valid = (dist >= 0) & (dist < WINDOW)
        distf = dist.astype(F32)
        qs = _split_pair(q_ref[0, r * WINDOW:(r + 1) * WINDOW, :])
        acc, sink_term = [], []
        for j in range(2):
            sink = sink_ref[2 * hp + j] * LOG2E
            s = jnp.where(valid, _nt_dot(qs[j], kb) - _alibi_slope(2 * hp + j) * distf, NEG)
            m = jnp.maximum(jnp.max(s, axis=1, keepdims=True), sink)
            p = jnp.exp2(s - m).astype(BF16)
            acc.append(jnp.dot(p, _values_with_ones(vb, j), preferred_element_type=F32))
            sink_term.append(jnp.exp2(sink - m))
        o_ref[0, r * WINDOW:(r + 1) * WINDOW, :] = _pair_output(acc[0], acc[1], *sink_term).astype(o_ref.dtype)


def _swa_call(qkv, sinks, b, s, tq):
    return pl.pallas_call(
        functools.partial(_swa_kernel, tq=tq),
        grid=(b, 2, s // tq),
        in_specs=[pl.BlockSpec(memory_space=pltpu.SMEM),
                  pl.BlockSpec((1, tq, LANES), lambda bi, hp, i: (bi, i, hp)),
                  pl.BlockSpec((1, s, LANES), lambda bi, hp, i: (bi, 0, 2 + hp)),
                  pl.BlockSpec((1, s, LANES), lambda bi, hp, i: (bi, 0, 4 + hp))],
        out_specs=pl.BlockSpec((1, tq, LANES), lambda bi, hp, i: (bi, i, hp)),
        out_shape=jax.ShapeDtypeStruct((b, s, 2 * LANES), BF16),
        compiler_params=pltpu.CompilerParams(dimension_semantics=("arbitrary",) * 3, vmem_limit_bytes=VMEM_LIMIT),
        name="attn_swa",
    )(sinks, qkv, qkv, qkv)


def _dsa_kernel(q_ref, k_ref, v_ref, qi_ref, ki_ref, wi_ref, o_ref, keys_ref, hi_ref, lo_ref, thr_ref,
                s_ref, mx_ref, acc_ref, *, seq, topk, qb):
    kc = LANES
    kw = 2 * kc
    i = pl.program_id(1)
    nwide = (i + 1) * (qb // kw)
    lane = lax.broadcasted_iota(I32, (1, LANES), 1)
    key_row = lax.broadcasted_iota(I32, (kc, qb), 0)
    query_pos = i * qb + lax.broadcasted_iota(I32, (1, qb), 1)

    qi = qi_ref[0]
    wi_t = wi_ref[0].T
    qms, wrows = [], []
    for hd in range(IDX_HEADS):
        g, r = divmod(hd, LANES // IDX_DIM)
        sel = (lane >= r * IDX_DIM) & (lane < (r + 1) * IDX_DIM)
        qg = qi[:, g * LANES:(g + 1) * LANES]
        qms.append(jnp.where(sel, qg, jnp.zeros_like(qg)))
        wrows.append(wi_t[_WI_LANE + hd:_WI_LANE + hd + 1, :])
    q_all = jnp.concatenate(qms, axis=0)
    key_minus_query = (lax.broadcasted_iota(I32, (kw, qb), 0) - lax.broadcasted_iota(I32, (kw, qb), 1))

    def score_chunk(c, carry):
        start = pl.multiple_of(c * kw, kw)
        logit = _nt_dot(ki_ref[0, pl.ds(start, kw), :], q_all)
        sc = wrows[0] * jnp.maximum(logit[:, :qb], 0.0)
        for hd in range(1, IDX_HEADS):
            sc = sc + wrows[hd] * jnp.maximum(logit[:, hd * qb:(hd + 1) * qb], 0.0)
        sc = jnp.where(key_minus_query <= i * qb - start, sc, -jnp.inf)
        sc = jnp.where(sc == 0.0, 0.0, sc)
        bits = pltpu.bitcast(sc, I32)
        keys = bits ^ ((bits >> 31) & 0x7FFFFFFF)
        keys_ref[2 * c] = keys[:kc]
        keys_ref[2 * c + 1] = keys[kc:]
        hi = (keys >> 16).astype(I16)
        lo = ((keys & 0xFFFF) + I16_MIN).astype(I16)
        hi_ref[2 * c] = hi[:kc]
        hi_ref[2 * c + 1] = hi[kc:]
        lo_ref[2 * c] = lo[:kc]
        lo_ref[2 * c + 1] = lo[kc:]
        return carry

    lax.fori_loop(0, nwide, score_chunk, 0)

    keep_all = jnp.full((1, qb), KEY_NEG_INF + 1, I32)
    thr_ref[...] = jnp.broadcast_to(keep_all, thr_ref.shape)

    def count(pred):
        def body(c2, acc):
            parts = []
            for c in (2 * c2, 2 * c2 + 1):
                hit = jnp.where(pred(keys_ref[c], c), 1.0, 0.0)
                parts += [hit[r * SUBLANES:(r + 1) * SUBLANES] for r in range(kc // SUBLANES)]
            while len(parts) > 1:
                parts = [a + b for a, b in zip(parts[::2], parts[1::2])]
            return acc + parts[0]
        acc = lax.fori_loop(0, nwide, body, jnp.zeros((SUBLANES, qb), F32))
        return jnp.sum(acc, axis=0, keepdims=True)

    @pl.when((i + 1) * qb > topk)
    def _():
        kf = float(topk)

        pack = 2 * SUBLANES

        def count16(plane_ref, cand):
            cand16 = cand.astype(I16)

            def body(c2, acc):
                parts = []
                for c in (2 * c2, 2 * c2 + 1):
                    hit = jnp.where(plane_ref[c] >= cand16, jnp.int16(1), jnp.int16(0))
                    parts += [hit[r * pack:(r + 1) * pack] for r in range(kc // pack)]
                while len(parts) > 1:
                    parts = [a + b for a, b in zip(parts[::2], parts[1::2])]
                return acc + parts[0]

            acc = lax.fori_loop(0, nwide, body, jnp.zeros((pack, qb), I16))
            return jnp.sum(acc.astype(F32), axis=0, keepdims=True)

        def search16(plane_ref, cnt_at_floor):
            def bit_step(it, carry):
                val, cnt_val = carry
                cand = val + lax.shift_left(jnp.int32(1), 15 - it)
                cnt = count16(plane_ref, cand)
                ok = cnt >= kf
                return jnp.where(ok, cand, val), jnp.where(ok, cnt, cnt_val)

            return lax.fori_loop(0, 16, bit_step, (jnp.full((1, qb), I16_MIN, I32), cnt_at_floor))

        thr_hi, cnt_hi = search16(hi_ref, jnp.full((1, qb), float(seq), F32))
        thr_hi16 = thr_hi.astype(I16)

        def pin(c, carry):
            h = hi_ref[c]
            lo_ref[c] = jnp.where(h > thr_hi16, jnp.int16(I16_MAX),
                                  jnp.where(h == thr_hi16, lo_ref[c], jnp.int16(I16_MIN)))
            return carry

        lax.fori_loop(0, 2 * nwide, pin, 0)
        thr_lo, cnt_thr = search16(lo_ref, cnt_hi)
        thr = thr_hi * 65536 + (thr_lo - I16_MIN)
        thr_ref[...] = jnp.broadcast_to(jnp.where(query_pos < topk, keep_all, thr), thr_ref.shape)

        @pl.when(jnp.max(cnt_thr) > kf)
        def _():
            need = kf - count(lambda kk, c: kk > thr)

            def idx_step(it, pos):
                cand = pos + lax.shift_left(jnp.int32(1), (seq.bit_length() - 2) - it)
                cnt = count(lambda kk, c: (kk == thr) & (key_row + c * kc < cand))
                return jnp.where(cnt < need, cand, pos)

            pos = lax.fori_loop(0, seq.bit_length() - 1, idx_step, jnp.zeros((1, qb), I32))

            def demote(c, carry):
                kk = keys_ref[c]
                keys_ref[c] = jnp.where((kk == thr) & (key_row + c * kc > pos), kk - 1, kk)
                return carry

            lax.fori_loop(0, 2 * nwide, demote, 0)

    q = q_ref[0]
    qs = _split_pair(q[:, :LANES]) + _split_pair(q[:, LANES:])
    thr_row = thr_ref[0:1, :]
    key_col = lax.broadcasted_iota(I32, (1, 2 * kw), 1)

    def shared(c, n):
        halves = [jnp.where(keys_ref[2 * c + h] >= thr_row, 0.0, NEG).T for h in range(2 * n)]
        key_off = (key_col[:, :n * kw] + (c * kw - i * qb)).astype(F32)
        return jnp.concatenate(halves, axis=1), key_off

    def logits(c, n, hd, diagonal, ctx):
        bias, key_off = ctx
        start = pl.multiple_of(c * kw, kw)
        slope = LOG2E * 2.0 ** -(N_HEADS + hd + 1)
        return _nt_dot(qs[hd], k_ref[0, pl.ds(start, n * kw), :]) + (bias + slope * key_off)

    def values(c, n, hd):
        return _values_with_ones(v_ref[0, pl.ds(pl.multiple_of(c * kw, kw), n * kw), :], hd % 2)

    _two_pass_attention(nwide - 1, logits, values, s_ref, mx_ref, acc_ref, N_HEADS, shared, diagonal_mask=False)
    o_ref[0] = jnp.concatenate([_pair_output(acc_ref[0], acc_ref[1]), _pair_output(acc_ref[2], acc_ref[3])],
                               axis=1).astype(o_ref.dtype)


def _dsa_call(od3, small3, b, s, topk, qb):
    kc = LANES
    scratch = [pltpu.VMEM((s // kc, kc, qb), I32), pltpu.VMEM((s // kc, kc, qb), I16),
               pltpu.VMEM((s // kc, kc, qb), I16), pltpu.VMEM((SUBLANES, qb), I32),
               pltpu.VMEM((N_HEADS, s // (2 * kc), qb, 2 * kc), F32),
               pltpu.VMEM((N_HEADS, qb, LANES), F32), pltpu.VMEM((N_HEADS, qb, LANES), F32)]
    return pl.pallas_call(
        functools.partial(_dsa_kernel, seq=s, topk=topk, qb=qb),
        grid=(b, s // qb),
        in_specs=[pl.BlockSpec((1, qb, 2 * LANES), lambda bi, i: (bi, i, 0)),
                  pl.BlockSpec((1, s, LANES), lambda bi, i: (bi, 0, 2)),
                  pl.BlockSpec((1, s, LANES), lambda bi, i: (bi, 0, 3)),
                  pl.BlockSpec((1, qb, 2 * LANES), lambda bi, i: (bi, i, 2)),
                  pl.BlockSpec((1, s, LANES), lambda bi, i: (bi, 0, 6)),
                  pl.BlockSpec((1, qb, LANES), lambda bi, i: (bi, i, 0))],
        out_specs=pl.BlockSpec((1, qb, 2 * LANES), lambda bi, i: (bi, i, 0)),
        out_shape=jax.ShapeDtypeStruct((b, s, 2 * LANES), BF16),
        scratch_shapes=scratch,
        compiler_params=pltpu.CompilerParams(dimension_semantics=("arbitrary",) * 2, vmem_limit_bytes=VMEM_LIMIT),
        name="attn_dsa",
    )(od3, od3, od3, od3, od3, small3)


def _outproj_kernel(x_ref, oa_ref, ob_ref, oc_ref, od_ref, w_ref, mod_ref, g2_ref, wr_ref, br_ref,
                    xo_ref, h_ref, rt_ref):
    d_model = x_ref.shape[1]
    merged = jnp.concatenate([oa_ref[...], ob_ref[...], oc_ref[...], od_ref[...]], axis=1)
    mix = jnp.dot(merged, w_ref[...], preferred_element_type=F32)
    xn = x_ref[...] + mod_ref[0, 2:3, :] * mix
    xo_ref[...] = xn
    h = _rms(xn, g2_ref[...]) * (1.0 + mod_ref[0, 4:5, :]) + mod_ref[0, 3:4, :]
    h_ref[:, :d_model] = h

    h_hi = h.astype(BF16)
    h_lo = (h - h_hi.astype(F32)).astype(BF16)
    logits = _nt_dot(wr_ref[0], h_hi) + (_nt_dot(wr_ref[0], h_lo) + _nt_dot(wr_ref[1], h_hi))
    score = 1.0 / (1.0 + jnp.exp(-logits))
    biased = score + br_ref[...]
    srow = [score[e:e + 1, :] for e in range(N_EXPERTS)]
    brow = [biased[e:e + 1, :] for e in range(N_EXPERTS)]
    per = N_EXPERTS // N_GROUPS
    best_v = best_g = None
    for g in range(N_GROUPS):
        r = brow[g * per:(g + 1) * per]
        top2 = None
        for a in range(per):
            for c in range(a + 1, per):
                top2 = r[a] + r[c] if top2 is None else jnp.maximum(top2, r[a] + r[c])
        if g == 0:
            best_v, best_g = top2, jnp.zeros_like(top2, dtype=I32)
        else:
            up = top2 > best_v
            best_v = jnp.where(up, top2, best_v)
            best_g = jnp.where(up, g, best_g)
    cand = [jnp.where(best_g == e // per, brow[e], -jnp.inf) for e in range(N_EXPERTS)]

    def first_max(vals):
        v, idx = vals[0], jnp.zeros_like(best_g)
        for e in range(1, N_EXPERTS):
            up = vals[e] > v
            v = jnp.where(up, vals[e], v)
            idx = jnp.where(up, e, idx)
        return idx

    i1 = first_max(cand)
    i2 = first_max([jnp.where(i1 == e, -jnp.inf, cand[e]) for e in range(N_EXPERTS)])
    s1 = sum(jnp.where(i1 == e, srow[e], 0.0) for e in range(N_EXPERTS))
    s2 = sum(jnp.where(i2 == e, srow[e], 0.0) for e in range(N_EXPERTS))
    den = s1 + s2
    first_low = i1 < i2
    la = jnp.minimum(i1, i2) - per * best_g
    lb = jnp.maximum(i1, i2) - per * best_g
    pair = jnp.where(la == 0, lb - 1, jnp.where(la == 1, lb + 1, 5))
    route = jnp.concatenate([(best_g * N_PAIRS + pair).astype(F32), jnp.where(first_low, s1, s2) / den,
                             jnp.where(first_low, s2, s1) / den, jnp.zeros((SUBLANES - 3, den.shape[1]), F32)], axis=0)
    rt_ref[...] = route
    pad = jnp.zeros((LANES - SUBLANES, den.shape[1]), F32)
    h_ref[:, d_model:] = jnp.concatenate([route, pad], axis=0).T


def _outproj_call(x2, outs, w_out, mod_l, g2, wr_t, br, seq, tm):
    t, d = x2.shape
    tpb = seq // tm
    row = lambda n: pl.BlockSpec((tm, n), lambda i: (i, 0))
    const = lambda shape: pl.BlockSpec(shape, lambda i: (0,) * len(shape))
    return pl.pallas_call(
        _outproj_kernel,
        grid=(t // tm,),
        in_specs=[row(d)] + [row(2 * LANES)] * 4 + [
            const(w_out.shape), pl.BlockSpec((1, 6, d), lambda i: (i // tpb, 0, 0)), const((1, d)),
            const(wr_t.shape), const(br.shape)],
        out_specs=[row(d), row(d + LANES), pl.BlockSpec((SUBLANES, tm), lambda i: (0, i))],
        out_shape=[jax.ShapeDtypeStruct((t, d), F32), jax.ShapeDtypeStruct((t, d + LANES), F32),
                   jax.ShapeDtypeStruct((SUBLANES, t), F32)],
        compiler_params=pltpu.CompilerParams(dimension_semantics=("arbitrary",), vmem_limit_bytes=VMEM_LIMIT),
        name="outproj_router",
    )(x2, *outs, w_out, mod_l, g2, wr_t, br)


def _routing_tables(cls, n_steps):
    t = cls.shape[0]
    onehot = (cls[:, None] == jnp.arange(N_CLASSES, dtype=I32)[None, :]).astype(I32)
    upto = jnp.cumsum(onehot, axis=0)
    rank = jnp.sum((upto - onehot) * onehot, axis=1)
    padded = (upto[-1] + MOE_ROWS - 1) // MOE_ROWS * MOE_ROWS
    ends = jnp.cumsum(padded)
    pos = jnp.sum(onehot * (ends - padded)[None, :], axis=1) + rank
    step_cls = jnp.sum((jnp.arange(n_steps, dtype=I32) * MOE_ROWS)[:, None] >= ends[None, :], axis=1)
    valid = (step_cls < N_CLASSES).astype(I32)
    step_cls = jnp.minimum(step_cls, N_CLASSES - 1)
    base = (step_cls // N_PAIRS) * (N_EXPERTS // N_GROUPS)
    e_low = base + jnp.asarray(PAIR_LOW, I32)[step_cls % N_PAIRS]
    e_high = base + jnp.asarray(PAIR_HIGH, I32)[step_cls % N_PAIRS]
    return pos.reshape(t // PERM_ROWS, 1, PERM_ROWS), e_low, e_high, valid


def _row_copy(src_ref, src_row, dst_ref, dst_row, sem):
    return pltpu.make_async_copy(src_ref.at[pl.ds(src_row, 1)], dst_ref.at[pl.ds(dst_row, 1)], sem)


def _issue_rows(copy_row):
    def issue(r8, carry):
        for u in range(ROW_UNROLL):
            copy_row(r8 * ROW_UNROLL + u).start()
        return carry

    lax.fori_loop(0, PERM_ROWS // ROW_UNROLL, issue, 0)


def _rows_done(src_ref, dst_ref, sem):
    pltpu.make_async_copy(src_ref.at[pl.ds(0, PERM_ROWS)], dst_ref.at[pl.ds(0, PERM_ROWS)], sem).wait()


def _scatter_rows_kernel(pos_ref, src_ref, init_ref, dst_ref, sem):
    del init_ref
    _issue_rows(lambda r: _row_copy(src_ref, r, dst_ref, pos_ref[0, 0, r], sem))
    _rows_done(src_ref, dst_ref, sem)


def _scatter_rows_call(pos3, src, n_sorted):
    t, d = src.shape
    return pl.pallas_call(
        _scatter_rows_kernel,
        grid=(t // PERM_ROWS,),
        in_specs=[pl.BlockSpec((1, 1, PERM_ROWS), lambda j: (j, 0, 0), memory_space=pltpu.SMEM),
                  pl.BlockSpec((PERM_ROWS, d), lambda j: (j, 0)), pl.BlockSpec(memory_space=pl.ANY)],
        out_specs=pl.BlockSpec(memory_space=pl.ANY),
        out_shape=jax.ShapeDtypeStruct((n_sorted, d), src.dtype),
        scratch_shapes=[pltpu.SemaphoreType.DMA(())],
        input_output_aliases={2: 0},
        compiler_params=pltpu.CompilerParams(dimension_semantics=("arbitrary",), has_side_effects=True),
        name="moe_sort_rows",
    )(pos3, src, jnp.zeros((n_sorted, d), src.dtype))


def _moe_kernel(e_low_ref, e_high_ref, valid_ref, h_ref, wgu_low_ref, wgu_high_ref, wd_low_ref, wd_high_ref, o_ref):
    del e_low_ref, e_high_ref
    step = pl.program_id(0)
    d_model = o_ref.shape[1]

    @pl.when(valid_ref[step] == 0)
    def _():
        o_ref[...] = jnp.zeros(o_ref.shape, F32)

    @pl.when(valid_ref[step] != 0)
    def _():
        h = h_ref[:, :d_model].astype(BF16)
        w_low = h_ref[:, d_model + 1:d_model + 2]
        w_high = h_ref[:, d_model + 2:d_model + 3]

        def expert(wgu_ref, wd_ref):
            gu = jnp.dot(h, wgu_ref[0], preferred_element_type=F32)
            gate, up = gu[:, :D_EXPERT], gu[:, D_EXPERT:]
            hid = (gate / (1.0 + jnp.exp(-gate)) * up).astype(BF16)
            return jnp.dot(hid, wd_ref[0], preferred_element_type=F32)

        o_ref[...] = w_low * expert(wgu_low_ref, wd_low_ref) + w_high * expert(wgu_high_ref, wd_high_ref)


def _moe_call(hs, e_low, e_high, valid, wgu, wd):
    n_sorted = hs.shape[0]
    d = wd.shape[2]
    grid_spec = pltpu.PrefetchScalarGridSpec(
        num_scalar_prefetch=3,
        grid=(n_sorted // MOE_ROWS,),
        in_specs=[pl.BlockSpec((MOE_ROWS, hs.shape[1]), lambda j, lo, hi, ok: (j, 0)),
                  pl.BlockSpec((1, d, 2 * D_EXPERT), lambda j, lo, hi, ok: (lo[j], 0, 0)),
                  pl.BlockSpec((1, d, 2 * D_EXPERT), lambda j, lo, hi, ok: (hi[j], 0, 0)),
                  pl.BlockSpec((1, D_EXPERT, d), lambda j, lo, hi, ok: (lo[j], 0, 0)),
                  pl.BlockSpec((1, D_EXPERT, d), lambda j, lo, hi, ok: (hi[j], 0, 0))],
        out_specs=pl.BlockSpec((MOE_ROWS, d), lambda j, lo, hi, ok: (j, 0)))
    return pl.pallas_call(
        _moe_kernel,
        grid_spec=grid_spec,
        out_shape=jax.ShapeDtypeStruct((n_sorted, d), F32),
        compiler_params=pltpu.CompilerParams(dimension_semantics=("arbitrary",), vmem_limit_bytes=VMEM_LIMIT),
        name="moe",
    )(e_low, e_high, valid, hs, wgu, wgu, wd, wd)


def _gather_residual_kernel(pos_ref, pos_next_ref, ys_ref, x_ref, mod_ref, gf_ref, o_ref, rows_ref, sems, *, final):
    j = pl.program_id(0)
    slot = j % 2

    def request(index_ref, into):
        _issue_rows(lambda r: _row_copy(ys_ref, index_ref[0, 0, r], rows_ref.at[into], r, sems.at[into]))

    @pl.when(j == 0)
    def _():
        request(pos_ref, slot)

    @pl.when(j + 1 < pl.num_programs(0))
    def _():
        request(pos_next_ref, 1 - slot)

    _rows_done(ys_ref, rows_ref.at[slot], sems.at[slot])
    xn = x_ref[...] + mod_ref[0, 5:6, :] * rows_ref[slot]
    o_ref[...] = _rms(xn, gf_ref[...]) if final else xn


def _gather_residual_call(pos3, ys, x2, mod_l, gf, seq, final):
    t, d = x2.shape
    tpb = seq // PERM_ROWS
    last = t // PERM_ROWS - 1
    return pl.pallas_call(
        functools.partial(_gather_residual_kernel, final=final),
        grid=(t // PERM_ROWS,),
        in_specs=[pl.BlockSpec((1, 1, PERM_ROWS), lambda j: (j, 0, 0), memory_space=pltpu.SMEM),
                  pl.BlockSpec((1, 1, PERM_ROWS), lambda j: (jnp.minimum(j + 1, last), 0, 0),
                               memory_space=pltpu.SMEM),
                  pl.BlockSpec(memory_space=pl.ANY),
                  pl.BlockSpec((PERM_ROWS, d), lambda j: (j, 0)),
                  pl.BlockSpec((1, 6, d), lambda j: (j // tpb, 0, 0)),
                  pl.BlockSpec((1, d), lambda j: (0, 0))],
        out_specs=pl.BlockSpec((PERM_ROWS, d), lambda j: (j, 0)),
        out_shape=jax.ShapeDtypeStruct((t, d), F32),
        scratch_shapes=[pltpu.VMEM((2, PERM_ROWS, d), F32), pltpu.SemaphoreType.DMA((2,))],
        compiler_params=pltpu.CompilerParams(dimension_semantics=("arbitrary",), vmem_limit_bytes=VMEM_LIMIT),
        name="moe_unsort_residual",
    )(pos3, pos3, ys, x2, mod_l, gf)


def _swap_half(w):
    half = w.shape[1] // 2
    return jnp.concatenate([-w[:, half:], w[:, :half]], axis=1)


def _layer_weights(w_in, w_q_up, w_kv_up):
    d = w_in.shape[0]
    pts = np.cumsum(IN_SIZES)[:-1].tolist()
    (a_q, a_k, a_v, a_f, b_cq, b_ckv, b_kr, c_q, c_k, c_v,
     d_q, d_k, d_v, d_qi, d_ki, d_wi) = jnp.split(w_in, pts, axis=1)
    qs = HEAD_DIM ** -0.5
    small = jnp.concatenate([
        a_f, d_wi * ((IDX_HEADS * IDX_DIM) ** -0.5),
        jnp.zeros((d, _KR_LANE - _WI_LANE - IDX_HEADS), F32),
        b_kr, _swap_half(b_kr), d_ki], axis=1)
    w_all = jnp.concatenate([
        a_q * qs, a_k, a_v,
        c_q * qs, c_k, c_v,
        d_q * qs, d_qi, d_k, d_v,
        small, b_cq, b_ckv], axis=1).astype(BF16)

    per_q = MLA_NOPE + MLA_ROPE
    wq = []
    for hd in range(N_HEADS):
        blk = w_q_up[:, hd * per_q:(hd + 1) * per_q]
        rot = blk[:, MLA_NOPE:]
        wq += [blk[:, :MLA_NOPE], rot, _swap_half(rot)]
    wq = jnp.concatenate(wq, axis=1).astype(BF16)

    place = np.zeros((LANES, LANES), np.float32)
    place[_KR_LANE + np.arange(MLA_ROPE), MLA_NOPE + np.arange(MLA_ROPE)] = 1.0
    place = jnp.asarray(place)
    kcols, vcols = [], []
    for hd in range(N_HEADS):
        blk = w_kv_up[:, hd * 2 * HEAD_DIM:(hd + 1) * 2 * HEAD_DIM]
        knope = jnp.concatenate([blk[:, :MLA_NOPE], jnp.zeros((MLA_KV_RANK, LANES - MLA_NOPE), F32)], axis=1)
        kcols.append(jnp.concatenate([knope, place], axis=0))
        vcols.append(jnp.concatenate([blk[:, MLA_NOPE:], jnp.zeros((LANES, HEAD_DIM), F32)], axis=0))
    wkv = jnp.concatenate(kcols + vcols, axis=1).astype(BF16)
    return w_all, wq, wkv


def _rope_tables(seq):
    half = MLA_ROPE // 2
    inv = ROPE_THETA ** (-jnp.arange(half, dtype=F32) / half)
    ang = jnp.arange(seq, dtype=F32)[:, None] * inv[None, :]
    cos = jnp.tile(jnp.cos(ang), (1, 2))
    sin = jnp.tile(jnp.sin(ang), (1, 2))
    scale = LOG2E * (MLA_NOPE + MLA_ROPE) ** -0.5
    z = lambda n: jnp.zeros((seq, n), F32)
    tab_q = jnp.concatenate([jnp.full((seq, MLA_NOPE), scale, F32), cos * scale, z(MLA_ROPE)], axis=1)
    tab_qs = jnp.concatenate([z(MLA_NOPE), sin * scale, z(MLA_ROPE)], axis=1)
    tab_k = jnp.concatenate([z(_KR_LANE), cos, z(LANES - _KR_LANE - MLA_ROPE)], axis=1)
    tab_ks = jnp.concatenate([z(_KR_LANE), sin, z(LANES - _KR_LANE - MLA_ROPE)], axis=1)
    return jnp.stack([tab_q, tab_qs, tab_k, tab_ks])


def kernel(x, c, w_ada, b_ada, g_norm1, w_in, b_forget, g_q_mla, w_q_up, g_kv_mla, w_kv_up, sinks, w_out,
           g_norm2, w_router, b_router, w_gate, w_up, w_down, g_final):
    b, s, d = x.shape
    depth = w_in.shape[0]
    t = b * s
    topk = min(TOPK_MAX, s // 4)
    tm = min(1024, s)
    tq = min(512, s)
    tq_swa = min(512, s)
    qb_dsa = 2 * LANES
    n_steps = t // MOE_ROWS + N_CLASSES
    assert s % tm == 0 and s % tq == 0 and s % qb_dsa == 0 and s % PERM_ROWS == 0

    mod = _ada_call(c, w_ada, b_ada).reshape(depth, b, 6, d)
    tabs = _rope_tables(s)
    wr_hi = w_router.T.astype(BF16)
    wr_t = jnp.stack([wr_hi, (w_router.T - wr_hi.astype(F32)).astype(BF16)])
    br = b_router.reshape(N_EXPERTS, 1)
    x2 = x.reshape(t, d)
    for l in range(depth):
        w_all, wq, wkv = _layer_weights(w_in[l], w_q_up[l], w_kv_up[l])
        oa, oc, od, osm, oqb, okb, ovb = _inproj_call(
            x2, mod[l], g_norm1[l].reshape(1, d), w_all, wq, wkv,
            g_q_mla[l].reshape(1, -1), g_kv_mla[l].reshape(1, -1), tabs, s, tm)
        small3 = osm.reshape(b, s, LANES)
        bf = jnp.zeros((1, LANES), F32).at[0, :N_HEADS].set(b_forget[l])
        fcum = _fox_cumsum_call(small3, bf)
        frow = jnp.swapaxes(fcum[:, :, :8], 1, 2)
        out_a = _causal_attn_call((oa.reshape(b, s, -1), fcum, frow), True, b, s, tq)
        out_b = _causal_attn_call((oqb.reshape(b, s, -1), okb.reshape(b, s, -1), ovb.reshape(b, s, -1)),
                                  False, b, s, tq)
        out_c = _swa_call(oc.reshape(b, s, -1), sinks[l], b, s, tq_swa)
        out_d = _dsa_call(od.reshape(b, s, -1), small3, b, s, topk, qb_dsa)
        outs = [o.reshape(t, 2 * LANES) for o in (out_a, out_b, out_c, out_d)]
        x2, h2, route = _outproj_call(x2, outs, w_out[l].astype(BF16), mod[l], g_norm2[l].reshape(1, d),
                                      wr_t, br, s, tm)
        pos3, e_low, e_high, valid = _routing_tables(route[0].astype(I32), n_steps)
        wgu = jnp.concatenate([w_gate[l], w_up[l]], axis=2).astype(BF16)
        hs = _scatter_rows_call(pos3, h2, n_steps * MOE_ROWS)
        ys = _moe_call(hs, e_low, e_high, valid, wgu, w_down[l].astype(BF16))
        x2 = _gather_residual_call(pos3, ys, x2, mod[l], g_final.reshape(1, d), s, l == depth - 1)
    return x2.reshape(b, s, d)
```

```python
import functools

import jax
import jax.numpy as jnp
import numpy as np
from jax import lax
from jax.experimental import pallas as pl
from jax.experimental.pallas import tpu as pltpu

F32 = jnp.float32
BF16 = jnp.bfloat16
I32 = jnp.int32
I16 = jnp.int16
I16_MIN, I16_MAX = -(2 ** 15), 2 ** 15 - 1

EPS = 1e-6
HEAD_DIM = 64
LANES = 128
SUBLANES = 8
N_HEADS = 4
MLA_Q_RANK = 256
MLA_KV_RANK = 128
MLA_NOPE = 64
MLA_ROPE = 32
ROPE_THETA = 10000.0
WINDOW = 128
IDX_HEADS = 8
IDX_DIM = 32
TOPK_MAX = 256
N_EXPERTS = 16
N_GROUPS = 4
N_PAIRS = 6
N_CLASSES = N_GROUPS * N_PAIRS
PAIR_LOW = (0, 0, 0, 1, 1, 2)
PAIR_HIGH = (1, 2, 3, 2, 3, 3)
D_EXPERT = 256
MOE_ROWS = 512
PERM_ROWS = 1024
ROW_UNROLL = 8
IN_SIZES = (256, 256, 256, 4, 256, 128, 32, 256, 128, 128, 256, 64, 64, 256, 32, 8)

LOG2E = 1.4426950408889634
NEG = -1e30
INT_MIN = -(2 ** 31)
KEY_NEG_INF = INT_MIN + 0x7FFFFF
VMEM_LIMIT = 56 * 1024 * 1024

_CA, _CC, _CD, _CS, _CQ, _CKV, _CEND = 0, 768, 1280, 1920, 2048, 2304, 2432
_WI_LANE, _KR_LANE = 4, 32


def _nt_dot(a, b):
    return lax.dot_general(a, b, (((1,), (1,)), ((), ())), preferred_element_type=F32)


def _rms(x, g):
    return x * lax.rsqrt(jnp.mean(x * x, axis=-1, keepdims=True) + EPS) * g


def _ada_kernel(c_ref, w_ref, b_ref, o_ref):
    c = c_ref[...]
    act = (c / (1.0 + jnp.exp(-c))).astype(BF16)
    o_ref[0] = jnp.dot(act, w_ref[0].astype(BF16), preferred_element_type=F32) + b_ref[0]


def _ada_call(c, w_ada, b_ada):
    depth, d, n = w_ada.shape
    bsz = c.shape[0]
    tn = 1024
    return pl.pallas_call(
        _ada_kernel,
        grid=(depth, n // tn),
        in_specs=[pl.BlockSpec((bsz, d), lambda l, j: (0, 0)),
                  pl.BlockSpec((1, d, tn), lambda l, j: (l, 0, j)),
                  pl.BlockSpec((1, 1, tn), lambda l, j: (l, 0, j))],
        out_specs=pl.BlockSpec((1, bsz, tn), lambda l, j: (l, 0, j)),
        out_shape=jax.ShapeDtypeStruct((depth, bsz, n), F32),
        compiler_params=pltpu.CompilerParams(dimension_semantics=("arbitrary", "arbitrary"),
                                             vmem_limit_bytes=VMEM_LIMIT),
        name="adaln",
    )(c, w_ada, b_ada.reshape(depth, 1, n))


def _inproj_kernel(x_ref, mod_ref, g1_ref, w_ref, wq_ref, wkv_ref, gq_ref, gkv_ref, tab_ref,
                   oa_ref, oc_ref, od_ref, os_ref, oqb_ref, okb_ref, ovb_ref):
    h = _rms(x_ref[...], g1_ref[...]) * (1.0 + mod_ref[0, 1:2, :]) + mod_ref[0, 0:1, :]
    h = h.astype(BF16)

    def proj(lo, hi):
        return jnp.dot(h, w_ref[:, lo:hi], preferred_element_type=F32)

    q_width = N_HEADS * HEAD_DIM
    low = _lane_is_low()

    def both_halves(pair):
        swapped = pltpu.roll(pair, HEAD_DIM, axis=1)
        return jnp.where(low, pair, swapped), jnp.where(low, swapped, pair)

    pa = proj(_CA, _CC)
    oa_ref[:, :q_width] = (pa[:, :q_width] * LOG2E).astype(BF16)
    oa_ref[:, q_width:] = pa[:, q_width:].astype(BF16)

    pc = proj(_CC, _CD)
    oc_ref[:, :q_width] = (pc[:, :q_width] * LOG2E).astype(BF16)
    for n in range(2):
        tiles = both_halves(pc[:, q_width + n * LANES:q_width + (n + 1) * LANES])
        for g in range(2):
            lo_col = q_width + (2 * n + g) * LANES
            oc_ref[:, lo_col:lo_col + LANES] = tiles[g].astype(BF16)

    pd = proj(_CD, _CS)
    small = proj(_CS, _CQ)
    os_ref[...] = small
    od_ref[:, :q_width] = (pd[:, :q_width] * LOG2E).astype(BF16)
    k_twice, v_twice = both_halves(pd[:, 2 * q_width:])
    od_ref[:, q_width:q_width + LANES] = k_twice.astype(BF16)
    od_ref[:, q_width + LANES:2 * q_width] = v_twice.astype(BF16)
    od_ref[:, 2 * q_width:3 * q_width] = pd[:, q_width:2 * q_width].astype(BF16)
    lane = lax.broadcasted_iota(I32, (1, LANES), 1)
    ki4 = small
    for n in range(1, LANES // IDX_DIM):
        ki4 = jnp.where(lane < LANES - n * IDX_DIM, pltpu.roll(small, LANES - n * IDX_DIM, axis=1), ki4)
    od_ref[:, 3 * q_width:] = ki4.astype(BF16)

    cq = _rms(proj(_CQ, _CKV), gq_ref[...]).astype(BF16)
    qf = jnp.dot(cq, wq_ref[...], preferred_element_type=F32)
    tab_q, tab_qs, tab_k, tab_ks = tab_ref[0], tab_ref[1], tab_ref[2], tab_ref[3]
    for hd in range(N_HEADS):
        qg = qf[:, hd * LANES:(hd + 1) * LANES]
        qr = qg * tab_q + pltpu.roll(qg, LANES - MLA_ROPE, axis=1) * tab_qs
        oqb_ref[:, hd * LANES:(hd + 1) * LANES] = qr.astype(BF16)

    ckv = _rms(proj(_CKV, _CEND), gkv_ref[...]).astype(BF16)
    kr = (small * tab_k + pltpu.roll(small, LANES - MLA_ROPE, axis=1) * tab_ks).astype(BF16)
    kvf = jnp.dot(jnp.concatenate([ckv, kr], axis=1), wkv_ref[...], preferred_element_type=F32)
    okb_ref[...] = kvf[:, :4 * LANES].astype(BF16)
    ovb_ref[...] = kvf[:, 4 * LANES:].astype(BF16)


def _inproj_call(x2, mod_l, g1, w_all, wq, wkv, gq, gkv, tabs, seq, tm):
    t, d = x2.shape
    tpb = seq // tm
    row = lambda n: pl.BlockSpec((tm, n), lambda i: (i, 0))
    const = lambda shape: pl.BlockSpec(shape, lambda i: (0,) * len(shape))
    widths = (768, 768, 896, 128, 512, 512, 256)
    dtypes = (BF16, BF16, BF16, F32, BF16, BF16, BF16)
    return pl.pallas_call(
        _inproj_kernel,
        grid=(t // tm,),
        in_specs=[row(d),
                  pl.BlockSpec((1, 6, d), lambda i: (i // tpb, 0, 0)),
                  const((1, d)), const(w_all.shape), const(wq.shape), const(wkv.shape),
                  const((1, MLA_Q_RANK)), const((1, MLA_KV_RANK)),
                  pl.BlockSpec((4, tm, LANES), lambda i: (0, i % tpb, 0))],
        out_specs=[row(n) for n in widths],
        out_shape=[jax.ShapeDtypeStruct((t, n), dt) for n, dt in zip(widths, dtypes)],
        compiler_params=pltpu.CompilerParams(dimension_semantics=("arbitrary",), vmem_limit_bytes=VMEM_LIMIT),
        name="inproj",
    )(x2, mod_l, g1, w_all, wq, wkv, gq, gkv, tabs)


def _fox_cumsum_kernel(s_ref, b_ref, o_ref):
    z = s_ref[0] + b_ref[...]
    lf = jnp.minimum(z, 0.0) - jnp.log(1.0 + jnp.exp(-jnp.abs(z)))
    n = lf.shape[0]
    row = lax.broadcasted_iota(I32, lf.shape, 0)
    d = 1
    while d < n:
        lf = lf + jnp.where(row >= d, pltpu.roll(lf, d, axis=0), 0.0)
        d *= 2
    o_ref[0] = lf * LOG2E


def _fox_cumsum_call(small3, bf):
    b, s, _ = small3.shape
    return pl.pallas_call(
        _fox_cumsum_kernel,
        grid=(b,),
        in_specs=[pl.BlockSpec((1, s, LANES), lambda i: (i, 0, 0)),
                  pl.BlockSpec((1, LANES), lambda i: (0, 0))],
        out_specs=pl.BlockSpec((1, s, LANES), lambda i: (i, 0, 0)),
        out_shape=jax.ShapeDtypeStruct((b, s, LANES), F32),
        compiler_params=pltpu.CompilerParams(dimension_semantics=("arbitrary",), vmem_limit_bytes=VMEM_LIMIT),
        name="fox_cumsum",
    )(small3, bf)


def _lane_is_low():
    return lax.broadcasted_iota(I32, (1, LANES), 1) < HEAD_DIM


def _split_pair(q):
    low = _lane_is_low()
    zero = jnp.zeros_like(q)
    return jnp.where(low, q, zero), jnp.where(low, zero, q)


def _values_with_ones(v, j):
    low = _lane_is_low()
    return jnp.where(low if j == 0 else ~low, v, jnp.ones_like(v))


def _pair_output(a0, a1, extra0=None, extra1=None):
    l0 = pltpu.roll(a0, HEAD_DIM, axis=1)
    l1 = pltpu.roll(a1, HEAD_DIM, axis=1)
    if extra0 is not None:
        l0, l1 = l0 + extra0, l1 + extra1
    return jnp.where(_lane_is_low(), a0 / l0, a1 / l1)


def _two_pass_attention(n_before, logits, values, s_ref, mx_ref, acc_ref, n_heads, shared=None, diagonal_mask=True):
    tq, tk = s_ref.shape[-2:]
    mx_ref[...] = jnp.full(mx_ref.shape, NEG, F32)

    def store(c, n, diagonal):
        ctx = shared(c, n) if shared is not None else None
        for j in range(n_heads):
            s = logits(c, n, j, diagonal, ctx)
            for t in range(n):
                s_ref[j, c + t] = s[:, t * tk:(t + 1) * tk]
            parts = [s[:, blk * LANES:(blk + 1) * LANES] for blk in range(n * tk // LANES)]
            while len(parts) > 1:
                parts = [jnp.maximum(a, b) for a, b in zip(parts[::2], parts[1::2])]
            mx_ref[j] = jnp.maximum(mx_ref[j], parts[0])

    def store_two(c2, carry):
        store(2 * c2, 2, False)
        return carry

    if diagonal_mask:
        lax.fori_loop(0, n_before // 2, store_two, 0)

        @pl.when(n_before % 2 == 1)
        def _():
            store(n_before - 1, 1, False)

        store(n_before, 1, True)
    else:
        lax.fori_loop(0, (n_before + 1) // 2, store_two, 0)

        @pl.when(n_before % 2 == 0)
        def _():
            store(n_before, 1, False)

    row_max = [jnp.max(mx_ref[j], axis=1, keepdims=True) for j in range(n_heads)]
    shift = [jnp.broadcast_to(m, (tq, tk)) for m in row_max]
    acc_ref[...] = jnp.zeros(acc_ref.shape, F32)

    def accum(c, n):
        for j in range(n_heads):
            p = [jnp.exp2(s_ref[j, c + t] - shift[j]).astype(BF16) for t in range(n)]
            p = p[0] if n == 1 else jnp.concatenate(p, axis=1)
            acc_ref[j] += jnp.dot(p, values(c, n, j), preferred_element_type=F32)

    def accum_two(c2, carry):
        accum(2 * c2, 2)
        return carry

    lax.fori_loop(0, (n_before + 1) // 2, accum_two, 0)

    @pl.when(n_before % 2 == 0)
    def _():
        accum(n_before, 1)

    return row_max


def _causal_attn_kernel(*refs, fox, tq):
    if fox:
        q_ref, k_ref, v_ref, fc_ref, fr_ref, o_ref, s_ref, mx_ref, acc_ref = refs
    else:
        q_ref, k_ref, v_ref, o_ref, s_ref, mx_ref, acc_ref = refs
    tk = tq
    hp = pl.program_id(1)
    i = pl.program_id(2)
    if fox:
        qs = _split_pair(q_ref[0])
        lane = lax.broadcasted_iota(I32, (1, LANES), 1)
        fcol = [jnp.sum(jnp.where(lane == 2 * hp + j, fc_ref[0], 0.0), axis=1, keepdims=True) for j in range(2)]
    else:
        qs = (q_ref[0, :, :LANES], q_ref[0, :, LANES:])
    causal = (lax.broadcasted_iota(I32, (tq, tk), 1) <= lax.broadcasted_iota(I32, (tq, tk), 0))

    def logits(c, n, j, diagonal, _):
        start = pl.multiple_of(c * tk, tk)
        if fox:
            s = _nt_dot(qs[j], k_ref[0, pl.ds(start, n * tk), :])
            s = (s + fcol[j]) - fr_ref[0, pl.ds(2 * hp + j, 1), pl.ds(start, n * tk)]
        else:
            s = _nt_dot(qs[j], k_ref[0, pl.ds(start, n * tk), j * LANES:(j + 1) * LANES])
        return jnp.where(causal, s, NEG) if diagonal else s

    def values(c, n, j):
        return _values_with_ones(v_ref[0, pl.ds(pl.multiple_of(c * tk, tk), n * tk), :], j)

    _two_pass_attention(i, logits, values, s_ref, mx_ref, acc_ref, 2)
    o_ref[0] = _pair_output(acc_ref[0], acc_ref[1]).astype(o_ref.dtype)


def _causal_attn_call(arrs, fox, b, s, tq):
    scratch = [pltpu.VMEM((2, s // tq, tq, tq), F32), pltpu.VMEM((2, tq, LANES), F32),
               pltpu.VMEM((2, tq, LANES), F32)]
    if fox:
        qkv, fcol, frow = arrs
        operands = (qkv, qkv, qkv, fcol, frow)
        in_specs = [pl.BlockSpec((1, tq, LANES), lambda bi, hp, i: (bi, i, hp)),
                    pl.BlockSpec((1, s, LANES), lambda bi, hp, i: (bi, 0, 2 + hp)),
                    pl.BlockSpec((1, s, LANES), lambda bi, hp, i: (bi, 0, 4 + hp)),
                    pl.BlockSpec((1, tq, LANES), lambda bi, hp, i: (bi, i, 0)),
                    pl.BlockSpec((1, 8, s), lambda bi, hp, i: (bi, 0, 0))]
    else:
        operands = arrs
        in_specs = [pl.BlockSpec((1, tq, 2 * LANES), lambda bi, hp, i: (bi, i, hp)),
                    pl.BlockSpec((1, s, 2 * LANES), lambda bi, hp, i: (bi, 0, hp)),
                    pl.BlockSpec((1, s, LANES), lambda bi, hp, i: (bi, 0, hp))]
    return pl.pallas_call(
        functools.partial(_causal_attn_kernel, fox=fox, tq=tq),
        grid=(b, 2, s // tq),
        in_specs=in_specs,
        out_specs=pl.BlockSpec((1, tq, LANES), lambda bi, hp, i: (bi, i, hp)),
        out_shape=jax.ShapeDtypeStruct((b, s, 2 * LANES), BF16),
        scratch_shapes=scratch,
        compiler_params=pltpu.CompilerParams(dimension_semantics=("arbitrary",) * 3, vmem_limit_bytes=VMEM_LIMIT),
        name="attn_fox" if fox else "attn_mla",
    )(*operands)


def _alibi_slope(head):
    return lax.shift_left(jnp.int32(1), 7 - head).astype(F32) * (LOG2E * 2.0 ** -8)


def _swa_kernel(sink_ref, q_ref, k_ref, v_ref, o_ref, *, tq):
    hp = pl.program_id(1)
    i = pl.program_id(2)
    band = 2 * WINDOW
    rel = lax.broadcasted_iota(I32, (WINDOW, band), 0) - lax.broadcasted_iota(I32, (WINDOW, band), 1)
    for r in range(tq // WINDOW):
        q_start = i * tq + r * WINDOW
        k_start = pl.multiple_of(jnp.maximum(q_start - WINDOW, 0), WINDOW)
        kb = k_ref[0, pl.ds(k_start, band), :]
        vb = v_ref[0, pl.ds(k_start, band), :]
        dist = rel + (q_start - k_start)
        valid = (dist >= 0) & (dist < WINDOW)
        distf = dist.astype(F32)
        qs = _split_pair(q_ref[0, r * WINDOW:(r + 1) * WINDOW, :])
        acc, sink_term = [], []
        for j in range(2):
            sink = sink_ref[2 * hp + j] * LOG2E
            s = jnp.where(valid, _nt_dot(qs[j], kb) - _alibi_slope(2 * hp + j) * distf, NEG)
            m = jnp.maximum(jnp.max(s, axis=1, keepdims=True), sink)
            p = jnp.exp2(s - m).astype(BF16)
            acc.append(jnp.dot(p, _values_with_ones(vb, j), preferred_element_type=F32))
            sink_term.append(jnp.exp2(sink - m))
        o_ref[0, r * WINDOW:(r + 1) * WINDOW, :] = _pair_output(acc[0], acc[1], *sink_term).astype(o_ref.dtype)


def _swa_call(qkv, sinks, b, s, tq):
    return pl.pallas_call(
        functools.partial(_swa_kernel, tq=tq),
        grid=(b, 2, s // tq),
        in_specs=[pl.BlockSpec(memory_space=pltpu.SMEM),
                  pl.BlockSpec((1, tq, LANES), lambda bi, hp, i: (bi, i, hp)),
                  pl.BlockSpec((1, s, LANES), lambda bi, hp, i: (bi, 0, 2 + hp)),
                  pl.BlockSpec((1, s, LANES), lambda bi, hp, i: (bi, 0, 4 + hp))],
        out_specs=pl.BlockSpec((1, tq, LANES), lambda bi, hp, i: (bi, i, hp)),
        out_shape=jax.ShapeDtypeStruct((b, s, 2 * LANES), BF16),
        compiler_params=pltpu.CompilerParams(dimension_semantics=("arbitrary",) * 3, vmem_limit_bytes=VMEM_LIMIT),
        name="attn_swa",
    )(sinks, qkv, qkv, qkv)


def _dsa_kernel(q_ref, k_ref, v_ref, qi_ref, ki_ref, wi_ref, o_ref, keys_ref, hi_ref, lo_ref, thr_ref,
                s_ref, mx_ref, acc_ref, *, seq, topk, qb):
    kc = LANES
    kw = 2 * kc
    i = pl.program_id(1)
    nwide = (i + 1) * (qb // kw)
    lane = lax.broadcasted_iota(I32, (1, LANES), 1)
    key_row = lax.broadcasted_iota(I32, (kc, qb), 0)
    query_pos = i * qb + lax.broadcasted_iota(I32, (1, qb), 1)

    qi = qi_ref[0]
    wi_t = wi_ref[0].T
    qms, wrows = [], []
    for hd in range(IDX_HEADS):
        g, r = divmod(hd, LANES // IDX_DIM)
        sel = (lane >= r * IDX_DIM) & (lane < (r + 1) * IDX_DIM)
        qg = qi[:, g * LANES:(g + 1) * LANES]
        qms.append(jnp.where(sel, qg, jnp.zeros_like(qg)))
        wrows.append(wi_t[_WI_LANE + hd:_WI_LANE + hd + 1, :])
    q_all = jnp.concatenate(qms, axis=0)
    key_minus_query = (lax.broadcasted_iota(I32, (kw, qb), 0) - lax.broadcasted_iota(I32, (kw, qb), 1))

    def score_chunk(c, carry):
        start = pl.multiple_of(c * kw, kw)
        logit = _nt_dot(ki_ref[0, pl.ds(start, kw), :], q_all)
        sc = wrows[0] * jnp.maximum(logit[:, :qb], 0.0)
        for hd in range(1, IDX_HEADS):
            sc = sc + wrows[hd] * jnp.maximum(logit[:, hd * qb:(hd + 1) * qb], 0.0)
        sc = jnp.where(key_minus_query <= i * qb - start, sc, -jnp.inf)
        sc = jnp.where(sc == 0.0, 0.0, sc)
        bits = pltpu.bitcast(sc, I32)
        keys = bits ^ ((bits >> 31) & 0x7FFFFFFF)
        keys_ref[2 * c] = keys[:kc]
        keys_ref[2 * c + 1] = keys[kc:]
        hi = (keys >> 16).astype(I16)
        lo = ((keys & 0xFFFF) + I16_MIN).astype(I16)
        hi_ref[2 * c] = hi[:kc]
        hi_ref[2 * c + 1] = hi[kc:]
        lo_ref[2 * c] = lo[:kc]
        lo_ref[2 * c + 1] = lo[kc:]
        return carry

    lax.fori_loop(0, nwide, score_chunk, 0)

    keep_all = jnp.full((1, qb), KEY_NEG_INF + 1, I32)
    thr_ref[...] = jnp.broadcast_to(keep_all, thr_ref.shape)

    def count(pred):
        def body(c2, acc):
            parts = []
            for c in (2 * c2, 2 * c2 + 1):
                hit = jnp.where(pred(keys_ref[c], c), 1.0, 0.0)
                parts += [hit[r * SUBLANES:(r + 1) * SUBLANES] for r in range(kc // SUBLANES)]
            while len(parts) > 1:
                parts = [a + b for a, b in zip(parts[::2], parts[1::2])]
            return acc + parts[0]
        acc = lax.fori_loop(0, nwide, body, jnp.zeros((SUBLANES, qb), F32))
        return jnp.sum(acc, axis=0, keepdims=True)

    @pl.when((i + 1) * qb > topk)
    def _():
        kf = float(topk)

        pack = 2 * SUBLANES

        def count16(plane_ref, cand):
            cand16 = cand.astype(I16)

            def body(c2, acc):
                parts = []
                for c in (2 * c2, 2 * c2 + 1):
                    hit = jnp.where(plane_ref[c] >= cand16, jnp.int16(1), jnp.int16(0))
                    parts += [hit[r * pack:(r + 1) * pack] for r in range(kc // pack)]
                while len(parts) > 1:
                    parts = [a + b for a, b in zip(parts[::2], parts[1::2])]
                return acc + parts[0]

            acc = lax.fori_loop(0, nwide, body, jnp.zeros((pack, qb), I16))
            return jnp.sum(acc.astype(F32), axis=0, keepdims=True)

        def search16(plane_ref, cnt_at_floor):
            def bit_step(it, carry):
                val, cnt_val = carry
                cand = val + lax.shift_left(jnp.int32(1), 15 - it)
                cnt = count16(plane_ref, cand)
                ok = cnt >= kf
                return jnp.where(ok, cand, val), jnp.where(ok, cnt, cnt_val)

            return lax.fori_loop(0, 16, bit_step, (jnp.full((1, qb), I16_MIN, I32), cnt_at_floor))

        thr_hi, cnt_hi = search16(hi_ref, jnp.full((1, qb), float(seq), F32))
        thr_hi16 = thr_hi.astype(I16)

        def pin(c, carry):
            h = hi_ref[c]
            lo_ref[c] = jnp.where(h > thr_hi16, jnp.int16(I16_MAX),
                                  jnp.where(h == thr_hi16, lo_ref[c], jnp.int16(I16_MIN)))
            return carry

        lax.fori_loop(0, 2 * nwide, pin, 0)
        thr_lo, cnt_thr = search16(lo_ref, cnt_hi)
        thr = thr_hi * 65536 + (thr_lo - I16_MIN)
        thr_ref[...] = jnp.broadcast_to(jnp.where(query_pos < topk, keep_all, thr), thr_ref.shape)

        @pl.when(jnp.max(cnt_thr) > kf)
        def _():
            need = kf - count(lambda kk, c: kk > thr)

            def idx_step(it, pos):
                cand = pos + lax.shift_left(jnp.int32(1), (seq.bit_length() - 2) - it)
                cnt = count(lambda kk, c: (kk == thr) & (key_row + c * kc < cand))
                return jnp.where(cnt < need, cand, pos)

            pos = lax.fori_loop(0, seq.bit_length() - 1, idx_step, jnp.zeros((1, qb), I32))

            def demote(c, carry):
                kk = keys_ref[c]
                keys_ref[c] = jnp.where((kk == thr) & (key_row + c * kc > pos), kk - 1, kk)
                return carry

            lax.fori_loop(0, 2 * nwide, demote, 0)

    q = q_ref[0]
    qs = _split_pair(q[:, :LANES]) + _split_pair(q[:, LANES:])
    thr_row = thr_ref[0:1, :]
    key_col = lax.broadcasted_iota(I32, (1, 2 * kw), 1)

    def shared(c, n):
        halves = [jnp.where(keys_ref[2 * c + h] >= thr_row, 0.0, NEG).T for h in range(2 * n)]
        key_off = (key_col[:, :n * kw] + (c * kw - i * qb)).astype(F32)
        return jnp.concatenate(halves, axis=1), key_off

    def logits(c, n, hd, diagonal, ctx):
        bias, key_off = ctx
        start = pl.multiple_of(c * kw, kw)
        slope = LOG2E * 2.0 ** -(N_HEADS + hd + 1)
        return _nt_dot(qs[hd], k_ref[0, pl.ds(start, n * kw), :]) + (bias + slope * key_off)

    def values(c, n, hd):
        return _values_with_ones(v_ref[0, pl.ds(pl.multiple_of(c * kw, kw), n * kw), :], hd % 2)

    _two_pass_attention(nwide - 1, logits, values, s_ref, mx_ref, acc_ref, N_HEADS, shared, diagonal_mask=False)
    o_ref[0] = jnp.concatenate([_pair_output(acc_ref[0], acc_ref[1]), _pair_output(acc_ref[2], acc_ref[3])],
                               axis=1).astype(o_ref.dtype)


def _dsa_call(od3, small3, b, s, topk, qb):
    kc = LANES
    scratch = [pltpu.VMEM((s // kc, kc, qb), I32), pltpu.VMEM((s // kc, kc, qb), I16),
               pltpu.VMEM((s // kc, kc, qb), I16), pltpu.VMEM((SUBLANES, qb), I32),
               pltpu.VMEM((N_HEADS, s // (2 * kc), qb, 2 * kc), F32),
               pltpu.VMEM((N_HEADS, qb, LANES), F32), pltpu.VMEM((N_HEADS, qb, LANES), F32)]
    return pl.pallas_call(
        functools.partial(_dsa_kernel, seq=s, topk=topk, qb=qb),
        grid=(b, s // qb),
        in_specs=[pl.BlockSpec((1, qb, 2 * LANES), lambda bi, i: (bi, i, 0)),
                  pl.BlockSpec((1, s, LANES), lambda bi, i: (bi, 0, 2)),
                  pl.BlockSpec((1, s, LANES), lambda bi, i: (bi, 0, 3)),
                  pl.BlockSpec((1, qb, 2 * LANES), lambda bi, i: (bi, i, 2)),
                  pl.BlockSpec((1, s, LANES), lambda bi, i: (bi, 0, 6)),
                  pl.BlockSpec((1, qb, LANES), lambda bi, i: (bi, i, 0))],
        out_specs=pl.BlockSpec((1, qb, 2 * LANES), lambda bi, i: (bi, i, 0)),
        out_shape=jax.ShapeDtypeStruct((b, s, 2 * LANES), BF16),
        scratch_shapes=scratch,
        compiler_params=pltpu.CompilerParams(dimension_semantics=("arbitrary",) * 2, vmem_limit_bytes=VMEM_LIMIT),
        name="attn_dsa",
    )(od3, od3, od3, od3, od3, small3)


def _outproj_kernel(x_ref, oa_ref, ob_ref, oc_ref, od_ref, w_ref, mod_ref, g2_ref, wr_ref, br_ref,
                    xo_ref, h_ref, rt_ref):
    d_model = x_ref.shape[1]
    merged = jnp.concatenate([oa_ref[...], ob_ref[...], oc_ref[...], od_ref[...]], axis=1)
    mix = jnp.dot(merged, w_ref[...], preferred_element_type=F32)
    xn = x_ref[...] + mod_ref[0, 2:3, :] * mix
    xo_ref[...] = xn
    h = _rms(xn, g2_ref[...]) * (1.0 + mod_ref[0, 4:5, :]) + mod_ref[0, 3:4, :]
    h_ref[:, :d_model] = h

    h_hi = h.astype(BF16)
    h_lo = (h - h_hi.astype(F32)).astype(BF16)
    logits = _nt_dot(wr_ref[0], h_hi) + (_nt_dot(wr_ref[0], h_lo) + _nt_dot(wr_ref[1], h_hi))
    score = 1.0 / (1.0 + jnp.exp(-logits))
    biased = score + br_ref[...]
    srow = [score[e:e + 1, :] for e in range(N_EXPERTS)]
    brow = [biased[e:e + 1, :] for e in range(N_EXPERTS)]
    per = N_EXPERTS // N_GROUPS
    best_v = best_g = None
    for g in range(N_GROUPS):
        r = brow[g * per:(g + 1) * per]
        top2 = None
        for a in range(per):
            for c in range(a + 1, per):
                top2 = r[a] + r[c] if top2 is None else jnp.maximum(top2, r[a] + r[c])
        if g == 0:
            best_v, best_g = top2, jnp.zeros_like(top2, dtype=I32)
        else:
            up = top2 > best_v
            best_v = jnp.where(up, top2, best_v)
            best_g = jnp.where(up, g, best_g)
    cand = [jnp.where(best_g == e // per, brow[e], -jnp.inf) for e in range(N_EXPERTS)]

    def first_max(vals):
        v, idx = vals[0], jnp.zeros_like(best_g)
        for e in range(1, N_EXPERTS):
            up = vals[e] > v
            v = jnp.where(up, vals[e], v)
            idx = jnp.where(up, e, idx)
        return idx

    i1 = first_max(cand)
    i2 = first_max([jnp.where(i1 == e, -jnp.inf, cand[e]) for e in range(N_EXPERTS)])
    s1 = sum(jnp.where(i1 == e, srow[e], 0.0) for e in range(N_EXPERTS))
    s2 = sum(jnp.where(i2 == e, srow[e], 0.0) for e in range(N_EXPERTS))
    den = s1 + s2
    first_low = i1 < i2
    la = jnp.minimum(i1, i2) - per * best_g
    lb = jnp.maximum(i1, i2) - per * best_g
    pair = jnp.where(la == 0, lb - 1, jnp.where(la == 1, lb + 1, 5))
    route = jnp.concatenate([(best_g * N_PAIRS + pair).astype(F32), jnp.where(first_low, s1, s2) / den,
                             jnp.where(first_low, s2, s1) / den, jnp.zeros((SUBLANES - 3, den.shape[1]), F32)], axis=0)
    rt_ref[...] = route
    pad = jnp.zeros((LANES - SUBLANES, den.shape[1]), F32)
    h_ref[:, d_model:] = jnp.concatenate([route, pad], axis=0).T


def _outproj_call(x2, outs, w_out, mod_l, g2, wr_t, br, seq, tm):
    t, d = x2.shape
    tpb = seq // tm
    row = lambda n: pl.BlockSpec((tm, n), lambda i: (i, 0))
    const = lambda shape: pl.BlockSpec(shape, lambda i: (0,) * len(shape))
    return pl.pallas_call(
        _outproj_kernel,
        grid=(t // tm,),
        in_specs=[row(d)] + [row(2 * LANES)] * 4 + [
            const(w_out.shape), pl.BlockSpec((1, 6, d), lambda i: (i // tpb, 0, 0)), const((1, d)),
            const(wr_t.shape), const(br.shape)],
        out_specs=[row(d), row(d + LANES), pl.BlockSpec((SUBLANES, tm), lambda i: (0, i))],
        out_shape=[jax.ShapeDtypeStruct((t, d), F32), jax.ShapeDtypeStruct((t, d + LANES), F32),
                   jax.ShapeDtypeStruct((SUBLANES, t), F32)],
        compiler_params=pltpu.CompilerParams(dimension_semantics=("arbitrary",), vmem_limit_bytes=VMEM_LIMIT),
        name="outproj_router",
    )(x2, *outs, w_out, mod_l, g2, wr_t, br)


def _routing_tables(cls, n_steps):
    t = cls.shape[0]
    onehot = (cls[:, None] == jnp.arange(N_CLASSES, dtype=I32)[None, :]).astype(I32)
    upto = jnp.cumsum(onehot, axis=0)
    rank = jnp.sum((upto - onehot) * onehot, axis=1)
    padded = (upto[-1] + MOE_ROWS - 1) // MOE_ROWS * MOE_ROWS
    ends = jnp.cumsum(padded)
    pos = jnp.sum(onehot * (ends - padded)[None, :], axis=1) + rank
    step_cls = jnp.sum((jnp.arange(n_steps, dtype=I32) * MOE_ROWS)[:, None] >= ends[None, :], axis=1)
    valid = (step_cls < N_CLASSES).astype(I32)
    step_cls = jnp.minimum(step_cls, N_CLASSES - 1)
    base = (step_cls // N_PAIRS) * (N_EXPERTS // N_GROUPS)
    e_low = base + jnp.asarray(PAIR_LOW, I32)[step_cls % N_PAIRS]
    e_high = base + jnp.asarray(PAIR_HIGH, I32)[step_cls % N_PAIRS]
    return pos.reshape(t // PERM_ROWS, 1, PERM_ROWS), e_low, e_high, valid


def _row_copy(src_ref, src_row, dst_ref, dst_row, sem):
    return pltpu.make_async_copy(src_ref.at[pl.ds(src_row, 1)], dst_ref.at[pl.ds(dst_row, 1)], sem)


def _issue_rows(copy_row):
    def issue(r8, carry):
        for u in range(ROW_UNROLL):
            copy_row(r8 * ROW_UNROLL + u).start(priority=u % 2)
        return carry

    lax.fori_loop(0, PERM_ROWS // ROW_UNROLL, issue, 0)


def _rows_done(src_ref, dst_ref, sem):
    pltpu.make_async_copy(src_ref.at[pl.ds(0, PERM_ROWS)], dst_ref.at[pl.ds(0, PERM_ROWS)], sem).wait()


def _scatter_rows_kernel(pos_ref, src_ref, init_ref, dst_ref, sem):
    del init_ref
    _issue_rows(lambda r: _row_copy(src_ref, r, dst_ref, pos_ref[0, 0, r], sem))
    _rows_done(src_ref, dst_ref, sem)


def _scatter_rows_call(pos3, src, init):
    t, d = src.shape
    n_sorted = init.shape[0]
    return pl.pallas_call(
        _scatter_rows_kernel,
        grid=(t // PERM_ROWS,),
        in_specs=[pl.BlockSpec((1, 1, PERM_ROWS), lambda j: (j, 0, 0), memory_space=pltpu.SMEM),
                  pl.BlockSpec((PERM_ROWS, d), lambda j: (j, 0)), pl.BlockSpec(memory_space=pl.ANY)],
        out_specs=pl.BlockSpec(memory_space=pl.ANY),
        out_shape=jax.ShapeDtypeStruct((n_sorted, d), src.dtype),
        scratch_shapes=[pltpu.SemaphoreType.DMA(())],
        input_output_aliases={2: 0},
        compiler_params=pltpu.CompilerParams(dimension_semantics=("arbitrary",), has_side_effects=True),
        name="moe_sort_rows",
    )(pos3, src, init)


def _moe_kernel(e_low_ref, e_high_ref, valid_ref, h_ref, wgu_low_ref, wgu_high_ref, wd_low_ref, wd_high_ref, o_ref):
    del e_low_ref, e_high_ref
    step = pl.program_id(0)
    d_model = o_ref.shape[1]

    @pl.when(valid_ref[step] == 0)
    def _():
        o_ref[...] = jnp.zeros(o_ref.shape, F32)

    @pl.when(valid_ref[step] != 0)
    def _():
        h = h_ref[:, :d_model].astype(BF16)
        w_low = h_ref[:, d_model + 1:d_model + 2]
        w_high = h_ref[:, d_model + 2:d_model + 3]

        def expert(wgu_ref, wd_ref):
            gu = jnp.dot(h, wgu_ref[0], preferred_element_type=F32)
            gate, up = gu[:, :D_EXPERT], gu[:, D_EXPERT:]
            hid = (gate / (1.0 + jnp.exp(-gate)) * up).astype(BF16)
            return jnp.dot(hid, wd_ref[0], preferred_element_type=F32)

        o_ref[...] = w_low * expert(wgu_low_ref, wd_low_ref) + w_high * expert(wgu_high_ref, wd_high_ref)


def _moe_call(hs, e_low, e_high, valid, wgu, wd):
    n_sorted = hs.shape[0]
    d = wd.shape[2]
    grid_spec = pltpu.PrefetchScalarGridSpec(
        num_scalar_prefetch=3,
        grid=(n_sorted // MOE_ROWS,),
        in_specs=[pl.BlockSpec((MOE_ROWS, hs.shape[1]), lambda j, lo, hi, ok: (j, 0)),
                  pl.BlockSpec((1, d, 2 * D_EXPERT), lambda j, lo, hi, ok: (lo[j], 0, 0)),
                  pl.BlockSpec((1, d, 2 * D_EXPERT), lambda j, lo, hi, ok: (hi[j], 0, 0)),
                  pl.BlockSpec((1, D_EXPERT, d), lambda j, lo, hi, ok: (lo[j], 0, 0)),
                  pl.BlockSpec((1, D_EXPERT, d), lambda j, lo, hi, ok: (hi[j], 0, 0))],
        out_specs=pl.BlockSpec((MOE_ROWS, d), lambda j, lo, hi, ok: (j, 0)))
    return pl.pallas_call(
        _moe_kernel,
        grid_spec=grid_spec,
        out_shape=jax.ShapeDtypeStruct((n_sorted, d), F32),
        compiler_params=pltpu.CompilerParams(dimension_semantics=("arbitrary",), vmem_limit_bytes=VMEM_LIMIT),
        name="moe",
    )(e_low, e_high, valid, hs, wgu, wgu, wd, wd)


def _gather_residual_kernel(pos_ref, pos_next_ref, ys_ref, x_ref, mod_ref, gf_ref, o_ref, rows_ref, sems, *, final):
    j = pl.program_id(0)
    slot = j % 2

    def request(index_ref, into):
        _issue_rows(lambda r: _row_copy(ys_ref, index_ref[0, 0, r], rows_ref.at[into], r, sems.at[into]))

    @pl.when(j == 0)
    def _():
        request(pos_ref, slot)

    @pl.when(j + 1 < pl.num_programs(0))
    def _():
        request(pos_next_ref, 1 - slot)

    _rows_done(ys_ref, rows_ref.at[slot], sems.at[slot])
    xn = x_ref[...] + mod_ref[0, 5:6, :] * rows_ref[slot]
    o_ref[...] = _rms(xn, gf_ref[...]) if final else xn


def _gather_residual_call(pos3, ys, x2, mod_l, gf, seq, final):
    t, d = x2.shape
    tpb = seq // PERM_ROWS
    last = t // PERM_ROWS - 1
    return pl.pallas_call(
        functools.partial(_gather_residual_kernel, final=final),
        grid=(t // PERM_ROWS,),
        in_specs=[pl.BlockSpec((1, 1, PERM_ROWS), lambda j: (j, 0, 0), memory_space=pltpu.SMEM),
                  pl.BlockSpec((1, 1, PERM_ROWS), lambda j: (jnp.minimum(j + 1, last), 0, 0),
                               memory_space=pltpu.SMEM),
                  pl.BlockSpec(memory_space=pl.ANY),
                  pl.BlockSpec((PERM_ROWS, d), lambda j: (j, 0)),
                  pl.BlockSpec((1, 6, d), lambda j: (j // tpb, 0, 0)),
                  pl.BlockSpec((1, d), lambda j: (0, 0))],
        out_specs=pl.BlockSpec((PERM_ROWS, d), lambda j: (j, 0)),
        out_shape=jax.ShapeDtypeStruct((t, d), F32),
        scratch_shapes=[pltpu.VMEM((2, PERM_ROWS, d), F32), pltpu.SemaphoreType.DMA((2,))],
        compiler_params=pltpu.CompilerParams(dimension_semantics=("arbitrary",), vmem_limit_bytes=VMEM_LIMIT),
        name="moe_unsort_residual",
    )(pos3, pos3, ys, x2, mod_l, gf)


def _swap_half(w):
    half = w.shape[1] // 2
    return jnp.concatenate([-w[:, half:], w[:, :half]], axis=1)


def _layer_weights(w_in, w_q_up, w_kv_up):
    d = w_in.shape[0]
    pts = np.cumsum(IN_SIZES)[:-1].tolist()
    (a_q, a_k, a_v, a_f, b_cq, b_ckv, b_kr, c_q, c_k, c_v,
     d_q, d_k, d_v, d_qi, d_ki, d_wi) = jnp.split(w_in, pts, axis=1)
    qs = HEAD_DIM ** -0.5
    small = jnp.concatenate([
        a_f, d_wi * ((IDX_HEADS * IDX_DIM) ** -0.5),
        jnp.zeros((d, _KR_LANE - _WI_LANE - IDX_HEADS), F32),
        b_kr, _swap_half(b_kr), d_ki], axis=1)
    w_all = jnp.concatenate([
        a_q * qs, a_k, a_v,
        c_q * qs, c_k, c_v,
        d_q * qs, d_qi, d_k, d_v,
        small, b_cq, b_ckv], axis=1).astype(BF16)

    per_q = MLA_NOPE + MLA_ROPE
    wq = []
    for hd in range(N_HEADS):
        blk = w_q_up[:, hd * per_q:(hd + 1) * per_q]
        rot = blk[:, MLA_NOPE:]
        wq += [blk[:, :MLA_NOPE], rot, _swap_half(rot)]
    wq = jnp.concatenate(wq, axis=1).astype(BF16)

    place = np.zeros((LANES, LANES), np.float32)
    place[_KR_LANE + np.arange(MLA_ROPE), MLA_NOPE + np.arange(MLA_ROPE)] = 1.0
    place = jnp.asarray(place)
    kcols, vcols = [], []
    for hd in range(N_HEADS):
        blk = w_kv_up[:, hd * 2 * HEAD_DIM:(hd + 1) * 2 * HEAD_DIM]
        knope = jnp.concatenate([blk[:, :MLA_NOPE], jnp.zeros((MLA_KV_RANK, LANES - MLA_NOPE), F32)], axis=1)
        kcols.append(jnp.concatenate([knope, place], axis=0))
        vcols.append(jnp.concatenate([blk[:, MLA_NOPE:], jnp.zeros((LANES, HEAD_DIM), F32)], axis=0))
    wkv = jnp.concatenate(kcols + vcols, axis=1).astype(BF16)
    return w_all, wq, wkv


def _rope_tables(seq):
    half = MLA_ROPE // 2
    inv = ROPE_THETA ** (-jnp.arange(half, dtype=F32) / half)
    ang = jnp.arange(seq, dtype=F32)[:, None] * inv[None, :]
    cos = jnp.tile(jnp.cos(ang), (1, 2))
    sin = jnp.tile(jnp.sin(ang), (1, 2))
    scale = LOG2E * (MLA_NOPE + MLA_ROPE) ** -0.5
    z = lambda n: jnp.zeros((seq, n), F32)
    tab_q = jnp.concatenate([jnp.full((seq, MLA_NOPE), scale, F32), cos * scale, z(MLA_ROPE)], axis=1)
    tab_qs = jnp.concatenate([z(MLA_NOPE), sin * scale, z(MLA_ROPE)], axis=1)
    tab_k = jnp.concatenate([z(_KR_LANE), cos, z(LANES - _KR_LANE - MLA_ROPE)], axis=1)
    tab_ks = jnp.concatenate([z(_KR_LANE), sin, z(LANES - _KR_LANE - MLA_ROPE)], axis=1)
    return jnp.stack([tab_q, tab_qs, tab_k, tab_ks])


def kernel(x, c, w_ada, b_ada, g_norm1, w_in, b_forget, g_q_mla, w_q_up, g_kv_mla, w_kv_up, sinks, w_out,
           g_norm2, w_router, b_router, w_gate, w_up, w_down, g_final):
    b, s, d = x.shape
    depth = w_in.shape[0]
    t = b * s
    topk = min(TOPK_MAX, s // 4)
    tm = min(1024, s)
    tq = min(512, s)
    tq_swa = min(512, s)
    qb_dsa = 2 * LANES
    n_steps = t // MOE_ROWS + N_CLASSES
    assert s % tm == 0 and s % tq == 0 and s % qb_dsa == 0 and s % PERM_ROWS == 0

    mod = _ada_call(c, w_ada, b_ada).reshape(depth, b, 6, d)
    tabs = _rope_tables(s)
    wr_hi = w_router.T.astype(BF16)
    wr_t = jnp.stack([wr_hi, (w_router.T - wr_hi.astype(F32)).astype(BF16)])
    br = b_router.reshape(N_EXPERTS, 1)
    x2 = x.reshape(t, d)
    hs = None
    for l in range(depth):
        w_all, wq, wkv = _layer_weights(w_in[l], w_q_up[l], w_kv_up[l])
        oa, oc, od, osm, oqb, okb, ovb = _inproj_call(
            x2, mod[l], g_norm1[l].reshape(1, d), w_all, wq, wkv,
            g_q_mla[l].reshape(1, -1), g_kv_mla[l].reshape(1, -1), tabs, s, tm)
        small3 = osm.reshape(b, s, LANES)
        bf = jnp.zeros((1, LANES), F32).at[0, :N_HEADS].set(b_forget[l])
        fcum = _fox_cumsum_call(small3, bf)
        frow = jnp.swapaxes(fcum[:, :, :8], 1, 2)
        out_a = _causal_attn_call((oa.reshape(b, s, -1), fcum, frow), True, b, s, tq)
        out_b = _causal_attn_call((oqb.reshape(b, s, -1), okb.reshape(b, s, -1), ovb.reshape(b, s, -1)),
                                  False, b, s, tq)
        out_c = _swa_call(oc.reshape(b, s, -1), sinks[l], b, s, tq_swa)
        out_d = _dsa_call(od.reshape(b, s, -1), small3, b, s, topk, qb_dsa)
        outs = [o.reshape(t, 2 * LANES) for o in (out_a, out_b, out_c, out_d)]
        x2, h2, route = _outproj_call(x2, outs, w_out[l].astype(BF16), mod[l], g_norm2[l].reshape(1, d),
                                      wr_t, br, s, tm)
        pos3, e_low, e_high, valid = _routing_tables(route[0].astype(I32), n_steps)
        wgu = jnp.concatenate([w_gate[l], w_up[l]], axis=2).astype(BF16)
        hs = _scatter_rows_call(pos3, h2, jnp.zeros((n_steps * MOE_ROWS, d + LANES), F32) if hs is None else hs)
        ys = _moe_call(hs, e_low, e_high, valid, wgu, w_down[l].astype(BF16))
        x2 = _gather_residual_call(pos3, ys, x2, mod[l], g_final.reshape(1, d), s, l == depth - 1)
    return x2.reshape(b, s, d)
```
